```python
import math
import jax, jax.numpy as jnp
from jax import lax
import numpy as np

D_MODEL = 1024
BATCH = 16
SEQ = 4096
DEPTH = 4

CHUNK = 64
N_PREV_CHUNKS = 8
BAND = CHUNK * (N_PREV_CHUNKS + 1)
MAX_REL = 128
HEAD_DIM = 64
MIX_WIDTH = D_MODEL
D_SB = MIX_WIDTH // 2
D_CA = MIX_WIDTH - D_SB
H_SB = D_SB // HEAD_DIM
H_CA = D_CA // HEAD_DIM
SB_QBLOCK = 128
D_FF = int(round(8 * D_MODEL / 3 / 256)) * 256
FFN_RESIDUAL = 0.5
ALPHA = (2 * DEPTH) ** 0.25
BETA_INIT = (8 * DEPTH) ** -0.25
LN_EPS = 1e-5
RMS_EPS = 1e-6

kernel_name = "hybrid_stickbreak_chunkattn_macaron_deepnorm"


def layer_norm(x, g, b):
    xf = x.astype(jnp.float32)
    mu = jnp.mean(xf, axis=-1, keepdims=True)
    xc = xf - mu
    var = jnp.mean(xc * xc, axis=-1, keepdims=True)
    y = xc * lax.rsqrt(var + LN_EPS) * g.astype(jnp.float32) + b.astype(jnp.float32)
    return y.astype(x.dtype)


def swiglu(x, w_gate, w_up, w_down):
    return (jax.nn.silu(x @ w_gate) * (x @ w_up)) @ w_down


def head_rmsnorm(o, g):
    of = o.astype(jnp.float32)
    of = of * lax.rsqrt(jnp.mean(of * of, axis=-1, keepdims=True) + RMS_EPS)
    b, s, h, d = o.shape
    return (of.reshape(b, s, h * d) * g.astype(jnp.float32)).astype(o.dtype)


def stick_breaking_attention(q, k, v):
    seq = q.shape[2]
    scale = 1.0 / math.sqrt(q.shape[-1])
    outs = []
    for i in range(seq // SB_QBLOCK):
        q0 = i * SB_QBLOCK
        q1 = q0 + SB_QBLOCK
        qi = q[:, :, q0:q1]
        kp = k[:, :, :q1]
        vp = v[:, :, :q1]
        z = jnp.einsum('bhqd,bhkd->bhqk', qi, kp).astype(jnp.float32) * scale
        t_pos = q0 + jnp.arange(SB_QBLOCK)[:, None]
        s_pos = jnp.arange(q1)[None, :]
        causal = s_pos < t_pos
        log_keep = jnp.where(causal, jax.nn.log_sigmoid(-z), 0.0)
        log_surv = lax.cumsum(log_keep, axis=3, reverse=True) - log_keep
        w = jnp.where(causal, jnp.exp(jax.nn.log_sigmoid(z) + log_surv), 0.0)
        outs.append(jnp.einsum('bhqk,bhkd->bhqd', w.astype(v.dtype), vp))
    return jnp.concatenate(outs, axis=2)


def chunked_rel_attention(q, k, v, rel_bias):
    b, seq, h, d = q.shape
    n_chunks = seq // CHUNK
    pad = N_PREV_CHUNKS * CHUNK
    scale = 1.0 / math.sqrt(d)
    k_pad = jnp.pad(k, ((0, 0), (pad, 0), (0, 0), (0, 0)))
    v_pad = jnp.pad(v, ((0, 0), (pad, 0), (0, 0), (0, 0)))
    qi = jnp.arange(CHUNK)[:, None]
    kj = jnp.arange(BAND)[None, :]
    rel = qi + pad - kj
    bias = rel_bias[:, jnp.clip(rel, -MAX_REL, MAX_REL) + MAX_REL].astype(jnp.float32)
    neg = jnp.finfo(jnp.float32).min

    def one_chunk(c):
        start = c * CHUNK
        qc = lax.dynamic_slice_in_dim(q, start, CHUNK, axis=1)
        kc = lax.dynamic_slice_in_dim(k_pad, start, BAND, axis=1)
        vc = lax.dynamic_slice_in_dim(v_pad, start, BAND, axis=1)
        sc = jnp.einsum('bqhd,bkhd->bhqk', qc, kc).astype(jnp.float32) * scale + bias
        valid = (start - pad + jnp.arange(BAND)) >= 0
        sc = jnp.where(valid[None, None, None, :], sc, neg)
        p = jax.nn.softmax(sc, axis=-1).astype(v.dtype)
        return jnp.einsum('bhqk,bkhd->bqhd', p, vc)

    out = lax.map(one_chunk, jnp.arange(n_chunks))
    return jnp.moveaxis(out, 0, 1).reshape(b, seq, h, d)


def _fwd_setup_inputs(seed: int = 0) -> dict:
    key = jax.random.key(seed)
    ks = jax.random.split(key, 32)
    f32 = jnp.float32
    L, D, F = DEPTH, D_MODEL, D_FF
    std_d = D ** -0.5
    std_f = F ** -0.5

    def nrm(k, shape, std):
        return jax.random.normal(k, shape, f32) * std

    def gain(k, shape):
        return 1.0 + 0.02 * jax.random.normal(k, shape, f32)

    x = jax.random.normal(ks[0], (BATCH, SEQ, D), f32)
    w_in = jnp.concatenate([
        nrm(ks[1], (L, D, D_SB), std_d),
        nrm(ks[2], (L, D, D_SB), std_d),
        nrm(ks[3], (L, D, D_SB), std_d * BETA_INIT),
        nrm(ks[4], (L, D, D_CA), std_d),
        nrm(ks[5], (L, D, D_CA), std_d),
        nrm(ks[6], (L, D, D_CA), std_d * BETA_INIT),
    ], axis=-1)
    return {
        "x": x,
        "ffn1_w_gate": nrm(ks[7], (L, D, F), std_d),
        "ffn1_w_up": nrm(ks[8], (L, D, F), std_d),
        "ffn1_w_down": nrm(ks[9], (L, F, D), std_f * BETA_INIT),
        "ln1_g": gain(ks[10], (L, D)),
        "ln1_b": 0.02 * jax.random.normal(ks[11], (L, D), f32),
        "w_in": w_in,
        "rel_bias": 0.1 * jax.random.normal(ks[12], (L, H_CA, 2 * MAX_REL + 1), f32),
        "sb_out_g": gain(ks[13], (L, D_SB)),
        "ca_out_g": gain(ks[14], (L, D_CA)),
        "w_out": nrm(ks[15], (L, MIX_WIDTH, D), MIX_WIDTH ** -0.5 * BETA_INIT),
        "ln2_g": gain(ks[16], (L, D)),
        "ln2_b": 0.02 * jax.random.normal(ks[17], (L, D), f32),
        "ffn2_w_gate": nrm(ks[18], (L, D, F), std_d),
        "ffn2_w_up": nrm(ks[19], (L, D, F), std_d),
        "ffn2_w_down": nrm(ks[20], (L, F, D), std_f * BETA_INIT),
        "ln3_g": gain(ks[21], (L, D)),
        "ln3_b": 0.02 * jax.random.normal(ks[22], (L, D), f32),
    }


def _fwd_reference(x, ffn1_w_gate, ffn1_w_up, ffn1_w_down, ln1_g, ln1_b, w_in, rel_bias,
              sb_out_g, ca_out_g, w_out, ln2_g, ln2_b, ffn2_w_gate, ffn2_w_up,
              ffn2_w_down, ln3_g, ln3_b):
    b, seq, _ = x.shape
    cuts = [D_SB, 2 * D_SB, 3 * D_SB, 3 * D_SB + D_CA, 3 * D_SB + 2 * D_CA]
    for l in range(DEPTH):
        x = layer_norm(ALPHA * x + FFN_RESIDUAL * swiglu(x, ffn1_w_gate[l], ffn1_w_up[l], ffn1_w_down[l]),
                       ln1_g[l], ln1_b[l])
        h = x @ w_in[l]
        qa, ka, va, qb, kb, vb = jnp.split(h, cuts, axis=-1)
        to_bhsd = lambda t, nh: t.reshape(b, seq, nh, HEAD_DIM).transpose(0, 2, 1, 3)
        oa = stick_breaking_attention(to_bhsd(qa, H_SB), to_bhsd(ka, H_SB), to_bhsd(va, H_SB))
        oa = oa.transpose(0, 2, 1, 3)
        to_bshd = lambda t, nh: t.reshape(b, seq, nh, HEAD_DIM)
        ob = chunked_rel_attention(to_bshd(qb, H_CA), to_bshd(kb, H_CA), to_bshd(vb, H_CA), rel_bias[l])
        mixed = jnp.concatenate([head_rmsnorm(oa, sb_out_g[l]), head_rmsnorm(ob, ca_out_g[l])], axis=-1)
        x = layer_norm(ALPHA * x + mixed @ w_out[l], ln2_g[l], ln2_b[l])
        x = layer_norm(ALPHA * x + FFN_RESIDUAL * swiglu(x, ffn2_w_gate[l], ffn2_w_up[l], ffn2_w_down[l]),
                       ln3_g[l], ln3_b[l])
    return x


import jax as _jax
import jax.numpy as _jnp

TWIN_FORMAT = 'train_step'
FWD_PARAMS = ['x', 'ffn1_w_gate', 'ffn1_w_up', 'ffn1_w_down', 'ln1_g', 'ln1_b', 'w_in', 'rel_bias', 'sb_out_g', 'ca_out_g', 'w_out', 'ln2_g', 'ln2_b', 'ffn2_w_gate', 'ffn2_w_up', 'ffn2_w_down', 'ln3_g', 'ln3_b']
TWIN_WEIGHTS = ['ffn1_w_gate', 'ffn1_w_up', 'ffn1_w_down', 'ln1_g', 'ln1_b', 'w_in', 'rel_bias', 'sb_out_g', 'ca_out_g', 'w_out', 'ln2_g', 'ln2_b', 'ffn2_w_gate', 'ffn2_w_up', 'ffn2_w_down', 'ln3_g', 'ln3_b']
TWIN_DIFF_INPUT = 'x'
TWIN_INPUTS = ['x', 'ffn1_w_gate', 'ffn1_w_up', 'ffn1_w_down', 'ln1_g', 'ln1_b', 'w_in', 'rel_bias', 'sb_out_g', 'ca_out_g', 'w_out', 'ln2_g', 'ln2_b', 'ffn2_w_gate', 'ffn2_w_up', 'ffn2_w_down', 'ln3_g', 'ln3_b', 'loss_target', 'm_ffn1_w_gate', 'm_ffn1_w_up', 'm_ffn1_w_down', 'm_ln1_g', 'm_ln1_b', 'm_w_in', 'm_rel_bias', 'm_sb_out_g', 'm_ca_out_g', 'm_w_out', 'm_ln2_g', 'm_ln2_b', 'm_ffn2_w_gate', 'm_ffn2_w_up', 'm_ffn2_w_down', 'm_ln3_g', 'm_ln3_b', 'v_ffn1_w_gate', 'v_ffn1_w_up', 'v_ffn1_w_down', 'v_ln1_g', 'v_ln1_b', 'v_w_in', 'v_rel_bias', 'v_sb_out_g', 'v_ca_out_g', 'v_w_out', 'v_ln2_g', 'v_ln2_b', 'v_ffn2_w_gate', 'v_ffn2_w_up', 'v_ffn2_w_down', 'v_ln3_g', 'v_ln3_b']
TWIN_OUTPUTS = ['loss', 'grad_x', 'grad_ffn1_w_gate', 'grad_ffn1_w_up', 'grad_ffn1_w_down', 'grad_ln1_g', 'grad_ln1_b', 'grad_w_in', 'grad_rel_bias', 'grad_sb_out_g', 'grad_ca_out_g', 'grad_w_out', 'grad_ln2_g', 'grad_ln2_b', 'grad_ffn2_w_gate', 'grad_ffn2_w_up', 'grad_ffn2_w_down', 'grad_ln3_g', 'grad_ln3_b', 'delta_ffn1_w_gate', 'delta_ffn1_w_up', 'delta_ffn1_w_down', 'delta_ln1_g', 'delta_ln1_b', 'delta_w_in', 'delta_rel_bias', 'delta_sb_out_g', 'delta_ca_out_g', 'delta_w_out', 'delta_ln2_g', 'delta_ln2_b', 'delta_ffn2_w_gate', 'delta_ffn2_w_up', 'delta_ffn2_w_down', 'delta_ln3_g', 'delta_ln3_b', 'new_m_ffn1_w_gate', 'new_m_ffn1_w_up', 'new_m_ffn1_w_down', 'new_m_ln1_g', 'new_m_ln1_b', 'new_m_w_in', 'new_m_rel_bias', 'new_m_sb_out_g', 'new_m_ca_out_g', 'new_m_w_out', 'new_m_ln2_g', 'new_m_ln2_b', 'new_m_ffn2_w_gate', 'new_m_ffn2_w_up', 'new_m_ffn2_w_down', 'new_m_ln3_g', 'new_m_ln3_b', 'new_v_ffn1_w_gate', 'new_v_ffn1_w_up', 'new_v_ffn1_w_down', 'new_v_ln1_g', 'new_v_ln1_b', 'new_v_w_in', 'new_v_rel_bias', 'new_v_sb_out_g', 'new_v_ca_out_g', 'new_v_w_out', 'new_v_ln2_g', 'new_v_ln2_b', 'new_v_ffn2_w_gate', 'new_v_ffn2_w_up', 'new_v_ffn2_w_down', 'new_v_ln3_g', 'new_v_ln3_b']
TWIN_LEAF_KINDS = {'loss': 'loss', 'grad_x': 'grad_x', 'grad_ffn1_w_gate': 'grad_w', 'grad_ffn1_w_up': 'grad_w', 'grad_ffn1_w_down': 'grad_w', 'grad_ln1_g': 'grad_w', 'grad_ln1_b': 'grad_w', 'grad_w_in': 'grad_w', 'grad_rel_bias': 'grad_w', 'grad_sb_out_g': 'grad_w', 'grad_ca_out_g': 'grad_w', 'grad_w_out': 'grad_w', 'grad_ln2_g': 'grad_w', 'grad_ln2_b': 'grad_w', 'grad_ffn2_w_gate': 'grad_w', 'grad_ffn2_w_up': 'grad_w', 'grad_ffn2_w_down': 'grad_w', 'grad_ln3_g': 'grad_w', 'grad_ln3_b': 'grad_w', 'delta_ffn1_w_gate': 'delta_w', 'delta_ffn1_w_up': 'delta_w', 'delta_ffn1_w_down': 'delta_w', 'delta_ln1_g': 'delta_w', 'delta_ln1_b': 'delta_w', 'delta_w_in': 'delta_w', 'delta_rel_bias': 'delta_w', 'delta_sb_out_g': 'delta_w', 'delta_ca_out_g': 'delta_w', 'delta_w_out': 'delta_w', 'delta_ln2_g': 'delta_w', 'delta_ln2_b': 'delta_w', 'delta_ffn2_w_gate': 'delta_w', 'delta_ffn2_w_up': 'delta_w', 'delta_ffn2_w_down': 'delta_w', 'delta_ln3_g': 'delta_w', 'delta_ln3_b': 'delta_w', 'new_m_ffn1_w_gate': 'new_m', 'new_m_ffn1_w_up': 'new_m', 'new_m_ffn1_w_down': 'new_m', 'new_m_ln1_g': 'new_m', 'new_m_ln1_b': 'new_m', 'new_m_w_in': 'new_m', 'new_m_rel_bias': 'new_m', 'new_m_sb_out_g': 'new_m', 'new_m_ca_out_g': 'new_m', 'new_m_w_out': 'new_m', 'new_m_ln2_g': 'new_m', 'new_m_ln2_b': 'new_m', 'new_m_ffn2_w_gate': 'new_m', 'new_m_ffn2_w_up': 'new_m', 'new_m_ffn2_w_down': 'new_m', 'new_m_ln3_g': 'new_m', 'new_m_ln3_b': 'new_m', 'new_v_ffn1_w_gate': 'new_v', 'new_v_ffn1_w_up': 'new_v', 'new_v_ffn1_w_down': 'new_v', 'new_v_ln1_g': 'new_v', 'new_v_ln1_b': 'new_v', 'new_v_w_in': 'new_v', 'new_v_rel_bias': 'new_v', 'new_v_sb_out_g': 'new_v', 'new_v_ca_out_g': 'new_v', 'new_v_w_out': 'new_v', 'new_v_ln2_g': 'new_v', 'new_v_ln2_b': 'new_v', 'new_v_ffn2_w_gate': 'new_v', 'new_v_ffn2_w_up': 'new_v', 'new_v_ffn2_w_down': 'new_v', 'new_v_ln3_g': 'new_v', 'new_v_ln3_b': 'new_v'}


def _forward(args):
    return _fwd_reference(*[args[k] for k in FWD_PARAMS])


def _output_shape():
    out = _jax.eval_shape(lambda: _forward(_fwd_setup_inputs(0)))
    return out.shape, out.dtype

N_MICROBATCH = 1
ADAM_LR = 0.001
ADAM_B1 = 0.9
ADAM_B2 = 0.999
ADAM_EPS = 1e-08
ADAM_WD = 0.01
ADAM_STEP = 10
PER_EXAMPLE_BATCH_AXIS = {'x': 0, 'loss_target': 0}
SHARED_INPUTS = []
_WEIGHT_DTYPES = {'ffn1_w_gate': _jnp.float32, 'ffn1_w_up': _jnp.float32, 'ffn1_w_down': _jnp.float32, 'ln1_g': _jnp.float32, 'ln1_b': _jnp.float32, 'w_in': _jnp.float32, 'rel_bias': _jnp.float32, 'sb_out_g': _jnp.float32, 'ca_out_g': _jnp.float32, 'w_out': _jnp.float32, 'ln2_g': _jnp.float32, 'ln2_b': _jnp.float32, 'ffn2_w_gate': _jnp.float32, 'ffn2_w_up': _jnp.float32, 'ffn2_w_down': _jnp.float32, 'ln3_g': _jnp.float32, 'ln3_b': _jnp.float32}
MOMENT_SCALE = {'ffn1_w_gate': 1.205877e-02, 'ffn1_w_up': 1.171913e-02, 'ffn1_w_down': 4.612412e-02, 'ln1_g': 1.612675e+00, 'ln1_b': 3.750519e+00, 'w_in': 1.538287e-01, 'rel_bias': 2.106452e-02, 'sb_out_g': 6.426903e-02, 'ca_out_g': 1.427747e-01, 'w_out': 2.531781e-01, 'ln2_g': 1.754704e+00, 'ln2_b': 1.467023e+00, 'ffn2_w_gate': 1.176509e-02, 'ffn2_w_up': 1.141703e-02, 'ffn2_w_down': 4.502072e-02, 'ln3_g': 3.220863e+01, 'ln3_b': 5.933349e+00}


def _to_microbatches(a, axis):
    t = _jnp.moveaxis(a, axis, 0)
    t = t.reshape((N_MICROBATCH, t.shape[0] // N_MICROBATCH) + t.shape[1:])
    return _jnp.moveaxis(t, 1, axis + 1)


def setup_inputs(seed: int = 0) -> dict:
    inp = _fwd_setup_inputs(seed)
    key = _jax.random.fold_in(_jax.random.key(seed), 7919)
    shape, _ = _output_shape()
    out = dict(inp)
    out["loss_target"] = _jax.random.normal(_jax.random.fold_in(key, 0), shape, _jnp.float32)
    for i, name in enumerate(TWIN_WEIGHTS):
        w = inp[name].astype(_jnp.float32)
        if MOMENT_SCALE is None:
            s = _jnp.sqrt(_jnp.mean(_jnp.square(w)) + 1e-30)
        else:
            s = MOMENT_SCALE[name]
        km, kv = _jax.random.split(_jax.random.fold_in(key, i + 1))
        out[name] = w
        out["m_" + name] = s * _jax.random.normal(km, w.shape, _jnp.float32)
        out["v_" + name] = (s * s) * _jax.random.uniform(kv, w.shape, _jnp.float32, 0.5, 1.5)
    if N_MICROBATCH > 1:
        for name, axis in PER_EXAMPLE_BATCH_AXIS.items():
            out[name] = _to_microbatches(out[name], axis)
    return {'x': out['x'], 'ffn1_w_gate': out['ffn1_w_gate'], 'ffn1_w_up': out['ffn1_w_up'], 'ffn1_w_down': out['ffn1_w_down'], 'ln1_g': out['ln1_g'], 'ln1_b': out['ln1_b'], 'w_in': out['w_in'], 'rel_bias': out['rel_bias'], 'sb_out_g': out['sb_out_g'], 'ca_out_g': out['ca_out_g'], 'w_out': out['w_out'], 'ln2_g': out['ln2_g'], 'ln2_b': out['ln2_b'], 'ffn2_w_gate': out['ffn2_w_gate'], 'ffn2_w_up': out['ffn2_w_up'], 'ffn2_w_down': out['ffn2_w_down'], 'ln3_g': out['ln3_g'], 'ln3_b': out['ln3_b'], 'loss_target': out['loss_target'], 'm_ffn1_w_gate': out['m_ffn1_w_gate'], 'm_ffn1_w_up': out['m_ffn1_w_up'], 'm_ffn1_w_down': out['m_ffn1_w_down'], 'm_ln1_g': out['m_ln1_g'], 'm_ln1_b': out['m_ln1_b'], 'm_w_in': out['m_w_in'], 'm_rel_bias': out['m_rel_bias'], 'm_sb_out_g': out['m_sb_out_g'], 'm_ca_out_g': out['m_ca_out_g'], 'm_w_out': out['m_w_out'], 'm_ln2_g': out['m_ln2_g'], 'm_ln2_b': out['m_ln2_b'], 'm_ffn2_w_gate': out['m_ffn2_w_gate'], 'm_ffn2_w_up': out['m_ffn2_w_up'], 'm_ffn2_w_down': out['m_ffn2_w_down'], 'm_ln3_g': out['m_ln3_g'], 'm_ln3_b': out['m_ln3_b'], 'v_ffn1_w_gate': out['v_ffn1_w_gate'], 'v_ffn1_w_up': out['v_ffn1_w_up'], 'v_ffn1_w_down': out['v_ffn1_w_down'], 'v_ln1_g': out['v_ln1_g'], 'v_ln1_b': out['v_ln1_b'], 'v_w_in': out['v_w_in'], 'v_rel_bias': out['v_rel_bias'], 'v_sb_out_g': out['v_sb_out_g'], 'v_ca_out_g': out['v_ca_out_g'], 'v_w_out': out['v_w_out'], 'v_ln2_g': out['v_ln2_g'], 'v_ln2_b': out['v_ln2_b'], 'v_ffn2_w_gate': out['v_ffn2_w_gate'], 'v_ffn2_w_up': out['v_ffn2_w_up'], 'v_ffn2_w_down': out['v_ffn2_w_down'], 'v_ln3_g': out['v_ln3_g'], 'v_ln3_b': out['v_ln3_b']}


def _loss(weights, diff, rest, loss_target):
    with _jax.named_scope("forward"):
        args = {**rest, TWIN_DIFF_INPUT: diff, **{k: w.astype(_WEIGHT_DTYPES[k]) for k, w in weights.items()}}
        y = _forward(args)
    with _jax.named_scope("loss_head"):
        err = _jnp.square(y.astype(_jnp.float32) - loss_target)
        return 0.5 * _jnp.sum(_jnp.mean(err, axis=-1)) if err.ndim else 0.5 * err


def _adamw(w, g, m, v):
    m = ADAM_B1 * m + (1.0 - ADAM_B1) * g
    v = ADAM_B2 * v + (1.0 - ADAM_B2) * _jnp.square(g)
    m_hat = m / (1.0 - ADAM_B1 ** ADAM_STEP)
    v_hat = v / (1.0 - ADAM_B2 ** ADAM_STEP)
    delta = -ADAM_LR * (m_hat / (_jnp.sqrt(v_hat) + ADAM_EPS) + ADAM_WD * w)
    return delta, m, v


def reference(x, ffn1_w_gate, ffn1_w_up, ffn1_w_down, ln1_g, ln1_b, w_in, rel_bias, sb_out_g, ca_out_g, w_out, ln2_g, ln2_b, ffn2_w_gate, ffn2_w_up, ffn2_w_down, ln3_g, ln3_b, loss_target, m_ffn1_w_gate, m_ffn1_w_up, m_ffn1_w_down, m_ln1_g, m_ln1_b, m_w_in, m_rel_bias, m_sb_out_g, m_ca_out_g, m_w_out, m_ln2_g, m_ln2_b, m_ffn2_w_gate, m_ffn2_w_up, m_ffn2_w_down, m_ln3_g, m_ln3_b, v_ffn1_w_gate, v_ffn1_w_up, v_ffn1_w_down, v_ln1_g, v_ln1_b, v_w_in, v_rel_bias, v_sb_out_g, v_ca_out_g, v_w_out, v_ln2_g, v_ln2_b, v_ffn2_w_gate, v_ffn2_w_up, v_ffn2_w_down, v_ln3_g, v_ln3_b):
    given = dict(x=x, ffn1_w_gate=ffn1_w_gate, ffn1_w_up=ffn1_w_up, ffn1_w_down=ffn1_w_down, ln1_g=ln1_g, ln1_b=ln1_b, w_in=w_in, rel_bias=rel_bias, sb_out_g=sb_out_g, ca_out_g=ca_out_g, w_out=w_out, ln2_g=ln2_g, ln2_b=ln2_b, ffn2_w_gate=ffn2_w_gate, ffn2_w_up=ffn2_w_up, ffn2_w_down=ffn2_w_down, ln3_g=ln3_g, ln3_b=ln3_b, loss_target=loss_target, m_ffn1_w_gate=m_ffn1_w_gate, m_ffn1_w_up=m_ffn1_w_up, m_ffn1_w_down=m_ffn1_w_down, m_ln1_g=m_ln1_g, m_ln1_b=m_ln1_b, m_w_in=m_w_in, m_rel_bias=m_rel_bias, m_sb_out_g=m_sb_out_g, m_ca_out_g=m_ca_out_g, m_w_out=m_w_out, m_ln2_g=m_ln2_g, m_ln2_b=m_ln2_b, m_ffn2_w_gate=m_ffn2_w_gate, m_ffn2_w_up=m_ffn2_w_up, m_ffn2_w_down=m_ffn2_w_down, m_ln3_g=m_ln3_g, m_ln3_b=m_ln3_b, v_ffn1_w_gate=v_ffn1_w_gate, v_ffn1_w_up=v_ffn1_w_up, v_ffn1_w_down=v_ffn1_w_down, v_ln1_g=v_ln1_g, v_ln1_b=v_ln1_b, v_w_in=v_w_in, v_rel_bias=v_rel_bias, v_sb_out_g=v_sb_out_g, v_ca_out_g=v_ca_out_g, v_w_out=v_w_out, v_ln2_g=v_ln2_g, v_ln2_b=v_ln2_b, v_ffn2_w_gate=v_ffn2_w_gate, v_ffn2_w_up=v_ffn2_w_up, v_ffn2_w_down=v_ffn2_w_down, v_ln3_g=v_ln3_g, v_ln3_b=v_ln3_b)
    weights = {n: given[n] for n in TWIN_WEIGHTS}
    shared = {n: given[n] for n in SHARED_INPUTS}
    per_example = {n: given[n] for n in ['x']}
    grad_fn = _jax.value_and_grad(_loss, argnums=(0, 1))

    def one_microbatch(ex, loss_target):
        ex = dict(ex)
        diff = ex.pop(TWIN_DIFF_INPUT)
        return grad_fn(weights, diff, {**shared, **ex}, loss_target)

    if N_MICROBATCH == 1:
        loss, (grad_w, grad_x) = one_microbatch(per_example, given["loss_target"])
    else:
        def body(carry, xs):
            loss_sum, grad_sum = carry
            l_k, (gw_k, gx_k) = one_microbatch(xs[0], xs[1])
            with _jax.named_scope("update"):
                return (loss_sum + l_k, _jax.tree.map(_jnp.add, grad_sum, gw_k)), gx_k

        init = (_jnp.zeros((), _jnp.float32), _jax.tree.map(_jnp.zeros_like, weights))
        (loss, grad_w), grad_x = _jax.lax.scan(body, init, (per_example, given["loss_target"]))
    with _jax.named_scope("update"):
        delta_w, new_m, new_v = {}, {}, {}
        for n in TWIN_WEIGHTS:
            delta_w[n], new_m[n], new_v[n] = _adamw(weights[n], grad_w[n], given["m_" + n], given["v_" + n])
    return (loss, grad_x, *[grad_w[n] for n in TWIN_WEIGHTS], *[delta_w[n] for n in TWIN_WEIGHTS],
            *[new_m[n] for n in TWIN_WEIGHTS], *[new_v[n] for n in TWIN_WEIGHTS])
```

```python
import functools
import math

import jax
import jax.numpy as jnp
import numpy as np
from jax import lax
from jax.experimental import pallas as pl
from jax.experimental.pallas import tpu as pltpu

F32 = jnp.float32
BF16 = jnp.bfloat16

HEAD_DIM = 64
CHUNK = 64
N_PREV_CHUNKS = 8
MAX_REL = 128
DEPTH = 4
FFN_RESIDUAL = 0.5
ALPHA = (2 * DEPTH) ** 0.25
LN_EPS = 1e-5
RMS_EPS = 1e-6
ADAM_LR = 0.001
ADAM_B1 = 0.9
ADAM_B2 = 0.999
ADAM_EPS = 1e-08
ADAM_WD = 0.01
ADAM_STEP = 10

N_DEV = 8
VMEM_LIMIT_BYTES = 56 * 1024 * 1024
ROW_TILE = 512
SB_BLOCK = 128
CA_PAD = CHUNK * N_PREV_CHUNKS
CA_TQ = 2 * CHUNK
CA_WIN = CA_PAD + CA_TQ
NEG = -1e30
SB_SKIP_LOG = -104.0

NN = ((1,), (0,))
NT = ((1,), (1,))
TN = ((0,), (0,))


def _dot(a, b, dims):
    return lax.dot_general(a, b, (dims, ((), ())), preferred_element_type=F32)


def _params(*sem):
    return pltpu.CompilerParams(dimension_semantics=sem, vmem_limit_bytes=VMEM_LIMIT_BYTES)


def _sds(shape, dtype):
    return jax.ShapeDtypeStruct(shape, dtype)


def _row_tile(n, want=ROW_TILE):
    t = min(want, n)
    while n % t:
        t //= 2
    assert t >= 8, (n, want)
    return t


def ffn_up(xb, wg, wu):
    T, D = xb.shape
    F = wg.shape[0]
    tm = _row_tile(T)
    fc = _row_tile(F, 256)

    def body(x_ref, wg_ref, wu_ref, hg_ref, hu_ref, a_ref):
        x = x_ref[...]
        for j in range(F // fc):
            sl = pl.ds(j * fc, fc)
            hg = _dot(x, wg_ref[sl, :], NT)
            hu = _dot(x, wu_ref[sl, :], NT)
            hg_ref[:, sl] = hg.astype(BF16)
            hu_ref[:, sl] = hu.astype(BF16)
            a_ref[:, sl] = (hg * jax.nn.sigmoid(hg) * hu).astype(BF16)

    row = pl.BlockSpec((tm, F), lambda i: (i, 0))
    w = pl.BlockSpec((F, D), lambda i: (0, 0))
    return pl.pallas_call(
        body, grid=(T // tm,), name="ffn_up",
        in_specs=[pl.BlockSpec((tm, D), lambda i: (i, 0)), w, w],
        out_specs=[row, row, row], out_shape=[_sds((T, F), BF16)] * 3,
        compiler_params=_params("parallel"),
    )(xb, wg, wu)


def ffn_bwd_mid(dr, wd, hg, hu):
    T, D = dr.shape
    F = wd.shape[0]
    tm = _row_tile(T)
    fc = _row_tile(F, 256)

    def body(dr_ref, wd_ref, hg_ref, hu_ref, dhg_ref, dhu_ref):
        dr_ = dr_ref[...]
        for j in range(F // fc):
            sl = pl.ds(j * fc, fc)
            da = _dot(dr_, wd_ref[sl, :], NT)
            g = hg_ref[:, sl].astype(F32)
            u = hu_ref[:, sl].astype(F32)
            s = jax.nn.sigmoid(g)
            gs = g * s
            dhu_ref[:, sl] = (da * gs).astype(BF16)
            dhg_ref[:, sl] = (da * u * (s + gs * (1.0 - s))).astype(BF16)

    row = pl.BlockSpec((tm, F), lambda i: (i, 0))
    return pl.pallas_call(
        body, grid=(T // tm,), name="ffn_bwd_mid",
        in_specs=[pl.BlockSpec((tm, D), lambda i: (i, 0)), pl.BlockSpec((F, D), lambda i: (0, 0)), row, row],
        out_specs=[row, row], out_shape=[_sds((T, F), BF16)] * 2,
        compiler_params=_params("parallel"),
    )(dr, wd, hg, hu)


def mm_nn(pairs, add=None, out_dtype=F32, name="mm_nn"):
    M = pairs[0][0].shape[0]
    N = pairs[0][1].shape[1]
    tm = _row_tile(M)
    nc = _row_tile(N, 512)
    n_pairs = len(pairs)

    def body(*refs):
        a_refs = refs[:n_pairs]
        b_refs = refs[n_pairs:2 * n_pairs]
        add_ref = refs[2 * n_pairs] if add is not None else None
        o_ref = refs[-1]
        for j in range(N // nc):
            sl = pl.ds(j * nc, nc)
            acc = _dot(a_refs[0][...], b_refs[0][:, sl], NN)
            for a_ref, b_ref in zip(a_refs[1:], b_refs[1:]):
                acc = acc + _dot(a_ref[...], b_ref[:, sl], NN)
            if add_ref is not None:
                acc = acc + add_ref[:, sl]
            o_ref[:, sl] = acc.astype(out_dtype)

    in_specs = [pl.BlockSpec((tm, a.shape[1]), lambda i: (i, 0)) for a, _ in pairs]
    in_specs += [pl.BlockSpec(b.shape, lambda i: (0, 0)) for _, b in pairs]
    args = [a for a, _ in pairs] + [b for _, b in pairs]
    if add is not None:
        in_specs.append(pl.BlockSpec((tm, N), lambda i: (i, 0)))
        args.append(add)
    return pl.pallas_call(
        body, grid=(M // tm,), name=name, in_specs=in_specs,
        out_specs=pl.BlockSpec((tm, N), lambda i: (i, 0)), out_shape=_sds((M, N), out_dtype),
        compiler_params=_params("parallel"),
    )(*args)


def mm_nt(a, b, out_dtype=F32, name="mm_nt"):
    M, K = a.shape
    N = b.shape[0]
    tm = _row_tile(M)
    nc = _row_tile(N, 512)

    def body(a_ref, b_ref, o_ref):
        a_ = a_ref[...]
        for j in range(N // nc):
            sl = pl.ds(j * nc, nc)
            o_ref[:, sl] = _dot(a_, b_ref[sl, :], NT).astype(out_dtype)

    return pl.pallas_call(
        body, grid=(M // tm,), name=name,
        in_specs=[pl.BlockSpec((tm, K), lambda i: (i, 0)), pl.BlockSpec((N, K), lambda i: (0, 0))],
        out_specs=pl.BlockSpec((tm, N), lambda i: (i, 0)), out_shape=_sds((M, N), out_dtype),
        compiler_params=_params("parallel"),
    )(a, b)


def mm_tn(a, b, name="mm_tn"):
    T, M = a.shape
    N = b.shape[1]
    tk = _row_tile(T)
    mc = _row_tile(M, 256)
    steps = T // tk

    def body(a_ref, b_ref, o_ref, acc_ref):
        i = pl.program_id(0)

        @pl.when(i == 0)
        def _():
            acc_ref[...] = jnp.zeros_like(acc_ref)

        b_ = b_ref[...]
        for j in range(M // mc):
            sl = pl.ds(j * mc, mc)
            acc_ref[sl, :] += _dot(a_ref[:, sl], b_, TN)

        @pl.when(i == steps - 1)
        def _():
            o_ref[...] = acc_ref[...].astype(BF16)

    return pl.pallas_call(
        body, grid=(steps,), name=name,
        in_specs=[pl.BlockSpec((tk, M), lambda i: (i, 0)), pl.BlockSpec((tk, N), lambda i: (i, 0))],
        out_specs=pl.BlockSpec((M, N), lambda i: (0, 0)), out_shape=_sds((M, N), BF16),
        scratch_shapes=[pltpu.VMEM((M, N), F32)],
        compiler_params=_params("arbitrary"),
    )(a, b)


def ln_fwd(x, r, g, b, res_scale):
    T, D = x.shape
    tm = _row_tile(T)

    def body(x_ref, r_ref, g_ref, b_ref, u_ref, y_ref, yb_ref):
        u = ALPHA * x_ref[...] + res_scale * r_ref[...]
        mu = jnp.mean(u, axis=-1, keepdims=True)
        xc = u - mu
        var = jnp.mean(xc * xc, axis=-1, keepdims=True)
        y = xc * lax.rsqrt(var + LN_EPS) * g_ref[...] + b_ref[...]
        u_ref[...] = u
        y_ref[...] = y
        yb_ref[...] = y.astype(BF16)

    row = pl.BlockSpec((tm, D), lambda i: (i, 0))
    vec = pl.BlockSpec((1, D), lambda i: (0, 0))
    return pl.pallas_call(
        body, grid=(T // tm,), name="ln_fwd", in_specs=[row, row, vec, vec],
        out_specs=[row, row, row], out_shape=[_sds((T, D), F32), _sds((T, D), F32), _sds((T, D), BF16)],
        compiler_params=_params("parallel"),
    )(x, r, g.reshape(1, D), b.reshape(1, D))


def ln_bwd(u, dy, g, branch_scale):
    T, D = u.shape
    tm = _row_tile(T)

    def body(u_ref, dy_ref, g_ref, dxa_ref, dr_ref, dg_ref, db_ref):
        @pl.when(pl.program_id(0) == 0)
        def _():
            dg_ref[...] = jnp.zeros_like(dg_ref)
            db_ref[...] = jnp.zeros_like(db_ref)

        u_ = u_ref[...]
        dy_ = dy_ref[...]
        mu = jnp.mean(u_, axis=-1, keepdims=True)
        xc = u_ - mu
        var = jnp.mean(xc * xc, axis=-1, keepdims=True)
        rstd = lax.rsqrt(var + LN_EPS)
        xh = xc * rstd
        dxh = dy_ * g_ref[...]
        m1 = jnp.mean(dxh, axis=-1, keepdims=True)
        m2 = jnp.mean(dxh * xh, axis=-1, keepdims=True)
        du = rstd * (dxh - m1 - xh * m2)
        dxa_ref[...] = ALPHA * du
        dr_ref[...] = (branch_scale * du).astype(BF16)
        dg_ref[...] += jnp.sum((dy_ * xh).reshape(tm // 8, 8, D), axis=0)
        db_ref[...] += jnp.sum(dy_.reshape(tm // 8, 8, D), axis=0)

    row = pl.BlockSpec((tm, D), lambda i: (i, 0))
    acc = pl.BlockSpec((8, D), lambda i: (0, 0))
    dxa, dr, dg, db = pl.pallas_call(
        body, grid=(T // tm,), name="ln_bwd", in_specs=[row, row, pl.BlockSpec((1, D), lambda i: (0, 0))],
        out_specs=[row, row, acc, acc],
        out_shape=[_sds((T, D), F32), _sds((T, D), BF16), _sds((8, D), F32), _sds((8, D), F32)],
        compiler_params=_params("arbitrary"),
    )(u, dy, g.reshape(1, D))
    return dxa, dr, dg, db


def loss_head(y, target):
    T, D = y.shape
    tm = _row_tile(T)

    def body(y_ref, t_ref, dy_ref, l_ref):
        @pl.when(pl.program_id(0) == 0)
        def _():
            l_ref[...] = jnp.zeros_like(l_ref)

        e = y_ref[...] - t_ref[...]
        dy_ref[...] = e * (1.0 / D)
        l_ref[...] += jnp.sum((e * e).reshape(tm // 8, 8, D), axis=0) * (0.5 / D)

    row = pl.BlockSpec((tm, D), lambda i: (i, 0))
    return pl.pallas_call(
        body, grid=(T // tm,), name="loss_head", in_specs=[row, row],
        out_specs=[row, pl.BlockSpec((8, D), lambda i: (0, 0))],
        out_shape=[_sds((T, D), F32), _sds((8, D), F32)],
        compiler_params=_params("arbitrary"),
    )(y, target)


def adamw(w, g, m, v):
    R, C = w.shape
    tr = R
    for cand in (512, 256, 128, 64, 32, 16, 8):
        if R % cand == 0:
            tr = cand
            break
    c1 = 1.0 - ADAM_B1 ** ADAM_STEP
    c2 = 1.0 - ADAM_B2 ** ADAM_STEP

    def body(w_ref, g_ref, m_ref, v_ref, d_ref, mo_ref, vo_ref):
        g_ = g_ref[...]
        m_ = ADAM_B1 * m_ref[...] + (1.0 - ADAM_B1) * g_
        v_ = ADAM_B2 * v_ref[...] + (1.0 - ADAM_B2) * (g_ * g_)
        m_hat = m_ / c1
        v_hat = v_ / c2
        d_ref[...] = -ADAM_LR * (m_hat / (jnp.sqrt(v_hat) + ADAM_EPS) + ADAM_WD * w_ref[...])
        mo_ref[...] = m_
        vo_ref[...] = v_

    blk = pl.BlockSpec((tr, C), lambda i: (i, 0))
    return pl.pallas_call(
        body, grid=(R // tr,), name="adamw", in_specs=[blk] * 4, out_specs=[blk] * 3,
        out_shape=[_sds((R, C), F32)] * 3, compiler_params=_params("parallel"),
    )(w, g, m, v)


def _rms_fwd(o, gain):
    r = lax.rsqrt(jnp.mean(o * o, axis=-1, keepdims=True) + RMS_EPS)
    return (o * r * gain).astype(BF16)


def _rms_bwd(o, dm, gain):
    r = lax.rsqrt(jnp.mean(o * o, axis=-1, keepdims=True) + RMS_EPS)
    oh = o * r
    dg = jnp.sum(dm * oh, axis=0, keepdims=True)
    doh = dm * gain
    do = r * (doh - oh * jnp.mean(doh * oh, axis=-1, keepdims=True))
    return do, dg


def _split_dot(x, tri):
    n = x.shape[0]
    hi = x.astype(BF16)
    lo = (x - hi.astype(F32)).astype(BF16)
    both = _dot(jnp.concatenate([hi, lo], axis=0), tri, NN)
    return both[:n] + both[n:]


def _log_keep(z):
    return -(jnp.maximum(z, 0.0) + jnp.log(1.0 + jnp.exp(-jnp.abs(z))))


def sb_fwd(q, k, v, gain):
    BH, S, d = q.shape
    H = gain.shape[0]
    tb = min(SB_BLOCK, S)
    nkb = S // tb
    scale = 1.0 / math.sqrt(d)

    def body(q_ref, k_ref, v_ref, g_ref, o_ref, m_ref, c_ref):
        qi = pl.program_id(1)
        qs = (q_ref[0].astype(F32) * scale).astype(BF16)
        row = lax.broadcasted_iota(jnp.int32, (tb, tb), 0)
        col = lax.broadcasted_iota(jnp.int32, (tb, tb), 1)
        rev_incl = (row >= col).astype(BF16)
        lane = lax.broadcasted_iota(jnp.int32, (tb, nkb), 1)

        def block(kb, carry, mask):
            ks = pl.ds(pl.multiple_of(kb * tb, tb), tb)
            z = _dot(qs, k_ref[0, ks, :], NT)
            lk = _log_keep(z)
            if mask is not None:
                lk = jnp.where(mask, lk, 0.0)
            cum = _split_dot(lk, rev_incl)
            logw = z + cum + carry
            if mask is not None:
                logw = jnp.where(mask, logw, NEG)
            w = jnp.exp(logw)
            return _dot(w.astype(BF16), v_ref[0, ks, :], NN), cum[:, 0:1]

        zero = jnp.zeros((tb, 1), F32)
        pv, tot = block(qi, zero, col < row)
        cars = jnp.where(lane == qi, 0.0, NEG)

        def cond(st):
            kb, carry, _, _ = st
            return jnp.logical_and(kb >= 0, jnp.max(carry) > SB_SKIP_LOG)

        def step(st):
            kb, carry, acc, cars_ = st
            pv_, tot_ = block(kb, carry, None)
            return kb - 1, carry + tot_, acc + pv_, jnp.where(lane == kb, carry, cars_)

        _, _, acc, cars = lax.while_loop(cond, step, (qi - 1, tot, pv, cars))
        o_ref[0] = acc
        m_ref[0] = _rms_fwd(acc, g_ref[0])
        c_ref[0] = cars

    qspec = pl.BlockSpec((1, tb, d), lambda b, i: (b, i, 0))
    kvspec = pl.BlockSpec((1, S, d), lambda b, i: (b, 0, 0))
    return pl.pallas_call(
        body, grid=(BH, S // tb), name="sb_fwd",
        in_specs=[qspec, kvspec, kvspec, pl.BlockSpec((1, 1, d), lambda b, i: (b % H, 0, 0))],
        out_specs=[qspec, qspec, pl.BlockSpec((1, tb, nkb), lambda b, i: (b, i, 0))],
        out_shape=[_sds((BH, S, d), F32), _sds((BH, S, d), BF16), _sds((BH, S, nkb), F32)],
        compiler_params=_params("parallel", "arbitrary"),
    )(q, k, v, gain)


def sb_bwd(q, k, v, o, dm, gain, cars):
    BH, S, d = q.shape
    H = gain.shape[0]
    tb = min(SB_BLOCK, S)
    nkb = S // tb
    scale = 1.0 / math.sqrt(d)

    def body(q_ref, k_ref, v_ref, o_ref, dm_ref, g_ref, c_ref, dq_ref, dk_ref, dv_ref, dg_ref):
        qi = pl.program_id(1)

        @pl.when(qi == 0)
        def _():
            dk_ref[...] = jnp.zeros_like(dk_ref)
            dv_ref[...] = jnp.zeros_like(dv_ref)

        qs = (q_ref[0].astype(F32) * scale).astype(BF16)
        do, dg = _rms_bwd(o_ref[0], dm_ref[0], g_ref[0])
        dg_ref[0, 0] = dg
        dob = do.astype(BF16)
        cars_ = c_ref[0]
        row = lax.broadcasted_iota(jnp.int32, (tb, tb), 0)
        col = lax.broadcasted_iota(jnp.int32, (tb, tb), 1)
        rev_incl = (row >= col).astype(BF16)
        fwd_incl = (row <= col).astype(BF16)
        lane = lax.broadcasted_iota(jnp.int32, (tb, nkb), 1)

        def block(kb, gsum, dq, mask):
            ks = pl.ds(pl.multiple_of(kb * tb, tb), tb)
            kk = k_ref[0, ks, :]
            vv = v_ref[0, ks, :]
            z = _dot(qs, kk, NT)
            lk = _log_keep(z)
            if mask is not None:
                lk = jnp.where(mask, lk, 0.0)
            carry = jnp.sum(jnp.where(lane == kb, cars_, 0.0), axis=1, keepdims=True)
            logw = z + _split_dot(lk, rev_incl) + carry
            if mask is not None:
                logw = jnp.where(mask, logw, NEG)
            w = jnp.exp(logw)
            gw = w * _dot(dob, vv, NT)
            gcum = _split_dot(gw, fwd_incl) + gsum
            dz = gw - jax.nn.sigmoid(z) * gcum
            if mask is not None:
                dz = jnp.where(mask, dz, 0.0)
            dzb = dz.astype(BF16)
            dk_ref[0, ks, :] += _dot(dzb, qs, TN)
            dv_ref[0, ks, :] += _dot(w.astype(BF16), dob, TN)
            return gcum[:, tb - 1:tb], dq + _dot(dzb, kk, NN)

        visited = jnp.max(cars_, axis=0, keepdims=True) > SB_SKIP_LOG
        below = lax.broadcasted_iota(jnp.int32, (1, nkb), 1) < qi
        first = qi - jnp.sum(jnp.logical_and(visited, below).astype(jnp.int32))

        def step(kb, st):
            return block(kb, st[0], st[1], None)

        gsum, dq = lax.fori_loop(first, qi, step, (jnp.zeros((tb, 1), F32), jnp.zeros((tb, d), F32)))
        _, dq = block(qi, gsum, dq, col < row)
        dq_ref[0] = dq * scale

    qspec = pl.BlockSpec((1, tb, d), lambda b, i: (b, i, 0))
    kvspec = pl.BlockSpec((1, S, d), lambda b, i: (b, 0, 0))
    nq = S // tb
    return pl.pallas_call(
        body, grid=(BH, nq), name="sb_bwd",
        in_specs=[qspec, kvspec, kvspec, qspec, qspec, pl.BlockSpec((1, 1, d), lambda b, i: (b % H, 0, 0)),
                  pl.BlockSpec((1, tb, nkb), lambda b, i: (b, i, 0))],
        out_specs=[qspec, kvspec, kvspec, pl.BlockSpec((1, 1, 1, d), lambda b, i: (b, i, 0, 0))],
        out_shape=[_sds((BH, S, d), F32), _sds((BH, S, d), F32), _sds((BH, S, d), F32), _sds((BH, nq, 1, d), F32)],
        compiler_params=_params("parallel", "arbitrary"),
    )(q, k, v, o, dm, gain, cars)


def ca_bias_index():
    t = np.arange(CA_TQ)[:, None]
    j = np.arange(CA_WIN)[None, :]
    c = t // CHUNK
    allowed = (j >= CHUNK * c) & (j < CHUNK * c + CA_PAD + CHUNK)
    idx = np.clip(t - j + CA_PAD, -MAX_REL, MAX_REL) + MAX_REL
    return idx, allowed


def _ca_scores(q_ref, k_ref, b_ref, q0, scale):
    qs = (q_ref[0].astype(F32) * scale).astype(BF16)
    ks = pl.ds(pl.multiple_of(q0, CA_TQ), CA_WIN)
    kk = k_ref[0, ks, :]
    s = _dot(qs, kk, NT) + b_ref[0]
    col = lax.broadcasted_iota(jnp.int32, (CA_TQ, CA_WIN), 1)
    s = jnp.where(col + q0 >= CA_PAD, s, NEG)
    e = jnp.exp(s - jnp.max(s, axis=-1, keepdims=True))
    p = e * (1.0 / jnp.sum(e, axis=-1, keepdims=True))
    return qs, kk, ks, p


def ca_fwd(q, kp, vp, bias, gain, B):
    BH, S, d = q.shape
    H = gain.shape[0]
    scale = 1.0 / math.sqrt(d)

    def body(q_ref, k_ref, v_ref, b_ref, g_ref, o_ref, m_ref):
        q0 = pl.program_id(2) * CA_TQ
        _, _, ks, p = _ca_scores(q_ref, k_ref, b_ref, q0, scale)
        o = _dot(p.astype(BF16), v_ref[0, ks, :], NN)
        o_ref[0] = o
        m_ref[0] = _rms_fwd(o, g_ref[0])

    qspec = pl.BlockSpec((1, CA_TQ, d), lambda h, b, i: (b * H + h, i, 0))
    kvspec = pl.BlockSpec((1, S + CA_PAD, d), lambda h, b, i: (b * H + h, 0, 0))
    return pl.pallas_call(
        body, grid=(H, B, S // CA_TQ), name="ca_fwd",
        in_specs=[qspec, kvspec, kvspec, pl.BlockSpec((1, CA_TQ, CA_WIN), lambda h, b, i: (h, 0, 0)),
                  pl.BlockSpec((1, 1, d), lambda h, b, i: (h, 0, 0))],
        out_specs=[qspec, qspec], out_shape=[_sds((BH, S, d), F32), _sds((BH, S, d), BF16)],
        compiler_params=_params("parallel", "parallel", "arbitrary"),
    )(q, kp, vp, bias, gain)


def ca_bwd(q, kp, vp, bias, o, dm, gain, B):
    BH, S, d = q.shape
    H = gain.shape[0]
    scale = 1.0 / math.sqrt(d)
    nq = S // CA_TQ

    def body(q_ref, k_ref, v_ref, b_ref, o_ref, dm_ref, g_ref, dq_ref, dk_ref, dv_ref, dg_ref, db_ref):
        bi = pl.program_id(1)
        qi = pl.program_id(2)

        @pl.when(qi == 0)
        def _():
            dk_ref[...] = jnp.zeros_like(dk_ref)
            dv_ref[...] = jnp.zeros_like(dv_ref)

        @pl.when(jnp.logical_and(qi == 0, bi == 0))
        def _():
            db_ref[...] = jnp.zeros_like(db_ref)

        qs, kk, ks, p = _ca_scores(q_ref, k_ref, b_ref, qi * CA_TQ, scale)
        do, dg = _rms_bwd(o_ref[0], dm_ref[0], g_ref[0])
        dg_ref[0, 0] = dg
        dob = do.astype(BF16)
        dp = _dot(dob, v_ref[0, ks, :], NT)
        ds = p * (dp - jnp.sum(p * dp, axis=-1, keepdims=True))
        db_ref[0] += ds
        dsb = ds.astype(BF16)
        dq_ref[0] = _dot(dsb, kk, NN) * scale
        dk_ref[0, ks, :] += _dot(dsb, qs, TN)
        dv_ref[0, ks, :] += _dot(p.astype(BF16), dob, TN)

    qspec = pl.BlockSpec((1, CA_TQ, d), lambda h, b, i: (b * H + h, i, 0))
    kvspec = pl.BlockSpec((1, S + CA_PAD, d), lambda h, b, i: (b * H + h, 0, 0))
    bspec = pl.BlockSpec((1, CA_TQ, CA_WIN), lambda h, b, i: (h, 0, 0))
    return pl.pallas_call(
        body, grid=(H, B, nq), name="ca_bwd",
        in_specs=[qspec, kvspec, kvspec, bspec, qspec, qspec, pl.BlockSpec((1, 1, d), lambda h, b, i: (h, 0, 0))],
        out_specs=[qspec, kvspec, kvspec, pl.BlockSpec((1, 1, 1, d), lambda h, b, i: (b * H + h, i, 0, 0)), bspec],
        out_shape=[_sds((BH, S, d), F32), _sds((BH, S + CA_PAD, d), F32), _sds((BH, S + CA_PAD, d), F32),
                   _sds((BH, nq, 1, d), F32), _sds((H, CA_TQ, CA_WIN), F32)],
        compiler_params=_params("parallel", "arbitrary", "arbitrary"),
    )(q, kp, vp, bias, o, dm, gain)


def rel_bias_grad(db):
    H = db.shape[0]
    w = CA_WIN + CA_TQ - 1
    x = jnp.pad(db, ((0, 0), (0, 0), (0, w - CA_WIN))).reshape(H, CA_TQ * w)
    x = jnp.pad(x, ((0, 0), (0, CA_TQ))).reshape(H, CA_TQ, w + 1).sum(axis=1)
    c = np.arange(w + 1)
    dj = np.where(c < CA_WIN, c, c - (w + 1))
    idx = np.clip(CA_PAD - dj, -MAX_REL, MAX_REL) + MAX_REL
    onehot = (idx[:, None] == np.arange(2 * MAX_REL + 1)[None, :]).astype(np.float32)
    return jnp.dot(x, jnp.asarray(onehot), precision=lax.Precision.HIGHEST)


_ANY = pl.BlockSpec(memory_space=pl.ANY)
_MESH = pl.DeviceIdType.MESH


def _mesh_pos():
    return lax.axis_index("x"), lax.axis_index("y"), lax.axis_index("c")


def _dev_index(p):
    return 4 * p[0] + 2 * p[1] + p[2]


def _flip(pos, k):
    return tuple(1 - v if (k >> (2 - a)) & 1 else v for a, v in enumerate(pos))


def gather_weights(mats):
    nm = len(mats)
    rows = [m.shape[0] for m in mats]

    def body(*refs):
        ins = refs[:nm]
        outs = refs[nm:2 * nm]
        send_sems, recv_sems, local_sems = refs[2 * nm:]
        x, y, c = _mesh_pos()
        me, sibling = (x, y, c), (x, y, 1 - c)
        chips = [(1 - x, y), (x, 1 - y), (1 - x, 1 - y)]

        def block(m, p):
            start = pl.multiple_of(_dev_index(p) * rows[m], 16)
            return outs[m].at[pl.ds(start, rows[m]), :]

        def copy(k, m, blk, to, src=None):
            return pltpu.make_async_remote_copy(
                src_ref=block(m, blk) if src is None else src, dst_ref=block(m, blk),
                send_sem=send_sems.at[k, m], recv_sem=recv_sems.at[k, m], device_id=to, device_id_type=_MESH)

        mine = [pltpu.make_async_copy(ins[m], block(m, me), local_sems.at[m]) for m in range(nm)]
        for cp in mine:
            cp.start()
        first = [copy(0, m, me, sibling, src=ins[m]) for m in range(nm)]
        first += [copy(1 + j, m, me, (*chip, c), src=ins[m]) for j, chip in enumerate(chips) for m in range(nm)]
        for cp in first:
            cp.start()
        passed = []
        for j, chip in enumerate(chips):
            for m in range(nm):
                copy(1 + j, m, (*chip, c), me).wait_recv()
                fwd = copy(4 + j, m, (*chip, c), sibling)
                fwd.start()
                passed.append(fwd)
        for m in range(nm):
            copy(0, m, sibling, me).wait_recv()
        for j, chip in enumerate(chips):
            for m in range(nm):
                copy(4 + j, m, (*chip, 1 - c), me).wait_recv()
        for cp in first + passed:
            cp.wait_send()
        for cp in mine:
            cp.wait()

    return pl.pallas_call(
        body, name="gather_weights", in_specs=[_ANY] * nm, out_specs=[_ANY] * nm,
        out_shape=[_sds((N_DEV * m.shape[0], m.shape[1]), m.dtype) for m in mats],
        scratch_shapes=[pltpu.SemaphoreType.DMA((7, nm)), pltpu.SemaphoreType.DMA((7, nm)), pltpu.SemaphoreType.DMA((nm,))],
    )(*mats)


def exchange_grads(grads):
    nm = len(grads)
    rows = [g.shape[0] // N_DEV for g in grads]

    def body(*refs):
        ins = refs[:nm]
        outs = refs[nm:2 * nm]
        send_sems, recv_sems, local_sems = refs[2 * nm:]
        me = _mesh_pos()
        my = _dev_index(me)

        def piece(m, idx):
            return ins[m].at[pl.ds(pl.multiple_of(idx * rows[m], 16), rows[m]), :]

        def copy(k, m, src_idx, slot, to):
            return pltpu.make_async_remote_copy(
                src_ref=piece(m, src_idx), dst_ref=outs[m].at[slot],
                send_sem=send_sems.at[k - 1, m], recv_sem=recv_sems.at[k - 1, m], device_id=to, device_id_type=_MESH)

        mine = [pltpu.make_async_copy(piece(m, my), outs[m].at[my], local_sems.at[m]) for m in range(nm)]
        for cp in mine:
            cp.start()
        sends = []
        for k in range(1, N_DEV):
            peer = _flip(me, k)
            for m in range(nm):
                cp = copy(k, m, _dev_index(peer), my, peer)
                cp.start()
                sends.append(cp)
        for k in range(1, N_DEV):
            peer = _flip(me, k)
            for m in range(nm):
                copy(k, m, my, _dev_index(peer), peer).wait_recv()
        for cp in sends:
            cp.wait_send()
        for cp in mine:
            cp.wait()

    return pl.pallas_call(
        body, name="exchange_grads", in_specs=[_ANY] * nm, out_specs=[_ANY] * nm,
        out_shape=[_sds((N_DEV, g.shape[0] // N_DEV, g.shape[1]), g.dtype) for g in grads],
        scratch_shapes=[pltpu.SemaphoreType.DMA((7, nm)), pltpu.SemaphoreType.DMA((7, nm)), pltpu.SemaphoreType.DMA((nm,))],
    )(*grads)


def sum_slots(r):
    n, R, D = r.shape
    tc = _row_tile(D, 256)

    def body(r_ref, o_ref):
        acc = r_ref[0].astype(F32)
        for s in range(1, n):
            acc = acc + r_ref[s].astype(F32)
        o_ref[...] = acc

    return pl.pallas_call(
        body, grid=(D // tc,), name="sum_slots", in_specs=[pl.BlockSpec((n, R, tc), lambda i: (0, 0, i))],
        out_specs=pl.BlockSpec((R, tc), lambda i: (0, i)), out_shape=_sds((R, D), F32),
        compiler_params=_params("parallel"),
    )(r)


def allreduce_small(v):
    R, C = v.shape

    def body(v_ref, o_ref, buf, send_sems, recv_sems):
        me = _mesh_pos()
        my = _dev_index(me)
        buf[my] = v_ref[...]

        def copy(k, slot, to):
            return pltpu.make_async_remote_copy(
                src_ref=v_ref, dst_ref=buf.at[slot], send_sem=send_sems.at[k - 1], recv_sem=recv_sems.at[k - 1],
                device_id=to, device_id_type=_MESH)

        sends = []
        for k in range(1, N_DEV):
            cp = copy(k, my, _flip(me, k))
            cp.start()
            sends.append(cp)
        for k in range(1, N_DEV):
            peer = _flip(me, k)
            copy(k, _dev_index(peer), peer).wait_recv()
        acc = buf[0]
        for s in range(1, N_DEV):
            acc = acc + buf[s]
        o_ref[...] = acc
        for cp in sends:
            cp.wait_send()

    vm = pl.BlockSpec(memory_space=pltpu.VMEM)
    return pl.pallas_call(
        body, name="allreduce_small", in_specs=[vm], out_specs=vm, out_shape=_sds((R, C), F32),
        scratch_shapes=[pltpu.VMEM((N_DEV, R, C), F32), pltpu.SemaphoreType.DMA((7,)), pltpu.SemaphoreType.DMA((7,))],
    )(v)


WEIGHTS = ["ffn1_w_gate", "ffn1_w_up", "ffn1_w_down", "ln1_g", "ln1_b", "w_in", "rel_bias", "sb_out_g", "ca_out_g",
           "w_out", "ln2_g", "ln2_b", "ffn2_w_gate", "ffn2_w_up", "ffn2_w_down", "ln3_g", "ln3_b"]
BIG = ["ffn1_w_gate", "ffn1_w_up", "ffn2_w_gate", "ffn2_w_up", "w_in", "ffn1_w_down", "ffn2_w_down", "w_out"]
TRANSPOSED = BIG[:5]
SMALL = [n for n in WEIGHTS if n not in BIG]


def _heads(t, B, S, n):
    H = t.shape[1] // (n * HEAD_DIM)
    return t.reshape(B, S, n, H, HEAD_DIM).transpose(2, 0, 3, 1, 4).reshape(n, B * H, S, HEAD_DIM)


def _tokens(parts, B, S, dtype):
    n = len(parts)
    H = parts[0].shape[0] // B
    t = jnp.stack([p.astype(dtype) for p in parts]).reshape(n, B, H, S, HEAD_DIM).transpose(1, 3, 0, 2, 4)
    return t.reshape(B * S, n * H * HEAD_DIM)


def _pack(vals):
    flat = jnp.concatenate([v.reshape(-1).astype(F32) for v in vals])
    pad = -flat.shape[0] % (8 * 128)
    return jnp.pad(flat, (0, pad)).reshape(-1, 128)


def _unpack(packed, like):
    flat = packed.reshape(-1)
    out, off = [], 0
    for v in like:
        out.append(flat[off:off + v.size].reshape(v.shape))
        off += v.size
    return out


def _layer_fwd(x, xb, W, P, l, B, S):
    sv = {"xb": xb}
    sv["hg1"], sv["hu1"], sv["a1"] = ffn_up(xb, W["ffn1_w_gate"], W["ffn1_w_up"])
    y = mm_nn([(sv["a1"], W["ffn1_w_down"])], name="ffn_down")
    sv["u1"], x1, sv["x1b"] = ln_fwd(x, y, P["ln1_g"][l], P["ln1_b"][l], FFN_RESIDUAL)

    h = mm_nt(sv["x1b"], W["w_in"], out_dtype=BF16, name="proj_in")
    qa, ka, va, qb, kb, vb = _heads(h, B, S, 6)
    H = qa.shape[0] // B
    sv["gA"] = P["sb_out_g"][l].reshape(H, 1, HEAD_DIM)
    sv["gB"] = P["ca_out_g"][l].reshape(H, 1, HEAD_DIM)
    sv["qa"], sv["ka"], sv["va"], sv["qb"] = qa, ka, va, qb
    sv["oa"], ma, sv["cars"] = sb_fwd(qa, ka, va, sv["gA"])
    sv["kbp"] = jnp.pad(kb, ((0, 0), (CA_PAD, 0), (0, 0)))
    sv["vbp"] = jnp.pad(vb, ((0, 0), (CA_PAD, 0), (0, 0)))
    idx, allowed = ca_bias_index()
    sv["bias"] = jnp.where(allowed[None], P["rel_bias"][l][:, idx], NEG)
    sv["ob"], mb = ca_fwd(qb, sv["kbp"], sv["vbp"], sv["bias"], sv["gB"], B)
    sv["mixed"] = _tokens([ma, mb], B, S, BF16)
    y = mm_nn([(sv["mixed"], W["w_out"])], name="proj_out")
    sv["u2"], x2, sv["x2b"] = ln_fwd(x1, y, P["ln2_g"][l], P["ln2_b"][l], 1.0)

    sv["hg2"], sv["hu2"], sv["a2"] = ffn_up(sv["x2b"], W["ffn2_w_gate"], W["ffn2_w_up"])
    y = mm_nn([(sv["a2"], W["ffn2_w_down"])], name="ffn_down")
    sv["u3"], x3, x3b = ln_fwd(x2, y, P["ln3_g"][l], P["ln3_b"][l], FFN_RESIDUAL)
    return x3, x3b, sv


def _ffn_bwd(dr, dxa, xb, hg, hu, a, wg, wu, wd):
    dhg, dhu = ffn_bwd_mid(dr, wd, hg, hu)
    dx = mm_nn([(dhg, wg), (dhu, wu)], add=dxa, name="ffn_dx")
    return dx, mm_tn(dhg, xb, name="ffn_dw"), mm_tn(dhu, xb, name="ffn_dw"), mm_tn(a, dr, name="ffn_dw")


def _layer_bwd(dx, sv, W, P, l, B, S):
    G = {}
    dxa, dr, dg, db = ln_bwd(sv["u3"], dx, P["ln3_g"][l], FFN_RESIDUAL)
    G["ln3_g"], G["ln3_b"] = dg.sum(0), db.sum(0)
    dx, G["ffn2_w_gate"], G["ffn2_w_up"], G["ffn2_w_down"] = _ffn_bwd(
        dr, dxa, sv["x2b"], sv["hg2"], sv["hu2"], sv["a2"], W["ffn2_w_gate"], W["ffn2_w_up"], W["ffn2_w_down"])

    dxa, dr, dg, db = ln_bwd(sv["u2"], dx, P["ln2_g"][l], 1.0)
    G["ln2_g"], G["ln2_b"] = dg.sum(0), db.sum(0)
    dmixed = mm_nt(dr, W["w_out"], name="proj_out_dx")
    G["w_out"] = mm_tn(sv["mixed"], dr, name="proj_out_dw")
    dma, dmb = _heads(dmixed, B, S, 2)
    dqa, dka, dva, dga = sb_bwd(sv["qa"], sv["ka"], sv["va"], sv["oa"], dma, sv["gA"], sv["cars"])
    dqb, dkb, dvb, dgb, dbias = ca_bwd(sv["qb"], sv["kbp"], sv["vbp"], sv["bias"], sv["ob"], dmb, sv["gB"], B)
    H = dqa.shape[0] // B
    G["sb_out_g"] = dga.reshape(B, H, -1, HEAD_DIM).sum((0, 2)).reshape(-1)
    G["ca_out_g"] = dgb.reshape(B, H, -1, HEAD_DIM).sum((0, 2)).reshape(-1)
    G["rel_bias"] = rel_bias_grad(dbias)
    dh = _tokens([dqa, dka, dva, dqb, dkb[:, CA_PAD:], dvb[:, CA_PAD:]], B, S, BF16)
    dx = mm_nn([(dh, W["w_in"])], add=dxa, name="proj_in_dx")
    G["w_in"] = mm_tn(dh, sv["x1b"], name="proj_in_dw")

    dxa, dr, dg, db = ln_bwd(sv["u1"], dx, P["ln1_g"][l], FFN_RESIDUAL)
    G["ln1_g"], G["ln1_b"] = dg.sum(0), db.sum(0)
    dx, G["ffn1_w_gate"], G["ffn1_w_up"], G["ffn1_w_down"] = _ffn_bwd(
        dr, dxa, sv["xb"], sv["hg1"], sv["hu1"], sv["a1"], W["ffn1_w_gate"], W["ffn1_w_up"], W["ffn1_w_down"])
    return dx, G


def kernel(x, ffn1_w_gate, ffn1_w_up, ffn1_w_down, ln1_g, ln1_b, w_in, rel_bias, sb_out_g, ca_out_g, w_out, ln2_g, ln2_b, ffn2_w_gate, ffn2_w_up, ffn2_w_down, ln3_g, ln3_b, loss_target, m_ffn1_w_gate, m_ffn1_w_up, m_ffn1_w_down, m_ln1_g, m_ln1_b, m_w_in, m_rel_bias, m_sb_out_g, m_ca_out_g, m_w_out, m_ln2_g, m_ln2_b, m_ffn2_w_gate, m_ffn2_w_up, m_ffn2_w_down, m_ln3_g, m_ln3_b, v_ffn1_w_gate, v_ffn1_w_up, v_ffn1_w_down, v_ln1_g, v_ln1_b, v_w_in, v_rel_bias, v_sb_out_g, v_ca_out_g, v_w_out, v_ln2_g, v_ln2_b, v_ffn2_w_gate, v_ffn2_w_up, v_ffn2_w_down, v_ln3_g, v_ln3_b):
    given = dict(locals())
    P = {n: given[n] for n in WEIGHTS}
    M = {n: given["m_" + n] for n in WEIGHTS}
    V = {n: given["v_" + n] for n in WEIGHTS}
    B, S, D = x.shape
    L = ln1_g.shape[0]

    local = {n: (jnp.swapaxes(P[n], 1, 2) if n in TRANSPOSED else P[n]).astype(BF16) for n in BIG}
    W = []
    for l in range(L):
        full = gather_weights([local[n][l] for n in BIG])
        W.append(dict(zip(BIG, full)))

    xf = x.reshape(B * S, D)
    xb = xf.astype(BF16)
    saved = []
    for l in range(L):
        xf, xb, sv = _layer_fwd(xf, xb, W[l], P, l, B, S)
        saved.append(sv)
    dx, loss_part = loss_head(xf, loss_target.reshape(B * S, D))

    big_g = {n: [None] * L for n in BIG}
    small_g = {n: [None] * L for n in SMALL}
    for l in reversed(range(L)):
        dx, G = _layer_bwd(dx, saved[l], W[l], P, l, B, S)
        got = exchange_grads([G[n] for n in BIG])
        for n, r in zip(BIG, got):
            big_g[n][l] = sum_slots(r)
        for n in SMALL:
            small_g[n][l] = G[n]

    grads = {}
    for n in BIG:
        g = jnp.stack(big_g[n])
        grads[n] = jnp.swapaxes(g, 1, 2) if n in TRANSPOSED else g
    small_like = [P[n] for n in SMALL]
    packed = _pack([jnp.stack(small_g[n]) for n in SMALL] + [loss_part.sum()])
    total = allreduce_small(packed)
    *small_vals, loss = _unpack(total, small_like + [jnp.zeros((), F32)])
    grads.update(dict(zip(SMALL, small_vals)))

    delta, new_m, new_v = {}, {}, {}
    for n in BIG:
        shape = P[n].shape
        two_d = lambda a: a.reshape(shape[0] * shape[1], shape[2])
        d, m, v = adamw(two_d(P[n]), two_d(grads[n]), two_d(M[n]), two_d(V[n]))
        delta[n], new_m[n], new_v[n] = d.reshape(shape), m.reshape(shape), v.reshape(shape)
    one = jnp.ones((), F32)
    d, m, v = adamw(_pack(small_like + [one]), total, _pack([M[n] for n in SMALL] + [one]), _pack([V[n] for n in SMALL] + [one]))
    for dst, src in ((delta, d), (new_m, m), (new_v, v)):
        dst.update(dict(zip(SMALL, _unpack(src, small_like))))

    return (loss, dx.reshape(B, S, D), *[grads[n] for n in WEIGHTS], *[delta[n] for n in WEIGHTS],
            *[new_m[n] for n in WEIGHTS], *[new_v[n] for n in WEIGHTS])
```

```python
import math

import jax
import jax.numpy as jnp
import numpy as np
from jax import lax
from jax.experimental import pallas as pl
from jax.experimental.pallas import tpu as pltpu

F32 = jnp.float32
BF16 = jnp.bfloat16

HEAD_DIM = 64
CHUNK = 64
N_PREV_CHUNKS = 8
MAX_REL = 128
DEPTH = 4
FFN_RESIDUAL = 0.5
ALPHA = (2 * DEPTH) ** 0.25
LN_EPS = 1e-5
RMS_EPS = 1e-6
ADAM_LR = 0.001
ADAM_B1 = 0.9
ADAM_B2 = 0.999
ADAM_EPS = 1e-08
ADAM_WD = 0.01
ADAM_STEP = 10

N_DEV = 8
LANES = 128
VMEM_LIMIT_BYTES = 56 * 1024 * 1024
ROW_TILE = 512
SB_BLOCK = 128
CA_PAD = CHUNK * N_PREV_CHUNKS
CA_TQ = 2 * CHUNK
CA_WIN = CA_PAD + CA_TQ
FWD_HEADS = 8
BWD_HEADS = 4
NEG = -1e30
SB_SKIP_LOG = -104.0

NN = ((1,), (0,))
NT = ((1,), (1,))
TN = ((0,), (0,))


def _dot(a, b, dims):
    return lax.dot_general(a, b, (dims, ((), ())), preferred_element_type=F32)


def _params(*sem):
    return pltpu.CompilerParams(dimension_semantics=sem, vmem_limit_bytes=VMEM_LIMIT_BYTES)


def _sds(shape, dtype):
    return jax.ShapeDtypeStruct(shape, dtype)


def _row_tile(n, want=ROW_TILE):
    t = min(want, n)
    while n % t:
        t //= 2
    assert t >= 8, (n, want)
    return t


def ffn_up(xb, wg, wu):
    T, D = xb.shape
    F = wg.shape[0]
    tm = _row_tile(T)
    fc = _row_tile(F, 256)

    def body(x_ref, wg_ref, wu_ref, hg_ref, hu_ref, a_ref):
        x = x_ref[...]
        for j in range(F // fc):
            sl = pl.ds(j * fc, fc)
            hg = _dot(x, wg_ref[sl, :], NT)
            hu = _dot(x, wu_ref[sl, :], NT)
            hg_ref[:, sl] = hg.astype(BF16)
            hu_ref[:, sl] = hu.astype(BF16)
            a_ref[:, sl] = (hg * jax.nn.sigmoid(hg) * hu).astype(BF16)

    row = pl.BlockSpec((tm, F), lambda i: (i, 0))
    w = pl.BlockSpec((F, D), lambda i: (0, 0))
    return pl.pallas_call(
        body, grid=(T // tm,), name="ffn_up",
        in_specs=[pl.BlockSpec((tm, D), lambda i: (i, 0)), w, w],
        out_specs=[row, row, row], out_shape=[_sds((T, F), BF16)] * 3,
        compiler_params=_params("parallel"),
    )(xb, wg, wu)


def ffn_bwd_mid(dr, wd, hg, hu):
    T, D = dr.shape
    F = wd.shape[0]
    tm = _row_tile(T)
    fc = _row_tile(F, 256)

    def body(dr_ref, wd_ref, hg_ref, hu_ref, dhg_ref, dhu_ref):
        dr_ = dr_ref[...]
        for j in range(F // fc):
            sl = pl.ds(j * fc, fc)
            da = _dot(dr_, wd_ref[sl, :], NT)
            g = hg_ref[:, sl].astype(F32)
            u = hu_ref[:, sl].astype(F32)
            s = jax.nn.sigmoid(g)
            gs = g * s
            dhu_ref[:, sl] = (da * gs).astype(BF16)
            dhg_ref[:, sl] = (da * u * (s + gs * (1.0 - s))).astype(BF16)

    row = pl.BlockSpec((tm, F), lambda i: (i, 0))
    return pl.pallas_call(
        body, grid=(T // tm,), name="ffn_bwd_mid",
        in_specs=[pl.BlockSpec((tm, D), lambda i: (i, 0)), pl.BlockSpec((F, D), lambda i: (0, 0)), row, row],
        out_specs=[row, row], out_shape=[_sds((T, F), BF16)] * 2,
        compiler_params=_params("parallel"),
    )(dr, wd, hg, hu)


def mm_nn(pairs, add=None, out_dtype=F32, name="mm_nn"):
    M = pairs[0][0].shape[0]
    N = pairs[0][1].shape[1]
    tm = _row_tile(M)
    nc = _row_tile(N, 512)
    n_pairs = len(pairs)

    def body(*refs):
        a_refs = refs[:n_pairs]
        b_refs = refs[n_pairs:2 * n_pairs]
        add_ref = refs[2 * n_pairs] if add is not None else None
        o_ref = refs[-1]
        for j in range(N // nc):
            sl = pl.ds(j * nc, nc)
            acc = _dot(a_refs[0][...], b_refs[0][:, sl], NN)
            for a_ref, b_ref in zip(a_refs[1:], b_refs[1:]):
                acc = acc + _dot(a_ref[...], b_ref[:, sl], NN)
            if add_ref is not None:
                acc = acc + add_ref[:, sl]
            o_ref[:, sl] = acc.astype(out_dtype)

    in_specs = [pl.BlockSpec((tm, a.shape[1]), lambda i: (i, 0)) for a, _ in pairs]
    in_specs += [pl.BlockSpec(b.shape, lambda i: (0, 0)) for _, b in pairs]
    args = [a for a, _ in pairs] + [b for _, b in pairs]
    if add is not None:
        in_specs.append(pl.BlockSpec((tm, N), lambda i: (i, 0)))
        args.append(add)
    return pl.pallas_call(
        body, grid=(M // tm,), name=name, in_specs=in_specs,
        out_specs=pl.BlockSpec((tm, N), lambda i: (i, 0)), out_shape=_sds((M, N), out_dtype),
        compiler_params=_params("parallel"),
    )(*args)


def mm_nt(a, b, out_dtype=F32, name="mm_nt"):
    M, K = a.shape
    N = b.shape[0]
    tm = _row_tile(M)
    nc = _row_tile(N, 512)

    def body(a_ref, b_ref, o_ref):
        a_ = a_ref[...]
        for j in range(N // nc):
            sl = pl.ds(j * nc, nc)
            o_ref[:, sl] = _dot(a_, b_ref[sl, :], NT).astype(out_dtype)

    return pl.pallas_call(
        body, grid=(M // tm,), name=name,
        in_specs=[pl.BlockSpec((tm, K), lambda i: (i, 0)), pl.BlockSpec((N, K), lambda i: (0, 0))],
        out_specs=pl.BlockSpec((tm, N), lambda i: (i, 0)), out_shape=_sds((M, N), out_dtype),
        compiler_params=_params("parallel"),
    )(a, b)


def proj_in(xb, w, B, S):
    D = xb.shape[1]
    N = w.shape[0]
    tm = CA_PAD
    nblk = S // tm
    nc = _row_tile(N, 512)
    assert S % tm == 0

    def body(x_ref, w_ref, o_ref):
        i = pl.program_id(1)

        @pl.when(i == 0)
        def _():
            o_ref[...] = jnp.zeros_like(o_ref)

        @pl.when(i > 0)
        def _():
            x = x_ref[...]
            for j in range(N // nc):
                sl = pl.ds(j * nc, nc)
                o_ref[0, :, sl] = _dot(x, w_ref[sl, :], NT).astype(BF16)

    return pl.pallas_call(
        body, grid=(B, nblk + 1), name="proj_in",
        in_specs=[pl.BlockSpec((tm, D), lambda b, i: (b * nblk + jnp.maximum(i - 1, 0), 0)),
                  pl.BlockSpec((N, D), lambda b, i: (0, 0))],
        out_specs=pl.BlockSpec((1, tm, N), lambda b, i: (b, i, 0)), out_shape=_sds((B, CA_PAD + S, N), BF16),
        compiler_params=_params("parallel", "arbitrary"),
    )(xb, w)


def mm_tn(a, b, name="mm_tn"):
    T, M = a.shape
    N = b.shape[1]
    tk = _row_tile(T)
    mc = _row_tile(M, 256)
    steps = T // tk

    def body(a_ref, b_ref, o_ref, acc_ref):
        i = pl.program_id(0)

        @pl.when(i == 0)
        def _():
            acc_ref[...] = jnp.zeros_like(acc_ref)

        b_ = b_ref[...]
        for j in range(M // mc):
            sl = pl.ds(j * mc, mc)
            acc_ref[sl, :] += _dot(a_ref[:, sl], b_, TN)

        @pl.when(i == steps - 1)
        def _():
            o_ref[...] = acc_ref[...].astype(BF16)

    return pl.pallas_call(
        body, grid=(steps,), name=name,
        in_specs=[pl.BlockSpec((tk, M), lambda i: (i, 0)), pl.BlockSpec((tk, N), lambda i: (i, 0))],
        out_specs=pl.BlockSpec((M, N), lambda i: (0, 0)), out_shape=_sds((M, N), BF16),
        scratch_shapes=[pltpu.VMEM((M, N), F32)],
        compiler_params=_params("arbitrary"),
    )(a, b)


def ln_fwd(x, r, g, b, res_scale):
    T, D = x.shape
    tm = _row_tile(T)

    def body(x_ref, r_ref, g_ref, b_ref, u_ref, y_ref, yb_ref):
        u = ALPHA * x_ref[...] + res_scale * r_ref[...]
        mu = jnp.mean(u, axis=-1, keepdims=True)
        xc = u - mu
        var = jnp.mean(xc * xc, axis=-1, keepdims=True)
        y = xc * lax.rsqrt(var + LN_EPS) * g_ref[...] + b_ref[...]
        u_ref[...] = u
        y_ref[...] = y
        yb_ref[...] = y.astype(BF16)

    row = pl.BlockSpec((tm, D), lambda i: (i, 0))
    vec = pl.BlockSpec((1, D), lambda i: (0, 0))
    return pl.pallas_call(
        body, grid=(T // tm,), name="ln_fwd", in_specs=[row, row, vec, vec],
        out_specs=[row, row, row], out_shape=[_sds((T, D), F32), _sds((T, D), F32), _sds((T, D), BF16)],
        compiler_params=_params("parallel"),
    )(x, r, g.reshape(1, D), b.reshape(1, D))


def ln_bwd(u, dy, g, branch_scale):
    T, D = u.shape
    tm = _row_tile(T)

    def body(u_ref, dy_ref, g_ref, dxa_ref, dr_ref, dg_ref, db_ref):
        @pl.when(pl.program_id(0) == 0)
        def _():
            dg_ref[...] = jnp.zeros_like(dg_ref)
            db_ref[...] = jnp.zeros_like(db_ref)

        u_ = u_ref[...]
        dy_ = dy_ref[...]
        mu = jnp.mean(u_, axis=-1, keepdims=True)
        xc = u_ - mu
        var = jnp.mean(xc * xc, axis=-1, keepdims=True)
        rstd = lax.rsqrt(var + LN_EPS)
        xh = xc * rstd
        dxh = dy_ * g_ref[...]
        m1 = jnp.mean(dxh, axis=-1, keepdims=True)
        m2 = jnp.mean(dxh * xh, axis=-1, keepdims=True)
        du = rstd * (dxh - m1 - xh * m2)
        dxa_ref[...] = ALPHA * du
        dr_ref[...] = (branch_scale * du).astype(BF16)
        dg_ref[...] += jnp.sum((dy_ * xh).reshape(tm // 8, 8, D), axis=0)
        db_ref[...] += jnp.sum(dy_.reshape(tm // 8, 8, D), axis=0)

    row = pl.BlockSpec((tm, D), lambda i: (i, 0))
    acc = pl.BlockSpec((8, D), lambda i: (0, 0))
    dxa, dr, dg, db = pl.pallas_call(
        body, grid=(T // tm,), name="ln_bwd", in_specs=[row, row, pl.BlockSpec((1, D), lambda i: (0, 0))],
        out_specs=[row, row, acc, acc],
        out_shape=[_sds((T, D), F32), _sds((T, D), BF16), _sds((8, D), F32), _sds((8, D), F32)],
        compiler_params=_params("arbitrary"),
    )(u, dy, g.reshape(1, D))
    return dxa, dr, dg, db


def loss_head(y, target):
    T, D = y.shape
    tm = _row_tile(T)

    def body(y_ref, t_ref, dy_ref, l_ref):
        @pl.when(pl.program_id(0) == 0)
        def _():
            l_ref[...] = jnp.zeros_like(l_ref)

        e = y_ref[...] - t_ref[...]
        dy_ref[...] = e * (1.0 / D)
        l_ref[...] += jnp.sum((e * e).reshape(tm // 8, 8, D), axis=0) * (0.5 / D)

    row = pl.BlockSpec((tm, D), lambda i: (i, 0))
    return pl.pallas_call(
        body, grid=(T // tm,), name="loss_head", in_specs=[row, row],
        out_specs=[row, pl.BlockSpec((8, D), lambda i: (0, 0))],
        out_shape=[_sds((T, D), F32), _sds((8, D), F32)],
        compiler_params=_params("arbitrary"),
    )(y, target)


def adamw(w, g, m, v):
    R, C = w.shape
    tr = R
    for cand in (512, 256, 128, 64, 32, 16, 8):
        if R % cand == 0:
            tr = cand
            break
    c1 = 1.0 - ADAM_B1 ** ADAM_STEP
    c2 = 1.0 - ADAM_B2 ** ADAM_STEP

    def body(w_ref, g_ref, m_ref, v_ref, d_ref, mo_ref, vo_ref):
        g_ = g_ref[...]
        m_ = ADAM_B1 * m_ref[...] + (1.0 - ADAM_B1) * g_
        v_ = ADAM_B2 * v_ref[...] + (1.0 - ADAM_B2) * (g_ * g_)
        m_hat = m_ / c1
        v_hat = v_ / c2
        d_ref[...] = -ADAM_LR * (m_hat / (jnp.sqrt(v_hat) + ADAM_EPS) + ADAM_WD * w_ref[...])
        mo_ref[...] = m_
        vo_ref[...] = v_

    blk = pl.BlockSpec((tr, C), lambda i: (i, 0))
    return pl.pallas_call(
        body, grid=(R // tr,), name="adamw", in_specs=[blk] * 4, out_specs=[blk] * 3,
        out_shape=[_sds((R, C), F32)] * 3, compiler_params=_params("parallel"),
    )(w, g, m, v)


def _lane_lo(rows):
    return lax.broadcasted_iota(jnp.int32, (rows, LANES), 1) < HEAD_DIM


def _pair_mean(x, lo):
    s0 = jnp.sum(jnp.where(lo, x, 0.0), axis=-1, keepdims=True)
    s1 = jnp.sum(jnp.where(lo, 0.0, x), axis=-1, keepdims=True)
    return jnp.where(lo, s0, s1) * (1.0 / HEAD_DIM)


def _rms_fwd(o, gain, lo):
    r = lax.rsqrt(_pair_mean(o * o, lo) + RMS_EPS)
    return (o * r * gain).astype(BF16)


def _rms_bwd(o, dm, gain, lo):
    r = lax.rsqrt(_pair_mean(o * o, lo) + RMS_EPS)
    oh = o * r
    dg = jnp.sum(dm * oh, axis=0, keepdims=True)
    doh = dm * gain
    do = r * (doh - oh * _pair_mean(doh * oh, lo))
    return do, dg


def _split_heads(x, lo):
    zero = jnp.zeros_like(x)
    return [jnp.where(lo, x, zero), jnp.where(lo, zero, x)]


def _merge_pairs(per_head, lo):
    return [jnp.where(lo, per_head[2 * p], per_head[2 * p + 1]) for p in range(len(per_head) // 2)]


def _pair_cols(h):
    return slice((h // 2) * LANES, (h // 2 + 1) * LANES)


def _split_dot(x, tri):
    n = x.shape[0]
    hi = x.astype(BF16)
    lo = (x - hi.astype(F32)).astype(BF16)
    both = _dot(jnp.concatenate([hi, lo], axis=0), tri, NN)
    return both[:n] + both[n:]


def _log_keep(z):
    return -(jnp.maximum(z, 0.0) + jnp.log(1.0 + jnp.exp(-jnp.abs(z))))


def _attn_dims(h3p, gain, heads):
    B, SP, C = h3p.shape
    HD = gain.shape[1]
    hs = min(heads, HD // HEAD_DIM)
    W = hs * HEAD_DIM
    assert C == 6 * HD and W % LANES == 0 and HD % W == 0
    return B, SP - CA_PAD, HD, hs, W, HD // W


def sb_fwd(h3p, gain):
    B, S, HD, hs, W, ngrp = _attn_dims(h3p, gain, FWD_HEADS)
    tb = min(SB_BLOCK, S)
    nq = S // tb
    off = CA_PAD // tb
    scale = 1.0 / math.sqrt(HEAD_DIM)

    def body(q_ref, k_ref, v_ref, g_ref, o_ref, m_ref, c_ref):
        qi = pl.program_id(2)
        lo = _lane_lo(tb)
        row = lax.broadcasted_iota(jnp.int32, (tb, tb), 0)
        col = lax.broadcasted_iota(jnp.int32, (tb, tb), 1)
        rev_incl = (row >= col).astype(BF16)
        lane = lax.broadcasted_iota(jnp.int32, (tb, nq), 1)
        qm = []
        for p in range(hs // 2):
            qm += _split_heads((q_ref[0, :, p * LANES:(p + 1) * LANES].astype(F32) * scale).astype(BF16), lo)

        def block(h, kb, carry, mask):
            ks = pl.ds(pl.multiple_of(kb * tb + CA_PAD, tb), tb)
            z = _dot(qm[h], k_ref[0, ks, _pair_cols(h)], NT)
            lk = _log_keep(z)
            if mask is not None:
                lk = jnp.where(mask, lk, 0.0)
            cum = _split_dot(lk, rev_incl)
            logw = z + cum + carry
            if mask is not None:
                logw = jnp.where(mask, logw, NEG)
            w = jnp.exp(logw)
            return _dot(w.astype(BF16), v_ref[0, ks, _pair_cols(h)], NN), cum[:, 0:1]

        zero = jnp.zeros((tb, 1), F32)
        diag = [block(h, qi, zero, col < row) for h in range(hs)]
        accs = _merge_pairs([d[0] for d in diag], lo)
        carries = [d[1] for d in diag]
        cars = [jnp.where(lane == qi, 0.0, NEG)] * hs

        def cond(st):
            kb, carries_, _, _ = st
            top = carries_[0]
            for c in carries_[1:]:
                top = jnp.maximum(top, c)
            return jnp.logical_and(kb >= 0, jnp.max(top) > SB_SKIP_LOG)

        def step(st):
            kb, carries_, accs_, cars_ = st
            out = [block(h, kb, carries_[h], None) for h in range(hs)]
            pv = _merge_pairs([o[0] for o in out], lo)
            return (kb - 1, [c + o[1] for c, o in zip(carries_, out)], [a + p for a, p in zip(accs_, pv)],
                    [jnp.where(lane == kb, c, cs) for c, cs in zip(carries_, cars_)])

        _, _, accs, cars = lax.while_loop(cond, step, (qi - 1, carries, accs, cars))
        for p, acc in enumerate(accs):
            cols = slice(p * LANES, (p + 1) * LANES)
            o_ref[0, :, cols] = acc
            m_ref[0, :, cols] = _rms_fwd(acc, g_ref[:, cols], lo)
        for h in range(hs):
            c_ref[0, h] = cars[h]

    qspec = pl.BlockSpec((1, tb, W), lambda g, b, i: (b, i + off, g))
    ospec = pl.BlockSpec((1, tb, W), lambda g, b, i: (b, i, g))
    return pl.pallas_call(
        body, grid=(ngrp, B, nq), name="sb_fwd",
        in_specs=[qspec, pl.BlockSpec((1, CA_PAD + S, W), lambda g, b, i: (b, 0, ngrp + g)),
                  pl.BlockSpec((1, CA_PAD + S, W), lambda g, b, i: (b, 0, 2 * ngrp + g)),
                  pl.BlockSpec((1, W), lambda g, b, i: (0, g))],
        out_specs=[ospec, ospec, pl.BlockSpec((1, hs, tb, nq), lambda g, b, i: (b, g, i, 0))],
        out_shape=[_sds((B, S, HD), F32), _sds((B, S, HD), BF16), _sds((B, HD // HEAD_DIM, S, nq), F32)],
        compiler_params=_params("parallel", "parallel", "arbitrary"),
    )(h3p, h3p, h3p, gain)


def sb_bwd(h3p, o, dmix, gain, cars):
    B, S, HD, hs, W, ngrp = _attn_dims(h3p, gain, BWD_HEADS)
    tb = min(SB_BLOCK, S)
    nq = S // tb
    off = CA_PAD // tb
    scale = 1.0 / math.sqrt(HEAD_DIM)

    def body(q_ref, k_ref, v_ref, o_ref, dm_ref, g_ref, c_ref, dq_ref, dk_ref, dv_ref, dg_ref, dk_acc, dv_acc):
        qi = pl.program_id(2)

        @pl.when(qi == 0)
        def _():
            dk_acc[...] = jnp.zeros_like(dk_acc)
            dv_acc[...] = jnp.zeros_like(dv_acc)

        lo = _lane_lo(tb)
        row = lax.broadcasted_iota(jnp.int32, (tb, tb), 0)
        col = lax.broadcasted_iota(jnp.int32, (tb, tb), 1)
        rev_incl = (row >= col).astype(BF16)
        fwd_incl = (row <= col).astype(BF16)
        lane = lax.broadcasted_iota(jnp.int32, (tb, nq), 1)
        below = lax.broadcasted_iota(jnp.int32, (1, nq), 1) < qi
        qm, dom, cars_, seen = [], [], [], None
        for p in range(hs // 2):
            cols = slice(p * LANES, (p + 1) * LANES)
            qm += _split_heads((q_ref[0, :, cols].astype(F32) * scale).astype(BF16), lo)
            do, dg = _rms_bwd(o_ref[0, :, cols], dm_ref[0, :, cols], g_ref[:, cols], lo)
            dg_ref[0, 0, :, cols] = dg
            dom += _split_heads(do.astype(BF16), lo)
        for h in range(hs):
            cars_.append(c_ref[0, h])
            visited = jnp.logical_and(jnp.max(cars_[h], axis=0, keepdims=True) > SB_SKIP_LOG, below)
            n = jnp.sum(visited.astype(jnp.int32), axis=1, keepdims=True)
            seen = n if seen is None else jnp.maximum(seen, n)
        first = qi - jnp.max(seen)

        def block(h, kb, gsum, dq, mask):
            ks = pl.ds(pl.multiple_of(kb * tb + CA_PAD, tb), tb)
            ko = pl.ds(pl.multiple_of(kb * tb, tb), tb)
            kk = k_ref[0, ks, _pair_cols(h)]
            vv = v_ref[0, ks, _pair_cols(h)]
            z = _dot(qm[h], kk, NT)
            lk = _log_keep(z)
            if mask is not None:
                lk = jnp.where(mask, lk, 0.0)
            carry = jnp.sum(jnp.where(lane == kb, cars_[h], 0.0), axis=1, keepdims=True)
            logw = z + _split_dot(lk, rev_incl) + carry
            if mask is not None:
                logw = jnp.where(mask, logw, NEG)
            w = jnp.exp(logw)
            gw = w * _dot(dom[h], vv, NT)
            gcum = _split_dot(gw, fwd_incl) + gsum
            dz = gw - jax.nn.sigmoid(z) * gcum
            if mask is not None:
                dz = jnp.where(mask, dz, 0.0)
            dzb = dz.astype(BF16)
            dk_acc[ko, _pair_cols(h)] += _dot(dzb, qm[h], TN)
            dv_acc[ko, _pair_cols(h)] += _dot(w.astype(BF16), dom[h], TN)
            return gcum[:, tb - 1:tb], dq + _dot(dzb, kk, NN)

        def step(kb, st):
            out = [block(h, kb, st[0][h], st[1][h], None) for h in range(hs)]
            return [o_[0] for o_ in out], [o_[1] for o_ in out]

        init = ([jnp.zeros((tb, 1), F32)] * hs, [jnp.zeros((tb, LANES), F32)] * hs)
        gsum, dq = lax.fori_loop(first, qi, step, init)
        dq = [block(h, qi, gsum[h], dq[h], col < row)[1] for h in range(hs)]
        for p, d in enumerate(_merge_pairs(dq, lo)):
            dq_ref[0, :, p * LANES:(p + 1) * LANES] = (d * scale).astype(BF16)

        @pl.when(qi == nq - 1)
        def _():
            dk_ref[0] = dk_acc[...].astype(BF16)
            dv_ref[0] = dv_acc[...].astype(BF16)

    qspec = pl.BlockSpec((1, tb, W), lambda g, b, i: (b, i + off, g))
    ospec = pl.BlockSpec((1, tb, W), lambda g, b, i: (b, i, g))
    kvout = pl.BlockSpec((1, S, W), lambda g, b, i: (b, 0, g))
    return pl.pallas_call(
        body, grid=(ngrp, B, nq), name="sb_bwd",
        in_specs=[qspec, pl.BlockSpec((1, CA_PAD + S, W), lambda g, b, i: (b, 0, ngrp + g)),
                  pl.BlockSpec((1, CA_PAD + S, W), lambda g, b, i: (b, 0, 2 * ngrp + g)),
                  ospec, ospec, pl.BlockSpec((1, W), lambda g, b, i: (0, g)),
                  pl.BlockSpec((1, hs, tb, nq), lambda g, b, i: (b, g, i, 0))],
        out_specs=[ospec, kvout, kvout, pl.BlockSpec((1, 1, 1, W), lambda g, b, i: (b, i, 0, g))],
        out_shape=[_sds((B, S, HD), BF16), _sds((B, S, HD), BF16), _sds((B, S, HD), BF16), _sds((B, nq, 1, HD), F32)],
        scratch_shapes=[pltpu.VMEM((S, W), F32), pltpu.VMEM((S, W), F32)],
        compiler_params=_params("parallel", "parallel", "arbitrary"),
    )(h3p, h3p, h3p, o, dmix, gain, cars)


def _ca_rel_index():
    width = CA_WIN + CA_TQ
    c = np.arange(width)
    dj = np.where(c < CA_WIN, c, c - width)
    return np.clip(CA_PAD - dj, -MAX_REL, MAX_REL) + MAX_REL, width


def _ca_onehot():
    idx, _ = _ca_rel_index()
    return (idx[:, None] == np.arange(2 * MAX_REL + 1)[None, :]).astype(np.float32)


def ca_bias(rel_bias):
    _, width = _ca_rel_index()
    lead = rel_bias.shape[:-1]
    by_offset = jnp.dot(rel_bias, jnp.asarray(_ca_onehot().T), precision=lax.Precision.HIGHEST)
    tile = jnp.broadcast_to(by_offset[..., None, :], lead + (CA_TQ, width)).reshape(lead + (CA_TQ * width,))
    tile = tile[..., :CA_TQ * (width - 1)].reshape(lead + (CA_TQ, width - 1))[..., :CA_WIN]
    t = np.arange(CA_TQ)[:, None] // CHUNK * CHUNK
    j = np.arange(CA_WIN)[None, :]
    return jnp.where((j >= t) & (j < t + CA_PAD + CHUNK), tile, NEG)


def rel_bias_grad(db):
    H = db.shape[0]
    _, width = _ca_rel_index()
    x = jnp.pad(db, ((0, 0), (0, 0), (0, width - 1 - CA_WIN))).reshape(H, CA_TQ * (width - 1))
    x = jnp.pad(x, ((0, 0), (0, CA_TQ))).reshape(H, CA_TQ, width).sum(axis=1)
    return jnp.dot(x, jnp.asarray(_ca_onehot()), precision=lax.Precision.HIGHEST)


def _ca_probs(qm_h, kk, bias_h, valid):
    s = jnp.where(valid, _dot(qm_h, kk, NT) + bias_h, NEG)
    e = jnp.exp(s - jnp.max(s, axis=-1, keepdims=True))
    return e * (1.0 / jnp.sum(e, axis=-1, keepdims=True))


def ca_fwd(h3p, bias, gain):
    B, S, HD, hs, W, ngrp = _attn_dims(h3p, gain, FWD_HEADS)
    scale = 1.0 / math.sqrt(HEAD_DIM)
    off = CA_PAD // CA_TQ

    def body(q_ref, k_ref, v_ref, b_ref, g_ref, o_ref, m_ref):
        q0 = pl.program_id(2) * CA_TQ
        ks = pl.ds(pl.multiple_of(q0, CA_TQ), CA_WIN)
        lo = _lane_lo(CA_TQ)
        valid = lax.broadcasted_iota(jnp.int32, (CA_TQ, CA_WIN), 1) + q0 >= CA_PAD
        pv = []
        for p in range(hs // 2):
            cols = slice(p * LANES, (p + 1) * LANES)
            qm = _split_heads((q_ref[0, :, cols].astype(F32) * scale).astype(BF16), lo)
            kk = k_ref[0, ks, cols]
            vv = v_ref[0, ks, cols]
            for h in (2 * p, 2 * p + 1):
                prob = _ca_probs(qm[h % 2], kk, b_ref[h], valid)
                pv.append(_dot(prob.astype(BF16), vv, NN))
        for p, o in enumerate(_merge_pairs(pv, lo)):
            cols = slice(p * LANES, (p + 1) * LANES)
            o_ref[0, :, cols] = o
            m_ref[0, :, cols] = _rms_fwd(o, g_ref[:, cols], lo)

    ospec = pl.BlockSpec((1, CA_TQ, W), lambda g, b, i: (b, i, g))
    return pl.pallas_call(
        body, grid=(ngrp, B, S // CA_TQ), name="ca_fwd",
        in_specs=[pl.BlockSpec((1, CA_TQ, W), lambda g, b, i: (b, i + off, 3 * ngrp + g)),
                  pl.BlockSpec((1, CA_PAD + S, W), lambda g, b, i: (b, 0, 4 * ngrp + g)),
                  pl.BlockSpec((1, CA_PAD + S, W), lambda g, b, i: (b, 0, 5 * ngrp + g)),
                  pl.BlockSpec((hs, CA_TQ, CA_WIN), lambda g, b, i: (g, 0, 0)),
                  pl.BlockSpec((1, W), lambda g, b, i: (0, g))],
        out_specs=[ospec, ospec], out_shape=[_sds((B, S, HD), F32), _sds((B, S, HD), BF16)],
        compiler_params=_params("parallel", "parallel", "arbitrary"),
    )(h3p, h3p, h3p, bias, gain)


def ca_bwd(h3p, bias, o, dmix, gain):
    B, S, HD, hs, W, ngrp = _attn_dims(h3p, gain, BWD_HEADS)
    scale = 1.0 / math.sqrt(HEAD_DIM)
    nq = S // CA_TQ
    off = CA_PAD // CA_TQ

    def body(q_ref, k_ref, v_ref, b_ref, o_ref, dm_ref, g_ref, dq_ref, dk_ref, dv_ref, dg_ref, db_ref, dk_acc, dv_acc):
        bi = pl.program_id(1)
        qi = pl.program_id(2)

        @pl.when(qi == 0)
        def _():
            dk_acc[...] = jnp.zeros_like(dk_acc)
            dv_acc[...] = jnp.zeros_like(dv_acc)

        @pl.when(jnp.logical_and(qi == 0, bi == 0))
        def _():
            db_ref[...] = jnp.zeros_like(db_ref)

        q0 = qi * CA_TQ
        ks = pl.ds(pl.multiple_of(q0, CA_TQ), CA_WIN)
        lo = _lane_lo(CA_TQ)
        valid = lax.broadcasted_iota(jnp.int32, (CA_TQ, CA_WIN), 1) + q0 >= CA_PAD
        dq = []
        for p in range(hs // 2):
            cols = slice(p * LANES, (p + 1) * LANES)
            qm = _split_heads((q_ref[0, :, cols].astype(F32) * scale).astype(BF16), lo)
            do, dg = _rms_bwd(o_ref[0, :, cols], dm_ref[0, :, cols], g_ref[:, cols], lo)
            dg_ref[0, 0, :, cols] = dg
            dom = _split_heads(do.astype(BF16), lo)
            kk = k_ref[0, ks, cols]
            vv = v_ref[0, ks, cols]
            for h in (2 * p, 2 * p + 1):
                prob = _ca_probs(qm[h % 2], kk, b_ref[h], valid)
                dp = _dot(dom[h % 2], vv, NT)
                ds = prob * (dp - jnp.sum(prob * dp, axis=-1, keepdims=True))
                db_ref[h] += ds
                dsb = ds.astype(BF16)
                dq.append(_dot(dsb, kk, NN))
                dk_acc[ks, cols] += _dot(dsb, qm[h % 2], TN)
                dv_acc[ks, cols] += _dot(prob.astype(BF16), dom[h % 2], TN)
        for p, d in enumerate(_merge_pairs(dq, lo)):
            dq_ref[0, :, p * LANES:(p + 1) * LANES] = (d * scale).astype(BF16)

        @pl.when(qi == nq - 1)
        def _():
            dk_ref[0] = dk_acc[CA_PAD:, :].astype(BF16)
            dv_ref[0] = dv_acc[CA_PAD:, :].astype(BF16)

    ospec = pl.BlockSpec((1, CA_TQ, W), lambda g, b, i: (b, i, g))
    kvout = pl.BlockSpec((1, S, W), lambda g, b, i: (b, 0, g))
    bspec = pl.BlockSpec((hs, CA_TQ, CA_WIN), lambda g, b, i: (g, 0, 0))
    return pl.pallas_call(
        body, grid=(ngrp, B, nq), name="ca_bwd",
        in_specs=[pl.BlockSpec((1, CA_TQ, W), lambda g, b, i: (b, i + off, 3 * ngrp + g)),
                  pl.BlockSpec((1, CA_PAD + S, W), lambda g, b, i: (b, 0, 4 * ngrp + g)),
                  pl.BlockSpec((1, CA_PAD + S, W), lambda g, b, i: (b, 0, 5 * ngrp + g)),
                  bspec, ospec, pl.BlockSpec((1, CA_TQ, W), lambda g, b, i: (b, i, ngrp + g)),
                  pl.BlockSpec((1, W), lambda g, b, i: (0, g))],
        out_specs=[ospec, kvout, kvout, pl.BlockSpec((1, 1, 1, W), lambda g, b, i: (b, i, 0, g)), bspec],
        out_shape=[_sds((B, S, HD), BF16), _sds((B, S, HD), BF16), _sds((B, S, HD), BF16),
                   _sds((B, nq, 1, HD), F32), _sds((HD // HEAD_DIM, CA_TQ, CA_WIN), F32)],
        scratch_shapes=[pltpu.VMEM((CA_PAD + S, W), F32), pltpu.VMEM((CA_PAD + S, W), F32)],
        compiler_params=_params("parallel", "arbitrary", "arbitrary"),
    )(h3p, h3p, h3p, bias, o, dmix, gain)


_ANY = pl.BlockSpec(memory_space=pl.ANY)
_MESH = pl.DeviceIdType.MESH


def _mesh_pos():
    return lax.axis_index("x"), lax.axis_index("y"), lax.axis_index("c")


def _dev_index(p):
    return 4 * p[0] + 2 * p[1] + p[2]


def _flip(pos, k):
    return tuple(1 - v if (k >> (2 - a)) & 1 else v for a, v in enumerate(pos))


def gather_weights(mats):
    nm = len(mats)
    rows = [m.shape[0] for m in mats]

    def body(*refs):
        ins = refs[:nm]
        outs = refs[nm:2 * nm]
        send_sems, recv_sems, local_sems = refs[2 * nm:]
        x, y, c = _mesh_pos()
        me, sibling = (x, y, c), (x, y, 1 - c)
        chips = [(1 - x, y), (x, 1 - y), (1 - x, 1 - y)]

        def block(m, p):
            start = pl.multiple_of(_dev_index(p) * rows[m], 16)
            return outs[m].at[pl.ds(start, rows[m]), :]

        def copy(k, m, blk, to, src=None):
            return pltpu.make_async_remote_copy(
                src_ref=block(m, blk) if src is None else src, dst_ref=block(m, blk),
                send_sem=send_sems.at[k, m], recv_sem=recv_sems.at[k, m], device_id=to, device_id_type=_MESH)

        mine = [pltpu.make_async_copy(ins[m], block(m, me), local_sems.at[m]) for m in range(nm)]
        for cp in mine:
            cp.start()
        first = [copy(0, m, me, sibling, src=ins[m]) for m in range(nm)]
        first += [copy(1 + j, m, me, (*chip, c), src=ins[m]) for j, chip in enumerate(chips) for m in range(nm)]
        for cp in first:
            cp.start()
        passed = []
        for j, chip in enumerate(chips):
            for m in range(nm):
                copy(1 + j, m, (*chip, c), me).wait_recv()
                fwd = copy(4 + j, m, (*chip, c), sibling)
                fwd.start()
                passed.append(fwd)
        for m in range(nm):
            copy(0, m, sibling, me).wait_recv()
        for j, chip in enumerate(chips):
            for m in range(nm):
                copy(4 + j, m, (*chip, 1 - c), me).wait_recv()
        for cp in first + passed:
            cp.wait_send()
        for cp in mine:
            cp.wait()

    return pl.pallas_call(
        body, name="gather_weights", in_specs=[_ANY] * nm, out_specs=[_ANY] * nm,
        out_shape=[_sds((N_DEV * m.shape[0], m.shape[1]), m.dtype) for m in mats],
        scratch_shapes=[pltpu.SemaphoreType.DMA((7, nm)), pltpu.SemaphoreType.DMA((7, nm)), pltpu.SemaphoreType.DMA((nm,))],
    )(*mats)


def exchange_grads(grads):
    nm = len(grads)
    rows = [g.shape[0] // N_DEV for g in grads]

    def body(*refs):
        ins = refs[:nm]
        outs = refs[nm:2 * nm]
        send_sems, recv_sems, local_sems = refs[2 * nm:]
        me = _mesh_pos()
        my = _dev_index(me)

        def piece(m, idx):
            return ins[m].at[pl.ds(pl.multiple_of(idx * rows[m], 16), rows[m]), :]

        def copy(k, m, src_idx, slot, to):
            return pltpu.make_async_remote_copy(
                src_ref=piece(m, src_idx), dst_ref=outs[m].at[slot],
                send_sem=send_sems.at[k - 1, m], recv_sem=recv_sems.at[k - 1, m], device_id=to, device_id_type=_MESH)

        mine = [pltpu.make_async_copy(piece(m, my), outs[m].at[my], local_sems.at[m]) for m in range(nm)]
        for cp in mine:
            cp.start()
        sends = []
        for k in range(1, N_DEV):
            peer = _flip(me, k)
            for m in range(nm):
                cp = copy(k, m, _dev_index(peer), my, peer)
                cp.start()
                sends.append(cp)
        for k in range(1, N_DEV):
            peer = _flip(me, k)
            for m in range(nm):
                copy(k, m, my, _dev_index(peer), peer).wait_recv()
        for cp in sends:
            cp.wait_send()
        for cp in mine:
            cp.wait()

    return pl.pallas_call(
        body, name="exchange_grads", in_specs=[_ANY] * nm, out_specs=[_ANY] * nm,
        out_shape=[_sds((N_DEV, g.shape[0] // N_DEV, g.shape[1]), g.dtype) for g in grads],
        scratch_shapes=[pltpu.SemaphoreType.DMA((7, nm)), pltpu.SemaphoreType.DMA((7, nm)), pltpu.SemaphoreType.DMA((nm,))],
    )(*grads)


def sum_slots(r):
    n, R, D = r.shape
    tc = _row_tile(D, 256)

    def body(r_ref, o_ref):
        acc = r_ref[0].astype(F32)
        for s in range(1, n):
            acc = acc + r_ref[s].astype(F32)
        o_ref[...] = acc

    return pl.pallas_call(
        body, grid=(D // tc,), name="sum_slots", in_specs=[pl.BlockSpec((n, R, tc), lambda i: (0, 0, i))],
        out_specs=pl.BlockSpec((R, tc), lambda i: (0, i)), out_shape=_sds((R, D), F32),
        compiler_params=_params("parallel"),
    )(r)


def allreduce_small(v):
    R, C = v.shape

    def body(v_ref, o_ref, buf, send_sems, recv_sems):
        me = _mesh_pos()
        my = _dev_index(me)
        buf[my] = v_ref[...]

        def copy(k, slot, to):
            return pltpu.make_async_remote_copy(
                src_ref=v_ref, dst_ref=buf.at[slot], send_sem=send_sems.at[k - 1], recv_sem=recv_sems.at[k - 1],
                device_id=to, device_id_type=_MESH)

        sends = []
        for k in range(1, N_DEV):
            cp = copy(k, my, _flip(me, k))
            cp.start()
            sends.append(cp)
        for k in range(1, N_DEV):
            peer = _flip(me, k)
            copy(k, _dev_index(peer), peer).wait_recv()
        acc = buf[0]
        for s in range(1, N_DEV):
            acc = acc + buf[s]
        o_ref[...] = acc
        for cp in sends:
            cp.wait_send()

    vm = pl.BlockSpec(memory_space=pltpu.VMEM)
    return pl.pallas_call(
        body, name="allreduce_small", in_specs=[vm], out_specs=vm, out_shape=_sds((R, C), F32),
        scratch_shapes=[pltpu.VMEM((N_DEV, R, C), F32), pltpu.SemaphoreType.DMA((7,)), pltpu.SemaphoreType.DMA((7,))],
    )(v)


WEIGHTS = ["ffn1_w_gate", "ffn1_w_up", "ffn1_w_down", "ln1_g", "ln1_b", "w_in", "rel_bias", "sb_out_g", "ca_out_g",
           "w_out", "ln2_g", "ln2_b", "ffn2_w_gate", "ffn2_w_up", "ffn2_w_down", "ln3_g", "ln3_b"]
BIG = ["ffn1_w_gate", "ffn1_w_up", "ffn2_w_gate", "ffn2_w_up", "w_in", "ffn1_w_down", "ffn2_w_down", "w_out"]
TRANSPOSED = BIG[:5]
SMALL = [n for n in WEIGHTS if n not in BIG]


def _pack(vals):
    flat = jnp.concatenate([v.reshape(-1).astype(F32) for v in vals])
    pad = -flat.shape[0] % (8 * 128)
    return jnp.pad(flat, (0, pad)).reshape(-1, 128)


def _unpack(packed, like):
    flat = packed.reshape(-1)
    out, off = [], 0
    for v in like:
        out.append(flat[off:off + v.size].reshape(v.shape))
        off += v.size
    return out


def _row_blocks(w, n):
    r = w.shape[0] // n
    return [w[i * r:(i + 1) * r] for i in range(n)]


def _layer_fwd(x, xb, W, P, bias, l, B, S):
    T = B * S
    sv = {"xb": xb, "bias": bias}
    sv["hg1"], sv["hu1"], sv["a1"] = ffn_up(xb, W["ffn1_w_gate"], W["ffn1_w_up"])
    y = mm_nn([(sv["a1"], W["ffn1_w_down"])], name="ffn_down")
    sv["u1"], x1, sv["x1b"] = ln_fwd(x, y, P["ln1_g"][l], P["ln1_b"][l], FFN_RESIDUAL)

    sv["h"] = proj_in(sv["x1b"], W["w_in"], B, S)
    sv["gA"] = P["sb_out_g"][l].reshape(1, -1)
    sv["gB"] = P["ca_out_g"][l].reshape(1, -1)
    sv["oa"], ma, sv["cars"] = sb_fwd(sv["h"], sv["gA"])
    sv["ob"], mb = ca_fwd(sv["h"], bias, sv["gB"])
    sv["ma"], sv["mb"] = ma.reshape(T, -1), mb.reshape(T, -1)
    y = mm_nn(list(zip([sv["ma"], sv["mb"]], _row_blocks(W["w_out"], 2))), name="proj_out")
    sv["u2"], x2, sv["x2b"] = ln_fwd(x1, y, P["ln2_g"][l], P["ln2_b"][l], 1.0)

    sv["hg2"], sv["hu2"], sv["a2"] = ffn_up(sv["x2b"], W["ffn2_w_gate"], W["ffn2_w_up"])
    y = mm_nn([(sv["a2"], W["ffn2_w_down"])], name="ffn_down")
    sv["u3"], x3, x3b = ln_fwd(x2, y, P["ln3_g"][l], P["ln3_b"][l], FFN_RESIDUAL)
    return x3, x3b, sv


def _ffn_bwd(dr, dxa, xb, hg, hu, a, wg, wu, wd):
    dhg, dhu = ffn_bwd_mid(dr, wd, hg, hu)
    dx = mm_nn([(dhg, wg), (dhu, wu)], add=dxa, name="ffn_dx")
    return dx, mm_tn(dhg, xb, name="ffn_dw"), mm_tn(dhu, xb, name="ffn_dw"), mm_tn(a, dr, name="ffn_dw")


def _layer_bwd(dx, sv, W, P, l, B, S):
    T = B * S
    G = {}
    dxa, dr, dg, db = ln_bwd(sv["u3"], dx, P["ln3_g"][l], FFN_RESIDUAL)
    G["ln3_g"], G["ln3_b"] = dg.sum(0), db.sum(0)
    dx, G["ffn2_w_gate"], G["ffn2_w_up"], G["ffn2_w_down"] = _ffn_bwd(
        dr, dxa, sv["x2b"], sv["hg2"], sv["hu2"], sv["a2"], W["ffn2_w_gate"], W["ffn2_w_up"], W["ffn2_w_down"])

    dxa, dr, dg, db = ln_bwd(sv["u2"], dx, P["ln2_g"][l], 1.0)
    G["ln2_g"], G["ln2_b"] = dg.sum(0), db.sum(0)
    dmix = mm_nt(dr, W["w_out"], name="proj_out_dx").reshape(B, S, -1)
    G["w_out"] = jnp.concatenate([mm_tn(sv["ma"], dr, name="proj_out_dw"), mm_tn(sv["mb"], dr, name="proj_out_dw")])
    dqa, dka, dva, dga = sb_bwd(sv["h"], sv["oa"], dmix, sv["gA"], sv["cars"])
    dqb, dkb, dvb, dgb, dbias = ca_bwd(sv["h"], sv["bias"], sv["ob"], dmix, sv["gB"])
    G["sb_out_g"] = dga.sum((0, 1, 2))
    G["ca_out_g"] = dgb.sum((0, 1, 2))
    G["rel_bias"] = rel_bias_grad(dbias)
    dh = [t.reshape(T, -1) for t in (dqa, dka, dva, dqb, dkb, dvb)]
    dx = mm_nn(list(zip(dh, _row_blocks(W["w_in"], 6))), add=dxa, name="proj_in_dx")
    G["w_in"] = jnp.concatenate([mm_tn(t, sv["x1b"], name="proj_in_dw") for t in dh])

    dxa, dr, dg, db = ln_bwd(sv["u1"], dx, P["ln1_g"][l], FFN_RESIDUAL)
    G["ln1_g"], G["ln1_b"] = dg.sum(0), db.sum(0)
    dx, G["ffn1_w_gate"], G["ffn1_w_up"], G["ffn1_w_down"] = _ffn_bwd(
        dr, dxa, sv["xb"], sv["hg1"], sv["hu1"], sv["a1"], W["ffn1_w_gate"], W["ffn1_w_up"], W["ffn1_w_down"])
    return dx, G


def kernel(x, ffn1_w_gate, ffn1_w_up, ffn1_w_down, ln1_g, ln1_b, w_in, rel_bias, sb_out_g, ca_out_g, w_out, ln2_g, ln2_b, ffn2_w_gate, ffn2_w_up, ffn2_w_down, ln3_g, ln3_b, loss_target, m_ffn1_w_gate, m_ffn1_w_up, m_ffn1_w_down, m_ln1_g, m_ln1_b, m_w_in, m_rel_bias, m_sb_out_g, m_ca_out_g, m_w_out, m_ln2_g, m_ln2_b, m_ffn2_w_gate, m_ffn2_w_up, m_ffn2_w_down, m_ln3_g, m_ln3_b, v_ffn1_w_gate, v_ffn1_w_up, v_ffn1_w_down, v_ln1_g, v_ln1_b, v_w_in, v_rel_bias, v_sb_out_g, v_ca_out_g, v_w_out, v_ln2_g, v_ln2_b, v_ffn2_w_gate, v_ffn2_w_up, v_ffn2_w_down, v_ln3_g, v_ln3_b):
    given = dict(locals())
    P = {n: given[n] for n in WEIGHTS}
    M = {n: given["m_" + n] for n in WEIGHTS}
    V = {n: given["v_" + n] for n in WEIGHTS}
    B, S, D = x.shape
    L = ln1_g.shape[0]

    local = {n: (jnp.swapaxes(P[n], 1, 2) if n in TRANSPOSED else P[n]).astype(BF16) for n in BIG}
    W = []
    for l in range(L):
        full = gather_weights([local[n][l] for n in BIG])
        W.append(dict(zip(BIG, full)))
    bias = ca_bias(rel_bias)

    xf = x.reshape(B * S, D)
    xb = xf.astype(BF16)
    saved = []
    for l in range(L):
        xf, xb, sv = _layer_fwd(xf, xb, W[l], P, bias[l], l, B, S)
        saved.append(sv)
    dx, loss_part = loss_head(xf, loss_target.reshape(B * S, D))

    big_g = {n: [None] * L for n in BIG}
    small_g = {n: [None] * L for n in SMALL}
    for l in reversed(range(L)):
        dx, G = _layer_bwd(dx, saved[l], W[l], P, l, B, S)
        got = exchange_grads([G[n] for n in BIG])
        for n, r in zip(BIG, got):
            big_g[n][l] = sum_slots(r)
        for n in SMALL:
            small_g[n][l] = G[n]

    grads = {}
    for n in BIG:
        g = jnp.stack(big_g[n])
        grads[n] = jnp.swapaxes(g, 1, 2) if n in TRANSPOSED else g
    small_like = [P[n] for n in SMALL]
    packed = _pack([jnp.stack(small_g[n]) for n in SMALL] + [loss_part.sum()])
    total = allreduce_small(packed)
    *small_vals, loss = _unpack(total, small_like + [jnp.zeros((), F32)])
    grads.update(dict(zip(SMALL, small_vals)))

    delta, new_m, new_v = {}, {}, {}
    for n in BIG:
        shape = P[n].shape
        two_d = lambda a: a.reshape(shape[0] * shape[1], shape[2])
        d, m, v = adamw(two_d(P[n]), two_d(grads[n]), two_d(M[n]), two_d(V[n]))
        delta[n], new_m[n], new_v[n] = d.reshape(shape), m.reshape(shape), v.reshape(shape)
    one = jnp.ones((), F32)
    d, m, v = adamw(_pack(small_like + [one]), total, _pack([M[n] for n in SMALL] + [one]), _pack([V[n] for n in SMALL] + [one]))
    for dst, src in ((delta, d), (new_m, m), (new_v, v)):
        dst.update(dict(zip(SMALL, _unpack(src, small_like))))

    return (loss, dx.reshape(B, S, D), *[grads[n] for n in WEIGHTS], *[delta[n] for n in WEIGHTS],
            *[new_m[n] for n in WEIGHTS], *[new_v[n] for n in WEIGHTS])
```

```python
import math

import jax
import jax.numpy as jnp
import numpy as np
from jax import lax
from jax.experimental import pallas as pl
from jax.experimental.pallas import tpu as pltpu

F32 = jnp.float32
BF16 = jnp.bfloat16

HEAD_DIM = 64
CHUNK = 64
N_PREV_CHUNKS = 8
MAX_REL = 128
DEPTH = 4
FFN_RESIDUAL = 0.5
ALPHA = (2 * DEPTH) ** 0.25
LN_EPS = 1e-5
RMS_EPS = 1e-6
ADAM_LR = 0.001
ADAM_B1 = 0.9
ADAM_B2 = 0.999
ADAM_EPS = 1e-08
ADAM_WD = 0.01
ADAM_STEP = 10

N_DEV = 8
LANES = 128
VMEM_LIMIT_BYTES = 56 * 1024 * 1024
ROW_TILE = 512
SB_BLOCK = 128
CA_PAD = CHUNK * N_PREV_CHUNKS
CA_TQ = 2 * CHUNK
CA_WIN = CA_PAD + CA_TQ
FWD_HEADS = 8
SB_BWD_HEADS = 8
CA_BWD_HEADS = 4
NEG = -1e30
SB_SKIP_LOG = -104.0

NN = ((1,), (0,))
NT = ((1,), (1,))
TN = ((0,), (0,))


def _dot(a, b, dims):
    return lax.dot_general(a, b, (dims, ((), ())), preferred_element_type=F32)


def _params(*sem):
    return pltpu.CompilerParams(dimension_semantics=sem, vmem_limit_bytes=VMEM_LIMIT_BYTES)


def _sds(shape, dtype):
    return jax.ShapeDtypeStruct(shape, dtype)


def _row_tile(n, want=ROW_TILE):
    t = min(want, n)
    while n % t:
        t //= 2
    assert t >= 8, (n, want)
    return t


def ffn_up(xb, wg, wu):
    T, D = xb.shape
    F = wg.shape[0]
    tm = _row_tile(T)
    fc = _row_tile(F, 256)

    def body(x_ref, wg_ref, wu_ref, hg_ref, hu_ref, a_ref):
        x = x_ref[...]
        for j in range(F // fc):
            sl = pl.ds(j * fc, fc)
            hg = _dot(x, wg_ref[sl, :], NT)
            hu = _dot(x, wu_ref[sl, :], NT)
            hg_ref[:, sl] = hg.astype(BF16)
            hu_ref[:, sl] = hu.astype(BF16)
            a_ref[:, sl] = (hg * jax.nn.sigmoid(hg) * hu).astype(BF16)

    row = pl.BlockSpec((tm, F), lambda i: (i, 0))
    w = pl.BlockSpec((F, D), lambda i: (0, 0))
    return pl.pallas_call(
        body, grid=(T // tm,), name="ffn_up",
        in_specs=[pl.BlockSpec((tm, D), lambda i: (i, 0)), w, w],
        out_specs=[row, row, row], out_shape=[_sds((T, F), BF16)] * 3,
        compiler_params=_params("parallel"),
    )(xb, wg, wu)


def ffn_bwd_mid(dr, wd, hg, hu):
    T, D = dr.shape
    F = wd.shape[0]
    tm = _row_tile(T)
    fc = _row_tile(F, 256)

    def body(dr_ref, wd_ref, hg_ref, hu_ref, dhg_ref, dhu_ref):
        dr_ = dr_ref[...]
        for j in range(F // fc):
            sl = pl.ds(j * fc, fc)
            da = _dot(dr_, wd_ref[sl, :], NT)
            g = hg_ref[:, sl].astype(F32)
            u = hu_ref[:, sl].astype(F32)
            s = jax.nn.sigmoid(g)
            gs = g * s
            dhu_ref[:, sl] = (da * gs).astype(BF16)
            dhg_ref[:, sl] = (da * u * (s + gs * (1.0 - s))).astype(BF16)

    row = pl.BlockSpec((tm, F), lambda i: (i, 0))
    return pl.pallas_call(
        body, grid=(T // tm,), name="ffn_bwd_mid",
        in_specs=[pl.BlockSpec((tm, D), lambda i: (i, 0)), pl.BlockSpec((F, D), lambda i: (0, 0)), row, row],
        out_specs=[row, row], out_shape=[_sds((T, F), BF16)] * 2,
        compiler_params=_params("parallel"),
    )(dr, wd, hg, hu)


def mm_nn(pairs, add=None, out_dtype=F32, name="mm_nn"):
    M = pairs[0][0].shape[0]
    N = pairs[0][1].shape[1]
    tm = _row_tile(M)
    nc = _row_tile(N, 512)
    n_pairs = len(pairs)

    def body(*refs):
        a_refs = refs[:n_pairs]
        b_refs = refs[n_pairs:2 * n_pairs]
        add_ref = refs[2 * n_pairs] if add is not None else None
        o_ref = refs[-1]
        for j in range(N // nc):
            sl = pl.ds(j * nc, nc)
            acc = _dot(a_refs[0][...], b_refs[0][:, sl], NN)
            for a_ref, b_ref in zip(a_refs[1:], b_refs[1:]):
                acc = acc + _dot(a_ref[...], b_ref[:, sl], NN)
            if add_ref is not None:
                acc = acc + add_ref[:, sl]
            o_ref[:, sl] = acc.astype(out_dtype)

    in_specs = [pl.BlockSpec((tm, a.shape[1]), lambda i: (i, 0)) for a, _ in pairs]
    in_specs += [pl.BlockSpec(b.shape, lambda i: (0, 0)) for _, b in pairs]
    args = [a for a, _ in pairs] + [b for _, b in pairs]
    if add is not None:
        in_specs.append(pl.BlockSpec((tm, N), lambda i: (i, 0)))
        args.append(add)
    return pl.pallas_call(
        body, grid=(M // tm,), name=name, in_specs=in_specs,
        out_specs=pl.BlockSpec((tm, N), lambda i: (i, 0)), out_shape=_sds((M, N), out_dtype),
        compiler_params=_params("parallel"),
    )(*args)


def mm_nt(a, b, out_dtype=F32, name="mm_nt"):
    M, K = a.shape
    N = b.shape[0]
    tm = _row_tile(M)
    nc = _row_tile(N, 512)

    def body(a_ref, b_ref, o_ref):
        a_ = a_ref[...]
        for j in range(N // nc):
            sl = pl.ds(j * nc, nc)
            o_ref[:, sl] = _dot(a_, b_ref[sl, :], NT).astype(out_dtype)

    return pl.pallas_call(
        body, grid=(M // tm,), name=name,
        in_specs=[pl.BlockSpec((tm, K), lambda i: (i, 0)), pl.BlockSpec((N, K), lambda i: (0, 0))],
        out_specs=pl.BlockSpec((tm, N), lambda i: (i, 0)), out_shape=_sds((M, N), out_dtype),
        compiler_params=_params("parallel"),
    )(a, b)


def proj_in(xb, w, B, S):
    D = xb.shape[1]
    N = w.shape[0]
    tm = CA_PAD
    nblk = S // tm
    nc = _row_tile(N, 512)
    assert S % tm == 0

    def body(x_ref, w_ref, o_ref):
        i = pl.program_id(1)

        @pl.when(i == 0)
        def _():
            o_ref[...] = jnp.zeros_like(o_ref)

        @pl.when(i > 0)
        def _():
            x = x_ref[...]
            for j in range(N // nc):
                sl = pl.ds(j * nc, nc)
                o_ref[0, :, sl] = _dot(x, w_ref[sl, :], NT).astype(BF16)

    return pl.pallas_call(
        body, grid=(B, nblk + 1), name="proj_in",
        in_specs=[pl.BlockSpec((tm, D), lambda b, i: (b * nblk + jnp.maximum(i - 1, 0), 0)),
                  pl.BlockSpec((N, D), lambda b, i: (0, 0))],
        out_specs=pl.BlockSpec((1, tm, N), lambda b, i: (b, i, 0)), out_shape=_sds((B, CA_PAD + S, N), BF16),
        compiler_params=_params("parallel", "arbitrary"),
    )(xb, w)


def mm_tn(a, b, name="mm_tn"):
    T, M = a.shape
    N = b.shape[1]
    tk = _row_tile(T)
    mc = _row_tile(M, 256)
    steps = T // tk

    def body(a_ref, b_ref, o_ref, acc_ref):
        i = pl.program_id(0)

        @pl.when(i == 0)
        def _():
            acc_ref[...] = jnp.zeros_like(acc_ref)

        b_ = b_ref[...]
        for j in range(M // mc):
            sl = pl.ds(j * mc, mc)
            acc_ref[sl, :] += _dot(a_ref[:, sl], b_, TN)

        @pl.when(i == steps - 1)
        def _():
            o_ref[...] = acc_ref[...].astype(BF16)

    return pl.pallas_call(
        body, grid=(steps,), name=name,
        in_specs=[pl.BlockSpec((tk, M), lambda i: (i, 0)), pl.BlockSpec((tk, N), lambda i: (i, 0))],
        out_specs=pl.BlockSpec((M, N), lambda i: (0, 0)), out_shape=_sds((M, N), BF16),
        scratch_shapes=[pltpu.VMEM((M, N), F32)],
        compiler_params=_params("arbitrary"),
    )(a, b)


def ln_fwd(x, r, g, b, res_scale):
    T, D = x.shape
    tm = _row_tile(T)

    def body(x_ref, r_ref, g_ref, b_ref, u_ref, y_ref, yb_ref):
        u = ALPHA * x_ref[...] + res_scale * r_ref[...]
        mu = jnp.mean(u, axis=-1, keepdims=True)
        xc = u - mu
        var = jnp.mean(xc * xc, axis=-1, keepdims=True)
        y = xc * lax.rsqrt(var + LN_EPS) * g_ref[...] + b_ref[...]
        u_ref[...] = u
        y_ref[...] = y
        yb_ref[...] = y.astype(BF16)

    row = pl.BlockSpec((tm, D), lambda i: (i, 0))
    vec = pl.BlockSpec((1, D), lambda i: (0, 0))
    return pl.pallas_call(
        body, grid=(T // tm,), name="ln_fwd", in_specs=[row, row, vec, vec],
        out_specs=[row, row, row], out_shape=[_sds((T, D), F32), _sds((T, D), F32), _sds((T, D), BF16)],
        compiler_params=_params("parallel"),
    )(x, r, g.reshape(1, D), b.reshape(1, D))


def ln_bwd(u, dy, g, branch_scale):
    T, D = u.shape
    tm = _row_tile(T)

    def body(u_ref, dy_ref, g_ref, dxa_ref, dr_ref, dg_ref, db_ref):
        @pl.when(pl.program_id(0) == 0)
        def _():
            dg_ref[...] = jnp.zeros_like(dg_ref)
            db_ref[...] = jnp.zeros_like(db_ref)

        u_ = u_ref[...]
        dy_ = dy_ref[...]
        mu = jnp.mean(u_, axis=-1, keepdims=True)
        xc = u_ - mu
        var = jnp.mean(xc * xc, axis=-1, keepdims=True)
        rstd = lax.rsqrt(var + LN_EPS)
        xh = xc * rstd
        dxh = dy_ * g_ref[...]
        m1 = jnp.mean(dxh, axis=-1, keepdims=True)
        m2 = jnp.mean(dxh * xh, axis=-1, keepdims=True)
        du = rstd * (dxh - m1 - xh * m2)
        dxa_ref[...] = ALPHA * du
        dr_ref[...] = (branch_scale * du).astype(BF16)
        dg_ref[...] += jnp.sum((dy_ * xh).reshape(tm // 8, 8, D), axis=0)
        db_ref[...] += jnp.sum(dy_.reshape(tm // 8, 8, D), axis=0)

    row = pl.BlockSpec((tm, D), lambda i: (i, 0))
    acc = pl.BlockSpec((8, D), lambda i: (0, 0))
    dxa, dr, dg, db = pl.pallas_call(
        body, grid=(T // tm,), name="ln_bwd", in_specs=[row, row, pl.BlockSpec((1, D), lambda i: (0, 0))],
        out_specs=[row, row, acc, acc],
        out_shape=[_sds((T, D), F32), _sds((T, D), BF16), _sds((8, D), F32), _sds((8, D), F32)],
        compiler_params=_params("arbitrary"),
    )(u, dy, g.reshape(1, D))
    return dxa, dr, dg, db


def loss_head(y, target):
    T, D = y.shape
    tm = _row_tile(T)

    def body(y_ref, t_ref, dy_ref, l_ref):
        @pl.when(pl.program_id(0) == 0)
        def _():
            l_ref[...] = jnp.zeros_like(l_ref)

        e = y_ref[...] - t_ref[...]
        dy_ref[...] = e * (1.0 / D)
        l_ref[...] += jnp.sum((e * e).reshape(tm // 8, 8, D), axis=0) * (0.5 / D)

    row = pl.BlockSpec((tm, D), lambda i: (i, 0))
    return pl.pallas_call(
        body, grid=(T // tm,), name="loss_head", in_specs=[row, row],
        out_specs=[row, pl.BlockSpec((8, D), lambda i: (0, 0))],
        out_shape=[_sds((T, D), F32), _sds((8, D), F32)],
        compiler_params=_params("arbitrary"),
    )(y, target)


def adamw(w, g, m, v):
    R, C = w.shape
    tr = R
    for cand in (512, 256, 128, 64, 32, 16, 8):
        if R % cand == 0:
            tr = cand
            break
    c1 = 1.0 - ADAM_B1 ** ADAM_STEP
    c2 = 1.0 - ADAM_B2 ** ADAM_STEP

    def body(w_ref, g_ref, m_ref, v_ref, d_ref, mo_ref, vo_ref):
        g_ = g_ref[...]
        m_ = ADAM_B1 * m_ref[...] + (1.0 - ADAM_B1) * g_
        v_ = ADAM_B2 * v_ref[...] + (1.0 - ADAM_B2) * (g_ * g_)
        m_hat = m_ / c1
        v_hat = v_ / c2
        d_ref[...] = -ADAM_LR * (m_hat / (jnp.sqrt(v_hat) + ADAM_EPS) + ADAM_WD * w_ref[...])
        mo_ref[...] = m_
        vo_ref[...] = v_

    blk = pl.BlockSpec((tr, C), lambda i: (i, 0))
    return pl.pallas_call(
        body, grid=(R // tr,), name="adamw", in_specs=[blk] * 4, out_specs=[blk] * 3,
        out_shape=[_sds((R, C), F32)] * 3, compiler_params=_params("parallel"),
    )(w, g, m, v)


def _lane_lo(rows):
    return lax.broadcasted_iota(jnp.int32, (rows, LANES), 1) < HEAD_DIM


def _pair_mean(x, lo):
    s0 = jnp.sum(jnp.where(lo, x, 0.0), axis=-1, keepdims=True)
    s1 = jnp.sum(jnp.where(lo, 0.0, x), axis=-1, keepdims=True)
    return jnp.where(lo, s0, s1) * (1.0 / HEAD_DIM)


def _rms_fwd(o, gain, lo):
    r = lax.rsqrt(_pair_mean(o * o, lo) + RMS_EPS)
    return (o * r * gain).astype(BF16)


def _rms_bwd(o, dm, gain, lo):
    r = lax.rsqrt(_pair_mean(o * o, lo) + RMS_EPS)
    oh = o * r
    dg = jnp.sum(dm * oh, axis=0, keepdims=True)
    doh = dm * gain
    do = r * (doh - oh * _pair_mean(doh * oh, lo))
    return do, dg


def _split_heads(x, lo):
    zero = jnp.zeros_like(x)
    return [jnp.where(lo, x, zero), jnp.where(lo, zero, x)]


def _merge_pairs(per_head, lo):
    return [jnp.where(lo, per_head[2 * p], per_head[2 * p + 1]) for p in range(len(per_head) // 2)]


def _pair_cols(h):
    return slice((h // 2) * LANES, (h // 2 + 1) * LANES)


def _split_dot(x, tri):
    n = x.shape[0]
    hi = x.astype(BF16)
    lo = (x - hi.astype(F32)).astype(BF16)
    both = _dot(jnp.concatenate([hi, lo], axis=0), tri, NN)
    return both[:n] + both[n:]


def _log_keep(z):
    return -(jnp.maximum(z, 0.0) + jnp.log(1.0 + jnp.exp(-jnp.abs(z))))


def _attn_dims(h3p, gain, heads):
    B, SP, C = h3p.shape
    HD = gain.shape[1]
    hs = min(heads, HD // HEAD_DIM)
    W = hs * HEAD_DIM
    assert C == 6 * HD and W % LANES == 0 and HD % W == 0
    return B, SP - CA_PAD, HD, hs, W, HD // W


def sb_fwd(h3p, gain):
    B, S, HD, hs, W, ngrp = _attn_dims(h3p, gain, FWD_HEADS)
    tb = min(SB_BLOCK, S)
    nq = S // tb
    off = CA_PAD // tb
    scale = 1.0 / math.sqrt(HEAD_DIM)

    def body(q_ref, k_ref, v_ref, g_ref, o_ref, m_ref, c_ref):
        qi = pl.program_id(2)
        lo = _lane_lo(tb)
        row = lax.broadcasted_iota(jnp.int32, (tb, tb), 0)
        col = lax.broadcasted_iota(jnp.int32, (tb, tb), 1)
        rev_incl = (row >= col).astype(BF16)
        lane = lax.broadcasted_iota(jnp.int32, (tb, nq), 1)
        qm = []
        for p in range(hs // 2):
            qm += _split_heads((q_ref[0, :, p * LANES:(p + 1) * LANES].astype(F32) * scale).astype(BF16), lo)

        def blocks(kb, carries_, mask):
            ks = pl.ds(pl.multiple_of(kb * tb + CA_PAD, tb), tb)
            hh = range(hs)
            zs = [_dot(qm[h], k_ref[0, ks, _pair_cols(h)], NT) for h in hh]
            lks = [_log_keep(z) for z in zs]
            if mask is not None:
                lks = [jnp.where(mask, lk, 0.0) for lk in lks]
            cums = [_split_dot(lk, rev_incl) for lk in lks]
            ws = []
            for h in hh:
                logw = zs[h] + cums[h] + carries_[h]
                if mask is not None:
                    logw = jnp.where(mask, logw, NEG)
                ws.append(jnp.exp(logw).astype(BF16))
            pvs = [_dot(ws[h], v_ref[0, ks, _pair_cols(h)], NN) for h in hh]
            return [(pvs[h], cums[h][:, 0:1]) for h in hh]

        diag = blocks(qi, [jnp.zeros((tb, 1), F32)] * hs, col < row)
        accs = _merge_pairs([d[0] for d in diag], lo)
        carries = [d[1] for d in diag]
        cars = [jnp.where(lane == qi, 0.0, NEG)] * hs

        def cond(st):
            kb, carries_, _, _ = st
            top = carries_[0]
            for c in carries_[1:]:
                top = jnp.maximum(top, c)
            return jnp.logical_and(kb >= 0, jnp.max(top) > SB_SKIP_LOG)

        def step(st):
            kb, carries_, accs_, cars_ = st
            out = blocks(kb, carries_, None)
            pv = _merge_pairs([o[0] for o in out], lo)
            return (kb - 1, [c + o[1] for c, o in zip(carries_, out)], [a + p for a, p in zip(accs_, pv)],
                    [jnp.where(lane == kb, c, cs) for c, cs in zip(carries_, cars_)])

        _, _, accs, cars = lax.while_loop(cond, step, (qi - 1, carries, accs, cars))
        for p, acc in enumerate(accs):
            cols = slice(p * LANES, (p + 1) * LANES)
            o_ref[0, :, cols] = acc
            m_ref[0, :, cols] = _rms_fwd(acc, g_ref[:, cols], lo)
        for h in range(hs):
            c_ref[0, h] = cars[h]

    qspec = pl.BlockSpec((1, tb, W), lambda g, b, i: (b, i + off, g))
    ospec = pl.BlockSpec((1, tb, W), lambda g, b, i: (b, i, g))
    return pl.pallas_call(
        body, grid=(ngrp, B, nq), name="sb_fwd",
        in_specs=[qspec, pl.BlockSpec((1, CA_PAD + S, W), lambda g, b, i: (b, 0, ngrp + g)),
                  pl.BlockSpec((1, CA_PAD + S, W), lambda g, b, i: (b, 0, 2 * ngrp + g)),
                  pl.BlockSpec((1, W), lambda g, b, i: (0, g))],
        out_specs=[ospec, ospec, pl.BlockSpec((1, hs, tb, nq), lambda g, b, i: (b, g, i, 0))],
        out_shape=[_sds((B, S, HD), F32), _sds((B, S, HD), BF16), _sds((B, HD // HEAD_DIM, S, nq), F32)],
        compiler_params=_params("parallel", "parallel", "arbitrary"),
    )(h3p, h3p, h3p, gain)


def sb_bwd(h3p, o, dmix, gain, cars):
    B, S, HD, hs, W, ngrp = _attn_dims(h3p, gain, SB_BWD_HEADS)
    tb = min(SB_BLOCK, S)
    nq = S // tb
    off = CA_PAD // tb
    scale = 1.0 / math.sqrt(HEAD_DIM)

    def body(q_ref, k_ref, v_ref, o_ref, dm_ref, g_ref, c_ref, dq_ref, dk_ref, dv_ref, dg_ref, dk_acc, dv_acc):
        qi = pl.program_id(2)

        @pl.when(qi == 0)
        def _():
            dk_acc[...] = jnp.zeros_like(dk_acc)
            dv_acc[...] = jnp.zeros_like(dv_acc)

        lo = _lane_lo(tb)
        row = lax.broadcasted_iota(jnp.int32, (tb, tb), 0)
        col = lax.broadcasted_iota(jnp.int32, (tb, tb), 1)
        rev_incl = (row >= col).astype(BF16)
        fwd_incl = (row <= col).astype(BF16)
        lane = lax.broadcasted_iota(jnp.int32, (tb, nq), 1)
        below = lax.broadcasted_iota(jnp.int32, (1, nq), 1) < qi
        qm, dom, cars_, seen = [], [], [], None
        for p in range(hs // 2):
            cols = slice(p * LANES, (p + 1) * LANES)
            qm += _split_heads((q_ref[0, :, cols].astype(F32) * scale).astype(BF16), lo)
            do, dg = _rms_bwd(o_ref[0, :, cols], dm_ref[0, :, cols], g_ref[:, cols], lo)
            dg_ref[0, 0, :, cols] = dg
            dom += _split_heads(do.astype(BF16), lo)
        for h in range(hs):
            cars_.append(c_ref[0, h])
            visited = jnp.logical_and(jnp.max(cars_[h], axis=0, keepdims=True) > SB_SKIP_LOG, below)
            n = jnp.sum(visited.astype(jnp.int32), axis=1, keepdims=True)
            seen = n if seen is None else jnp.maximum(seen, n)
        first = qi - jnp.max(seen)

        def blocks(kb, gsums, dqs, mask):
            ks = pl.ds(pl.multiple_of(kb * tb + CA_PAD, tb), tb)
            ko = pl.ds(pl.multiple_of(kb * tb, tb), tb)
            hh = range(hs)
            kk = [k_ref[0, ks, p * LANES:(p + 1) * LANES] for p in range(hs // 2)]
            vv = [v_ref[0, ks, p * LANES:(p + 1) * LANES] for p in range(hs // 2)]
            zs = [_dot(qm[h], kk[h // 2], NT) for h in hh]
            dws = [_dot(dom[h], vv[h // 2], NT) for h in hh]
            raw = [_log_keep(z) for z in zs]
            lks = raw if mask is None else [jnp.where(mask, lk, 0.0) for lk in raw]
            cums = [_split_dot(lk, rev_incl) for lk in lks]
            ws = []
            for h in hh:
                carry = jnp.sum(jnp.where(lane == kb, cars_[h], 0.0), axis=1, keepdims=True)
                logw = zs[h] + cums[h] + carry
                if mask is not None:
                    logw = jnp.where(mask, logw, NEG)
                ws.append(jnp.exp(logw))
            gws = [ws[h] * dws[h] for h in hh]
            gcums = [_split_dot(gws[h], fwd_incl) + gsums[h] for h in hh]
            dzb = []
            for h in hh:
                dz = gws[h] - jnp.exp(zs[h] + raw[h]) * gcums[h]
                if mask is not None:
                    dz = jnp.where(mask, dz, 0.0)
                dzb.append(dz.astype(BF16))
            wb = [w.astype(BF16) for w in ws]
            new_dq = [dqs[h] + _dot(dzb[h], kk[h // 2], NN) for h in hh]
            for p in range(hs // 2):
                cols = slice(p * LANES, (p + 1) * LANES)
                dk_acc[ko, cols] += _dot(dzb[2 * p], qm[2 * p], TN) + _dot(dzb[2 * p + 1], qm[2 * p + 1], TN)
                dv_acc[ko, cols] += _dot(wb[2 * p], dom[2 * p], TN) + _dot(wb[2 * p + 1], dom[2 * p + 1], TN)
            return [g[:, tb - 1:tb] for g in gcums], new_dq

        def step(kb, st):
            return blocks(kb, st[0], st[1], None)

        init = ([jnp.zeros((tb, 1), F32)] * hs, [jnp.zeros((tb, LANES), F32)] * hs)
        gsum, dq = lax.fori_loop(first, qi, step, init)
        _, dq = blocks(qi, gsum, dq, col < row)
        for p, d in enumerate(_merge_pairs(dq, lo)):
            dq_ref[0, :, p * LANES:(p + 1) * LANES] = (d * scale).astype(BF16)

        @pl.when(qi == nq - 1)
        def _():
            dk_ref[0] = dk_acc[...].astype(BF16)
            dv_ref[0] = dv_acc[...].astype(BF16)

    once = pl.Buffered(1)
    qspec = pl.BlockSpec((1, tb, W), lambda g, b, i: (b, i + off, g))
    ospec = pl.BlockSpec((1, tb, W), lambda g, b, i: (b, i, g))
    kvout = pl.BlockSpec((1, S, W), lambda g, b, i: (b, 0, g), pipeline_mode=once)
    return pl.pallas_call(
        body, grid=(ngrp, B, nq), name="sb_bwd",
        in_specs=[qspec, pl.BlockSpec((1, CA_PAD + S, W), lambda g, b, i: (b, 0, ngrp + g), pipeline_mode=once),
                  pl.BlockSpec((1, CA_PAD + S, W), lambda g, b, i: (b, 0, 2 * ngrp + g), pipeline_mode=once),
                  ospec, ospec, pl.BlockSpec((1, W), lambda g, b, i: (0, g)),
                  pl.BlockSpec((1, hs, tb, nq), lambda g, b, i: (b, g, i, 0))],
        out_specs=[ospec, kvout, kvout, pl.BlockSpec((1, 1, 1, W), lambda g, b, i: (b, i, 0, g))],
        out_shape=[_sds((B, S, HD), BF16), _sds((B, S, HD), BF16), _sds((B, S, HD), BF16), _sds((B, nq, 1, HD), F32)],
        scratch_shapes=[pltpu.VMEM((S, W), F32), pltpu.VMEM((S, W), F32)],
        compiler_params=_params("parallel", "parallel", "arbitrary"),
    )(h3p, h3p, h3p, o, dmix, gain, cars)


def _ca_rel_index():
    width = CA_WIN + CA_TQ
    c = np.arange(width)
    dj = np.where(c < CA_WIN, c, c - width)
    return np.clip(CA_PAD - dj, -MAX_REL, MAX_REL) + MAX_REL, width


def _ca_onehot():
    idx, _ = _ca_rel_index()
    return (idx[:, None] == np.arange(2 * MAX_REL + 1)[None, :]).astype(np.float32)


def ca_bias(rel_bias):
    _, width = _ca_rel_index()
    lead = rel_bias.shape[:-1]
    by_offset = jnp.dot(rel_bias, jnp.asarray(_ca_onehot().T), precision=lax.Precision.HIGHEST)
    tile = jnp.broadcast_to(by_offset[..., None, :], lead + (CA_TQ, width)).reshape(lead + (CA_TQ * width,))
    tile = tile[..., :CA_TQ * (width - 1)].reshape(lead + (CA_TQ, width - 1))[..., :CA_WIN]
    t = np.arange(CA_TQ)[:, None] // CHUNK * CHUNK
    j = np.arange(CA_WIN)[None, :]
    return jnp.where((j >= t) & (j < t + CA_PAD + CHUNK), tile, NEG)


def rel_bias_grad(db):
    H = db.shape[0]
    _, width = _ca_rel_index()
    x = jnp.pad(db, ((0, 0), (0, 0), (0, width - 1 - CA_WIN))).reshape(H, CA_TQ * (width - 1))
    x = jnp.pad(x, ((0, 0), (0, CA_TQ))).reshape(H, CA_TQ, width).sum(axis=1)
    return jnp.dot(x, jnp.asarray(_ca_onehot()), precision=lax.Precision.HIGHEST)


def _ca_scores(qm_h, kk, bias_h, valid):
    return jnp.where(valid, _dot(qm_h, kk, NT) + bias_h, NEG)


def _ca_softmax(s):
    e = jnp.exp(s - jnp.max(s, axis=-1, keepdims=True))
    return e * (1.0 / jnp.sum(e, axis=-1, keepdims=True))


def ca_fwd(h3p, bias, gain):
    B, S, HD, hs, W, ngrp = _attn_dims(h3p, gain, FWD_HEADS)
    scale = 1.0 / math.sqrt(HEAD_DIM)
    off = CA_PAD // CA_TQ

    def body(q_ref, k_ref, v_ref, b_ref, g_ref, o_ref, m_ref):
        q0 = pl.program_id(2) * CA_TQ
        ks = pl.ds(pl.multiple_of(q0, CA_TQ), CA_WIN)
        lo = _lane_lo(CA_TQ)
        valid = lax.broadcasted_iota(jnp.int32, (CA_TQ, CA_WIN), 1) + q0 >= CA_PAD
        qm, kk, vv = [], [], []
        for p in range(hs // 2):
            cols = slice(p * LANES, (p + 1) * LANES)
            qm += _split_heads((q_ref[0, :, cols].astype(F32) * scale).astype(BF16), lo)
            kk.append(k_ref[0, ks, cols])
            vv.append(v_ref[0, ks, cols])
        ss = [_ca_scores(qm[h], kk[h // 2], b_ref[h], valid) for h in range(hs)]
        ps = [_ca_softmax(s).astype(BF16) for s in ss]
        pv = [_dot(ps[h], vv[h // 2], NN) for h in range(hs)]
        for p, o in enumerate(_merge_pairs(pv, lo)):
            cols = slice(p * LANES, (p + 1) * LANES)
            o_ref[0, :, cols] = o
            m_ref[0, :, cols] = _rms_fwd(o, g_ref[:, cols], lo)

    ospec = pl.BlockSpec((1, CA_TQ, W), lambda g, b, i: (b, i, g))
    return pl.pallas_call(
        body, grid=(ngrp, B, S // CA_TQ), name="ca_fwd",
        in_specs=[pl.BlockSpec((1, CA_TQ, W), lambda g, b, i: (b, i + off, 3 * ngrp + g)),
                  pl.BlockSpec((1, CA_PAD + S, W), lambda g, b, i: (b, 0, 4 * ngrp + g)),
                  pl.BlockSpec((1, CA_PAD + S, W), lambda g, b, i: (b, 0, 5 * ngrp + g)),
                  pl.BlockSpec((hs, CA_TQ, CA_WIN), lambda g, b, i: (g, 0, 0)),
                  pl.BlockSpec((1, W), lambda g, b, i: (0, g))],
        out_specs=[ospec, ospec], out_shape=[_sds((B, S, HD), F32), _sds((B, S, HD), BF16)],
        compiler_params=_params("parallel", "parallel", "arbitrary"),
    )(h3p, h3p, h3p, bias, gain)


def ca_bwd(h3p, bias, o, dmix, gain):
    B, S, HD, hs, W, ngrp = _attn_dims(h3p, gain, CA_BWD_HEADS)
    scale = 1.0 / math.sqrt(HEAD_DIM)
    nq = S // CA_TQ
    off = CA_PAD // CA_TQ

    def body(q_ref, k_ref, v_ref, b_ref, o_ref, dm_ref, g_ref, dq_ref, dk_ref, dv_ref, dg_ref, db_ref, dk_acc, dv_acc):
        bi = pl.program_id(1)
        qi = pl.program_id(2)

        @pl.when(qi == 0)
        def _():
            dk_acc[...] = jnp.zeros_like(dk_acc)
            dv_acc[...] = jnp.zeros_like(dv_acc)

        @pl.when(jnp.logical_and(qi == 0, bi == 0))
        def _():
            db_ref[...] = jnp.zeros_like(db_ref)

        q0 = qi * CA_TQ
        ks = pl.ds(pl.multiple_of(q0, CA_TQ), CA_WIN)
        lo = _lane_lo(CA_TQ)
        valid = lax.broadcasted_iota(jnp.int32, (CA_TQ, CA_WIN), 1) + q0 >= CA_PAD
        qm, dom, kk, vv = [], [], [], []
        for p in range(hs // 2):
            cols = slice(p * LANES, (p + 1) * LANES)
            qm += _split_heads((q_ref[0, :, cols].astype(F32) * scale).astype(BF16), lo)
            do, dg = _rms_bwd(o_ref[0, :, cols], dm_ref[0, :, cols], g_ref[:, cols], lo)
            dg_ref[0, 0, :, cols] = dg
            dom += _split_heads(do.astype(BF16), lo)
            kk.append(k_ref[0, ks, cols])
            vv.append(v_ref[0, ks, cols])
        hh = range(hs)
        ss = [_ca_scores(qm[h], kk[h // 2], b_ref[h], valid) for h in hh]
        dps = [_dot(dom[h], vv[h // 2], NT) for h in hh]
        ps = [_ca_softmax(s) for s in ss]
        dss = [ps[h] * (dps[h] - jnp.sum(ps[h] * dps[h], axis=-1, keepdims=True)) for h in hh]
        for h in hh:
            db_ref[h] += dss[h]
        dsb = [d.astype(BF16) for d in dss]
        pb = [p_.astype(BF16) for p_ in ps]
        dq = [_dot(dsb[h], kk[h // 2], NN) for h in hh]
        for p in range(hs // 2):
            cols = slice(p * LANES, (p + 1) * LANES)
            dk_acc[ks, cols] += _dot(dsb[2 * p], qm[2 * p], TN) + _dot(dsb[2 * p + 1], qm[2 * p + 1], TN)
            dv_acc[ks, cols] += _dot(pb[2 * p], dom[2 * p], TN) + _dot(pb[2 * p + 1], dom[2 * p + 1], TN)
        for p, d in enumerate(_merge_pairs(dq, lo)):
            dq_ref[0, :, p * LANES:(p + 1) * LANES] = (d * scale).astype(BF16)

        @pl.when(qi == nq - 1)
        def _():
            dk_ref[0] = dk_acc[CA_PAD:, :].astype(BF16)
            dv_ref[0] = dv_acc[CA_PAD:, :].astype(BF16)

    ospec = pl.BlockSpec((1, CA_TQ, W), lambda g, b, i: (b, i, g))
    kvout = pl.BlockSpec((1, S, W), lambda g, b, i: (b, 0, g))
    bspec = pl.BlockSpec((hs, CA_TQ, CA_WIN), lambda g, b, i: (g, 0, 0))
    return pl.pallas_call(
        body, grid=(ngrp, B, nq), name="ca_bwd",
        in_specs=[pl.BlockSpec((1, CA_TQ, W), lambda g, b, i: (b, i + off, 3 * ngrp + g)),
                  pl.BlockSpec((1, CA_PAD + S, W), lambda g, b, i: (b, 0, 4 * ngrp + g)),
                  pl.BlockSpec((1, CA_PAD + S, W), lambda g, b, i: (b, 0, 5 * ngrp + g)),
                  bspec, ospec, pl.BlockSpec((1, CA_TQ, W), lambda g, b, i: (b, i, ngrp + g)),
                  pl.BlockSpec((1, W), lambda g, b, i: (0, g))],
        out_specs=[ospec, kvout, kvout, pl.BlockSpec((1, 1, 1, W), lambda g, b, i: (b, i, 0, g)), bspec],
        out_shape=[_sds((B, S, HD), BF16), _sds((B, S, HD), BF16), _sds((B, S, HD), BF16),
                   _sds((B, nq, 1, HD), F32), _sds((HD // HEAD_DIM, CA_TQ, CA_WIN), F32)],
        scratch_shapes=[pltpu.VMEM((CA_PAD + S, W), F32), pltpu.VMEM((CA_PAD + S, W), F32)],
        compiler_params=_params("parallel", "arbitrary", "arbitrary"),
    )(h3p, h3p, h3p, bias, o, dmix, gain)


_ANY = pl.BlockSpec(memory_space=pl.ANY)
_MESH = pl.DeviceIdType.MESH


def _mesh_pos():
    return lax.axis_index("x"), lax.axis_index("y"), lax.axis_index("c")


def _dev_index(p):
    return 4 * p[0] + 2 * p[1] + p[2]


def _flip(pos, k):
    return tuple(1 - v if (k >> (2 - a)) & 1 else v for a, v in enumerate(pos))


def gather_weights(mats):
    nm = len(mats)
    rows = [m.shape[0] for m in mats]

    def body(*refs):
        ins = refs[:nm]
        outs = refs[nm:2 * nm]
        send_sems, recv_sems, local_sems = refs[2 * nm:]
        x, y, c = _mesh_pos()
        me, sibling = (x, y, c), (x, y, 1 - c)
        chips = [(1 - x, y), (x, 1 - y), (1 - x, 1 - y)]

        def block(m, p):
            start = pl.multiple_of(_dev_index(p) * rows[m], 16)
            return outs[m].at[pl.ds(start, rows[m]), :]

        def copy(k, m, blk, to, src=None):
            return pltpu.make_async_remote_copy(
                src_ref=block(m, blk) if src is None else src, dst_ref=block(m, blk),
                send_sem=send_sems.at[k, m], recv_sem=recv_sems.at[k, m], device_id=to, device_id_type=_MESH)

        mine = [pltpu.make_async_copy(ins[m], block(m, me), local_sems.at[m]) for m in range(nm)]
        for cp in mine:
            cp.start()
        first = [copy(0, m, me, sibling, src=ins[m]) for m in range(nm)]
        first += [copy(1 + j, m, me, (*chip, c), src=ins[m]) for j, chip in enumerate(chips) for m in range(nm)]
        for cp in first:
            cp.start()
        passed = []
        for j, chip in enumerate(chips):
            for m in range(nm):
                copy(1 + j, m, (*chip, c), me).wait_recv()
                fwd = copy(4 + j, m, (*chip, c), sibling)
                fwd.start()
                passed.append(fwd)
        for m in range(nm):
            copy(0, m, sibling, me).wait_recv()
        for j, chip in enumerate(chips):
            for m in range(nm):
                copy(4 + j, m, (*chip, 1 - c), me).wait_recv()
        for cp in first + passed:
            cp.wait_send()
        for cp in mine:
            cp.wait()

    return pl.pallas_call(
        body, name="gather_weights", in_specs=[_ANY] * nm, out_specs=[_ANY] * nm,
        out_shape=[_sds((N_DEV * m.shape[0], m.shape[1]), m.dtype) for m in mats],
        scratch_shapes=[pltpu.SemaphoreType.DMA((7, nm)), pltpu.SemaphoreType.DMA((7, nm)), pltpu.SemaphoreType.DMA((nm,))],
    )(*mats)


def exchange_grads(grads):
    nm = len(grads)
    rows = [g.shape[0] // N_DEV for g in grads]

    def body(*refs):
        ins = refs[:nm]
        outs = refs[nm:2 * nm]
        send_sems, recv_sems, local_sems = refs[2 * nm:]
        me = _mesh_pos()
        my = _dev_index(me)

        def piece(m, idx):
            return ins[m].at[pl.ds(pl.multiple_of(idx * rows[m], 16), rows[m]), :]

        def copy(k, m, src_idx, slot, to):
            return pltpu.make_async_remote_copy(
                src_ref=piece(m, src_idx), dst_ref=outs[m].at[slot],
                send_sem=send_sems.at[k - 1, m], recv_sem=recv_sems.at[k - 1, m], device_id=to, device_id_type=_MESH)

        mine = [pltpu.make_async_copy(piece(m, my), outs[m].at[my], local_sems.at[m]) for m in range(nm)]
        for cp in mine:
            cp.start()
        sends = []
        for k in range(1, N_DEV):
            peer = _flip(me, k)
            for m in range(nm):
                cp = copy(k, m, _dev_index(peer), my, peer)
                cp.start()
                sends.append(cp)
        for k in range(1, N_DEV):
            peer = _flip(me, k)
            for m in range(nm):
                copy(k, m, my, _dev_index(peer), peer).wait_recv()
        for cp in sends:
            cp.wait_send()
        for cp in mine:
            cp.wait()

    return pl.pallas_call(
        body, name="exchange_grads", in_specs=[_ANY] * nm, out_specs=[_ANY] * nm,
        out_shape=[_sds((N_DEV, g.shape[0] // N_DEV, g.shape[1]), g.dtype) for g in grads],
        scratch_shapes=[pltpu.SemaphoreType.DMA((7, nm)), pltpu.SemaphoreType.DMA((7, nm)), pltpu.SemaphoreType.DMA((nm,))],
    )(*grads)


def sum_slots(r):
    n, R, D = r.shape
    tc = _row_tile(D, 256)

    def body(r_ref, o_ref):
        acc = r_ref[0].astype(F32)
        for s in range(1, n):
            acc = acc + r_ref[s].astype(F32)
        o_ref[...] = acc

    return pl.pallas_call(
        body, grid=(D // tc,), name="sum_slots", in_specs=[pl.BlockSpec((n, R, tc), lambda i: (0, 0, i))],
        out_specs=pl.BlockSpec((R, tc), lambda i: (0, i)), out_shape=_sds((R, D), F32),
        compiler_params=_params("parallel"),
    )(r)


def allreduce_small(v):
    R, C = v.shape

    def body(v_ref, o_ref, buf, send_sems, recv_sems):
        me = _mesh_pos()
        my = _dev_index(me)
        buf[my] = v_ref[...]

        def copy(k, slot, to):
            return pltpu.make_async_remote_copy(
                src_ref=v_ref, dst_ref=buf.at[slot], send_sem=send_sems.at[k - 1], recv_sem=recv_sems.at[k - 1],
                device_id=to, device_id_type=_MESH)

        sends = []
        for k in range(1, N_DEV):
            cp = copy(k, my, _flip(me, k))
            cp.start()
            sends.append(cp)
        for k in range(1, N_DEV):
            peer = _flip(me, k)
            copy(k, _dev_index(peer), peer).wait_recv()
        acc = buf[0]
        for s in range(1, N_DEV):
            acc = acc + buf[s]
        o_ref[...] = acc
        for cp in sends:
            cp.wait_send()

    vm = pl.BlockSpec(memory_space=pltpu.VMEM)
    return pl.pallas_call(
        body, name="allreduce_small", in_specs=[vm], out_specs=vm, out_shape=_sds((R, C), F32),
        scratch_shapes=[pltpu.VMEM((N_DEV, R, C), F32), pltpu.SemaphoreType.DMA((7,)), pltpu.SemaphoreType.DMA((7,))],
    )(v)


WEIGHTS = ["ffn1_w_gate", "ffn1_w_up", "ffn1_w_down", "ln1_g", "ln1_b", "w_in", "rel_bias", "sb_out_g", "ca_out_g",
           "w_out", "ln2_g", "ln2_b", "ffn2_w_gate", "ffn2_w_up", "ffn2_w_down", "ln3_g", "ln3_b"]
BIG = ["ffn1_w_gate", "ffn1_w_up", "ffn2_w_gate", "ffn2_w_up", "w_in", "ffn1_w_down", "ffn2_w_down", "w_out"]
TRANSPOSED = BIG[:5]
SMALL = [n for n in WEIGHTS if n not in BIG]


def _pack(vals):
    flat = jnp.concatenate([v.reshape(-1).astype(F32) for v in vals])
    pad = -flat.shape[0] % (8 * 128)
    return jnp.pad(flat, (0, pad)).reshape(-1, 128)


def _unpack(packed, like):
    flat = packed.reshape(-1)
    out, off = [], 0
    for v in like:
        out.append(flat[off:off + v.size].reshape(v.shape))
        off += v.size
    return out


def _row_blocks(w, n):
    r = w.shape[0] // n
    return [w[i * r:(i + 1) * r] for i in range(n)]


def _layer_fwd(x, xb, W, P, bias, l, B, S):
    T = B * S
    sv = {"xb": xb, "bias": bias}
    sv["hg1"], sv["hu1"], sv["a1"] = ffn_up(xb, W["ffn1_w_gate"], W["ffn1_w_up"])
    y = mm_nn([(sv["a1"], W["ffn1_w_down"])], name="ffn_down")
    sv["u1"], x1, sv["x1b"] = ln_fwd(x, y, P["ln1_g"][l], P["ln1_b"][l], FFN_RESIDUAL)

    sv["h"] = proj_in(sv["x1b"], W["w_in"], B, S)
    sv["gA"] = P["sb_out_g"][l].reshape(1, -1)
    sv["gB"] = P["ca_out_g"][l].reshape(1, -1)
    sv["oa"], ma, sv["cars"] = sb_fwd(sv["h"], sv["gA"])
    sv["ob"], mb = ca_fwd(sv["h"], bias, sv["gB"])
    sv["ma"], sv["mb"] = ma.reshape(T, -1), mb.reshape(T, -1)
    y = mm_nn(list(zip([sv["ma"], sv["mb"]], _row_blocks(W["w_out"], 2))), name="proj_out")
    sv["u2"], x2, sv["x2b"] = ln_fwd(x1, y, P["ln2_g"][l], P["ln2_b"][l], 1.0)

    sv["hg2"], sv["hu2"], sv["a2"] = ffn_up(sv["x2b"], W["ffn2_w_gate"], W["ffn2_w_up"])
    y = mm_nn([(sv["a2"], W["ffn2_w_down"])], name="ffn_down")
    sv["u3"], x3, x3b = ln_fwd(x2, y, P["ln3_g"][l], P["ln3_b"][l], FFN_RESIDUAL)
    return x3, x3b, sv


def _ffn_bwd(dr, dxa, xb, hg, hu, a, wg, wu, wd):
    dhg, dhu = ffn_bwd_mid(dr, wd, hg, hu)
    dx = mm_nn([(dhg, wg), (dhu, wu)], add=dxa, name="ffn_dx")
    return dx, mm_tn(dhg, xb, name="ffn_dw"), mm_tn(dhu, xb, name="ffn_dw"), mm_tn(a, dr, name="ffn_dw")


def _layer_bwd(dx, sv, W, P, l, B, S):
    T = B * S
    G = {}
    dxa, dr, dg, db = ln_bwd(sv["u3"], dx, P["ln3_g"][l], FFN_RESIDUAL)
    G["ln3_g"], G["ln3_b"] = dg.sum(0), db.sum(0)
    dx, G["ffn2_w_gate"], G["ffn2_w_up"], G["ffn2_w_down"] = _ffn_bwd(
        dr, dxa, sv["x2b"], sv["hg2"], sv["hu2"], sv["a2"], W["ffn2_w_gate"], W["ffn2_w_up"], W["ffn2_w_down"])

    dxa, dr, dg, db = ln_bwd(sv["u2"], dx, P["ln2_g"][l], 1.0)
    G["ln2_g"], G["ln2_b"] = dg.sum(0), db.sum(0)
    dmix = mm_nt(dr, W["w_out"], name="proj_out_dx").reshape(B, S, -1)
    G["w_out"] = jnp.concatenate([mm_tn(sv["ma"], dr, name="proj_out_dw"), mm_tn(sv["mb"], dr, name="proj_out_dw")])
    dqa, dka, dva, dga = sb_bwd(sv["h"], sv["oa"], dmix, sv["gA"], sv["cars"])
    dqb, dkb, dvb, dgb, dbias = ca_bwd(sv["h"], sv["bias"], sv["ob"], dmix, sv["gB"])
    G["sb_out_g"] = dga.sum((0, 1, 2))
    G["ca_out_g"] = dgb.sum((0, 1, 2))
    G["rel_bias"] = rel_bias_grad(dbias)
    dh = [t.reshape(T, -1) for t in (dqa, dka, dva, dqb, dkb, dvb)]
    dx = mm_nn(list(zip(dh, _row_blocks(W["w_in"], 6))), add=dxa, name="proj_in_dx")
    G["w_in"] = jnp.concatenate([mm_tn(t, sv["x1b"], name="proj_in_dw") for t in dh])

    dxa, dr, dg, db = ln_bwd(sv["u1"], dx, P["ln1_g"][l], FFN_RESIDUAL)
    G["ln1_g"], G["ln1_b"] = dg.sum(0), db.sum(0)
    dx, G["ffn1_w_gate"], G["ffn1_w_up"], G["ffn1_w_down"] = _ffn_bwd(
        dr, dxa, sv["xb"], sv["hg1"], sv["hu1"], sv["a1"], W["ffn1_w_gate"], W["ffn1_w_up"], W["ffn1_w_down"])
    return dx, G


def kernel(x, ffn1_w_gate, ffn1_w_up, ffn1_w_down, ln1_g, ln1_b, w_in, rel_bias, sb_out_g, ca_out_g, w_out, ln2_g, ln2_b, ffn2_w_gate, ffn2_w_up, ffn2_w_down, ln3_g, ln3_b, loss_target, m_ffn1_w_gate, m_ffn1_w_up, m_ffn1_w_down, m_ln1_g, m_ln1_b, m_w_in, m_rel_bias, m_sb_out_g, m_ca_out_g, m_w_out, m_ln2_g, m_ln2_b, m_ffn2_w_gate, m_ffn2_w_up, m_ffn2_w_down, m_ln3_g, m_ln3_b, v_ffn1_w_gate, v_ffn1_w_up, v_ffn1_w_down, v_ln1_g, v_ln1_b, v_w_in, v_rel_bias, v_sb_out_g, v_ca_out_g, v_w_out, v_ln2_g, v_ln2_b, v_ffn2_w_gate, v_ffn2_w_up, v_ffn2_w_down, v_ln3_g, v_ln3_b):
    given = dict(locals())
    P = {n: given[n] for n in WEIGHTS}
    M = {n: given["m_" + n] for n in WEIGHTS}
    V = {n: given["v_" + n] for n in WEIGHTS}
    B, S, D = x.shape
    L = ln1_g.shape[0]

    local = {n: (jnp.swapaxes(P[n], 1, 2) if n in TRANSPOSED else P[n]).astype(BF16) for n in BIG}
    W = []
    for l in range(L):
        full = gather_weights([local[n][l] for n in BIG])
        W.append(dict(zip(BIG, full)))
    bias = ca_bias(rel_bias)

    xf = x.reshape(B * S, D)
    xb = xf.astype(BF16)
    saved = []
    for l in range(L):
        xf, xb, sv = _layer_fwd(xf, xb, W[l], P, bias[l], l, B, S)
        saved.append(sv)
    dx, loss_part = loss_head(xf, loss_target.reshape(B * S, D))

    big_g = {n: [None] * L for n in BIG}
    small_g = {n: [None] * L for n in SMALL}
    for l in reversed(range(L)):
        dx, G = _layer_bwd(dx, saved[l], W[l], P, l, B, S)
        got = exchange_grads([G[n] for n in BIG])
        for n, r in zip(BIG, got):
            big_g[n][l] = sum_slots(r)
        for n in SMALL:
            small_g[n][l] = G[n]

    grads = {}
    for n in BIG:
        g = jnp.stack(big_g[n])
        grads[n] = jnp.swapaxes(g, 1, 2) if n in TRANSPOSED else g
    small_like = [P[n] for n in SMALL]
    packed = _pack([jnp.stack(small_g[n]) for n in SMALL] + [loss_part.sum()])
    total = allreduce_small(packed)
    *small_vals, loss = _unpack(total, small_like + [jnp.zeros((), F32)])
    grads.update(dict(zip(SMALL, small_vals)))

    delta, new_m, new_v = {}, {}, {}
    for n in BIG:
        shape = P[n].shape
        two_d = lambda a: a.reshape(shape[0] * shape[1], shape[2])
        d, m, v = adamw(two_d(P[n]), two_d(grads[n]), two_d(M[n]), two_d(V[n]))
        delta[n], new_m[n], new_v[n] = d.reshape(shape), m.reshape(shape), v.reshape(shape)
    one = jnp.ones((), F32)
    d, m, v = adamw(_pack(small_like + [one]), total, _pack([M[n] for n in SMALL] + [one]), _pack([V[n] for n in SMALL] + [one]))
    for dst, src in ((delta, d), (new_m, m), (new_v, v)):
        dst.update(dict(zip(SMALL, _unpack(src, small_like))))

    return (loss, dx.reshape(B, S, D), *[grads[n] for n in WEIGHTS], *[delta[n] for n in WEIGHTS],
            *[new_m[n] for n in WEIGHTS], *[new_v[n] for n in WEIGHTS])
```

```python
import math

import jax
import jax.numpy as jnp
import numpy as np
from jax import lax
from jax.experimental import pallas as pl
from jax.experimental.pallas import tpu as pltpu

F32 = jnp.float32
BF16 = jnp.bfloat16

HEAD_DIM = 64
CHUNK = 64
N_PREV_CHUNKS = 8
MAX_REL = 128
DEPTH = 4
FFN_RESIDUAL = 0.5
ALPHA = (2 * DEPTH) ** 0.25
LN_EPS = 1e-5
RMS_EPS = 1e-6
ADAM_LR = 0.001
ADAM_B1 = 0.9
ADAM_B2 = 0.999
ADAM_EPS = 1e-08
ADAM_WD = 0.01
ADAM_STEP = 10

N_DEV = 8
LANES = 128
VMEM_LIMIT_BYTES = 56 * 1024 * 1024
ROW_TILE = 512
SB_BLOCK = 128
CA_PAD = CHUNK * N_PREV_CHUNKS
CA_TQ = 2 * CHUNK
CA_WIN = CA_PAD + CA_TQ
FWD_HEADS = 8
SB_BWD_HEADS = 8
CA_BWD_HEADS = 4
NEG = -1e30
SB_SKIP_LOG = -104.0

NN = ((1,), (0,))
NT = ((1,), (1,))
TN = ((0,), (0,))


def _dot(a, b, dims):
    return lax.dot_general(a, b, (dims, ((), ())), preferred_element_type=F32)


def _params(*sem):
    return pltpu.CompilerParams(dimension_semantics=sem, vmem_limit_bytes=VMEM_LIMIT_BYTES)


def _sds(shape, dtype):
    return jax.ShapeDtypeStruct(shape, dtype)


def _row_tile(n, want=ROW_TILE):
    t = min(want, n)
    while n % t:
        t //= 2
    assert t >= 8, (n, want)
    return t


def ffn_up(xb, wg, wu):
    T, D = xb.shape
    F = wg.shape[0]
    tm = _row_tile(T)
    fc = _row_tile(F, 256)

    def body(x_ref, wg_ref, wu_ref, hg_ref, hu_ref, a_ref):
        x = x_ref[...]
        for j in range(F // fc):
            sl = pl.ds(j * fc, fc)
            hg = _dot(x, wg_ref[sl, :], NT)
            hu = _dot(x, wu_ref[sl, :], NT)
            hg_ref[:, sl] = hg.astype(BF16)
            hu_ref[:, sl] = hu.astype(BF16)
            a_ref[:, sl] = (hg * jax.nn.sigmoid(hg) * hu).astype(BF16)

    row = pl.BlockSpec((tm, F), lambda i: (i, 0))
    w = pl.BlockSpec((F, D), lambda i: (0, 0))
    return pl.pallas_call(
        body, grid=(T // tm,), name="ffn_up",
        in_specs=[pl.BlockSpec((tm, D), lambda i: (i, 0)), w, w],
        out_specs=[row, row, row], out_shape=[_sds((T, F), BF16)] * 3,
        compiler_params=_params("parallel"),
    )(xb, wg, wu)


def ffn_bwd_mid(dr, wd, hg, hu):
    T, D = dr.shape
    F = wd.shape[0]
    tm = _row_tile(T)
    fc = _row_tile(F, 256)

    def body(dr_ref, wd_ref, hg_ref, hu_ref, dhg_ref, dhu_ref):
        dr_ = dr_ref[...]
        for j in range(F // fc):
            sl = pl.ds(j * fc, fc)
            da = _dot(dr_, wd_ref[sl, :], NT)
            g = hg_ref[:, sl].astype(F32)
            u = hu_ref[:, sl].astype(F32)
            s = jax.nn.sigmoid(g)
            gs = g * s
            dhu_ref[:, sl] = (da * gs).astype(BF16)
            dhg_ref[:, sl] = (da * u * (s + gs * (1.0 - s))).astype(BF16)

    row = pl.BlockSpec((tm, F), lambda i: (i, 0))
    return pl.pallas_call(
        body, grid=(T // tm,), name="ffn_bwd_mid",
        in_specs=[pl.BlockSpec((tm, D), lambda i: (i, 0)), pl.BlockSpec((F, D), lambda i: (0, 0)), row, row],
        out_specs=[row, row], out_shape=[_sds((T, F), BF16)] * 2,
        compiler_params=_params("parallel"),
    )(dr, wd, hg, hu)


def mm_nn(pairs, add=None, out_dtype=F32, name="mm_nn"):
    M = pairs[0][0].shape[0]
    N = pairs[0][1].shape[1]
    tm = _row_tile(M)
    nc = _row_tile(N, 512)
    n_pairs = len(pairs)

    def body(*refs):
        a_refs = refs[:n_pairs]
        b_refs = refs[n_pairs:2 * n_pairs]
        add_ref = refs[2 * n_pairs] if add is not None else None
        o_ref = refs[-1]
        for j in range(N // nc):
            sl = pl.ds(j * nc, nc)
            acc = _dot(a_refs[0][...], b_refs[0][:, sl], NN)
            for a_ref, b_ref in zip(a_refs[1:], b_refs[1:]):
                acc = acc + _dot(a_ref[...], b_ref[:, sl], NN)
            if add_ref is not None:
                acc = acc + add_ref[:, sl]
            o_ref[:, sl] = acc.astype(out_dtype)

    in_specs = [pl.BlockSpec((tm, a.shape[1]), lambda i: (i, 0)) for a, _ in pairs]
    in_specs += [pl.BlockSpec(b.shape, lambda i: (0, 0)) for _, b in pairs]
    args = [a for a, _ in pairs] + [b for _, b in pairs]
    if add is not None:
        in_specs.append(pl.BlockSpec((tm, N), lambda i: (i, 0)))
        args.append(add)
    return pl.pallas_call(
        body, grid=(M // tm,), name=name, in_specs=in_specs,
        out_specs=pl.BlockSpec((tm, N), lambda i: (i, 0)), out_shape=_sds((M, N), out_dtype),
        compiler_params=_params("parallel"),
    )(*args)


def mm_nt(a, b, out_dtype=F32, name="mm_nt"):
    M, K = a.shape
    N = b.shape[0]
    tm = _row_tile(M)
    nc = _row_tile(N, 512)

    def body(a_ref, b_ref, o_ref):
        a_ = a_ref[...]
        for j in range(N // nc):
            sl = pl.ds(j * nc, nc)
            o_ref[:, sl] = _dot(a_, b_ref[sl, :], NT).astype(out_dtype)

    return pl.pallas_call(
        body, grid=(M // tm,), name=name,
        in_specs=[pl.BlockSpec((tm, K), lambda i: (i, 0)), pl.BlockSpec((N, K), lambda i: (0, 0))],
        out_specs=pl.BlockSpec((tm, N), lambda i: (i, 0)), out_shape=_sds((M, N), out_dtype),
        compiler_params=_params("parallel"),
    )(a, b)


def proj_in(xb, w, B, S):
    D = xb.shape[1]
    N = w.shape[0]
    tm = CA_PAD
    nblk = S // tm
    nc = _row_tile(N, 512)
    assert S % tm == 0

    def body(x_ref, w_ref, o_ref):
        i = pl.program_id(1)

        @pl.when(i == 0)
        def _():
            o_ref[...] = jnp.zeros_like(o_ref)

        @pl.when(i > 0)
        def _():
            x = x_ref[...]
            for j in range(N // nc):
                sl = pl.ds(j * nc, nc)
                o_ref[0, :, sl] = _dot(x, w_ref[sl, :], NT).astype(BF16)

    return pl.pallas_call(
        body, grid=(B, nblk + 1), name="proj_in",
        in_specs=[pl.BlockSpec((tm, D), lambda b, i: (b * nblk + jnp.maximum(i - 1, 0), 0)),
                  pl.BlockSpec((N, D), lambda b, i: (0, 0))],
        out_specs=pl.BlockSpec((1, tm, N), lambda b, i: (b, i, 0)), out_shape=_sds((B, CA_PAD + S, N), BF16),
        compiler_params=_params("parallel", "arbitrary"),
    )(xb, w)


def mm_tn(a, b, name="mm_tn", carry=None):
    T, M = a.shape
    N = b.shape[1]
    tk = _row_tile(T)
    mc = _row_tile(M, 256)
    steps = T // tk

    def body(a_ref, b_ref, o_ref, acc_ref):
        i = pl.program_id(0)

        @pl.when(i == 0)
        def _():
            acc_ref[...] = jnp.zeros_like(acc_ref)

        b_ = b_ref[...]
        for j in range(M // mc):
            sl = pl.ds(j * mc, mc)
            acc_ref[sl, :] += _dot(a_ref[:, sl], b_, TN)

        @pl.when(i == steps - 1)
        def _():
            o_ref[...] = acc_ref[...].astype(BF16)

    (out,), got = _call(
        body, carry, grid=(steps,), name=name, sem=("arbitrary",), args=(a, b),
        in_specs=[pl.BlockSpec((tk, M), lambda i: (i, 0)), pl.BlockSpec((tk, N), lambda i: (i, 0))],
        out_specs=[pl.BlockSpec((M, N), lambda i: (0, 0))], out_shape=[_sds((M, N), BF16)],
        scratch_shapes=[pltpu.VMEM((M, N), F32)])
    return out, got


def ln_fwd(x, r, g, b, res_scale):
    T, D = x.shape
    tm = _row_tile(T)

    def body(x_ref, r_ref, g_ref, b_ref, u_ref, y_ref, yb_ref):
        u = ALPHA * x_ref[...] + res_scale * r_ref[...]
        mu = jnp.mean(u, axis=-1, keepdims=True)
        xc = u - mu
        var = jnp.mean(xc * xc, axis=-1, keepdims=True)
        y = xc * lax.rsqrt(var + LN_EPS) * g_ref[...] + b_ref[...]
        u_ref[...] = u
        y_ref[...] = y
        yb_ref[...] = y.astype(BF16)

    row = pl.BlockSpec((tm, D), lambda i: (i, 0))
    vec = pl.BlockSpec((1, D), lambda i: (0, 0))
    return pl.pallas_call(
        body, grid=(T // tm,), name="ln_fwd", in_specs=[row, row, vec, vec],
        out_specs=[row, row, row], out_shape=[_sds((T, D), F32), _sds((T, D), F32), _sds((T, D), BF16)],
        compiler_params=_params("parallel"),
    )(x, r, g.reshape(1, D), b.reshape(1, D))


def ln_bwd(u, dy, g, branch_scale):
    T, D = u.shape
    tm = _row_tile(T)

    def body(u_ref, dy_ref, g_ref, dxa_ref, dr_ref, dg_ref, db_ref):
        @pl.when(pl.program_id(0) == 0)
        def _():
            dg_ref[...] = jnp.zeros_like(dg_ref)
            db_ref[...] = jnp.zeros_like(db_ref)

        u_ = u_ref[...]
        dy_ = dy_ref[...]
        mu = jnp.mean(u_, axis=-1, keepdims=True)
        xc = u_ - mu
        var = jnp.mean(xc * xc, axis=-1, keepdims=True)
        rstd = lax.rsqrt(var + LN_EPS)
        xh = xc * rstd
        dxh = dy_ * g_ref[...]
        m1 = jnp.mean(dxh, axis=-1, keepdims=True)
        m2 = jnp.mean(dxh * xh, axis=-1, keepdims=True)
        du = rstd * (dxh - m1 - xh * m2)
        dxa_ref[...] = ALPHA * du
        dr_ref[...] = (branch_scale * du).astype(BF16)
        dg_ref[...] += jnp.sum((dy_ * xh).reshape(tm // 8, 8, D), axis=0)
        db_ref[...] += jnp.sum(dy_.reshape(tm // 8, 8, D), axis=0)

    row = pl.BlockSpec((tm, D), lambda i: (i, 0))
    acc = pl.BlockSpec((8, D), lambda i: (0, 0))
    dxa, dr, dg, db = pl.pallas_call(
        body, grid=(T // tm,), name="ln_bwd", in_specs=[row, row, pl.BlockSpec((1, D), lambda i: (0, 0))],
        out_specs=[row, row, acc, acc],
        out_shape=[_sds((T, D), F32), _sds((T, D), BF16), _sds((8, D), F32), _sds((8, D), F32)],
        compiler_params=_params("arbitrary"),
    )(u, dy, g.reshape(1, D))
    return dxa, dr, dg, db


def loss_head(y, target):
    T, D = y.shape
    tm = _row_tile(T)

    def body(y_ref, t_ref, dy_ref, l_ref):
        @pl.when(pl.program_id(0) == 0)
        def _():
            l_ref[...] = jnp.zeros_like(l_ref)

        e = y_ref[...] - t_ref[...]
        dy_ref[...] = e * (1.0 / D)
        l_ref[...] += jnp.sum((e * e).reshape(tm // 8, 8, D), axis=0) * (0.5 / D)

    row = pl.BlockSpec((tm, D), lambda i: (i, 0))
    return pl.pallas_call(
        body, grid=(T // tm,), name="loss_head", in_specs=[row, row],
        out_specs=[row, pl.BlockSpec((8, D), lambda i: (0, 0))],
        out_shape=[_sds((T, D), F32), _sds((8, D), F32)],
        compiler_params=_params("arbitrary"),
    )(y, target)


def adamw(w, g, m, v):
    R, C = w.shape
    tr = R
    for cand in (512, 256, 128, 64, 32, 16, 8):
        if R % cand == 0:
            tr = cand
            break
    c1 = 1.0 - ADAM_B1 ** ADAM_STEP
    c2 = 1.0 - ADAM_B2 ** ADAM_STEP

    def body(w_ref, g_ref, m_ref, v_ref, d_ref, mo_ref, vo_ref):
        g_ = g_ref[...]
        m_ = ADAM_B1 * m_ref[...] + (1.0 - ADAM_B1) * g_
        v_ = ADAM_B2 * v_ref[...] + (1.0 - ADAM_B2) * (g_ * g_)
        m_hat = m_ / c1
        v_hat = v_ / c2
        d_ref[...] = -ADAM_LR * (m_hat / (jnp.sqrt(v_hat) + ADAM_EPS) + ADAM_WD * w_ref[...])
        mo_ref[...] = m_
        vo_ref[...] = v_

    blk = pl.BlockSpec((tr, C), lambda i: (i, 0))
    return pl.pallas_call(
        body, grid=(R // tr,), name="adamw", in_specs=[blk] * 4, out_specs=[blk] * 3,
        out_shape=[_sds((R, C), F32)] * 3, compiler_params=_params("parallel"),
    )(w, g, m, v)


def _lane_lo(rows):
    return lax.broadcasted_iota(jnp.int32, (rows, LANES), 1) < HEAD_DIM


def _pair_mean(x, lo):
    s0 = jnp.sum(jnp.where(lo, x, 0.0), axis=-1, keepdims=True)
    s1 = jnp.sum(jnp.where(lo, 0.0, x), axis=-1, keepdims=True)
    return jnp.where(lo, s0, s1) * (1.0 / HEAD_DIM)


def _rms_fwd(o, gain, lo):
    r = lax.rsqrt(_pair_mean(o * o, lo) + RMS_EPS)
    return (o * r * gain).astype(BF16)


def _rms_bwd(o, dm, gain, lo):
    r = lax.rsqrt(_pair_mean(o * o, lo) + RMS_EPS)
    oh = o * r
    dg = jnp.sum(dm * oh, axis=0, keepdims=True)
    doh = dm * gain
    do = r * (doh - oh * _pair_mean(doh * oh, lo))
    return do, dg


def _split_heads(x, lo):
    zero = jnp.zeros_like(x)
    return [jnp.where(lo, x, zero), jnp.where(lo, zero, x)]


def _merge_pairs(per_head, lo):
    return [jnp.where(lo, per_head[2 * p], per_head[2 * p + 1]) for p in range(len(per_head) // 2)]


def _pair_cols(h):
    return slice((h // 2) * LANES, (h // 2 + 1) * LANES)


def _split_dot(x, tri):
    n = x.shape[0]
    hi = x.astype(BF16)
    lo = (x - hi.astype(F32)).astype(BF16)
    both = _dot(jnp.concatenate([hi, lo], axis=0), tri, NN)
    return both[:n] + both[n:]


def _log_keep(z):
    return -(jnp.maximum(z, 0.0) + jnp.log(1.0 + jnp.exp(-jnp.abs(z))))


def _attn_dims(h3p, gain, heads):
    B, SP, C = h3p.shape
    HD = gain.shape[1]
    hs = min(heads, HD // HEAD_DIM)
    W = hs * HEAD_DIM
    assert C == 6 * HD and W % LANES == 0 and HD % W == 0
    return B, SP - CA_PAD, HD, hs, W, HD // W


def sb_fwd(h3p, gain, carry=None):
    B, S, HD, hs, W, ngrp = _attn_dims(h3p, gain, FWD_HEADS)
    tb = min(SB_BLOCK, S)
    nq = S // tb
    off = CA_PAD // tb
    scale = 1.0 / math.sqrt(HEAD_DIM)

    def body(q_ref, k_ref, v_ref, g_ref, o_ref, m_ref, c_ref):
        qi = pl.program_id(2)
        lo = _lane_lo(tb)
        row = lax.broadcasted_iota(jnp.int32, (tb, tb), 0)
        col = lax.broadcasted_iota(jnp.int32, (tb, tb), 1)
        rev_incl = (row >= col).astype(BF16)
        lane = lax.broadcasted_iota(jnp.int32, (tb, nq), 1)
        qm = []
        for p in range(hs // 2):
            qm += _split_heads((q_ref[0, :, p * LANES:(p + 1) * LANES].astype(F32) * scale).astype(BF16), lo)

        def blocks(kb, carries_, mask):
            ks = pl.ds(pl.multiple_of(kb * tb + CA_PAD, tb), tb)
            hh = range(hs)
            zs = [_dot(qm[h], k_ref[0, ks, _pair_cols(h)], NT) for h in hh]
            lks = [_log_keep(z) for z in zs]
            if mask is not None:
                lks = [jnp.where(mask, lk, 0.0) for lk in lks]
            cums = [_split_dot(lk, rev_incl) for lk in lks]
            ws = []
            for h in hh:
                logw = zs[h] + cums[h] + carries_[h]
                if mask is not None:
                    logw = jnp.where(mask, logw, NEG)
                ws.append(jnp.exp(logw).astype(BF16))
            pvs = [_dot(ws[h], v_ref[0, ks, _pair_cols(h)], NN) for h in hh]
            return [(pvs[h], cums[h][:, 0:1]) for h in hh]

        diag = blocks(qi, [jnp.zeros((tb, 1), F32)] * hs, col < row)
        accs = _merge_pairs([d[0] for d in diag], lo)
        carries = [d[1] for d in diag]
        cars = [jnp.where(lane == qi, 0.0, NEG)] * hs

        def cond(st):
            kb, carries_, _, _ = st
            top = carries_[0]
            for c in carries_[1:]:
                top = jnp.maximum(top, c)
            return jnp.logical_and(kb >= 0, jnp.max(top) > SB_SKIP_LOG)

        def step(st):
            kb, carries_, accs_, cars_ = st
            out = blocks(kb, carries_, None)
            pv = _merge_pairs([o[0] for o in out], lo)
            return (kb - 1, [c + o[1] for c, o in zip(carries_, out)], [a + p for a, p in zip(accs_, pv)],
                    [jnp.where(lane == kb, c, cs) for c, cs in zip(carries_, cars_)])

        _, _, accs, cars = lax.while_loop(cond, step, (qi - 1, carries, accs, cars))
        for p, acc in enumerate(accs):
            cols = slice(p * LANES, (p + 1) * LANES)
            o_ref[0, :, cols] = acc
            m_ref[0, :, cols] = _rms_fwd(acc, g_ref[:, cols], lo)
        for h in range(hs):
            c_ref[0, h] = cars[h]

    qspec = pl.BlockSpec((1, tb, W), lambda g, b, i: (b, i + off, g))
    ospec = pl.BlockSpec((1, tb, W), lambda g, b, i: (b, i, g))
    return _call(
        body, carry, grid=(ngrp, B, nq), name="sb_fwd", sem=("parallel", "parallel", "arbitrary"),
        args=(h3p, h3p, h3p, gain),
        in_specs=[qspec, pl.BlockSpec((1, CA_PAD + S, W), lambda g, b, i: (b, 0, ngrp + g)),
                  pl.BlockSpec((1, CA_PAD + S, W), lambda g, b, i: (b, 0, 2 * ngrp + g)),
                  pl.BlockSpec((1, W), lambda g, b, i: (0, g))],
        out_specs=[ospec, ospec, pl.BlockSpec((1, hs, tb, nq), lambda g, b, i: (b, g, i, 0))],
        out_shape=[_sds((B, S, HD), F32), _sds((B, S, HD), BF16), _sds((B, HD // HEAD_DIM, S, nq), F32)])


def sb_bwd(h3p, o, dmix, gain, cars, carry=None):
    B, S, HD, hs, W, ngrp = _attn_dims(h3p, gain, SB_BWD_HEADS)
    tb = min(SB_BLOCK, S)
    nq = S // tb
    off = CA_PAD // tb
    scale = 1.0 / math.sqrt(HEAD_DIM)

    def body(q_ref, k_ref, v_ref, o_ref, dm_ref, g_ref, c_ref, dq_ref, dk_ref, dv_ref, dg_ref, dk_acc, dv_acc):
        qi = pl.program_id(2)

        @pl.when(qi == 0)
        def _():
            dk_acc[...] = jnp.zeros_like(dk_acc)
            dv_acc[...] = jnp.zeros_like(dv_acc)

        lo = _lane_lo(tb)
        row = lax.broadcasted_iota(jnp.int32, (tb, tb), 0)
        col = lax.broadcasted_iota(jnp.int32, (tb, tb), 1)
        rev_incl = (row >= col).astype(BF16)
        fwd_incl = (row <= col).astype(BF16)
        lane = lax.broadcasted_iota(jnp.int32, (tb, nq), 1)
        below = lax.broadcasted_iota(jnp.int32, (1, nq), 1) < qi
        qm, dom, cars_, seen = [], [], [], None
        for p in range(hs // 2):
            cols = slice(p * LANES, (p + 1) * LANES)
            qm += _split_heads((q_ref[0, :, cols].astype(F32) * scale).astype(BF16), lo)
            do, dg = _rms_bwd(o_ref[0, :, cols], dm_ref[0, :, cols], g_ref[:, cols], lo)
            dg_ref[0, 0, :, cols] = dg
            dom += _split_heads(do.astype(BF16), lo)
        for h in range(hs):
            cars_.append(c_ref[0, h])
            visited = jnp.logical_and(jnp.max(cars_[h], axis=0, keepdims=True) > SB_SKIP_LOG, below)
            n = jnp.sum(visited.astype(jnp.int32), axis=1, keepdims=True)
            seen = n if seen is None else jnp.maximum(seen, n)
        first = qi - jnp.max(seen)

        def blocks(kb, gsums, dqs, mask):
            ks = pl.ds(pl.multiple_of(kb * tb + CA_PAD, tb), tb)
            ko = pl.ds(pl.multiple_of(kb * tb, tb), tb)
            hh = range(hs)
            kk = [k_ref[0, ks, p * LANES:(p + 1) * LANES] for p in range(hs // 2)]
            vv = [v_ref[0, ks, p * LANES:(p + 1) * LANES] for p in range(hs // 2)]
            zs = [_dot(qm[h], kk[h // 2], NT) for h in hh]
            dws = [_dot(dom[h], vv[h // 2], NT) for h in hh]
            raw = [_log_keep(z) for z in zs]
            lks = raw if mask is None else [jnp.where(mask, lk, 0.0) for lk in raw]
            cums = [_split_dot(lk, rev_incl) for lk in lks]
            ws = []
            for h in hh:
                carry = jnp.sum(jnp.where(lane == kb, cars_[h], 0.0), axis=1, keepdims=True)
                logw = zs[h] + cums[h] + carry
                if mask is not None:
                    logw = jnp.where(mask, logw, NEG)
                ws.append(jnp.exp(logw))
            gws = [ws[h] * dws[h] for h in hh]
            gcums = [_split_dot(gws[h], fwd_incl) + gsums[h] for h in hh]
            dzb = []
            for h in hh:
                dz = gws[h] - jnp.exp(zs[h] + raw[h]) * gcums[h]
                if mask is not None:
                    dz = jnp.where(mask, dz, 0.0)
                dzb.append(dz.astype(BF16))
            wb = [w.astype(BF16) for w in ws]
            new_dq = [dqs[h] + _dot(dzb[h], kk[h // 2], NN) for h in hh]
            for p in range(hs // 2):
                cols = slice(p * LANES, (p + 1) * LANES)
                dk_acc[ko, cols] += _dot(dzb[2 * p], qm[2 * p], TN) + _dot(dzb[2 * p + 1], qm[2 * p + 1], TN)
                dv_acc[ko, cols] += _dot(wb[2 * p], dom[2 * p], TN) + _dot(wb[2 * p + 1], dom[2 * p + 1], TN)
            return [g[:, tb - 1:tb] for g in gcums], new_dq

        def step(kb, st):
            return blocks(kb, st[0], st[1], None)

        init = ([jnp.zeros((tb, 1), F32)] * hs, [jnp.zeros((tb, LANES), F32)] * hs)
        gsum, dq = lax.fori_loop(first, qi, step, init)
        _, dq = blocks(qi, gsum, dq, col < row)
        for p, d in enumerate(_merge_pairs(dq, lo)):
            dq_ref[0, :, p * LANES:(p + 1) * LANES] = (d * scale).astype(BF16)

        @pl.when(qi == nq - 1)
        def _():
            dk_ref[0] = dk_acc[...].astype(BF16)
            dv_ref[0] = dv_acc[...].astype(BF16)

    once = pl.Buffered(1)
    qspec = pl.BlockSpec((1, tb, W), lambda g, b, i: (b, i + off, g))
    ospec = pl.BlockSpec((1, tb, W), lambda g, b, i: (b, i, g))
    kvout = pl.BlockSpec((1, S, W), lambda g, b, i: (b, 0, g), pipeline_mode=once)
    return _call(
        body, carry, grid=(ngrp, B, nq), name="sb_bwd", sem=("parallel", "parallel", "arbitrary"),
        args=(h3p, h3p, h3p, o, dmix, gain, cars),
        in_specs=[qspec, pl.BlockSpec((1, CA_PAD + S, W), lambda g, b, i: (b, 0, ngrp + g), pipeline_mode=once),
                  pl.BlockSpec((1, CA_PAD + S, W), lambda g, b, i: (b, 0, 2 * ngrp + g), pipeline_mode=once),
                  ospec, ospec, pl.BlockSpec((1, W), lambda g, b, i: (0, g)),
                  pl.BlockSpec((1, hs, tb, nq), lambda g, b, i: (b, g, i, 0))],
        out_specs=[ospec, kvout, kvout, pl.BlockSpec((1, 1, 1, W), lambda g, b, i: (b, i, 0, g))],
        out_shape=[_sds((B, S, HD), BF16), _sds((B, S, HD), BF16), _sds((B, S, HD), BF16), _sds((B, nq, 1, HD), F32)],
        scratch_shapes=[pltpu.VMEM((S, W), F32), pltpu.VMEM((S, W), F32)])


def _ca_rel_index():
    width = CA_WIN + CA_TQ
    c = np.arange(width)
    dj = np.where(c < CA_WIN, c, c - width)
    return np.clip(CA_PAD - dj, -MAX_REL, MAX_REL) + MAX_REL, width


def _ca_onehot():
    idx, _ = _ca_rel_index()
    return (idx[:, None] == np.arange(2 * MAX_REL + 1)[None, :]).astype(np.float32)


def ca_bias(rel_bias):
    _, width = _ca_rel_index()
    lead = rel_bias.shape[:-1]
    by_offset = jnp.dot(rel_bias, jnp.asarray(_ca_onehot().T), precision=lax.Precision.HIGHEST)
    tile = jnp.broadcast_to(by_offset[..., None, :], lead + (CA_TQ, width)).reshape(lead + (CA_TQ * width,))
    tile = tile[..., :CA_TQ * (width - 1)].reshape(lead + (CA_TQ, width - 1))[..., :CA_WIN]
    t = np.arange(CA_TQ)[:, None] // CHUNK * CHUNK
    j = np.arange(CA_WIN)[None, :]
    return jnp.where((j >= t) & (j < t + CA_PAD + CHUNK), tile, NEG)


def rel_bias_grad(db):
    H = db.shape[0]
    _, width = _ca_rel_index()
    x = jnp.pad(db, ((0, 0), (0, 0), (0, width - 1 - CA_WIN))).reshape(H, CA_TQ * (width - 1))
    x = jnp.pad(x, ((0, 0), (0, CA_TQ))).reshape(H, CA_TQ, width).sum(axis=1)
    return jnp.dot(x, jnp.asarray(_ca_onehot()), precision=lax.Precision.HIGHEST)


def _ca_scores(qm_h, kk, bias_h, valid):
    return jnp.where(valid, _dot(qm_h, kk, NT) + bias_h, NEG)


def _ca_softmax(s):
    e = jnp.exp(s - jnp.max(s, axis=-1, keepdims=True))
    return e * (1.0 / jnp.sum(e, axis=-1, keepdims=True))


def ca_fwd(h3p, bias, gain):
    B, S, HD, hs, W, ngrp = _attn_dims(h3p, gain, FWD_HEADS)
    scale = 1.0 / math.sqrt(HEAD_DIM)
    off = CA_PAD // CA_TQ

    def body(q_ref, k_ref, v_ref, b_ref, g_ref, o_ref, m_ref):
        q0 = pl.program_id(2) * CA_TQ
        ks = pl.ds(pl.multiple_of(q0, CA_TQ), CA_WIN)
        lo = _lane_lo(CA_TQ)
        valid = lax.broadcasted_iota(jnp.int32, (CA_TQ, CA_WIN), 1) + q0 >= CA_PAD
        qm, kk, vv = [], [], []
        for p in range(hs // 2):
            cols = slice(p * LANES, (p + 1) * LANES)
            qm += _split_heads((q_ref[0, :, cols].astype(F32) * scale).astype(BF16), lo)
            kk.append(k_ref[0, ks, cols])
            vv.append(v_ref[0, ks, cols])
        ss = [_ca_scores(qm[h], kk[h // 2], b_ref[h], valid) for h in range(hs)]
        ps = [_ca_softmax(s).astype(BF16) for s in ss]
        pv = [_dot(ps[h], vv[h // 2], NN) for h in range(hs)]
        for p, o in enumerate(_merge_pairs(pv, lo)):
            cols = slice(p * LANES, (p + 1) * LANES)
            o_ref[0, :, cols] = o
            m_ref[0, :, cols] = _rms_fwd(o, g_ref[:, cols], lo)

    ospec = pl.BlockSpec((1, CA_TQ, W), lambda g, b, i: (b, i, g))
    return pl.pallas_call(
        body, grid=(ngrp, B, S // CA_TQ), name="ca_fwd",
        in_specs=[pl.BlockSpec((1, CA_TQ, W), lambda g, b, i: (b, i + off, 3 * ngrp + g)),
                  pl.BlockSpec((1, CA_PAD + S, W), lambda g, b, i: (b, 0, 4 * ngrp + g)),
                  pl.BlockSpec((1, CA_PAD + S, W), lambda g, b, i: (b, 0, 5 * ngrp + g)),
                  pl.BlockSpec((hs, CA_TQ, CA_WIN), lambda g, b, i: (g, 0, 0)),
                  pl.BlockSpec((1, W), lambda g, b, i: (0, g))],
        out_specs=[ospec, ospec], out_shape=[_sds((B, S, HD), F32), _sds((B, S, HD), BF16)],
        compiler_params=_params("parallel", "parallel", "arbitrary"),
    )(h3p, h3p, h3p, bias, gain)


def ca_bwd(h3p, bias, o, dmix, gain, carry=None):
    B, S, HD, hs, W, ngrp = _attn_dims(h3p, gain, CA_BWD_HEADS)
    scale = 1.0 / math.sqrt(HEAD_DIM)
    nq = S // CA_TQ
    off = CA_PAD // CA_TQ

    def body(q_ref, k_ref, v_ref, b_ref, o_ref, dm_ref, g_ref, dq_ref, dk_ref, dv_ref, dg_ref, db_ref, dk_acc, dv_acc):
        bi = pl.program_id(1)
        qi = pl.program_id(2)

        @pl.when(qi == 0)
        def _():
            dk_acc[...] = jnp.zeros_like(dk_acc)
            dv_acc[...] = jnp.zeros_like(dv_acc)

        @pl.when(jnp.logical_and(qi == 0, bi == 0))
        def _():
            db_ref[...] = jnp.zeros_like(db_ref)

        q0 = qi * CA_TQ
        ks = pl.ds(pl.multiple_of(q0, CA_TQ), CA_WIN)
        lo = _lane_lo(CA_TQ)
        valid = lax.broadcasted_iota(jnp.int32, (CA_TQ, CA_WIN), 1) + q0 >= CA_PAD
        qm, dom, kk, vv = [], [], [], []
        for p in range(hs // 2):
            cols = slice(p * LANES, (p + 1) * LANES)
            qm += _split_heads((q_ref[0, :, cols].astype(F32) * scale).astype(BF16), lo)
            do, dg = _rms_bwd(o_ref[0, :, cols], dm_ref[0, :, cols], g_ref[:, cols], lo)
            dg_ref[0, 0, :, cols] = dg
            dom += _split_heads(do.astype(BF16), lo)
            kk.append(k_ref[0, ks, cols])
            vv.append(v_ref[0, ks, cols])
        hh = range(hs)
        ss = [_ca_scores(qm[h], kk[h // 2], b_ref[h], valid) for h in hh]
        dps = [_dot(dom[h], vv[h // 2], NT) for h in hh]
        ps = [_ca_softmax(s) for s in ss]
        dss = [ps[h] * (dps[h] - jnp.sum(ps[h] * dps[h], axis=-1, keepdims=True)) for h in hh]
        for h in hh:
            db_ref[h] += dss[h]
        dsb = [d.astype(BF16) for d in dss]
        pb = [p_.astype(BF16) for p_ in ps]
        dq = [_dot(dsb[h], kk[h // 2], NN) for h in hh]
        for p in range(hs // 2):
            cols = slice(p * LANES, (p + 1) * LANES)
            dk_acc[ks, cols] += _dot(dsb[2 * p], qm[2 * p], TN) + _dot(dsb[2 * p + 1], qm[2 * p + 1], TN)
            dv_acc[ks, cols] += _dot(pb[2 * p], dom[2 * p], TN) + _dot(pb[2 * p + 1], dom[2 * p + 1], TN)
        for p, d in enumerate(_merge_pairs(dq, lo)):
            dq_ref[0, :, p * LANES:(p + 1) * LANES] = (d * scale).astype(BF16)

        @pl.when(qi == nq - 1)
        def _():
            dk_ref[0] = dk_acc[CA_PAD:, :].astype(BF16)
            dv_ref[0] = dv_acc[CA_PAD:, :].astype(BF16)

    ospec = pl.BlockSpec((1, CA_TQ, W), lambda g, b, i: (b, i, g))
    kvout = pl.BlockSpec((1, S, W), lambda g, b, i: (b, 0, g))
    bspec = pl.BlockSpec((hs, CA_TQ, CA_WIN), lambda g, b, i: (g, 0, 0))
    return _call(
        body, carry, grid=(ngrp, B, nq), name="ca_bwd", sem=("parallel", "arbitrary", "arbitrary"),
        args=(h3p, h3p, h3p, bias, o, dmix, gain),
        in_specs=[pl.BlockSpec((1, CA_TQ, W), lambda g, b, i: (b, i + off, 3 * ngrp + g)),
                  pl.BlockSpec((1, CA_PAD + S, W), lambda g, b, i: (b, 0, 4 * ngrp + g)),
                  pl.BlockSpec((1, CA_PAD + S, W), lambda g, b, i: (b, 0, 5 * ngrp + g)),
                  bspec, ospec, pl.BlockSpec((1, CA_TQ, W), lambda g, b, i: (b, i, ngrp + g)),
                  pl.BlockSpec((1, W), lambda g, b, i: (0, g))],
        out_specs=[ospec, kvout, kvout, pl.BlockSpec((1, 1, 1, W), lambda g, b, i: (b, i, 0, g)), bspec],
        out_shape=[_sds((B, S, HD), BF16), _sds((B, S, HD), BF16), _sds((B, S, HD), BF16),
                   _sds((B, nq, 1, HD), F32), _sds((HD // HEAD_DIM, CA_TQ, CA_WIN), F32)],
        scratch_shapes=[pltpu.VMEM((CA_PAD + S, W), F32), pltpu.VMEM((CA_PAD + S, W), F32)])


_ANY = pl.BlockSpec(memory_space=pl.ANY)
_MESH = pl.DeviceIdType.MESH


def _mesh_pos():
    return lax.axis_index("x"), lax.axis_index("y"), lax.axis_index("c")


def _dev_index(p):
    return 4 * p[0] + 2 * p[1] + p[2]


def _flip(pos, k):
    return tuple(1 - v if (k >> (2 - a)) & 1 else v for a, v in enumerate(pos))


class _Plan:
    def __init__(self, operands):
        self.operands = list(operands)
        self.nm = len(self.operands)
        self.scratch = [pltpu.SemaphoreType.DMA((7, self.nm)), pltpu.SemaphoreType.DMA((7, self.nm)),
                        pltpu.SemaphoreType.DMA((self.nm,))]


class _Gather(_Plan):
    def __init__(self, mats):
        super().__init__(mats)
        self.rows = [m.shape[0] for m in mats]
        self.out_shape = [_sds((N_DEV * m.shape[0], m.shape[1]), m.dtype) for m in mats]

    def begin(self, ins, outs, sems):
        nm, rows = self.nm, self.rows
        send_sems, recv_sems, local_sems = sems
        x, y, c = _mesh_pos()
        me, sibling = (x, y, c), (x, y, 1 - c)
        chips = [(1 - x, y), (x, 1 - y), (1 - x, 1 - y)]

        def block(m, p):
            start = pl.multiple_of(_dev_index(p) * rows[m], 16)
            return outs[m].at[pl.ds(start, rows[m]), :]

        def copy(k, m, blk, to, src=None):
            return pltpu.make_async_remote_copy(
                src_ref=block(m, blk) if src is None else src, dst_ref=block(m, blk),
                send_sem=send_sems.at[k, m], recv_sem=recv_sems.at[k, m], device_id=to, device_id_type=_MESH)

        def mine():
            return [pltpu.make_async_copy(ins[m], block(m, me), local_sems.at[m]) for m in range(nm)]

        def first():
            own = [copy(0, m, me, sibling, src=ins[m]) for m in range(nm)]
            return own + [copy(1 + j, m, me, (*chip, c), src=ins[m]) for j, chip in enumerate(chips) for m in range(nm)]

        def start():
            for cp in mine() + first():
                cp.start()

        def finish():
            passed = []
            for j, chip in enumerate(chips):
                for m in range(nm):
                    copy(1 + j, m, (*chip, c), me).wait_recv()
                    fwd = copy(4 + j, m, (*chip, c), sibling)
                    fwd.start()
                    passed.append(fwd)
            for m in range(nm):
                copy(0, m, sibling, me).wait_recv()
            for j, chip in enumerate(chips):
                for m in range(nm):
                    copy(4 + j, m, (*chip, 1 - c), me).wait_recv()
            for cp in first() + passed:
                cp.wait_send()
            for cp in mine():
                cp.wait()

        return start, finish


class _Exchange(_Plan):
    def __init__(self, grads):
        super().__init__(grads)
        self.rows = [g.shape[0] // N_DEV for g in grads]
        self.out_shape = [_sds((N_DEV, g.shape[0] // N_DEV, g.shape[1]), g.dtype) for g in grads]

    def begin(self, ins, outs, sems):
        nm, rows = self.nm, self.rows
        send_sems, recv_sems, local_sems = sems
        me = _mesh_pos()
        my = _dev_index(me)

        def piece(m, idx):
            return ins[m].at[pl.ds(pl.multiple_of(idx * rows[m], 16), rows[m]), :]

        def copy(k, m, src_idx, slot, to):
            return pltpu.make_async_remote_copy(
                src_ref=piece(m, src_idx), dst_ref=outs[m].at[slot],
                send_sem=send_sems.at[k - 1, m], recv_sem=recv_sems.at[k - 1, m], device_id=to, device_id_type=_MESH)

        def mine():
            return [pltpu.make_async_copy(piece(m, my), outs[m].at[my], local_sems.at[m]) for m in range(nm)]

        def sends():
            return [copy(k, m, _dev_index(_flip(me, k)), my, _flip(me, k)) for k in range(1, N_DEV) for m in range(nm)]

        def start():
            for cp in mine() + sends():
                cp.start()

        def finish():
            for k in range(1, N_DEV):
                peer = _flip(me, k)
                for m in range(nm):
                    copy(k, m, my, _dev_index(peer), peer).wait_recv()
            for cp in sends():
                cp.wait_send()
            for cp in mine():
                cp.wait()

        return start, finish


def _run_plan(plan, name):
    nm = plan.nm

    def body(*refs):
        start, finish = plan.begin(refs[:nm], refs[nm:2 * nm], refs[2 * nm:])
        start()
        finish()

    return pl.pallas_call(body, name=name, in_specs=[_ANY] * nm, out_specs=[_ANY] * nm, out_shape=plan.out_shape,
                          scratch_shapes=plan.scratch)(*plan.operands)


def _call(body, carry, *, grid, in_specs, out_specs, out_shape, args, name, sem, scratch_shapes=()):
    if carry is None:
        outs = pl.pallas_call(body, grid=grid, name=name, in_specs=in_specs, out_specs=out_specs, out_shape=out_shape,
                              scratch_shapes=list(scratch_shapes), compiler_params=_params(*sem))(*args)
        return outs, None
    n_in, n_out, n_sc, nm = len(in_specs), len(out_specs), len(scratch_shapes), carry.nm

    def carrier(*refs):
        refs = list(refs)
        own_in, refs = refs[:n_in], refs[n_in:]
        c_in, refs = refs[:nm], refs[nm:]
        own_out, refs = refs[:n_out], refs[n_out:]
        c_out, refs = refs[:nm], refs[nm:]
        own_sc, c_sc = refs[:n_sc], refs[n_sc:]
        start, finish = carry.begin(c_in, c_out, c_sc)
        first = last = None
        for axis, n in enumerate(grid):
            i = pl.program_id(axis)
            first = (i == 0) if first is None else jnp.logical_and(first, i == 0)
            last = (i == n - 1) if last is None else jnp.logical_and(last, i == n - 1)
        pl.when(first)(start)
        body(*own_in, *own_out, *own_sc)
        pl.when(last)(finish)

    outs = pl.pallas_call(
        carrier, grid=grid, name=name + "_carry", in_specs=list(in_specs) + [_ANY] * nm,
        out_specs=list(out_specs) + [_ANY] * nm, out_shape=list(out_shape) + carry.out_shape,
        scratch_shapes=list(scratch_shapes) + carry.scratch,
        compiler_params=_params(*["arbitrary"] * len(grid)),
    )(*args, *carry.operands)
    return outs[:n_out], outs[n_out:]


def sum_slots(r):
    n, R, D = r.shape
    tc = _row_tile(D, 256)

    def body(r_ref, o_ref):
        acc = r_ref[0].astype(F32)
        for s in range(1, n):
            acc = acc + r_ref[s].astype(F32)
        o_ref[...] = acc

    return pl.pallas_call(
        body, grid=(D // tc,), name="sum_slots", in_specs=[pl.BlockSpec((n, R, tc), lambda i: (0, 0, i))],
        out_specs=pl.BlockSpec((R, tc), lambda i: (0, i)), out_shape=_sds((R, D), F32),
        compiler_params=_params("parallel"),
    )(r)


def allreduce_small(v):
    R, C = v.shape

    def body(v_ref, o_ref, buf, send_sems, recv_sems):
        me = _mesh_pos()
        my = _dev_index(me)
        buf[my] = v_ref[...]

        def copy(k, slot, to):
            return pltpu.make_async_remote_copy(
                src_ref=v_ref, dst_ref=buf.at[slot], send_sem=send_sems.at[k - 1], recv_sem=recv_sems.at[k - 1],
                device_id=to, device_id_type=_MESH)

        sends = []
        for k in range(1, N_DEV):
            cp = copy(k, my, _flip(me, k))
            cp.start()
            sends.append(cp)
        for k in range(1, N_DEV):
            peer = _flip(me, k)
            copy(k, _dev_index(peer), peer).wait_recv()
        acc = buf[0]
        for s in range(1, N_DEV):
            acc = acc + buf[s]
        o_ref[...] = acc
        for cp in sends:
            cp.wait_send()

    vm = pl.BlockSpec(memory_space=pltpu.VMEM)
    return pl.pallas_call(
        body, name="allreduce_small", in_specs=[vm], out_specs=vm, out_shape=_sds((R, C), F32),
        scratch_shapes=[pltpu.VMEM((N_DEV, R, C), F32), pltpu.SemaphoreType.DMA((7,)), pltpu.SemaphoreType.DMA((7,))],
    )(v)


WEIGHTS = ["ffn1_w_gate", "ffn1_w_up", "ffn1_w_down", "ln1_g", "ln1_b", "w_in", "rel_bias", "sb_out_g", "ca_out_g",
           "w_out", "ln2_g", "ln2_b", "ffn2_w_gate", "ffn2_w_up", "ffn2_w_down", "ln3_g", "ln3_b"]
BIG = ["ffn1_w_gate", "ffn1_w_up", "ffn2_w_gate", "ffn2_w_up", "w_in", "ffn1_w_down", "ffn2_w_down", "w_out"]
TRANSPOSED = BIG[:5]
SMALL = [n for n in WEIGHTS if n not in BIG]


def _pack(vals):
    flat = jnp.concatenate([v.reshape(-1).astype(F32) for v in vals])
    pad = -flat.shape[0] % (8 * 128)
    return jnp.pad(flat, (0, pad)).reshape(-1, 128)


def _unpack(packed, like):
    flat = packed.reshape(-1)
    out, off = [], 0
    for v in like:
        out.append(flat[off:off + v.size].reshape(v.shape))
        off += v.size
    return out


def _row_blocks(w, n):
    r = w.shape[0] // n
    return [w[i * r:(i + 1) * r] for i in range(n)]


def _layer_fwd(x, xb, W, P, bias, l, B, S, next_gather):
    T = B * S
    sv = {"xb": xb, "bias": bias}
    sv["hg1"], sv["hu1"], sv["a1"] = ffn_up(xb, W["ffn1_w_gate"], W["ffn1_w_up"])
    y = mm_nn([(sv["a1"], W["ffn1_w_down"])], name="ffn_down")
    sv["u1"], x1, sv["x1b"] = ln_fwd(x, y, P["ln1_g"][l], P["ln1_b"][l], FFN_RESIDUAL)

    sv["h"] = proj_in(sv["x1b"], W["w_in"], B, S)
    sv["gA"] = P["sb_out_g"][l].reshape(1, -1)
    sv["gB"] = P["ca_out_g"][l].reshape(1, -1)
    (sv["oa"], ma, sv["cars"]), gathered = sb_fwd(sv["h"], sv["gA"], carry=next_gather)
    sv["ob"], mb = ca_fwd(sv["h"], bias, sv["gB"])
    sv["ma"], sv["mb"] = ma.reshape(T, -1), mb.reshape(T, -1)
    y = mm_nn(list(zip([sv["ma"], sv["mb"]], _row_blocks(W["w_out"], 2))), name="proj_out")
    sv["u2"], x2, sv["x2b"] = ln_fwd(x1, y, P["ln2_g"][l], P["ln2_b"][l], 1.0)

    sv["hg2"], sv["hu2"], sv["a2"] = ffn_up(sv["x2b"], W["ffn2_w_gate"], W["ffn2_w_up"])
    y = mm_nn([(sv["a2"], W["ffn2_w_down"])], name="ffn_down")
    sv["u3"], x3, x3b = ln_fwd(x2, y, P["ln3_g"][l], P["ln3_b"][l], FFN_RESIDUAL)
    return x3, x3b, sv, gathered


class _Riders:
    def __init__(self):
        self.received = []

    def plan(self, entries):
        return _Exchange([e[2] for e in entries]) if entries else None

    def landed(self, entries, slots):
        for (name, layer, _), r in zip(entries, slots or []):
            self.received.append((name, layer, r))


def _ffn_bwd(dr, dxa, xb, hg, hu, a, wg, wu, wd, riders, rides):
    dhg, dhu = ffn_bwd_mid(dr, wd, hg, hu)
    dx = mm_nn([(dhg, wg), (dhu, wu)], add=dxa, name="ffn_dx")
    grads = []
    for (lhs, rhs), ride in zip(((dhg, xb), (dhu, xb), (a, dr)), rides):
        g, slots = mm_tn(lhs, rhs, name="ffn_dw", carry=riders.plan(ride))
        riders.landed(ride, slots)
        grads.append(g)
    return dx, grads


def _layer_bwd(dx, sv, W, P, l, B, S, riders, pending):
    T = B * S
    G = {}
    dxa, dr, dg, db = ln_bwd(sv["u3"], dx, P["ln3_g"][l], FFN_RESIDUAL)
    G["ln3_g"], G["ln3_b"] = dg.sum(0), db.sum(0)
    rides = [[e] for e in pending] + [[]] * (3 - len(pending))
    dx, (gg, gu, gd) = _ffn_bwd(dr, dxa, sv["x2b"], sv["hg2"], sv["hu2"], sv["a2"], W["ffn2_w_gate"], W["ffn2_w_up"],
                                W["ffn2_w_down"], riders, rides)

    dxa, dr, dg, db = ln_bwd(sv["u2"], dx, P["ln2_g"][l], 1.0)
    G["ln2_g"], G["ln2_b"] = dg.sum(0), db.sum(0)
    dmix = mm_nt(dr, W["w_out"], name="proj_out_dx").reshape(B, S, -1)
    g_out = jnp.concatenate([mm_tn(sv["ma"], dr, name="proj_out_dw")[0], mm_tn(sv["mb"], dr, name="proj_out_dw")[0]])
    ride = [("ffn2_w_gate", l, gg), ("ffn2_w_up", l, gu)]
    (dqa, dka, dva, dga), slots = sb_bwd(sv["h"], sv["oa"], dmix, sv["gA"], sv["cars"], carry=riders.plan(ride))
    riders.landed(ride, slots)
    ride = [("ffn2_w_down", l, gd)]
    (dqb, dkb, dvb, dgb, dbias), slots = ca_bwd(sv["h"], sv["bias"], sv["ob"], dmix, sv["gB"], carry=riders.plan(ride))
    riders.landed(ride, slots)
    G["sb_out_g"] = dga.sum((0, 1, 2))
    G["ca_out_g"] = dgb.sum((0, 1, 2))
    G["rel_bias"] = rel_bias_grad(dbias)
    dh = [t.reshape(T, -1) for t in (dqa, dka, dva, dqb, dkb, dvb)]
    dx = mm_nn(list(zip(dh, _row_blocks(W["w_in"], 6))), add=dxa, name="proj_in_dx")
    g_in = jnp.concatenate([mm_tn(t, sv["x1b"], name="proj_in_dw")[0] for t in dh])

    dxa, dr, dg, db = ln_bwd(sv["u1"], dx, P["ln1_g"][l], FFN_RESIDUAL)
    G["ln1_g"], G["ln1_b"] = dg.sum(0), db.sum(0)
    rides = [[("w_in", l, g_in)], [("w_out", l, g_out)], []]
    dx, (gg, gu, gd) = _ffn_bwd(dr, dxa, sv["xb"], sv["hg1"], sv["hu1"], sv["a1"], W["ffn1_w_gate"], W["ffn1_w_up"],
                                W["ffn1_w_down"], riders, rides)
    return dx, G, [("ffn1_w_gate", l, gg), ("ffn1_w_up", l, gu), ("ffn1_w_down", l, gd)]


def kernel(x, ffn1_w_gate, ffn1_w_up, ffn1_w_down, ln1_g, ln1_b, w_in, rel_bias, sb_out_g, ca_out_g, w_out, ln2_g, ln2_b, ffn2_w_gate, ffn2_w_up, ffn2_w_down, ln3_g, ln3_b, loss_target, m_ffn1_w_gate, m_ffn1_w_up, m_ffn1_w_down, m_ln1_g, m_ln1_b, m_w_in, m_rel_bias, m_sb_out_g, m_ca_out_g, m_w_out, m_ln2_g, m_ln2_b, m_ffn2_w_gate, m_ffn2_w_up, m_ffn2_w_down, m_ln3_g, m_ln3_b, v_ffn1_w_gate, v_ffn1_w_up, v_ffn1_w_down, v_ln1_g, v_ln1_b, v_w_in, v_rel_bias, v_sb_out_g, v_ca_out_g, v_w_out, v_ln2_g, v_ln2_b, v_ffn2_w_gate, v_ffn2_w_up, v_ffn2_w_down, v_ln3_g, v_ln3_b):
    given = dict(locals())
    P = {n: given[n] for n in WEIGHTS}
    M = {n: given["m_" + n] for n in WEIGHTS}
    V = {n: given["v_" + n] for n in WEIGHTS}
    B, S, D = x.shape
    L = ln1_g.shape[0]

    local = {n: (jnp.swapaxes(P[n], 1, 2) if n in TRANSPOSED else P[n]).astype(BF16) for n in BIG}
    bias = ca_bias(rel_bias)

    xf = x.reshape(B * S, D)
    xb = xf.astype(BF16)
    W, saved = [], []
    gathered = _run_plan(_Gather([local[n][0] for n in BIG]), "gather_weights")
    for l in range(L):
        W.append(dict(zip(BIG, gathered)))
        next_gather = _Gather([local[n][l + 1] for n in BIG]) if l + 1 < L else None
        xf, xb, sv, gathered = _layer_fwd(xf, xb, W[l], P, bias[l], l, B, S, next_gather)
        saved.append(sv)
    dx, loss_part = loss_head(xf, loss_target.reshape(B * S, D))

    big_g = {n: [None] * L for n in BIG}
    small_g = {n: [None] * L for n in SMALL}
    riders, pending = _Riders(), []
    for l in reversed(range(L)):
        dx, G, pending = _layer_bwd(dx, saved[l], W[l], P, l, B, S, riders, pending)
        for n in SMALL:
            small_g[n][l] = G[n]
    riders.landed(pending, _run_plan(riders.plan(pending), "exchange_grads"))
    for n, l, slots in riders.received:
        big_g[n][l] = sum_slots(slots)

    grads = {}
    for n in BIG:
        g = jnp.stack(big_g[n])
        grads[n] = jnp.swapaxes(g, 1, 2) if n in TRANSPOSED else g
    small_like = [P[n] for n in SMALL]
    packed = _pack([jnp.stack(small_g[n]) for n in SMALL] + [loss_part.sum()])
    total = allreduce_small(packed)
    *small_vals, loss = _unpack(total, small_like + [jnp.zeros((), F32)])
    grads.update(dict(zip(SMALL, small_vals)))

    delta, new_m, new_v = {}, {}, {}
    for n in BIG:
        shape = P[n].shape
        two_d = lambda a: a.reshape(shape[0] * shape[1], shape[2])
        d, m, v = adamw(two_d(P[n]), two_d(grads[n]), two_d(M[n]), two_d(V[n]))
        delta[n], new_m[n], new_v[n] = d.reshape(shape), m.reshape(shape), v.reshape(shape)
    one = jnp.ones((), F32)
    d, m, v = adamw(_pack(small_like + [one]), total, _pack([M[n] for n in SMALL] + [one]), _pack([V[n] for n in SMALL] + [one]))
    for dst, src in ((delta, d), (new_m, m), (new_v, v)):
        dst.update(dict(zip(SMALL, _unpack(src, small_like))))

    return (loss, dx.reshape(B, S, D), *[grads[n] for n in WEIGHTS], *[delta[n] for n in WEIGHTS],
            *[new_m[n] for n in WEIGHTS], *[new_v[n] for n in WEIGHTS])
```

```python
import math

import jax
import jax.numpy as jnp
import numpy as np
from jax import lax
from jax.experimental import pallas as pl
from jax.experimental.pallas import tpu as pltpu

F32 = jnp.float32
BF16 = jnp.bfloat16

HEAD_DIM = 64
CHUNK = 64
N_PREV_CHUNKS = 8
MAX_REL = 128
DEPTH = 4
FFN_RESIDUAL = 0.5
ALPHA = (2 * DEPTH) ** 0.25
LN_EPS = 1e-5
RMS_EPS = 1e-6
ADAM_LR = 0.001
ADAM_B1 = 0.9
ADAM_B2 = 0.999
ADAM_EPS = 1e-08
ADAM_WD = 0.01
ADAM_STEP = 10

N_DEV = 8
LANES = 128
VMEM_LIMIT_BYTES = 56 * 1024 * 1024
ROW_TILE = 512
SB_BLOCK = 128
CA_PAD = CHUNK * N_PREV_CHUNKS
CA_TQ = 2 * CHUNK
CA_WIN = CA_PAD + CA_TQ
FWD_HEADS = 8
SB_BWD_HEADS = 8
CA_BWD_HEADS = 4
NEG = -1e30
SB_SKIP_LOG = -104.0

NN = ((1,), (0,))
NT = ((1,), (1,))
TN = ((0,), (0,))


def _dot(a, b, dims):
    return lax.dot_general(a, b, (dims, ((), ())), preferred_element_type=F32)


def _params(*sem):
    return pltpu.CompilerParams(dimension_semantics=sem, vmem_limit_bytes=VMEM_LIMIT_BYTES)


def _sds(shape, dtype):
    return jax.ShapeDtypeStruct(shape, dtype)


def _row_tile(n, want=ROW_TILE):
    t = min(want, n)
    while n % t:
        t //= 2
    assert t >= 8, (n, want)
    return t


def ffn_up(xb, wg, wu):
    T, D = xb.shape
    F = wg.shape[0]
    tm = _row_tile(T)
    fc = _row_tile(F, 256)

    def body(x_ref, wg_ref, wu_ref, hg_ref, hu_ref, a_ref):
        x = x_ref[...]
        for j in range(F // fc):
            sl = pl.ds(j * fc, fc)
            hg = _dot(x, wg_ref[sl, :], NT)
            hu = _dot(x, wu_ref[sl, :], NT)
            hg_ref[:, sl] = hg.astype(BF16)
            hu_ref[:, sl] = hu.astype(BF16)
            a_ref[:, sl] = (hg * jax.nn.sigmoid(hg) * hu).astype(BF16)

    row = pl.BlockSpec((tm, F), lambda i: (i, 0))
    w = pl.BlockSpec((F, D), lambda i: (0, 0))
    return pl.pallas_call(
        body, grid=(T // tm,), name="ffn_up",
        in_specs=[pl.BlockSpec((tm, D), lambda i: (i, 0)), w, w],
        out_specs=[row, row, row], out_shape=[_sds((T, F), BF16)] * 3,
        compiler_params=_params("parallel"),
    )(xb, wg, wu)


def ffn_bwd_mid(dr, wd, hg, hu):
    T, D = dr.shape
    F = wd.shape[0]
    tm = _row_tile(T)
    fc = _row_tile(F, 256)

    def body(dr_ref, wd_ref, hg_ref, hu_ref, dhg_ref, dhu_ref):
        dr_ = dr_ref[...]
        for j in range(F // fc):
            sl = pl.ds(j * fc, fc)
            da = _dot(dr_, wd_ref[sl, :], NT)
            g = hg_ref[:, sl].astype(F32)
            u = hu_ref[:, sl].astype(F32)
            s = jax.nn.sigmoid(g)
            gs = g * s
            dhu_ref[:, sl] = (da * gs).astype(BF16)
            dhg_ref[:, sl] = (da * u * (s + gs * (1.0 - s))).astype(BF16)

    row = pl.BlockSpec((tm, F), lambda i: (i, 0))
    return pl.pallas_call(
        body, grid=(T // tm,), name="ffn_bwd_mid",
        in_specs=[pl.BlockSpec((tm, D), lambda i: (i, 0)), pl.BlockSpec((F, D), lambda i: (0, 0)), row, row],
        out_specs=[row, row], out_shape=[_sds((T, F), BF16)] * 2,
        compiler_params=_params("parallel"),
    )(dr, wd, hg, hu)


def mm_nn(pairs, add=None, out_dtype=F32, name="mm_nn"):
    M = pairs[0][0].shape[0]
    N = pairs[0][1].shape[1]
    tm = _row_tile(M)
    nc = _row_tile(N, 512)
    n_pairs = len(pairs)

    def body(*refs):
        a_refs = refs[:n_pairs]
        b_refs = refs[n_pairs:2 * n_pairs]
        add_ref = refs[2 * n_pairs] if add is not None else None
        o_ref = refs[-1]
        for j in range(N // nc):
            sl = pl.ds(j * nc, nc)
            acc = _dot(a_refs[0][...], b_refs[0][:, sl], NN)
            for a_ref, b_ref in zip(a_refs[1:], b_refs[1:]):
                acc = acc + _dot(a_ref[...], b_ref[:, sl], NN)
            if add_ref is not None:
                acc = acc + add_ref[:, sl]
            o_ref[:, sl] = acc.astype(out_dtype)

    in_specs = [pl.BlockSpec((tm, a.shape[1]), lambda i: (i, 0)) for a, _ in pairs]
    in_specs += [pl.BlockSpec(b.shape, lambda i: (0, 0)) for _, b in pairs]
    args = [a for a, _ in pairs] + [b for _, b in pairs]
    if add is not None:
        in_specs.append(pl.BlockSpec((tm, N), lambda i: (i, 0)))
        args.append(add)
    return pl.pallas_call(
        body, grid=(M // tm,), name=name, in_specs=in_specs,
        out_specs=pl.BlockSpec((tm, N), lambda i: (i, 0)), out_shape=_sds((M, N), out_dtype),
        compiler_params=_params("parallel"),
    )(*args)


def _mm_specs(pairs, tm):
    specs = [pl.BlockSpec((tm, a.shape[1]), lambda i: (i, 0)) for a, _ in pairs]
    specs += [pl.BlockSpec(b.shape, lambda i: (0, 0), pipeline_mode=pl.Buffered(1)) for _, b in pairs]
    return specs, [a for a, _ in pairs] + [b for _, b in pairs]


def _mm_sum(a_refs, b_refs):
    acc = _dot(a_refs[0][...], b_refs[0][...], NN)
    for a_ref, b_ref in zip(a_refs[1:], b_refs[1:]):
        acc = acc + _dot(a_ref[...], b_ref[...], NN)
    return acc


def _ln_fwd_rows(u, g, b):
    mu = jnp.mean(u, axis=-1, keepdims=True)
    xc = u - mu
    var = jnp.mean(xc * xc, axis=-1, keepdims=True)
    return xc * lax.rsqrt(var + LN_EPS) * g + b


def _ln_bwd_rows(u, dy, g):
    mu = jnp.mean(u, axis=-1, keepdims=True)
    xc = u - mu
    var = jnp.mean(xc * xc, axis=-1, keepdims=True)
    rstd = lax.rsqrt(var + LN_EPS)
    xh = xc * rstd
    dxh = dy * g
    m1 = jnp.mean(dxh, axis=-1, keepdims=True)
    m2 = jnp.mean(dxh * xh, axis=-1, keepdims=True)
    return rstd * (dxh - m1 - xh * m2), dy * xh


def mm_nn_ln(pairs, x, g, b, res_scale, name):
    M, N = x.shape
    tm = _row_tile(M)
    n_pairs = len(pairs)

    def body(*refs):
        a_refs, b_refs = refs[:n_pairs], refs[n_pairs:2 * n_pairs]
        x_ref, g_ref, b_ref, u_ref, y_ref, yb_ref = refs[2 * n_pairs:]
        u = ALPHA * x_ref[...] + res_scale * _mm_sum(a_refs, b_refs)
        y = _ln_fwd_rows(u, g_ref[...], b_ref[...])
        u_ref[...] = u
        y_ref[...] = y
        yb_ref[...] = y.astype(BF16)

    in_specs, args = _mm_specs(pairs, tm)
    row = pl.BlockSpec((tm, N), lambda i: (i, 0))
    vec = pl.BlockSpec((1, N), lambda i: (0, 0))
    return pl.pallas_call(
        body, grid=(M // tm,), name=name, in_specs=in_specs + [row, vec, vec],
        out_specs=[row, row, row], out_shape=[_sds((M, N), F32), _sds((M, N), F32), _sds((M, N), BF16)],
        compiler_params=_params("parallel"),
    )(*args, x, g.reshape(1, N), b.reshape(1, N))


def mm_nn_ln_bwd(pairs, add, u, g, branch_scale, name):
    M, N = u.shape
    tm = _row_tile(M)
    n_pairs = len(pairs)

    def body(*refs):
        a_refs, b_refs = refs[:n_pairs], refs[n_pairs:2 * n_pairs]
        add_ref, u_ref, g_ref, dxa_ref, dr_ref, dg_ref, db_ref = refs[2 * n_pairs:]

        @pl.when(pl.program_id(0) == 0)
        def _():
            dg_ref[...] = jnp.zeros_like(dg_ref)
            db_ref[...] = jnp.zeros_like(db_ref)

        dy = _mm_sum(a_refs, b_refs) + add_ref[...]
        du, dyxh = _ln_bwd_rows(u_ref[...], dy, g_ref[...])
        dxa_ref[...] = ALPHA * du
        dr_ref[...] = (branch_scale * du).astype(BF16)
        dg_ref[...] += jnp.sum(dyxh.reshape(tm // 8, 8, N), axis=0)
        db_ref[...] += jnp.sum(dy.reshape(tm // 8, 8, N), axis=0)

    in_specs, args = _mm_specs(pairs, tm)
    row = pl.BlockSpec((tm, N), lambda i: (i, 0))
    acc = pl.BlockSpec((8, N), lambda i: (0, 0))
    return pl.pallas_call(
        body, grid=(M // tm,), name=name, in_specs=in_specs + [row, row, pl.BlockSpec((1, N), lambda i: (0, 0))],
        out_specs=[row, row, acc, acc],
        out_shape=[_sds((M, N), F32), _sds((M, N), BF16), _sds((8, N), F32), _sds((8, N), F32)],
        compiler_params=_params("arbitrary"),
    )(*args, add, u, g.reshape(1, N))


def mm_nt(a, b, out_dtype=F32, name="mm_nt"):
    M, K = a.shape
    N = b.shape[0]
    tm = _row_tile(M)
    nc = _row_tile(N, 512)

    def body(a_ref, b_ref, o_ref):
        a_ = a_ref[...]
        for j in range(N // nc):
            sl = pl.ds(j * nc, nc)
            o_ref[:, sl] = _dot(a_, b_ref[sl, :], NT).astype(out_dtype)

    return pl.pallas_call(
        body, grid=(M // tm,), name=name,
        in_specs=[pl.BlockSpec((tm, K), lambda i: (i, 0)), pl.BlockSpec((N, K), lambda i: (0, 0))],
        out_specs=pl.BlockSpec((tm, N), lambda i: (i, 0)), out_shape=_sds((M, N), out_dtype),
        compiler_params=_params("parallel"),
    )(a, b)


def proj_in(xb, w, B, S):
    D = xb.shape[1]
    N = w.shape[0]
    tm = CA_PAD
    nblk = S // tm
    nc = _row_tile(N, 512)
    assert S % tm == 0

    def body(x_ref, w_ref, o_ref):
        i = pl.program_id(1)

        @pl.when(i == 0)
        def _():
            o_ref[...] = jnp.zeros_like(o_ref)

        @pl.when(i > 0)
        def _():
            x = x_ref[...]
            for j in range(N // nc):
                sl = pl.ds(j * nc, nc)
                o_ref[0, :, sl] = _dot(x, w_ref[sl, :], NT).astype(BF16)

    return pl.pallas_call(
        body, grid=(B, nblk + 1), name="proj_in",
        in_specs=[pl.BlockSpec((tm, D), lambda b, i: (b * nblk + jnp.maximum(i - 1, 0), 0)),
                  pl.BlockSpec((N, D), lambda b, i: (0, 0))],
        out_specs=pl.BlockSpec((1, tm, N), lambda b, i: (b, i, 0)), out_shape=_sds((B, CA_PAD + S, N), BF16),
        compiler_params=_params("parallel", "arbitrary"),
    )(xb, w)


def mm_tn(parts, b, name="mm_tn", carry=None):
    T, N = b.shape
    widths = [a.shape[1] for a in parts]
    M = sum(widths)
    tk = _row_tile(T)
    steps = T // tk
    n_parts = len(parts)

    def body(*refs):
        a_refs, b_ref, o_ref, acc_ref = refs[:n_parts], refs[n_parts], refs[n_parts + 1], refs[n_parts + 2]
        i = pl.program_id(0)

        @pl.when(i == 0)
        def _():
            acc_ref[...] = jnp.zeros_like(acc_ref)

        b_ = b_ref[...]
        row = 0
        for a_ref, width in zip(a_refs, widths):
            mc = _row_tile(width, 256)
            for j in range(width // mc):
                acc_ref[pl.ds(row, mc), :] += _dot(a_ref[:, pl.ds(j * mc, mc)], b_, TN)
                row += mc

        @pl.when(i == steps - 1)
        def _():
            o_ref[...] = acc_ref[...].astype(BF16)

    (out,), got = _call(
        body, carry, grid=(steps,), name=name, sem=("arbitrary",), args=(*parts, b),
        in_specs=[pl.BlockSpec((tk, w), lambda i: (i, 0)) for w in widths] + [pl.BlockSpec((tk, N), lambda i: (i, 0))],
        out_specs=[pl.BlockSpec((M, N), lambda i: (0, 0))], out_shape=[_sds((M, N), BF16)],
        scratch_shapes=[pltpu.VMEM((M, N), F32)])
    return out, got


def ffn_dw(dhg, dhu, a, xb, dr, carry=None):
    T, F = dhg.shape
    D = xb.shape[1]
    tk = _row_tile(T)
    mc = _row_tile(F, 256)
    steps = T // tk
    last = steps - 1

    def body(dhg_ref, dhu_ref, a_ref, xb_ref, dr_ref, o_ref, acc_ref):
        p = pl.program_id(0)
        i = pl.program_id(1)

        @pl.when(i == 0)
        def _():
            acc_ref[...] = jnp.zeros_like(acc_ref)

        def accumulate(lhs_ref, rhs_ref):
            rhs = rhs_ref[...]
            for j in range(F // mc):
                sl = pl.ds(j * mc, mc)
                acc_ref[sl, :] += _dot(lhs_ref[:, sl], rhs, TN)

        pl.when(p == 0)(lambda: accumulate(dhg_ref, xb_ref))
        pl.when(p == 1)(lambda: accumulate(dhu_ref, xb_ref))
        pl.when(p == 2)(lambda: accumulate(a_ref, dr_ref))

        @pl.when(i == last)
        def _():
            o_ref[0] = acc_ref[...].astype(BF16)

    def during(phases, width):
        lo, hi = phases
        return pl.BlockSpec((tk, width), lambda p, i: (jnp.where(p < lo, 0, jnp.where(p > hi, last, i)), 0))

    (out,), got = _call(
        body, carry, grid=(3, steps), name="ffn_dw", sem=("arbitrary", "arbitrary"), args=(dhg, dhu, a, xb, dr),
        in_specs=[during((0, 0), F), during((1, 1), F), during((2, 2), F), during((0, 1), D), during((2, 2), D)],
        out_specs=[pl.BlockSpec((1, F, D), lambda p, i: (p, 0, 0))], out_shape=[_sds((3, F, D), BF16)],
        scratch_shapes=[pltpu.VMEM((F, D), F32)])
    return out, got


def ln_bwd(u, dy, g, branch_scale):
    T, D = u.shape
    tm = _row_tile(T)

    def body(u_ref, dy_ref, g_ref, dxa_ref, dr_ref, dg_ref, db_ref):
        @pl.when(pl.program_id(0) == 0)
        def _():
            dg_ref[...] = jnp.zeros_like(dg_ref)
            db_ref[...] = jnp.zeros_like(db_ref)

        dy_ = dy_ref[...]
        du, dyxh = _ln_bwd_rows(u_ref[...], dy_, g_ref[...])
        dxa_ref[...] = ALPHA * du
        dr_ref[...] = (branch_scale * du).astype(BF16)
        dg_ref[...] += jnp.sum(dyxh.reshape(tm // 8, 8, D), axis=0)
        db_ref[...] += jnp.sum(dy_.reshape(tm // 8, 8, D), axis=0)

    row = pl.BlockSpec((tm, D), lambda i: (i, 0))
    acc = pl.BlockSpec((8, D), lambda i: (0, 0))
    dxa, dr, dg, db = pl.pallas_call(
        body, grid=(T // tm,), name="ln_bwd", in_specs=[row, row, pl.BlockSpec((1, D), lambda i: (0, 0))],
        out_specs=[row, row, acc, acc],
        out_shape=[_sds((T, D), F32), _sds((T, D), BF16), _sds((8, D), F32), _sds((8, D), F32)],
        compiler_params=_params("arbitrary"),
    )(u, dy, g.reshape(1, D))
    return dxa, dr, dg, db


def loss_head(y, target):
    T, D = y.shape
    tm = _row_tile(T)

    def body(y_ref, t_ref, dy_ref, l_ref):
        @pl.when(pl.program_id(0) == 0)
        def _():
            l_ref[...] = jnp.zeros_like(l_ref)

        e = y_ref[...] - t_ref[...]
        dy_ref[...] = e * (1.0 / D)
        l_ref[...] += jnp.sum((e * e).reshape(tm // 8, 8, D), axis=0) * (0.5 / D)

    row = pl.BlockSpec((tm, D), lambda i: (i, 0))
    return pl.pallas_call(
        body, grid=(T // tm,), name="loss_head", in_specs=[row, row],
        out_specs=[row, pl.BlockSpec((8, D), lambda i: (0, 0))],
        out_shape=[_sds((T, D), F32), _sds((8, D), F32)],
        compiler_params=_params("arbitrary"),
    )(y, target)


def adamw(w, g, m, v):
    R, C = w.shape
    tr = R
    for cand in (512, 256, 128, 64, 32, 16, 8):
        if R % cand == 0:
            tr = cand
            break
    c1 = 1.0 - ADAM_B1 ** ADAM_STEP
    c2 = 1.0 - ADAM_B2 ** ADAM_STEP

    def body(w_ref, g_ref, m_ref, v_ref, d_ref, mo_ref, vo_ref):
        g_ = g_ref[...]
        m_ = ADAM_B1 * m_ref[...] + (1.0 - ADAM_B1) * g_
        v_ = ADAM_B2 * v_ref[...] + (1.0 - ADAM_B2) * (g_ * g_)
        m_hat = m_ / c1
        v_hat = v_ / c2
        d_ref[...] = -ADAM_LR * (m_hat / (jnp.sqrt(v_hat) + ADAM_EPS) + ADAM_WD * w_ref[...])
        mo_ref[...] = m_
        vo_ref[...] = v_

    blk = pl.BlockSpec((tr, C), lambda i: (i, 0))
    return pl.pallas_call(
        body, grid=(R // tr,), name="adamw", in_specs=[blk] * 4, out_specs=[blk] * 3,
        out_shape=[_sds((R, C), F32)] * 3, compiler_params=_params("parallel"),
    )(w, g, m, v)


def _lane_lo(rows):
    return lax.broadcasted_iota(jnp.int32, (rows, LANES), 1) < HEAD_DIM


def _pair_mean(x, lo):
    s0 = jnp.sum(jnp.where(lo, x, 0.0), axis=-1, keepdims=True)
    s1 = jnp.sum(jnp.where(lo, 0.0, x), axis=-1, keepdims=True)
    return jnp.where(lo, s0, s1) * (1.0 / HEAD_DIM)


def _rms_fwd(o, gain, lo):
    r = lax.rsqrt(_pair_mean(o * o, lo) + RMS_EPS)
    return (o * r * gain).astype(BF16)


def _rms_bwd(o, dm, gain, lo):
    r = lax.rsqrt(_pair_mean(o * o, lo) + RMS_EPS)
    oh = o * r
    dg = jnp.sum(dm * oh, axis=0, keepdims=True)
    doh = dm * gain
    do = r * (doh - oh * _pair_mean(doh * oh, lo))
    return do, dg


def _split_heads(x, lo):
    zero = jnp.zeros_like(x)
    return [jnp.where(lo, x, zero), jnp.where(lo, zero, x)]


def _merge_pairs(per_head, lo):
    return [jnp.where(lo, per_head[2 * p], per_head[2 * p + 1]) for p in range(len(per_head) // 2)]


def _pair_cols(h):
    return slice((h // 2) * LANES, (h // 2 + 1) * LANES)


def _split_dot(x, tri):
    n = x.shape[0]
    hi = x.astype(BF16)
    lo = (x - hi.astype(F32)).astype(BF16)
    both = _dot(jnp.concatenate([hi, lo], axis=0), tri, NN)
    return both[:n] + both[n:]


def _log_keep(z):
    return -(jnp.maximum(z, 0.0) + jnp.log(1.0 + jnp.exp(-jnp.abs(z))))


def _attn_dims(h3p, gain, heads):
    B, SP, C = h3p.shape
    HD = gain.shape[1]
    hs = min(heads, HD // HEAD_DIM)
    W = hs * HEAD_DIM
    assert C == 6 * HD and W % LANES == 0 and HD % W == 0
    return B, SP - CA_PAD, HD, hs, W, HD // W


def sb_fwd(h3p, gain, carry=None):
    B, S, HD, hs, W, ngrp = _attn_dims(h3p, gain, FWD_HEADS)
    tb = min(SB_BLOCK, S)
    nq = S // tb
    off = CA_PAD // tb
    scale = 1.0 / math.sqrt(HEAD_DIM)

    def body(q_ref, k_ref, v_ref, g_ref, o_ref, m_ref, c_ref):
        qi = pl.program_id(2)
        lo = _lane_lo(tb)
        row = lax.broadcasted_iota(jnp.int32, (tb, tb), 0)
        col = lax.broadcasted_iota(jnp.int32, (tb, tb), 1)
        rev_incl = (row >= col).astype(BF16)
        lane = lax.broadcasted_iota(jnp.int32, (tb, nq), 1)
        qm = []
        for p in range(hs // 2):
            qm += _split_heads((q_ref[0, :, p * LANES:(p + 1) * LANES].astype(F32) * scale).astype(BF16), lo)

        def blocks(kb, carries_, mask):
            ks = pl.ds(pl.multiple_of(kb * tb + CA_PAD, tb), tb)
            hh = range(hs)
            zs = [_dot(qm[h], k_ref[0, ks, _pair_cols(h)], NT) for h in hh]
            lks = [_log_keep(z) for z in zs]
            if mask is not None:
                lks = [jnp.where(mask, lk, 0.0) for lk in lks]
            cums = [_split_dot(lk, rev_incl) for lk in lks]
            ws = []
            for h in hh:
                logw = zs[h] + cums[h] + carries_[h]
                if mask is not None:
                    logw = jnp.where(mask, logw, NEG)
                ws.append(jnp.exp(logw).astype(BF16))
            pvs = [_dot(ws[h], v_ref[0, ks, _pair_cols(h)], NN) for h in hh]
            return [(pvs[h], cums[h][:, 0:1]) for h in hh]

        diag = blocks(qi, [jnp.zeros((tb, 1), F32)] * hs, col < row)
        accs = _merge_pairs([d[0] for d in diag], lo)
        carries = [d[1] for d in diag]
        cars = [jnp.where(lane == qi, 0.0, NEG)] * hs

        def cond(st):
            kb, carries_, _, _ = st
            top = carries_[0]
            for c in carries_[1:]:
                top = jnp.maximum(top, c)
            return jnp.logical_and(kb >= 0, jnp.max(top) > SB_SKIP_LOG)

        def step(st):
            kb, carries_, accs_, cars_ = st
            out = blocks(kb, carries_, None)
            pv = _merge_pairs([o[0] for o in out], lo)
            return (kb - 1, [c + o[1] for c, o in zip(carries_, out)], [a + p for a, p in zip(accs_, pv)],
                    [jnp.where(lane == kb, c, cs) for c, cs in zip(carries_, cars_)])

        _, _, accs, cars = lax.while_loop(cond, step, (qi - 1, carries, accs, cars))
        for p, acc in enumerate(accs):
            cols = slice(p * LANES, (p + 1) * LANES)
            o_ref[0, :, cols] = acc
            m_ref[0, :, cols] = _rms_fwd(acc, g_ref[:, cols], lo)
        for h in range(hs):
            c_ref[0, h] = cars[h]

    qspec = pl.BlockSpec((1, tb, W), lambda g, b, i: (b, i + off, g))
    ospec = pl.BlockSpec((1, tb, W), lambda g, b, i: (b, i, g))
    return _call(
        body, carry, grid=(ngrp, B, nq), name="sb_fwd", sem=("parallel", "parallel", "arbitrary"),
        args=(h3p, h3p, h3p, gain),
        in_specs=[qspec, pl.BlockSpec((1, CA_PAD + S, W), lambda g, b, i: (b, 0, ngrp + g)),
                  pl.BlockSpec((1, CA_PAD + S, W), lambda g, b, i: (b, 0, 2 * ngrp + g)),
                  pl.BlockSpec((1, W), lambda g, b, i: (0, g))],
        out_specs=[ospec, ospec, pl.BlockSpec((1, hs, tb, nq), lambda g, b, i: (b, g, i, 0))],
        out_shape=[_sds((B, S, HD), F32), _sds((B, S, HD), BF16), _sds((B, HD // HEAD_DIM, S, nq), F32)])


def sb_bwd(h3p, o, dmix, gain, cars, carry=None):
    B, S, HD, hs, W, ngrp = _attn_dims(h3p, gain, SB_BWD_HEADS)
    tb = min(SB_BLOCK, S)
    nq = S // tb
    off = CA_PAD // tb
    scale = 1.0 / math.sqrt(HEAD_DIM)

    def body(q_ref, k_ref, v_ref, o_ref, dm_ref, g_ref, c_ref, dq_ref, dk_ref, dv_ref, dg_ref, dk_acc, dv_acc):
        qi = pl.program_id(2)

        @pl.when(qi == 0)
        def _():
            dk_acc[...] = jnp.zeros_like(dk_acc)
            dv_acc[...] = jnp.zeros_like(dv_acc)

        lo = _lane_lo(tb)
        row = lax.broadcasted_iota(jnp.int32, (tb, tb), 0)
        col = lax.broadcasted_iota(jnp.int32, (tb, tb), 1)
        rev_incl = (row >= col).astype(BF16)
        fwd_incl = (row <= col).astype(BF16)
        lane = lax.broadcasted_iota(jnp.int32, (tb, nq), 1)
        below = lax.broadcasted_iota(jnp.int32, (1, nq), 1) < qi
        qm, dom, cars_, seen = [], [], [], None
        for p in range(hs // 2):
            cols = slice(p * LANES, (p + 1) * LANES)
            qm += _split_heads((q_ref[0, :, cols].astype(F32) * scale).astype(BF16), lo)
            do, dg = _rms_bwd(o_ref[0, :, cols], dm_ref[0, :, cols], g_ref[:, cols], lo)
            dg_ref[0, 0, :, cols] = dg
            dom += _split_heads(do.astype(BF16), lo)
        for h in range(hs):
            cars_.append(c_ref[0, h])
            visited = jnp.logical_and(jnp.max(cars_[h], axis=0, keepdims=True) > SB_SKIP_LOG, below)
            n = jnp.sum(visited.astype(jnp.int32), axis=1, keepdims=True)
            seen = n if seen is None else jnp.maximum(seen, n)
        first = qi - jnp.max(seen)

        def blocks(kb, gsums, dqs, mask):
            ks = pl.ds(pl.multiple_of(kb * tb + CA_PAD, tb), tb)
            ko = pl.ds(pl.multiple_of(kb * tb, tb), tb)
            hh = range(hs)
            kk = [k_ref[0, ks, p * LANES:(p + 1) * LANES] for p in range(hs // 2)]
            vv = [v_ref[0, ks, p * LANES:(p + 1) * LANES] for p in range(hs // 2)]
            zs = [_dot(qm[h], kk[h // 2], NT) for h in hh]
            dws = [_dot(dom[h], vv[h // 2], NT) for h in hh]
            raw = [_log_keep(z) for z in zs]
            lks = raw if mask is None else [jnp.where(mask, lk, 0.0) for lk in raw]
            cums = [_split_dot(lk, rev_incl) for lk in lks]
            ws = []
            for h in hh:
                carry = jnp.sum(jnp.where(lane == kb, cars_[h], 0.0), axis=1, keepdims=True)
                logw = zs[h] + cums[h] + carry
                if mask is not None:
                    logw = jnp.where(mask, logw, NEG)
                ws.append(jnp.exp(logw))
            gws = [ws[h] * dws[h] for h in hh]
            gcums = [_split_dot(gws[h], fwd_incl) + gsums[h] for h in hh]
            dzb = []
            for h in hh:
                dz = gws[h] - jnp.exp(zs[h] + raw[h]) * gcums[h]
                if mask is not None:
                    dz = jnp.where(mask, dz, 0.0)
                dzb.append(dz.astype(BF16))
            wb = [w.astype(BF16) for w in ws]
            new_dq = [dqs[h] + _dot(dzb[h], kk[h // 2], NN) for h in hh]
            for p in range(hs // 2):
                cols = slice(p * LANES, (p + 1) * LANES)
                dk_acc[ko, cols] += _dot(dzb[2 * p], qm[2 * p], TN) + _dot(dzb[2 * p + 1], qm[2 * p + 1], TN)
                dv_acc[ko, cols] += _dot(wb[2 * p], dom[2 * p], TN) + _dot(wb[2 * p + 1], dom[2 * p + 1], TN)
            return [g[:, tb - 1:tb] for g in gcums], new_dq

        def step(kb, st):
            return blocks(kb, st[0], st[1], None)

        init = ([jnp.zeros((tb, 1), F32)] * hs, [jnp.zeros((tb, LANES), F32)] * hs)
        gsum, dq = lax.fori_loop(first, qi, step, init)
        _, dq = blocks(qi, gsum, dq, col < row)
        for p, d in enumerate(_merge_pairs(dq, lo)):
            dq_ref[0, :, p * LANES:(p + 1) * LANES] = (d * scale).astype(BF16)

        @pl.when(qi == nq - 1)
        def _():
            dk_ref[0] = dk_acc[...].astype(BF16)
            dv_ref[0] = dv_acc[...].astype(BF16)

    once = pl.Buffered(1)
    qspec = pl.BlockSpec((1, tb, W), lambda g, b, i: (b, i + off, g))
    ospec = pl.BlockSpec((1, tb, W), lambda g, b, i: (b, i, g))
    kvout = pl.BlockSpec((1, S, W), lambda g, b, i: (b, 0, g), pipeline_mode=once)
    return _call(
        body, carry, grid=(ngrp, B, nq), name="sb_bwd", sem=("parallel", "parallel", "arbitrary"),
        args=(h3p, h3p, h3p, o, dmix, gain, cars),
        in_specs=[qspec, pl.BlockSpec((1, CA_PAD + S, W), lambda g, b, i: (b, 0, ngrp + g), pipeline_mode=once),
                  pl.BlockSpec((1, CA_PAD + S, W), lambda g, b, i: (b, 0, 2 * ngrp + g), pipeline_mode=once),
                  ospec, ospec, pl.BlockSpec((1, W), lambda g, b, i: (0, g)),
                  pl.BlockSpec((1, hs, tb, nq), lambda g, b, i: (b, g, i, 0))],
        out_specs=[ospec, kvout, kvout, pl.BlockSpec((1, 1, 1, W), lambda g, b, i: (b, i, 0, g))],
        out_shape=[_sds((B, S, HD), BF16), _sds((B, S, HD), BF16), _sds((B, S, HD), BF16), _sds((B, nq, 1, HD), F32)],
        scratch_shapes=[pltpu.VMEM((S, W), F32), pltpu.VMEM((S, W), F32)])


def _ca_rel_index():
    width = CA_WIN + CA_TQ
    c = np.arange(width)
    dj = np.where(c < CA_WIN, c, c - width)
    return np.clip(CA_PAD - dj, -MAX_REL, MAX_REL) + MAX_REL, width


def _ca_onehot():
    idx, _ = _ca_rel_index()
    return (idx[:, None] == np.arange(2 * MAX_REL + 1)[None, :]).astype(np.float32)


def ca_bias(rel_bias):
    _, width = _ca_rel_index()
    lead = rel_bias.shape[:-1]
    by_offset = jnp.dot(rel_bias, jnp.asarray(_ca_onehot().T), precision=lax.Precision.HIGHEST)
    tile = jnp.broadcast_to(by_offset[..., None, :], lead + (CA_TQ, width)).reshape(lead + (CA_TQ * width,))
    tile = tile[..., :CA_TQ * (width - 1)].reshape(lead + (CA_TQ, width - 1))[..., :CA_WIN]
    t = np.arange(CA_TQ)[:, None] // CHUNK * CHUNK
    j = np.arange(CA_WIN)[None, :]
    return jnp.where((j >= t) & (j < t + CA_PAD + CHUNK), tile, NEG)


def rel_bias_grad(db):
    H = db.shape[0]
    _, width = _ca_rel_index()
    x = jnp.pad(db, ((0, 0), (0, 0), (0, width - 1 - CA_WIN))).reshape(H, CA_TQ * (width - 1))
    x = jnp.pad(x, ((0, 0), (0, CA_TQ))).reshape(H, CA_TQ, width).sum(axis=1)
    return jnp.dot(x, jnp.asarray(_ca_onehot()), precision=lax.Precision.HIGHEST)


def _ca_scores(qm_h, kk, bias_h, valid):
    return jnp.where(valid, _dot(qm_h, kk, NT) + bias_h, NEG)


def _ca_softmax(s):
    e = jnp.exp(s - jnp.max(s, axis=-1, keepdims=True))
    return e * (1.0 / jnp.sum(e, axis=-1, keepdims=True))


def ca_fwd(h3p, bias, gain):
    B, S, HD, hs, W, ngrp = _attn_dims(h3p, gain, FWD_HEADS)
    scale = 1.0 / math.sqrt(HEAD_DIM)
    off = CA_PAD // CA_TQ

    def body(q_ref, k_ref, v_ref, b_ref, g_ref, o_ref, m_ref):
        q0 = pl.program_id(2) * CA_TQ
        ks = pl.ds(pl.multiple_of(q0, CA_TQ), CA_WIN)
        lo = _lane_lo(CA_TQ)
        valid = lax.broadcasted_iota(jnp.int32, (CA_TQ, CA_WIN), 1) + q0 >= CA_PAD
        qm, kk, vv = [], [], []
        for p in range(hs // 2):
            cols = slice(p * LANES, (p + 1) * LANES)
            qm += _split_heads((q_ref[0, :, cols].astype(F32) * scale).astype(BF16), lo)
            kk.append(k_ref[0, ks, cols])
            vv.append(v_ref[0, ks, cols])
        ss = [_ca_scores(qm[h], kk[h // 2], b_ref[h], valid) for h in range(hs)]
        ps = [_ca_softmax(s).astype(BF16) for s in ss]
        pv = [_dot(ps[h], vv[h // 2], NN) for h in range(hs)]
        for p, o in enumerate(_merge_pairs(pv, lo)):
            cols = slice(p * LANES, (p + 1) * LANES)
            o_ref[0, :, cols] = o
            m_ref[0, :, cols] = _rms_fwd(o, g_ref[:, cols], lo)

    ospec = pl.BlockSpec((1, CA_TQ, W), lambda g, b, i: (b, i, g))
    return pl.pallas_call(
        body, grid=(ngrp, B, S // CA_TQ), name="ca_fwd",
        in_specs=[pl.BlockSpec((1, CA_TQ, W), lambda g, b, i: (b, i + off, 3 * ngrp + g)),
                  pl.BlockSpec((1, CA_PAD + S, W), lambda g, b, i: (b, 0, 4 * ngrp + g)),
                  pl.BlockSpec((1, CA_PAD + S, W), lambda g, b, i: (b, 0, 5 * ngrp + g)),
                  pl.BlockSpec((hs, CA_TQ, CA_WIN), lambda g, b, i: (g, 0, 0)),
                  pl.BlockSpec((1, W), lambda g, b, i: (0, g))],
        out_specs=[ospec, ospec], out_shape=[_sds((B, S, HD), F32), _sds((B, S, HD), BF16)],
        compiler_params=_params("parallel", "parallel", "arbitrary"),
    )(h3p, h3p, h3p, bias, gain)


def ca_bwd(h3p, bias, o, dmix, gain, carry=None):
    B, S, HD, hs, W, ngrp = _attn_dims(h3p, gain, CA_BWD_HEADS)
    scale = 1.0 / math.sqrt(HEAD_DIM)
    nq = S // CA_TQ
    off = CA_PAD // CA_TQ

    def body(q_ref, k_ref, v_ref, b_ref, o_ref, dm_ref, g_ref, dq_ref, dk_ref, dv_ref, dg_ref, db_ref, dk_acc, dv_acc):
        bi = pl.program_id(1)
        qi = pl.program_id(2)

        @pl.when(qi == 0)
        def _():
            dk_acc[...] = jnp.zeros_like(dk_acc)
            dv_acc[...] = jnp.zeros_like(dv_acc)

        @pl.when(jnp.logical_and(qi == 0, bi == 0))
        def _():
            db_ref[...] = jnp.zeros_like(db_ref)

        q0 = qi * CA_TQ
        ks = pl.ds(pl.multiple_of(q0, CA_TQ), CA_WIN)
        lo = _lane_lo(CA_TQ)
        valid = lax.broadcasted_iota(jnp.int32, (CA_TQ, CA_WIN), 1) + q0 >= CA_PAD
        qm, dom, kk, vv = [], [], [], []
        for p in range(hs // 2):
            cols = slice(p * LANES, (p + 1) * LANES)
            qm += _split_heads((q_ref[0, :, cols].astype(F32) * scale).astype(BF16), lo)
            do, dg = _rms_bwd(o_ref[0, :, cols], dm_ref[0, :, cols], g_ref[:, cols], lo)
            dg_ref[0, 0, :, cols] = dg
            dom += _split_heads(do.astype(BF16), lo)
            kk.append(k_ref[0, ks, cols])
            vv.append(v_ref[0, ks, cols])
        hh = range(hs)
        ss = [_ca_scores(qm[h], kk[h // 2], b_ref[h], valid) for h in hh]
        dps = [_dot(dom[h], vv[h // 2], NT) for h in hh]
        ps = [_ca_softmax(s) for s in ss]
        dss = [ps[h] * (dps[h] - jnp.sum(ps[h] * dps[h], axis=-1, keepdims=True)) for h in hh]
        for h in hh:
            db_ref[h] += dss[h]
        dsb = [d.astype(BF16) for d in dss]
        pb = [p_.astype(BF16) for p_ in ps]
        dq = [_dot(dsb[h], kk[h // 2], NN) for h in hh]
        for p in range(hs // 2):
            cols = slice(p * LANES, (p + 1) * LANES)
            dk_acc[ks, cols] += _dot(dsb[2 * p], qm[2 * p], TN) + _dot(dsb[2 * p + 1], qm[2 * p + 1], TN)
            dv_acc[ks, cols] += _dot(pb[2 * p], dom[2 * p], TN) + _dot(pb[2 * p + 1], dom[2 * p + 1], TN)
        for p, d in enumerate(_merge_pairs(dq, lo)):
            dq_ref[0, :, p * LANES:(p + 1) * LANES] = (d * scale).astype(BF16)

        @pl.when(qi == nq - 1)
        def _():
            dk_ref[0] = dk_acc[CA_PAD:, :].astype(BF16)
            dv_ref[0] = dv_acc[CA_PAD:, :].astype(BF16)

    ospec = pl.BlockSpec((1, CA_TQ, W), lambda g, b, i: (b, i, g))
    kvout = pl.BlockSpec((1, S, W), lambda g, b, i: (b, 0, g))
    bspec = pl.BlockSpec((hs, CA_TQ, CA_WIN), lambda g, b, i: (g, 0, 0))
    return _call(
        body, carry, grid=(ngrp, B, nq), name="ca_bwd", sem=("parallel", "arbitrary", "arbitrary"),
        args=(h3p, h3p, h3p, bias, o, dmix, gain),
        in_specs=[pl.BlockSpec((1, CA_TQ, W), lambda g, b, i: (b, i + off, 3 * ngrp + g)),
                  pl.BlockSpec((1, CA_PAD + S, W), lambda g, b, i: (b, 0, 4 * ngrp + g)),
                  pl.BlockSpec((1, CA_PAD + S, W), lambda g, b, i: (b, 0, 5 * ngrp + g)),
                  bspec, ospec, pl.BlockSpec((1, CA_TQ, W), lambda g, b, i: (b, i, ngrp + g)),
                  pl.BlockSpec((1, W), lambda g, b, i: (0, g))],
        out_specs=[ospec, kvout, kvout, pl.BlockSpec((1, 1, 1, W), lambda g, b, i: (b, i, 0, g)), bspec],
        out_shape=[_sds((B, S, HD), BF16), _sds((B, S, HD), BF16), _sds((B, S, HD), BF16),
                   _sds((B, nq, 1, HD), F32), _sds((HD // HEAD_DIM, CA_TQ, CA_WIN), F32)],
        scratch_shapes=[pltpu.VMEM((CA_PAD + S, W), F32), pltpu.VMEM((CA_PAD + S, W), F32)])


_ANY = pl.BlockSpec(memory_space=pl.ANY)
_MESH = pl.DeviceIdType.MESH


def _mesh_pos():
    return lax.axis_index("x"), lax.axis_index("y"), lax.axis_index("c")


def _dev_index(p):
    return 4 * p[0] + 2 * p[1] + p[2]


def _flip(pos, k):
    return tuple(1 - v if (k >> (2 - a)) & 1 else v for a, v in enumerate(pos))


class _Plan:
    def __init__(self, operands):
        self.operands = list(operands)
        self.nm = len(self.operands)
        self.scratch = [pltpu.SemaphoreType.DMA((7, self.nm)), pltpu.SemaphoreType.DMA((7, self.nm)),
                        pltpu.SemaphoreType.DMA((self.nm,))]


class _Gather(_Plan):
    def __init__(self, mats):
        super().__init__(mats)
        self.rows = [m.shape[0] for m in mats]
        self.out_shape = [_sds((N_DEV * m.shape[0], m.shape[1]), m.dtype) for m in mats]

    def begin(self, ins, outs, sems):
        nm, rows = self.nm, self.rows
        send_sems, recv_sems, local_sems = sems
        x, y, c = _mesh_pos()
        me, sibling = (x, y, c), (x, y, 1 - c)
        chips = [(1 - x, y), (x, 1 - y), (1 - x, 1 - y)]

        def block(m, p):
            start = pl.multiple_of(_dev_index(p) * rows[m], 16)
            return outs[m].at[pl.ds(start, rows[m]), :]

        def copy(k, m, blk, to, src=None):
            return pltpu.make_async_remote_copy(
                src_ref=block(m, blk) if src is None else src, dst_ref=block(m, blk),
                send_sem=send_sems.at[k, m], recv_sem=recv_sems.at[k, m], device_id=to, device_id_type=_MESH)

        def mine():
            return [pltpu.make_async_copy(ins[m], block(m, me), local_sems.at[m]) for m in range(nm)]

        def first():
            own = [copy(0, m, me, sibling, src=ins[m]) for m in range(nm)]
            return own + [copy(1 + j, m, me, (*chip, c), src=ins[m]) for j, chip in enumerate(chips) for m in range(nm)]

        def start():
            for cp in mine() + first():
                cp.start()

        def finish():
            passed = []
            for j, chip in enumerate(chips):
                for m in range(nm):
                    copy(1 + j, m, (*chip, c), me).wait_recv()
                    fwd = copy(4 + j, m, (*chip, c), sibling)
                    fwd.start()
                    passed.append(fwd)
            for m in range(nm):
                copy(0, m, sibling, me).wait_recv()
            for j, chip in enumerate(chips):
                for m in range(nm):
                    copy(4 + j, m, (*chip, 1 - c), me).wait_recv()
            for cp in first() + passed:
                cp.wait_send()
            for cp in mine():
                cp.wait()

        return start, finish


class _Exchange(_Plan):
    def __init__(self, grads):
        super().__init__(grads)
        self.rows = [g.shape[0] // N_DEV for g in grads]
        self.out_shape = [_sds((N_DEV, g.shape[0] // N_DEV, g.shape[1]), g.dtype) for g in grads]

    def begin(self, ins, outs, sems):
        nm, rows = self.nm, self.rows
        send_sems, recv_sems, local_sems = sems
        me = _mesh_pos()
        my = _dev_index(me)

        def piece(m, idx):
            return ins[m].at[pl.ds(pl.multiple_of(idx * rows[m], 16), rows[m]), :]

        def copy(k, m, src_idx, slot, to):
            return pltpu.make_async_remote_copy(
                src_ref=piece(m, src_idx), dst_ref=outs[m].at[slot],
                send_sem=send_sems.at[k - 1, m], recv_sem=recv_sems.at[k - 1, m], device_id=to, device_id_type=_MESH)

        def mine():
            return [pltpu.make_async_copy(piece(m, my), outs[m].at[my], local_sems.at[m]) for m in range(nm)]

        def sends():
            return [copy(k, m, _dev_index(_flip(me, k)), my, _flip(me, k)) for k in range(1, N_DEV) for m in range(nm)]

        def start():
            for cp in mine() + sends():
                cp.start()

        def finish():
            for k in range(1, N_DEV):
                peer = _flip(me, k)
                for m in range(nm):
                    copy(k, m, my, _dev_index(peer), peer).wait_recv()
            for cp in sends():
                cp.wait_send()
            for cp in mine():
                cp.wait()

        return start, finish


def _run_plan(plan, name):
    nm = plan.nm

    def body(*refs):
        start, finish = plan.begin(refs[:nm], refs[nm:2 * nm], refs[2 * nm:])
        start()
        finish()

    return pl.pallas_call(body, name=name, in_specs=[_ANY] * nm, out_specs=[_ANY] * nm, out_shape=plan.out_shape,
                          scratch_shapes=plan.scratch)(*plan.operands)


def _call(body, carry, *, grid, in_specs, out_specs, out_shape, args, name, sem, scratch_shapes=()):
    if carry is None:
        outs = pl.pallas_call(body, grid=grid, name=name, in_specs=in_specs, out_specs=out_specs, out_shape=out_shape,
                              scratch_shapes=list(scratch_shapes), compiler_params=_params(*sem))(*args)
        return outs, None
    n_in, n_out, n_sc, nm = len(in_specs), len(out_specs), len(scratch_shapes), carry.nm

    def carrier(*refs):
        refs = list(refs)
        own_in, refs = refs[:n_in], refs[n_in:]
        c_in, refs = refs[:nm], refs[nm:]
        own_out, refs = refs[:n_out], refs[n_out:]
        c_out, refs = refs[:nm], refs[nm:]
        own_sc, c_sc = refs[:n_sc], refs[n_sc:]
        start, finish = carry.begin(c_in, c_out, c_sc)
        first = last = None
        for axis, n in enumerate(grid):
            i = pl.program_id(axis)
            first = (i == 0) if first is None else jnp.logical_and(first, i == 0)
            last = (i == n - 1) if last is None else jnp.logical_and(last, i == n - 1)
        pl.when(first)(start)
        body(*own_in, *own_out, *own_sc)
        pl.when(last)(finish)

    outs = pl.pallas_call(
        carrier, grid=grid, name=name + "_carry", in_specs=list(in_specs) + [_ANY] * nm,
        out_specs=list(out_specs) + [_ANY] * nm, out_shape=list(out_shape) + carry.out_shape,
        scratch_shapes=list(scratch_shapes) + carry.scratch,
        compiler_params=_params(*["arbitrary"] * len(grid)),
    )(*args, *carry.operands)
    return outs[:n_out], outs[n_out:]


def sum_slots(r):
    n, R, D = r.shape
    tc = _row_tile(D, 256)

    def body(r_ref, o_ref):
        acc = r_ref[0].astype(F32)
        for s in range(1, n):
            acc = acc + r_ref[s].astype(F32)
        o_ref[...] = acc

    return pl.pallas_call(
        body, grid=(D // tc,), name="sum_slots", in_specs=[pl.BlockSpec((n, R, tc), lambda i: (0, 0, i))],
        out_specs=pl.BlockSpec((R, tc), lambda i: (0, i)), out_shape=_sds((R, D), F32),
        compiler_params=_params("parallel"),
    )(r)


def allreduce_small(v):
    R, C = v.shape

    def body(v_ref, o_ref, buf, send_sems, recv_sems):
        me = _mesh_pos()
        my = _dev_index(me)
        buf[my] = v_ref[...]

        def copy(k, slot, to):
            return pltpu.make_async_remote_copy(
                src_ref=v_ref, dst_ref=buf.at[slot], send_sem=send_sems.at[k - 1], recv_sem=recv_sems.at[k - 1],
                device_id=to, device_id_type=_MESH)

        sends = []
        for k in range(1, N_DEV):
            cp = copy(k, my, _flip(me, k))
            cp.start()
            sends.append(cp)
        for k in range(1, N_DEV):
            peer = _flip(me, k)
            copy(k, _dev_index(peer), peer).wait_recv()
        acc = buf[0]
        for s in range(1, N_DEV):
            acc = acc + buf[s]
        o_ref[...] = acc
        for cp in sends:
            cp.wait_send()

    vm = pl.BlockSpec(memory_space=pltpu.VMEM)
    return pl.pallas_call(
        body, name="allreduce_small", in_specs=[vm], out_specs=vm, out_shape=_sds((R, C), F32),
        scratch_shapes=[pltpu.VMEM((N_DEV, R, C), F32), pltpu.SemaphoreType.DMA((7,)), pltpu.SemaphoreType.DMA((7,))],
    )(v)


WEIGHTS = ["ffn1_w_gate", "ffn1_w_up", "ffn1_w_down", "ln1_g", "ln1_b", "w_in", "rel_bias", "sb_out_g", "ca_out_g",
           "w_out", "ln2_g", "ln2_b", "ffn2_w_gate", "ffn2_w_up", "ffn2_w_down", "ln3_g", "ln3_b"]
BIG = ["ffn1_w_gate", "ffn1_w_up", "ffn2_w_gate", "ffn2_w_up", "w_in", "ffn1_w_down", "ffn2_w_down", "w_out"]
TRANSPOSED = BIG[:5]
SMALL = [n for n in WEIGHTS if n not in BIG]


def _pack(vals):
    flat = jnp.concatenate([v.reshape(-1).astype(F32) for v in vals])
    pad = -flat.shape[0] % (8 * 128)
    return jnp.pad(flat, (0, pad)).reshape(-1, 128)


def _unpack(packed, like):
    flat = packed.reshape(-1)
    out, off = [], 0
    for v in like:
        out.append(flat[off:off + v.size].reshape(v.shape))
        off += v.size
    return out


def _row_blocks(w, n):
    r = w.shape[0] // n
    return [w[i * r:(i + 1) * r] for i in range(n)]


def _layer_fwd(x, xb, W, P, bias, l, B, S, next_gather):
    T = B * S
    sv = {"xb": xb, "bias": bias}
    sv["hg1"], sv["hu1"], sv["a1"] = ffn_up(xb, W["ffn1_w_gate"], W["ffn1_w_up"])
    sv["u1"], x1, sv["x1b"] = mm_nn_ln([(sv["a1"], W["ffn1_w_down"])], x, P["ln1_g"][l], P["ln1_b"][l], FFN_RESIDUAL,
                                       "ffn_down_ln")

    sv["h"] = proj_in(sv["x1b"], W["w_in"], B, S)
    sv["gA"] = P["sb_out_g"][l].reshape(1, -1)
    sv["gB"] = P["ca_out_g"][l].reshape(1, -1)
    (sv["oa"], ma, sv["cars"]), gathered = sb_fwd(sv["h"], sv["gA"], carry=next_gather)
    sv["ob"], mb = ca_fwd(sv["h"], bias, sv["gB"])
    sv["ma"], sv["mb"] = ma.reshape(T, -1), mb.reshape(T, -1)
    sv["u2"], x2, sv["x2b"] = mm_nn_ln(list(zip([sv["ma"], sv["mb"]], _row_blocks(W["w_out"], 2))), x1,
                                       P["ln2_g"][l], P["ln2_b"][l], 1.0, "proj_out_ln")

    sv["hg2"], sv["hu2"], sv["a2"] = ffn_up(sv["x2b"], W["ffn2_w_gate"], W["ffn2_w_up"])
    sv["u3"], x3, x3b = mm_nn_ln([(sv["a2"], W["ffn2_w_down"])], x2, P["ln3_g"][l], P["ln3_b"][l], FFN_RESIDUAL,
                                 "ffn_down_ln")
    return x3, x3b, sv, gathered


class _Riders:
    def __init__(self):
        self.received = []

    def plan(self, entries):
        return _Exchange([e[2] for e in entries]) if entries else None

    def landed(self, entries, slots):
        for (name, layer, _), r in zip(entries, slots or []):
            self.received.append((name, layer, r))


def _ffn_bwd(dr, dxa, xb, hg, hu, a, wg, wu, wd, riders, ride, below):
    dhg, dhu = ffn_bwd_mid(dr, wd, hg, hu)
    pairs = [(dhg, wg), (dhu, wu)]
    if below is None:
        down = mm_nn(pairs, add=dxa, name="ffn_dx")
    else:
        down = mm_nn_ln_bwd(pairs, dxa, *below, name="ffn_dx_ln")
    grads, slots = ffn_dw(dhg, dhu, a, xb, dr, carry=riders.plan(ride))
    riders.landed(ride, slots)
    return down, grads


def _layer_bwd(dxa, dr, sv, W, P, l, B, S, riders, pending, below):
    T = B * S
    G = {}
    (dxa, dr, dg, db), (gg, gu, gd) = _ffn_bwd(
        dr, dxa, sv["x2b"], sv["hg2"], sv["hu2"], sv["a2"], W["ffn2_w_gate"], W["ffn2_w_up"], W["ffn2_w_down"],
        riders, pending, (sv["u2"], P["ln2_g"][l], 1.0))
    G["ln2_g"], G["ln2_b"] = dg.sum(0), db.sum(0)
    dmix = mm_nt(dr, W["w_out"], name="proj_out_dx").reshape(B, S, -1)
    g_out = mm_tn([sv["ma"], sv["mb"]], dr, name="proj_out_dw")[0]
    ride = [("ffn2_w_gate", l, gg), ("ffn2_w_up", l, gu)]
    (dqa, dka, dva, dga), slots = sb_bwd(sv["h"], sv["oa"], dmix, sv["gA"], sv["cars"], carry=riders.plan(ride))
    riders.landed(ride, slots)
    ride = [("ffn2_w_down", l, gd)]
    (dqb, dkb, dvb, dgb, dbias), slots = ca_bwd(sv["h"], sv["bias"], sv["ob"], dmix, sv["gB"], carry=riders.plan(ride))
    riders.landed(ride, slots)
    G["sb_out_g"] = dga.sum((0, 1, 2))
    G["ca_out_g"] = dgb.sum((0, 1, 2))
    G["rel_bias"] = rel_bias_grad(dbias)
    dh = [t.reshape(T, -1) for t in (dqa, dka, dva, dqb, dkb, dvb)]
    dxa, dr, dg, db = mm_nn_ln_bwd(list(zip(dh, _row_blocks(W["w_in"], 6))), dxa, sv["u1"], P["ln1_g"][l], FFN_RESIDUAL,
                                   name="proj_in_dx_ln")
    G["ln1_g"], G["ln1_b"] = dg.sum(0), db.sum(0)
    g_in = mm_tn(dh, sv["x1b"], name="proj_in_dw")[0]

    down, (gg, gu, gd) = _ffn_bwd(dr, dxa, sv["xb"], sv["hg1"], sv["hu1"], sv["a1"], W["ffn1_w_gate"], W["ffn1_w_up"],
                                  W["ffn1_w_down"], riders, [("w_in", l, g_in), ("w_out", l, g_out)], below)
    return down, G, [("ffn1_w_gate", l, gg), ("ffn1_w_up", l, gu), ("ffn1_w_down", l, gd)]


def kernel(x, ffn1_w_gate, ffn1_w_up, ffn1_w_down, ln1_g, ln1_b, w_in, rel_bias, sb_out_g, ca_out_g, w_out, ln2_g, ln2_b, ffn2_w_gate, ffn2_w_up, ffn2_w_down, ln3_g, ln3_b, loss_target, m_ffn1_w_gate, m_ffn1_w_up, m_ffn1_w_down, m_ln1_g, m_ln1_b, m_w_in, m_rel_bias, m_sb_out_g, m_ca_out_g, m_w_out, m_ln2_g, m_ln2_b, m_ffn2_w_gate, m_ffn2_w_up, m_ffn2_w_down, m_ln3_g, m_ln3_b, v_ffn1_w_gate, v_ffn1_w_up, v_ffn1_w_down, v_ln1_g, v_ln1_b, v_w_in, v_rel_bias, v_sb_out_g, v_ca_out_g, v_w_out, v_ln2_g, v_ln2_b, v_ffn2_w_gate, v_ffn2_w_up, v_ffn2_w_down, v_ln3_g, v_ln3_b):
    given = dict(locals())
    P = {n: given[n] for n in WEIGHTS}
    M = {n: given["m_" + n] for n in WEIGHTS}
    V = {n: given["v_" + n] for n in WEIGHTS}
    B, S, D = x.shape
    L = ln1_g.shape[0]

    local = {n: (jnp.swapaxes(P[n], 1, 2) if n in TRANSPOSED else P[n]).astype(BF16) for n in BIG}
    bias = ca_bias(rel_bias)

    xf = x.reshape(B * S, D)
    xb = xf.astype(BF16)
    W, saved = [], []
    gathered = _run_plan(_Gather([local[n][0] for n in BIG]), "gather_weights")
    for l in range(L):
        W.append(dict(zip(BIG, gathered)))
        next_gather = _Gather([local[n][l + 1] for n in BIG]) if l + 1 < L else None
        xf, xb, sv, gathered = _layer_fwd(xf, xb, W[l], P, bias[l], l, B, S, next_gather)
        saved.append(sv)
    dy, loss_part = loss_head(xf, loss_target.reshape(B * S, D))

    big_g = {n: [None] * L for n in BIG}
    small_g = {n: [None] * L for n in SMALL}
    riders, pending = _Riders(), []
    down = ln_bwd(saved[L - 1]["u3"], dy, ln3_g[L - 1], FFN_RESIDUAL)
    for l in reversed(range(L)):
        dxa, dr, dg, db = down
        small_g["ln3_g"][l], small_g["ln3_b"][l] = dg.sum(0), db.sum(0)
        below = (saved[l - 1]["u3"], ln3_g[l - 1], FFN_RESIDUAL) if l > 0 else None
        down, G, pending = _layer_bwd(dxa, dr, saved[l], W[l], P, l, B, S, riders, pending, below)
        for n, g in G.items():
            small_g[n][l] = g
    dx = down
    riders.landed(pending, _run_plan(riders.plan(pending), "exchange_grads"))
    for n, l, slots in riders.received:
        big_g[n][l] = sum_slots(slots)

    grads = {}
    for n in BIG:
        g = jnp.stack(big_g[n])
        grads[n] = jnp.swapaxes(g, 1, 2) if n in TRANSPOSED else g
    small_like = [P[n] for n in SMALL]
    packed = _pack([jnp.stack(small_g[n]) for n in SMALL] + [loss_part.sum()])
    total = allreduce_small(packed)
    *small_vals, loss = _unpack(total, small_like + [jnp.zeros((), F32)])
    grads.update(dict(zip(SMALL, small_vals)))

    delta, new_m, new_v = {}, {}, {}
    for n in BIG:
        shape = P[n].shape
        two_d = lambda a: a.reshape(shape[0] * shape[1], shape[2])
        d, m, v = adamw(two_d(P[n]), two_d(grads[n]), two_d(M[n]), two_d(V[n]))
        delta[n], new_m[n], new_v[n] = d.reshape(shape), m.reshape(shape), v.reshape(shape)
    one = jnp.ones((), F32)
    d, m, v = adamw(_pack(small_like + [one]), total, _pack([M[n] for n in SMALL] + [one]), _pack([V[n] for n in SMALL] + [one]))
    for dst, src in ((delta, d), (new_m, m), (new_v, v)):
        dst.update(dict(zip(SMALL, _unpack(src, small_like))))

    return (loss, dx.reshape(B, S, D), *[grads[n] for n in WEIGHTS], *[delta[n] for n in WEIGHTS],
            *[new_m[n] for n in WEIGHTS], *[new_v[n] for n in WEIGHTS])
```

```python
import math

import jax
import jax.numpy as jnp
import numpy as np
from jax import lax
from jax.experimental import pallas as pl
from jax.experimental.pallas import tpu as pltpu

F32 = jnp.float32
BF16 = jnp.bfloat16

HEAD_DIM = 64
CHUNK = 64
N_PREV_CHUNKS = 8
MAX_REL = 128
DEPTH = 4
FFN_RESIDUAL = 0.5
ALPHA = (2 * DEPTH) ** 0.25
LN_EPS = 1e-5
RMS_EPS = 1e-6
ADAM_LR = 0.001
ADAM_B1 = 0.9
ADAM_B2 = 0.999
ADAM_EPS = 1e-08
ADAM_WD = 0.01
ADAM_STEP = 10

N_DEV = 8
LANES = 128
VMEM_LIMIT_BYTES = 56 * 1024 * 1024
ROW_TILE = 512
SB_BLOCK = 128
CA_PAD = CHUNK * N_PREV_CHUNKS
CA_TQ = 2 * CHUNK
CA_WIN = CA_PAD + CA_TQ
FWD_HEADS = 8
BWD_HEADS = 8
NEG = -1e30
SB_SKIP_LOG = -104.0

NN = ((1,), (0,))
NT = ((1,), (1,))
TN = ((0,), (0,))


def _dot(a, b, dims):
    return lax.dot_general(a, b, (dims, ((), ())), preferred_element_type=F32)


def _params(*sem):
    return pltpu.CompilerParams(dimension_semantics=sem, vmem_limit_bytes=VMEM_LIMIT_BYTES)


def _sds(shape, dtype):
    return jax.ShapeDtypeStruct(shape, dtype)


def _row_tile(n, want=ROW_TILE):
    t = min(want, n)
    while n % t:
        t //= 2
    assert t >= 8, (n, want)
    return t


def ffn_up(xb, wg, wu, carry=None):
    T, D = xb.shape
    F = wg.shape[0]
    tm = _row_tile(T)
    fc = _row_tile(F, 256)

    def body(x_ref, wg_ref, wu_ref, hg_ref, hu_ref, a_ref):
        x = x_ref[...]
        for j in range(F // fc):
            sl = pl.ds(j * fc, fc)
            hg = _dot(x, wg_ref[sl, :], NT)
            hu = _dot(x, wu_ref[sl, :], NT)
            hg_ref[:, sl] = hg.astype(BF16)
            hu_ref[:, sl] = hu.astype(BF16)
            a_ref[:, sl] = (hg * jax.nn.sigmoid(hg) * hu).astype(BF16)

    row = pl.BlockSpec((tm, F), lambda i: (i, 0))
    w = pl.BlockSpec((F, D), lambda i: (0, 0))
    return _call(body, carry, grid=(T // tm,), name="ffn_up", sem=("parallel",), args=(xb, wg, wu),
                 in_specs=[pl.BlockSpec((tm, D), lambda i: (i, 0)), w, w],
                 out_specs=[row, row, row], out_shape=[_sds((T, F), BF16)] * 3)


def ffn_bwd_mid(dr, wd, hg, hu):
    T, D = dr.shape
    F = wd.shape[0]
    tm = _row_tile(T)
    fc = _row_tile(F, 256)

    def body(dr_ref, wd_ref, hg_ref, hu_ref, dhg_ref, dhu_ref):
        dr_ = dr_ref[...]
        for j in range(F // fc):
            sl = pl.ds(j * fc, fc)
            da = _dot(dr_, wd_ref[sl, :], NT)
            g = hg_ref[:, sl].astype(F32)
            u = hu_ref[:, sl].astype(F32)
            s = jax.nn.sigmoid(g)
            gs = g * s
            dhu_ref[:, sl] = (da * gs).astype(BF16)
            dhg_ref[:, sl] = (da * u * (s + gs * (1.0 - s))).astype(BF16)

    row = pl.BlockSpec((tm, F), lambda i: (i, 0))
    return pl.pallas_call(
        body, grid=(T // tm,), name="ffn_bwd_mid",
        in_specs=[pl.BlockSpec((tm, D), lambda i: (i, 0)), pl.BlockSpec((F, D), lambda i: (0, 0)), row, row],
        out_specs=[row, row], out_shape=[_sds((T, F), BF16)] * 2,
        compiler_params=_params("parallel"),
    )(dr, wd, hg, hu)


def mm_nn(pairs, add=None, out_dtype=F32, name="mm_nn"):
    M = pairs[0][0].shape[0]
    N = pairs[0][1].shape[1]
    tm = _row_tile(M)
    nc = _row_tile(N, 512)
    n_pairs = len(pairs)

    def body(*refs):
        a_refs = refs[:n_pairs]
        b_refs = refs[n_pairs:2 * n_pairs]
        add_ref = refs[2 * n_pairs] if add is not None else None
        o_ref = refs[-1]
        for j in range(N // nc):
            sl = pl.ds(j * nc, nc)
            acc = _dot(a_refs[0][...], b_refs[0][:, sl], NN)
            for a_ref, b_ref in zip(a_refs[1:], b_refs[1:]):
                acc = acc + _dot(a_ref[...], b_ref[:, sl], NN)
            if add_ref is not None:
                acc = acc + add_ref[:, sl]
            o_ref[:, sl] = acc.astype(out_dtype)

    in_specs = [pl.BlockSpec((tm, a.shape[1]), lambda i: (i, 0)) for a, _ in pairs]
    in_specs += [pl.BlockSpec(b.shape, lambda i: (0, 0)) for _, b in pairs]
    args = [a for a, _ in pairs] + [b for _, b in pairs]
    if add is not None:
        in_specs.append(pl.BlockSpec((tm, N), lambda i: (i, 0)))
        args.append(add)
    return pl.pallas_call(
        body, grid=(M // tm,), name=name, in_specs=in_specs,
        out_specs=pl.BlockSpec((tm, N), lambda i: (i, 0)), out_shape=_sds((M, N), out_dtype),
        compiler_params=_params("parallel"),
    )(*args)


def _mm_specs(pairs, tm):
    specs = [pl.BlockSpec((tm, a.shape[1]), lambda i: (i, 0)) for a, _ in pairs]
    specs += [pl.BlockSpec(b.shape, lambda i: (0, 0), pipeline_mode=pl.Buffered(1)) for _, b in pairs]
    return specs, [a for a, _ in pairs] + [b for _, b in pairs]


def _mm_sum(a_refs, b_refs):
    acc = _dot(a_refs[0][...], b_refs[0][...], NN)
    for a_ref, b_ref in zip(a_refs[1:], b_refs[1:]):
        acc = acc + _dot(a_ref[...], b_ref[...], NN)
    return acc


def _ln_fwd_rows(u, g, b):
    mu = jnp.mean(u, axis=-1, keepdims=True)
    xc = u - mu
    var = jnp.mean(xc * xc, axis=-1, keepdims=True)
    return xc * lax.rsqrt(var + LN_EPS) * g + b


def _ln_bwd_rows(u, dy, g):
    mu = jnp.mean(u, axis=-1, keepdims=True)
    xc = u - mu
    var = jnp.mean(xc * xc, axis=-1, keepdims=True)
    rstd = lax.rsqrt(var + LN_EPS)
    xh = xc * rstd
    dxh = dy * g
    m1 = jnp.mean(dxh, axis=-1, keepdims=True)
    m2 = jnp.mean(dxh * xh, axis=-1, keepdims=True)
    return rstd * (dxh - m1 - xh * m2), dy * xh


def mm_nn_ln(pairs, x, g, b, res_scale, name):
    M, N = x.shape
    tm = _row_tile(M)
    n_pairs = len(pairs)

    def body(*refs):
        a_refs, b_refs = refs[:n_pairs], refs[n_pairs:2 * n_pairs]
        x_ref, g_ref, b_ref, u_ref, y_ref, yb_ref = refs[2 * n_pairs:]
        u = ALPHA * x_ref[...] + res_scale * _mm_sum(a_refs, b_refs)
        y = _ln_fwd_rows(u, g_ref[...], b_ref[...])
        u_ref[...] = u
        y_ref[...] = y
        yb_ref[...] = y.astype(BF16)

    in_specs, args = _mm_specs(pairs, tm)
    row = pl.BlockSpec((tm, N), lambda i: (i, 0))
    vec = pl.BlockSpec((1, N), lambda i: (0, 0))
    return pl.pallas_call(
        body, grid=(M // tm,), name=name, in_specs=in_specs + [row, vec, vec],
        out_specs=[row, row, row], out_shape=[_sds((M, N), F32), _sds((M, N), F32), _sds((M, N), BF16)],
        compiler_params=_params("parallel"),
    )(*args, x, g.reshape(1, N), b.reshape(1, N))


def mm_nn_ln_bwd(pairs, add, u, g, branch_scale, name):
    M, N = u.shape
    tm = _row_tile(M)
    n_pairs = len(pairs)

    def body(*refs):
        a_refs, b_refs = refs[:n_pairs], refs[n_pairs:2 * n_pairs]
        add_ref, u_ref, g_ref, dxa_ref, dr_ref, dg_ref, db_ref = refs[2 * n_pairs:]

        @pl.when(pl.program_id(0) == 0)
        def _():
            dg_ref[...] = jnp.zeros_like(dg_ref)
            db_ref[...] = jnp.zeros_like(db_ref)

        dy = _mm_sum(a_refs, b_refs) + add_ref[...]
        du, dyxh = _ln_bwd_rows(u_ref[...], dy, g_ref[...])
        dxa_ref[...] = ALPHA * du
        dr_ref[...] = (branch_scale * du).astype(BF16)
        dg_ref[...] += jnp.sum(dyxh.reshape(tm // 8, 8, N), axis=0)
        db_ref[...] += jnp.sum(dy.reshape(tm // 8, 8, N), axis=0)

    in_specs, args = _mm_specs(pairs, tm)
    row = pl.BlockSpec((tm, N), lambda i: (i, 0))
    acc = pl.BlockSpec((8, N), lambda i: (0, 0))
    return pl.pallas_call(
        body, grid=(M // tm,), name=name, in_specs=in_specs + [row, row, pl.BlockSpec((1, N), lambda i: (0, 0))],
        out_specs=[row, row, acc, acc],
        out_shape=[_sds((M, N), F32), _sds((M, N), BF16), _sds((8, N), F32), _sds((8, N), F32)],
        compiler_params=_params("arbitrary"),
    )(*args, add, u, g.reshape(1, N))


def mm_nt(a, b, out_dtype=F32, name="mm_nt"):
    M, K = a.shape
    N = b.shape[0]
    tm = _row_tile(M)
    nc = _row_tile(N, 512)

    def body(a_ref, b_ref, o_ref):
        a_ = a_ref[...]
        for j in range(N // nc):
            sl = pl.ds(j * nc, nc)
            o_ref[:, sl] = _dot(a_, b_ref[sl, :], NT).astype(out_dtype)

    return pl.pallas_call(
        body, grid=(M // tm,), name=name,
        in_specs=[pl.BlockSpec((tm, K), lambda i: (i, 0)), pl.BlockSpec((N, K), lambda i: (0, 0))],
        out_specs=pl.BlockSpec((tm, N), lambda i: (i, 0)), out_shape=_sds((M, N), out_dtype),
        compiler_params=_params("parallel"),
    )(a, b)


def proj_in(xb, w, B, S):
    D = xb.shape[1]
    N = w.shape[0]
    tm = CA_PAD
    nblk = S // tm
    nc = _row_tile(N, 512)
    assert S % tm == 0

    def body(x_ref, w_ref, o_ref):
        i = pl.program_id(1)

        @pl.when(i == 0)
        def _():
            o_ref[...] = jnp.zeros_like(o_ref)

        @pl.when(i > 0)
        def _():
            x = x_ref[...]
            for j in range(N // nc):
                sl = pl.ds(j * nc, nc)
                o_ref[0, :, sl] = _dot(x, w_ref[sl, :], NT).astype(BF16)

    return pl.pallas_call(
        body, grid=(B, nblk + 1), name="proj_in",
        in_specs=[pl.BlockSpec((tm, D), lambda b, i: (b * nblk + jnp.maximum(i - 1, 0), 0)),
                  pl.BlockSpec((N, D), lambda b, i: (0, 0))],
        out_specs=pl.BlockSpec((1, tm, N), lambda b, i: (b, i, 0)), out_shape=_sds((B, CA_PAD + S, N), BF16),
        compiler_params=_params("parallel", "arbitrary"),
    )(xb, w)


def mm_tn(parts, b, name="mm_tn", carry=None):
    T, N = b.shape
    widths = [a.shape[1] for a in parts]
    M = sum(widths)
    tk = _row_tile(T)
    steps = T // tk
    n_parts = len(parts)

    def body(*refs):
        a_refs, b_ref, o_ref, acc_ref = refs[:n_parts], refs[n_parts], refs[n_parts + 1], refs[n_parts + 2]
        i = pl.program_id(0)

        @pl.when(i == 0)
        def _():
            acc_ref[...] = jnp.zeros_like(acc_ref)

        b_ = b_ref[...]
        row = 0
        for a_ref, width in zip(a_refs, widths):
            mc = _row_tile(width, 256)
            for j in range(width // mc):
                acc_ref[pl.ds(row, mc), :] += _dot(a_ref[:, pl.ds(j * mc, mc)], b_, TN)
                row += mc

        @pl.when(i == steps - 1)
        def _():
            o_ref[...] = acc_ref[...].astype(BF16)

    (out,), got = _call(
        body, carry, grid=(steps,), name=name, sem=("arbitrary",), args=(*parts, b),
        in_specs=[pl.BlockSpec((tk, w), lambda i: (i, 0)) for w in widths] + [pl.BlockSpec((tk, N), lambda i: (i, 0))],
        out_specs=[pl.BlockSpec((M, N), lambda i: (0, 0))], out_shape=[_sds((M, N), BF16)],
        scratch_shapes=[pltpu.VMEM((M, N), F32)])
    return out, got


def ffn_dw(dhg, dhu, a, xb, dr, carry=None):
    T, F = dhg.shape
    D = xb.shape[1]
    tk = _row_tile(T)
    mc = _row_tile(F, 256)
    steps = T // tk
    last = steps - 1

    def body(dhg_ref, dhu_ref, a_ref, xb_ref, dr_ref, o_ref, acc_ref):
        p = pl.program_id(0)
        i = pl.program_id(1)

        @pl.when(i == 0)
        def _():
            acc_ref[...] = jnp.zeros_like(acc_ref)

        def accumulate(lhs_ref, rhs_ref):
            rhs = rhs_ref[...]
            for j in range(F // mc):
                sl = pl.ds(j * mc, mc)
                acc_ref[sl, :] += _dot(lhs_ref[:, sl], rhs, TN)

        pl.when(p == 0)(lambda: accumulate(dhg_ref, xb_ref))
        pl.when(p == 1)(lambda: accumulate(dhu_ref, xb_ref))
        pl.when(p == 2)(lambda: accumulate(a_ref, dr_ref))

        @pl.when(i == last)
        def _():
            o_ref[0] = acc_ref[...].astype(BF16)

    def during(phases, width):
        lo, hi = phases
        return pl.BlockSpec((tk, width), lambda p, i: (jnp.where(p < lo, 0, jnp.where(p > hi, last, i)), 0))

    (out,), got = _call(
        body, carry, grid=(3, steps), name="ffn_dw", sem=("arbitrary", "arbitrary"), args=(dhg, dhu, a, xb, dr),
        in_specs=[during((0, 0), F), during((1, 1), F), during((2, 2), F), during((0, 1), D), during((2, 2), D)],
        out_specs=[pl.BlockSpec((1, F, D), lambda p, i: (p, 0, 0))], out_shape=[_sds((3, F, D), BF16)],
        scratch_shapes=[pltpu.VMEM((F, D), F32)])
    return out, got


def ln_bwd(u, dy, g, branch_scale):
    T, D = u.shape
    tm = _row_tile(T)

    def body(u_ref, dy_ref, g_ref, dxa_ref, dr_ref, dg_ref, db_ref):
        @pl.when(pl.program_id(0) == 0)
        def _():
            dg_ref[...] = jnp.zeros_like(dg_ref)
            db_ref[...] = jnp.zeros_like(db_ref)

        dy_ = dy_ref[...]
        du, dyxh = _ln_bwd_rows(u_ref[...], dy_, g_ref[...])
        dxa_ref[...] = ALPHA * du
        dr_ref[...] = (branch_scale * du).astype(BF16)
        dg_ref[...] += jnp.sum(dyxh.reshape(tm // 8, 8, D), axis=0)
        db_ref[...] += jnp.sum(dy_.reshape(tm // 8, 8, D), axis=0)

    row = pl.BlockSpec((tm, D), lambda i: (i, 0))
    acc = pl.BlockSpec((8, D), lambda i: (0, 0))
    dxa, dr, dg, db = pl.pallas_call(
        body, grid=(T // tm,), name="ln_bwd", in_specs=[row, row, pl.BlockSpec((1, D), lambda i: (0, 0))],
        out_specs=[row, row, acc, acc],
        out_shape=[_sds((T, D), F32), _sds((T, D), BF16), _sds((8, D), F32), _sds((8, D), F32)],
        compiler_params=_params("arbitrary"),
    )(u, dy, g.reshape(1, D))
    return dxa, dr, dg, db


def loss_head(y, target):
    T, D = y.shape
    tm = _row_tile(T)

    def body(y_ref, t_ref, dy_ref, l_ref):
        @pl.when(pl.program_id(0) == 0)
        def _():
            l_ref[...] = jnp.zeros_like(l_ref)

        e = y_ref[...] - t_ref[...]
        dy_ref[...] = e * (1.0 / D)
        l_ref[...] += jnp.sum((e * e).reshape(tm // 8, 8, D), axis=0) * (0.5 / D)

    row = pl.BlockSpec((tm, D), lambda i: (i, 0))
    return pl.pallas_call(
        body, grid=(T // tm,), name="loss_head", in_specs=[row, row],
        out_specs=[row, pl.BlockSpec((8, D), lambda i: (0, 0))],
        out_shape=[_sds((T, D), F32), _sds((8, D), F32)],
        compiler_params=_params("arbitrary"),
    )(y, target)


def adamw(w, g, m, v):
    R, C = w.shape
    tr = R
    for cand in (512, 256, 128, 64, 32, 16, 8):
        if R % cand == 0:
            tr = cand
            break
    c1 = 1.0 - ADAM_B1 ** ADAM_STEP
    c2 = 1.0 - ADAM_B2 ** ADAM_STEP

    def body(w_ref, g_ref, m_ref, v_ref, d_ref, mo_ref, vo_ref):
        g_ = g_ref[...]
        m_ = ADAM_B1 * m_ref[...] + (1.0 - ADAM_B1) * g_
        v_ = ADAM_B2 * v_ref[...] + (1.0 - ADAM_B2) * (g_ * g_)
        m_hat = m_ / c1
        v_hat = v_ / c2
        d_ref[...] = -ADAM_LR * (m_hat / (jnp.sqrt(v_hat) + ADAM_EPS) + ADAM_WD * w_ref[...])
        mo_ref[...] = m_
        vo_ref[...] = v_

    blk = pl.BlockSpec((tr, C), lambda i: (i, 0))
    return pl.pallas_call(
        body, grid=(R // tr,), name="adamw", in_specs=[blk] * 4, out_specs=[blk] * 3,
        out_shape=[_sds((R, C), F32)] * 3, compiler_params=_params("parallel"),
    )(w, g, m, v)


def _lane_lo(rows):
    return lax.broadcasted_iota(jnp.int32, (rows, LANES), 1) < HEAD_DIM


def _pair_mean(x, lo):
    s0 = jnp.sum(jnp.where(lo, x, 0.0), axis=-1, keepdims=True)
    s1 = jnp.sum(jnp.where(lo, 0.0, x), axis=-1, keepdims=True)
    return jnp.where(lo, s0, s1) * (1.0 / HEAD_DIM)


def _rms_fwd(o, gain, lo):
    r = lax.rsqrt(_pair_mean(o * o, lo) + RMS_EPS)
    return (o * r * gain).astype(BF16)


def _rms_bwd(o, dm, gain, lo):
    r = lax.rsqrt(_pair_mean(o * o, lo) + RMS_EPS)
    oh = o * r
    dg = jnp.sum(dm * oh, axis=0, keepdims=True)
    doh = dm * gain
    do = r * (doh - oh * _pair_mean(doh * oh, lo))
    return do, dg


def _split_heads(x, lo):
    zero = jnp.zeros_like(x)
    return [jnp.where(lo, x, zero), jnp.where(lo, zero, x)]


def _merge_pairs(per_head, lo):
    return [jnp.where(lo, per_head[2 * p], per_head[2 * p + 1]) for p in range(len(per_head) // 2)]


def _pair_cols(h):
    return slice((h // 2) * LANES, (h // 2 + 1) * LANES)


def _split_dot(x, tri):
    n = x.shape[0]
    hi = x.astype(BF16)
    lo = (x - hi.astype(F32)).astype(BF16)
    both = _dot(jnp.concatenate([hi, lo], axis=0), tri, NN)
    return both[:n] + both[n:]


def _log_keep(z):
    return -(jnp.maximum(z, 0.0) + jnp.log(1.0 + jnp.exp(-jnp.abs(z))))


def _attn_dims(h3p, gain, heads):
    B, SP, C = h3p.shape
    HD = gain.shape[1]
    hs = min(heads, HD // HEAD_DIM)
    W = hs * HEAD_DIM
    assert C == 6 * HD and W % LANES == 0 and HD % W == 0
    return B, SP - CA_PAD, HD, hs, W, HD // W


def sb_fwd(h3p, gain, carry=None):
    B, S, HD, hs, W, ngrp = _attn_dims(h3p, gain, FWD_HEADS)
    tb = min(SB_BLOCK, S)
    nq = S // tb
    off = CA_PAD // tb
    scale = 1.0 / math.sqrt(HEAD_DIM)

    def body(q_ref, k_ref, v_ref, g_ref, o_ref, m_ref, c_ref):
        qi = pl.program_id(2)
        lo = _lane_lo(tb)
        row = lax.broadcasted_iota(jnp.int32, (tb, tb), 0)
        col = lax.broadcasted_iota(jnp.int32, (tb, tb), 1)
        rev_incl = (row >= col).astype(BF16)
        lane = lax.broadcasted_iota(jnp.int32, (tb, nq), 1)
        qm = []
        for p in range(hs // 2):
            qm += _split_heads((q_ref[0, :, p * LANES:(p + 1) * LANES].astype(F32) * scale).astype(BF16), lo)

        def blocks(kb, carries_, mask):
            ks = pl.ds(pl.multiple_of(kb * tb + CA_PAD, tb), tb)
            hh = range(hs)
            zs = [_dot(qm[h], k_ref[0, ks, _pair_cols(h)], NT) for h in hh]
            lks = [_log_keep(z) for z in zs]
            if mask is not None:
                lks = [jnp.where(mask, lk, 0.0) for lk in lks]
            cums = [_split_dot(lk, rev_incl) for lk in lks]
            ws = []
            for h in hh:
                logw = zs[h] + cums[h] + carries_[h]
                if mask is not None:
                    logw = jnp.where(mask, logw, NEG)
                ws.append(jnp.exp(logw).astype(BF16))
            pvs = [_dot(ws[h], v_ref[0, ks, _pair_cols(h)], NN) for h in hh]
            return [(pvs[h], cums[h][:, 0:1]) for h in hh]

        diag = blocks(qi, [jnp.zeros((tb, 1), F32)] * hs, col < row)
        accs = _merge_pairs([d[0] for d in diag], lo)
        carries = [d[1] for d in diag]
        cars = [jnp.where(lane == qi, 0.0, NEG)] * hs

        def cond(st):
            kb, carries_, _, _ = st
            top = carries_[0]
            for c in carries_[1:]:
                top = jnp.maximum(top, c)
            return jnp.logical_and(kb >= 0, jnp.max(top) > SB_SKIP_LOG)

        def step(st):
            kb, carries_, accs_, cars_ = st
            out = blocks(kb, carries_, None)
            pv = _merge_pairs([o[0] for o in out], lo)
            return (kb - 1, [c + o[1] for c, o in zip(carries_, out)], [a + p for a, p in zip(accs_, pv)],
                    [jnp.where(lane == kb, c, cs) for c, cs in zip(carries_, cars_)])

        _, _, accs, cars = lax.while_loop(cond, step, (qi - 1, carries, accs, cars))
        for p, acc in enumerate(accs):
            cols = slice(p * LANES, (p + 1) * LANES)
            o_ref[0, :, cols] = acc
            m_ref[0, :, cols] = _rms_fwd(acc, g_ref[:, cols], lo)
        for h in range(hs):
            c_ref[0, h] = cars[h]

    qspec = pl.BlockSpec((1, tb, W), lambda g, b, i: (b, i + off, g))
    ospec = pl.BlockSpec((1, tb, W), lambda g, b, i: (b, i, g))
    return _call(
        body, carry, grid=(ngrp, B, nq), name="sb_fwd", sem=("parallel", "parallel", "arbitrary"),
        args=(h3p, h3p, h3p, gain),
        in_specs=[qspec, pl.BlockSpec((1, CA_PAD + S, W), lambda g, b, i: (b, 0, ngrp + g)),
                  pl.BlockSpec((1, CA_PAD + S, W), lambda g, b, i: (b, 0, 2 * ngrp + g)),
                  pl.BlockSpec((1, W), lambda g, b, i: (0, g))],
        out_specs=[ospec, ospec, pl.BlockSpec((1, hs, tb, nq), lambda g, b, i: (b, g, i, 0))],
        out_shape=[_sds((B, S, HD), F32), _sds((B, S, HD), BF16), _sds((B, HD // HEAD_DIM, S, nq), F32)])


def sb_bwd(h3p, o, dmix, gain, cars, carry=None):
    B, S, HD, hs, W, ngrp = _attn_dims(h3p, gain, BWD_HEADS)
    tb = min(SB_BLOCK, S)
    nq = S // tb
    off = CA_PAD // tb
    scale = 1.0 / math.sqrt(HEAD_DIM)

    def body(q_ref, k_ref, v_ref, o_ref, dm_ref, g_ref, c_ref, dq_ref, dk_ref, dv_ref, dg_ref, dk_acc, dv_acc):
        qi = pl.program_id(2)

        @pl.when(qi == 0)
        def _():
            dk_acc[...] = jnp.zeros_like(dk_acc)
            dv_acc[...] = jnp.zeros_like(dv_acc)

        lo = _lane_lo(tb)
        row = lax.broadcasted_iota(jnp.int32, (tb, tb), 0)
        col = lax.broadcasted_iota(jnp.int32, (tb, tb), 1)
        rev_incl = (row >= col).astype(BF16)
        fwd_incl = (row <= col).astype(BF16)
        lane = lax.broadcasted_iota(jnp.int32, (tb, nq), 1)
        below = lax.broadcasted_iota(jnp.int32, (1, nq), 1) < qi
        qm, dom, cars_, seen = [], [], [], None
        for p in range(hs // 2):
            cols = slice(p * LANES, (p + 1) * LANES)
            qm += _split_heads((q_ref[0, :, cols].astype(F32) * scale).astype(BF16), lo)
            do, dg = _rms_bwd(o_ref[0, :, cols], dm_ref[0, :, cols], g_ref[:, cols], lo)
            dg_ref[0, 0, :, cols] = dg
            dom += _split_heads(do.astype(BF16), lo)
        for h in range(hs):
            cars_.append(c_ref[0, h])
            visited = jnp.logical_and(jnp.max(cars_[h], axis=0, keepdims=True) > SB_SKIP_LOG, below)
            n = jnp.sum(visited.astype(jnp.int32), axis=1, keepdims=True)
            seen = n if seen is None else jnp.maximum(seen, n)
        first = qi - jnp.max(seen)

        def blocks(kb, gsums, dqs, mask):
            ks = pl.ds(pl.multiple_of(kb * tb + CA_PAD, tb), tb)
            ko = pl.ds(pl.multiple_of(kb * tb, tb), tb)
            hh = range(hs)
            kk = [k_ref[0, ks, p * LANES:(p + 1) * LANES] for p in range(hs // 2)]
            vv = [v_ref[0, ks, p * LANES:(p + 1) * LANES] for p in range(hs // 2)]
            zs = [_dot(qm[h], kk[h // 2], NT) for h in hh]
            dws = [_dot(dom[h], vv[h // 2], NT) for h in hh]
            raw = [_log_keep(z) for z in zs]
            lks = raw if mask is None else [jnp.where(mask, lk, 0.0) for lk in raw]
            cums = [_split_dot(lk, rev_incl) for lk in lks]
            ws = []
            for h in hh:
                carry = jnp.sum(jnp.where(lane == kb, cars_[h], 0.0), axis=1, keepdims=True)
                logw = zs[h] + cums[h] + carry
                if mask is not None:
                    logw = jnp.where(mask, logw, NEG)
                ws.append(jnp.exp(logw))
            gws = [ws[h] * dws[h] for h in hh]
            gcums = [_split_dot(gws[h], fwd_incl) + gsums[h] for h in hh]
            dzb = []
            for h in hh:
                dz = gws[h] - jnp.exp(zs[h] + raw[h]) * gcums[h]
                if mask is not None:
                    dz = jnp.where(mask, dz, 0.0)
                dzb.append(dz.astype(BF16))
            wb = [w.astype(BF16) for w in ws]
            new_dq = [dqs[h] + _dot(dzb[h], kk[h // 2], NN) for h in hh]
            for p in range(hs // 2):
                cols = slice(p * LANES, (p + 1) * LANES)
                dk_acc[ko, cols] += _dot(dzb[2 * p], qm[2 * p], TN) + _dot(dzb[2 * p + 1], qm[2 * p + 1], TN)
                dv_acc[ko, cols] += _dot(wb[2 * p], dom[2 * p], TN) + _dot(wb[2 * p + 1], dom[2 * p + 1], TN)
            return [g[:, tb - 1:tb] for g in gcums], new_dq

        def step(kb, st):
            return blocks(kb, st[0], st[1], None)

        init = ([jnp.zeros((tb, 1), F32)] * hs, [jnp.zeros((tb, LANES), F32)] * hs)
        gsum, dq = lax.fori_loop(first, qi, step, init)
        _, dq = blocks(qi, gsum, dq, col < row)
        for p, d in enumerate(_merge_pairs(dq, lo)):
            dq_ref[0, :, p * LANES:(p + 1) * LANES] = (d * scale).astype(BF16)

        @pl.when(qi == nq - 1)
        def _():
            dk_ref[0] = dk_acc[...].astype(BF16)
            dv_ref[0] = dv_acc[...].astype(BF16)

    once = pl.Buffered(1)
    qspec = pl.BlockSpec((1, tb, W), lambda g, b, i: (b, i + off, g))
    ospec = pl.BlockSpec((1, tb, W), lambda g, b, i: (b, i, g))
    kvout = pl.BlockSpec((1, S, W), lambda g, b, i: (b, 0, g), pipeline_mode=once)
    return _call(
        body, carry, grid=(ngrp, B, nq), name="sb_bwd", sem=("parallel", "parallel", "arbitrary"),
        args=(h3p, h3p, h3p, o, dmix, gain, cars),
        in_specs=[qspec, pl.BlockSpec((1, CA_PAD + S, W), lambda g, b, i: (b, 0, ngrp + g), pipeline_mode=once),
                  pl.BlockSpec((1, CA_PAD + S, W), lambda g, b, i: (b, 0, 2 * ngrp + g), pipeline_mode=once),
                  ospec, ospec, pl.BlockSpec((1, W), lambda g, b, i: (0, g)),
                  pl.BlockSpec((1, hs, tb, nq), lambda g, b, i: (b, g, i, 0))],
        out_specs=[ospec, kvout, kvout, pl.BlockSpec((1, 1, 1, W), lambda g, b, i: (b, i, 0, g))],
        out_shape=[_sds((B, S, HD), BF16), _sds((B, S, HD), BF16), _sds((B, S, HD), BF16), _sds((B, nq, 1, HD), F32)],
        scratch_shapes=[pltpu.VMEM((S, W), F32), pltpu.VMEM((S, W), F32)])


def _ca_rel_index():
    width = CA_WIN + CA_TQ
    c = np.arange(width)
    dj = np.where(c < CA_WIN, c, c - width)
    return np.clip(CA_PAD - dj, -MAX_REL, MAX_REL) + MAX_REL, width


def _ca_onehot():
    idx, _ = _ca_rel_index()
    return (idx[:, None] == np.arange(2 * MAX_REL + 1)[None, :]).astype(np.float32)


def ca_bias(rel_bias):
    _, width = _ca_rel_index()
    lead = rel_bias.shape[:-1]
    by_offset = jnp.dot(rel_bias, jnp.asarray(_ca_onehot().T), precision=lax.Precision.HIGHEST)
    tile = jnp.broadcast_to(by_offset[..., None, :], lead + (CA_TQ, width)).reshape(lead + (CA_TQ * width,))
    tile = tile[..., :CA_TQ * (width - 1)].reshape(lead + (CA_TQ, width - 1))[..., :CA_WIN]
    t = np.arange(CA_TQ)[:, None] // CHUNK * CHUNK
    j = np.arange(CA_WIN)[None, :]
    return jnp.where((j >= t) & (j < t + CA_PAD + CHUNK), tile, NEG)


def rel_bias_grad(db):
    H = db.shape[0]
    _, width = _ca_rel_index()
    x = jnp.pad(db, ((0, 0), (0, 0), (0, width - 1 - CA_WIN))).reshape(H, CA_TQ * (width - 1))
    x = jnp.pad(x, ((0, 0), (0, CA_TQ))).reshape(H, CA_TQ, width).sum(axis=1)
    return jnp.dot(x, jnp.asarray(_ca_onehot()), precision=lax.Precision.HIGHEST)


def _ca_scores(qm_h, kk, bias_h, valid):
    return jnp.where(valid, _dot(qm_h, kk, NT) + bias_h, NEG)


def _ca_softmax(s):
    e = jnp.exp(s - jnp.max(s, axis=-1, keepdims=True))
    return e * (1.0 / jnp.sum(e, axis=-1, keepdims=True))


def ca_fwd(h3p, bias, gain):
    B, S, HD, hs, W, ngrp = _attn_dims(h3p, gain, FWD_HEADS)
    scale = 1.0 / math.sqrt(HEAD_DIM)
    off = CA_PAD // CA_TQ

    def body(q_ref, k_ref, v_ref, b_ref, g_ref, o_ref, m_ref):
        q0 = pl.program_id(2) * CA_TQ
        ks = pl.ds(pl.multiple_of(q0, CA_TQ), CA_WIN)
        lo = _lane_lo(CA_TQ)
        valid = lax.broadcasted_iota(jnp.int32, (CA_TQ, CA_WIN), 1) + q0 >= CA_PAD
        qm, kk, vv = [], [], []
        for p in range(hs // 2):
            cols = slice(p * LANES, (p + 1) * LANES)
            qm += _split_heads((q_ref[0, :, cols].astype(F32) * scale).astype(BF16), lo)
            kk.append(k_ref[0, ks, cols])
            vv.append(v_ref[0, ks, cols])
        ss = [_ca_scores(qm[h], kk[h // 2], b_ref[h], valid) for h in range(hs)]
        ps = [_ca_softmax(s).astype(BF16) for s in ss]
        pv = [_dot(ps[h], vv[h // 2], NN) for h in range(hs)]
        for p, o in enumerate(_merge_pairs(pv, lo)):
            cols = slice(p * LANES, (p + 1) * LANES)
            o_ref[0, :, cols] = o
            m_ref[0, :, cols] = _rms_fwd(o, g_ref[:, cols], lo)

    ospec = pl.BlockSpec((1, CA_TQ, W), lambda g, b, i: (b, i, g))
    return pl.pallas_call(
        body, grid=(ngrp, B, S // CA_TQ), name="ca_fwd",
        in_specs=[pl.BlockSpec((1, CA_TQ, W), lambda g, b, i: (b, i + off, 3 * ngrp + g)),
                  pl.BlockSpec((1, CA_PAD + S, W), lambda g, b, i: (b, 0, 4 * ngrp + g)),
                  pl.BlockSpec((1, CA_PAD + S, W), lambda g, b, i: (b, 0, 5 * ngrp + g)),
                  pl.BlockSpec((hs, CA_TQ, CA_WIN), lambda g, b, i: (g, 0, 0)),
                  pl.BlockSpec((1, W), lambda g, b, i: (0, g))],
        out_specs=[ospec, ospec], out_shape=[_sds((B, S, HD), F32), _sds((B, S, HD), BF16)],
        compiler_params=_params("parallel", "parallel", "arbitrary"),
    )(h3p, h3p, h3p, bias, gain)


def ca_bwd(h3p, bias, o, dmix, gain, carry=None):
    B, S, HD, hs, W, ngrp = _attn_dims(h3p, gain, BWD_HEADS)
    scale = 1.0 / math.sqrt(HEAD_DIM)
    nq = S // CA_TQ
    off = CA_PAD // CA_TQ

    def body(q_ref, k_ref, v_ref, b_ref, o_ref, dm_ref, g_ref, dq_ref, dk_ref, dv_ref, dg_ref, db_ref, dk_acc, dv_acc):
        bi = pl.program_id(1)
        qi = pl.program_id(2)

        @pl.when(qi == 0)
        def _():
            dk_acc[...] = jnp.zeros_like(dk_acc)
            dv_acc[...] = jnp.zeros_like(dv_acc)

        @pl.when(jnp.logical_and(qi == 0, bi == 0))
        def _():
            db_ref[...] = jnp.zeros_like(db_ref)

        q0 = qi * CA_TQ
        ks = pl.ds(pl.multiple_of(q0, CA_TQ), CA_WIN)
        lo = _lane_lo(CA_TQ)
        valid = lax.broadcasted_iota(jnp.int32, (CA_TQ, CA_WIN), 1) + q0 >= CA_PAD
        qm, dom, kk, vv = [], [], [], []
        for p in range(hs // 2):
            cols = slice(p * LANES, (p + 1) * LANES)
            qm += _split_heads((q_ref[0, :, cols].astype(F32) * scale).astype(BF16), lo)
            do, dg = _rms_bwd(o_ref[0, :, cols], dm_ref[0, :, cols], g_ref[:, cols], lo)
            dg_ref[0, 0, :, cols] = dg
            dom += _split_heads(do.astype(BF16), lo)
            kk.append(k_ref[0, ks, cols])
            vv.append(v_ref[0, ks, cols])
        hh = range(hs)
        ss = [_ca_scores(qm[h], kk[h // 2], b_ref[h], valid) for h in hh]
        dps = [_dot(dom[h], vv[h // 2], NT) for h in hh]
        ps = [_ca_softmax(s) for s in ss]
        dss = [ps[h] * (dps[h] - jnp.sum(ps[h] * dps[h], axis=-1, keepdims=True)) for h in hh]
        for h in hh:
            db_ref[h] += dss[h]
        dsb = [d.astype(BF16) for d in dss]
        pb = [p_.astype(BF16) for p_ in ps]
        dq = [_dot(dsb[h], kk[h // 2], NN) for h in hh]
        for p in range(hs // 2):
            cols = slice(p * LANES, (p + 1) * LANES)
            dk_acc[ks, cols] += _dot(dsb[2 * p], qm[2 * p], TN) + _dot(dsb[2 * p + 1], qm[2 * p + 1], TN)
            dv_acc[ks, cols] += _dot(pb[2 * p], dom[2 * p], TN) + _dot(pb[2 * p + 1], dom[2 * p + 1], TN)
        for p, d in enumerate(_merge_pairs(dq, lo)):
            dq_ref[0, :, p * LANES:(p + 1) * LANES] = (d * scale).astype(BF16)

        @pl.when(qi == nq - 1)
        def _():
            dk_ref[0] = dk_acc[CA_PAD:, :].astype(BF16)
            dv_ref[0] = dv_acc[CA_PAD:, :].astype(BF16)

    once = pl.Buffered(1)
    ospec = pl.BlockSpec((1, CA_TQ, W), lambda g, b, i: (b, i, g))
    kvout = pl.BlockSpec((1, S, W), lambda g, b, i: (b, 0, g), pipeline_mode=once)
    bspec = pl.BlockSpec((hs, CA_TQ, CA_WIN), lambda g, b, i: (g, 0, 0))
    return _call(
        body, carry, grid=(ngrp, B, nq), name="ca_bwd", sem=("parallel", "arbitrary", "arbitrary"),
        args=(h3p, h3p, h3p, bias, o, dmix, gain),
        in_specs=[pl.BlockSpec((1, CA_TQ, W), lambda g, b, i: (b, i + off, 3 * ngrp + g)),
                  pl.BlockSpec((1, CA_PAD + S, W), lambda g, b, i: (b, 0, 4 * ngrp + g), pipeline_mode=once),
                  pl.BlockSpec((1, CA_PAD + S, W), lambda g, b, i: (b, 0, 5 * ngrp + g), pipeline_mode=once),
                  bspec, ospec, pl.BlockSpec((1, CA_TQ, W), lambda g, b, i: (b, i, ngrp + g)),
                  pl.BlockSpec((1, W), lambda g, b, i: (0, g))],
        out_specs=[ospec, kvout, kvout, pl.BlockSpec((1, 1, 1, W), lambda g, b, i: (b, i, 0, g)), bspec],
        out_shape=[_sds((B, S, HD), BF16), _sds((B, S, HD), BF16), _sds((B, S, HD), BF16),
                   _sds((B, nq, 1, HD), F32), _sds((HD // HEAD_DIM, CA_TQ, CA_WIN), F32)],
        scratch_shapes=[pltpu.VMEM((CA_PAD + S, W), F32), pltpu.VMEM((CA_PAD + S, W), F32)])


_ANY = pl.BlockSpec(memory_space=pl.ANY)
_MESH = pl.DeviceIdType.MESH


def _mesh_pos():
    return lax.axis_index("x"), lax.axis_index("y"), lax.axis_index("c")


def _dev_index(p):
    return 4 * p[0] + 2 * p[1] + p[2]


def _flip(pos, k):
    return tuple(1 - v if (k >> (2 - a)) & 1 else v for a, v in enumerate(pos))


class _Plan:
    def __init__(self, operands, n_matrices):
        self.operands = list(operands)
        self.n_ops = len(self.operands)
        self.nm = n_matrices
        self.scratch = [pltpu.SemaphoreType.DMA((7, self.nm)), pltpu.SemaphoreType.DMA((7, self.nm)),
                        pltpu.SemaphoreType.DMA((self.nm,))]


class _Gather(_Plan):
    def __init__(self, mats):
        super().__init__(mats, len(mats))
        self.rows = [m.shape[0] for m in mats]
        self.out_shape = [_sds((N_DEV * m.shape[0], m.shape[1]), m.dtype) for m in mats]

    def begin(self, ins, outs, sems):
        nm, rows = self.nm, self.rows
        send_sems, recv_sems, local_sems = sems
        x, y, c = _mesh_pos()
        me, sibling = (x, y, c), (x, y, 1 - c)
        chips = [(1 - x, y), (x, 1 - y), (1 - x, 1 - y)]

        def block(m, p):
            start = pl.multiple_of(_dev_index(p) * rows[m], 16)
            return outs[m].at[pl.ds(start, rows[m]), :]

        def copy(k, m, blk, to, src=None):
            return pltpu.make_async_remote_copy(
                src_ref=block(m, blk) if src is None else src, dst_ref=block(m, blk),
                send_sem=send_sems.at[k, m], recv_sem=recv_sems.at[k, m], device_id=to, device_id_type=_MESH)

        def mine():
            return [pltpu.make_async_copy(ins[m], block(m, me), local_sems.at[m]) for m in range(nm)]

        def first():
            own = [copy(0, m, me, sibling, src=ins[m]) for m in range(nm)]
            return own + [copy(1 + j, m, me, (*chip, c), src=ins[m]) for j, chip in enumerate(chips) for m in range(nm)]

        def start():
            for cp in mine() + first():
                cp.start()

        def finish():
            passed = []
            for j, chip in enumerate(chips):
                for m in range(nm):
                    copy(1 + j, m, (*chip, c), me).wait_recv()
                    fwd = copy(4 + j, m, (*chip, c), sibling)
                    fwd.start()
                    passed.append(fwd)
            for m in range(nm):
                copy(0, m, sibling, me).wait_recv()
            for j, chip in enumerate(chips):
                for m in range(nm):
                    copy(4 + j, m, (*chip, 1 - c), me).wait_recv()
            for cp in first() + passed:
                cp.wait_send()
            for cp in mine():
                cp.wait()

        return start, finish


class _Exchange(_Plan):
    def __init__(self, grads):
        self.where = [(i, j) for i, g in enumerate(grads) for j in (range(g.shape[0]) if g.ndim == 3 else [None])]
        super().__init__(grads, len(self.where))
        self.rows = [grads[i].shape[-2] // N_DEV for i, _ in self.where]
        self.out_shape = [_sds(g.shape[:-2] + (N_DEV, g.shape[-2] // N_DEV, g.shape[-1]), g.dtype) for g in grads]

    def begin(self, ins, outs, sems):
        nm, rows = self.nm, self.rows
        send_sems, recv_sems, local_sems = sems
        me = _mesh_pos()
        my = _dev_index(me)

        def piece(m, idx):
            i, j = self.where[m]
            rows_ = pl.ds(pl.multiple_of(idx * rows[m], 16), rows[m])
            return ins[i].at[rows_, :] if j is None else ins[i].at[j, rows_, :]

        def landing(m, slot):
            i, j = self.where[m]
            return outs[i].at[slot] if j is None else outs[i].at[j, slot]

        def copy(k, m, src_idx, slot, to):
            return pltpu.make_async_remote_copy(
                src_ref=piece(m, src_idx), dst_ref=landing(m, slot),
                send_sem=send_sems.at[k - 1, m], recv_sem=recv_sems.at[k - 1, m], device_id=to, device_id_type=_MESH)

        def mine():
            return [pltpu.make_async_copy(piece(m, my), landing(m, my), local_sems.at[m]) for m in range(nm)]

        def sends():
            return [copy(k, m, _dev_index(_flip(me, k)), my, _flip(me, k)) for k in range(1, N_DEV) for m in range(nm)]

        def start():
            for cp in mine() + sends():
                cp.start()

        def finish():
            for k in range(1, N_DEV):
                peer = _flip(me, k)
                for m in range(nm):
                    copy(k, m, my, _dev_index(peer), peer).wait_recv()
            for cp in sends():
                cp.wait_send()
            for cp in mine():
                cp.wait()

        return start, finish


def _run_plan(plan, name):
    nm = plan.n_ops

    def body(*refs):
        start, finish = plan.begin(refs[:nm], refs[nm:2 * nm], refs[2 * nm:])
        start()
        finish()

    return pl.pallas_call(body, name=name, in_specs=[_ANY] * nm, out_specs=[_ANY] * nm, out_shape=plan.out_shape,
                          scratch_shapes=plan.scratch)(*plan.operands)


def _call(body, carry, *, grid, in_specs, out_specs, out_shape, args, name, sem, scratch_shapes=()):
    if carry is None:
        outs = pl.pallas_call(body, grid=grid, name=name, in_specs=in_specs, out_specs=out_specs, out_shape=out_shape,
                              scratch_shapes=list(scratch_shapes), compiler_params=_params(*sem))(*args)
        return outs, None
    n_in, n_out, n_sc, nm = len(in_specs), len(out_specs), len(scratch_shapes), carry.n_ops

    def carrier(*refs):
        refs = list(refs)
        own_in, refs = refs[:n_in], refs[n_in:]
        c_in, refs = refs[:nm], refs[nm:]
        own_out, refs = refs[:n_out], refs[n_out:]
        c_out, refs = refs[:nm], refs[nm:]
        own_sc, c_sc = refs[:n_sc], refs[n_sc:]
        start, finish = carry.begin(c_in, c_out, c_sc)
        first = last = None
        for axis, n in enumerate(grid):
            i = pl.program_id(axis)
            first = (i == 0) if first is None else jnp.logical_and(first, i == 0)
            last = (i == n - 1) if last is None else jnp.logical_and(last, i == n - 1)
        pl.when(first)(start)
        body(*own_in, *own_out, *own_sc)
        pl.when(last)(finish)

    outs = pl.pallas_call(
        carrier, grid=grid, name=name + "_carry", in_specs=list(in_specs) + [_ANY] * nm,
        out_specs=list(out_specs) + [_ANY] * nm, out_shape=list(out_shape) + carry.out_shape,
        scratch_shapes=list(scratch_shapes) + carry.scratch,
        compiler_params=_params(*["arbitrary"] * len(grid)),
    )(*args, *carry.operands)
    return outs[:n_out], outs[n_out:]


def sum_slots(r, j=None):
    n, R, D = r.shape[-3:]
    tc = _row_tile(D, 256)

    def body(r_ref, o_ref):
        slot = (lambda s: r_ref[s]) if j is None else (lambda s: r_ref[0, s])
        acc = slot(0).astype(F32)
        for s in range(1, n):
            acc = acc + slot(s).astype(F32)
        o_ref[...] = acc

    spec = (pl.BlockSpec((n, R, tc), lambda i: (0, 0, i)) if j is None
            else pl.BlockSpec((1, n, R, tc), lambda i: (j, 0, 0, i)))
    return pl.pallas_call(
        body, grid=(D // tc,), name="sum_slots", in_specs=[spec],
        out_specs=pl.BlockSpec((R, tc), lambda i: (0, i)), out_shape=_sds((R, D), F32),
        compiler_params=_params("parallel"),
    )(r)


def allreduce_small(v):
    R, C = v.shape

    def body(v_ref, o_ref, buf, send_sems, recv_sems):
        me = _mesh_pos()
        my = _dev_index(me)
        buf[my] = v_ref[...]

        def copy(k, slot, to):
            return pltpu.make_async_remote_copy(
                src_ref=v_ref, dst_ref=buf.at[slot], send_sem=send_sems.at[k - 1], recv_sem=recv_sems.at[k - 1],
                device_id=to, device_id_type=_MESH)

        sends = []
        for k in range(1, N_DEV):
            cp = copy(k, my, _flip(me, k))
            cp.start()
            sends.append(cp)
        for k in range(1, N_DEV):
            peer = _flip(me, k)
            copy(k, _dev_index(peer), peer).wait_recv()
        acc = buf[0]
        for s in range(1, N_DEV):
            acc = acc + buf[s]
        o_ref[...] = acc
        for cp in sends:
            cp.wait_send()

    vm = pl.BlockSpec(memory_space=pltpu.VMEM)
    return pl.pallas_call(
        body, name="allreduce_small", in_specs=[vm], out_specs=vm, out_shape=_sds((R, C), F32),
        scratch_shapes=[pltpu.VMEM((N_DEV, R, C), F32), pltpu.SemaphoreType.DMA((7,)), pltpu.SemaphoreType.DMA((7,))],
    )(v)


WEIGHTS = ["ffn1_w_gate", "ffn1_w_up", "ffn1_w_down", "ln1_g", "ln1_b", "w_in", "rel_bias", "sb_out_g", "ca_out_g",
           "w_out", "ln2_g", "ln2_b", "ffn2_w_gate", "ffn2_w_up", "ffn2_w_down", "ln3_g", "ln3_b"]
BIG = ["ffn1_w_gate", "ffn1_w_up", "ffn2_w_gate", "ffn2_w_up", "w_in", "ffn1_w_down", "ffn2_w_down", "w_out"]
TRANSPOSED = BIG[:5]
SMALL = [n for n in WEIGHTS if n not in BIG]


def _pack(vals):
    flat = jnp.concatenate([v.reshape(-1).astype(F32) for v in vals])
    pad = -flat.shape[0] % (8 * 128)
    return jnp.pad(flat, (0, pad)).reshape(-1, 128)


def _unpack(packed, like):
    flat = packed.reshape(-1)
    out, off = [], 0
    for v in like:
        out.append(flat[off:off + v.size].reshape(v.shape))
        off += v.size
    return out


def _row_blocks(w, n):
    r = w.shape[0] // n
    return [w[i * r:(i + 1) * r] for i in range(n)]


def _layer_fwd(x, xb, W, P, bias, l, B, S, next_gather, rest=None):
    T = B * S
    sv = {"xb": xb, "bias": bias}
    (sv["hg1"], sv["hu1"], sv["a1"]), got = ffn_up(xb, W["ffn1_w_gate"], W["ffn1_w_up"], carry=rest and rest[1])
    if rest:
        W.update(zip(rest[0], got))
    sv["u1"], x1, sv["x1b"] = mm_nn_ln([(sv["a1"], W["ffn1_w_down"])], x, P["ln1_g"][l], P["ln1_b"][l], FFN_RESIDUAL,
                                       "ffn_down_ln")

    sv["h"] = proj_in(sv["x1b"], W["w_in"], B, S)
    sv["gA"] = P["sb_out_g"][l].reshape(1, -1)
    sv["gB"] = P["ca_out_g"][l].reshape(1, -1)
    (sv["oa"], ma, sv["cars"]), gathered = sb_fwd(sv["h"], sv["gA"], carry=next_gather)
    sv["ob"], mb = ca_fwd(sv["h"], bias, sv["gB"])
    sv["ma"], sv["mb"] = ma.reshape(T, -1), mb.reshape(T, -1)
    sv["u2"], x2, sv["x2b"] = mm_nn_ln(list(zip([sv["ma"], sv["mb"]], _row_blocks(W["w_out"], 2))), x1,
                                       P["ln2_g"][l], P["ln2_b"][l], 1.0, "proj_out_ln")

    (sv["hg2"], sv["hu2"], sv["a2"]), _ = ffn_up(sv["x2b"], W["ffn2_w_gate"], W["ffn2_w_up"])
    sv["u3"], x3, x3b = mm_nn_ln([(sv["a2"], W["ffn2_w_down"])], x2, P["ln3_g"][l], P["ln3_b"][l], FFN_RESIDUAL,
                                 "ffn_down_ln")
    return x3, x3b, sv, gathered


class _Riders:
    def __init__(self):
        self.received = []

    def plan(self, entries):
        return _Exchange([e[2] for e in entries]) if entries else None

    def landed(self, entries, slots):
        for (names, layer, _), r in zip(entries, slots or []):
            for j, name in enumerate(names):
                self.received.append((name, layer, r, j if r.ndim == 4 else None))


def _ffn_bwd(dr, dxa, xb, hg, hu, a, wg, wu, wd, riders, ride, below):
    dhg, dhu = ffn_bwd_mid(dr, wd, hg, hu)
    pairs = [(dhg, wg), (dhu, wu)]
    if below is None:
        down = mm_nn(pairs, add=dxa, name="ffn_dx")
    else:
        down = mm_nn_ln_bwd(pairs, dxa, *below, name="ffn_dx_ln")
    grads, slots = ffn_dw(dhg, dhu, a, xb, dr, carry=riders.plan(ride))
    riders.landed(ride, slots)
    return down, grads


def _layer_bwd(dxa, dr, sv, W, P, l, B, S, riders, pending, below):
    T = B * S
    G = {}
    (dxa, dr, dg, db), g_ffn2 = _ffn_bwd(
        dr, dxa, sv["x2b"], sv["hg2"], sv["hu2"], sv["a2"], W["ffn2_w_gate"], W["ffn2_w_up"], W["ffn2_w_down"],
        riders, pending, (sv["u2"], P["ln2_g"][l], 1.0))
    G["ln2_g"], G["ln2_b"] = dg.sum(0), db.sum(0)
    dmix = mm_nt(dr, W["w_out"], name="proj_out_dx").reshape(B, S, -1)
    g_out = mm_tn([sv["ma"], sv["mb"]], dr, name="proj_out_dw")[0]
    ride = [(("ffn2_w_gate", "ffn2_w_up", "ffn2_w_down"), l, g_ffn2)]
    (dqa, dka, dva, dga), slots = sb_bwd(sv["h"], sv["oa"], dmix, sv["gA"], sv["cars"], carry=riders.plan(ride))
    riders.landed(ride, slots)
    (dqb, dkb, dvb, dgb, dbias), _ = ca_bwd(sv["h"], sv["bias"], sv["ob"], dmix, sv["gB"])
    G["sb_out_g"] = dga.sum((0, 1, 2))
    G["ca_out_g"] = dgb.sum((0, 1, 2))
    G["rel_bias"] = rel_bias_grad(dbias)
    dh = [t.reshape(T, -1) for t in (dqa, dka, dva, dqb, dkb, dvb)]
    dxa, dr, dg, db = mm_nn_ln_bwd(list(zip(dh, _row_blocks(W["w_in"], 6))), dxa, sv["u1"], P["ln1_g"][l], FFN_RESIDUAL,
                                   name="proj_in_dx_ln")
    G["ln1_g"], G["ln1_b"] = dg.sum(0), db.sum(0)
    g_in = mm_tn(dh, sv["x1b"], name="proj_in_dw")[0]

    down, g_ffn1 = _ffn_bwd(dr, dxa, sv["xb"], sv["hg1"], sv["hu1"], sv["a1"], W["ffn1_w_gate"], W["ffn1_w_up"],
                            W["ffn1_w_down"], riders, [(("w_in",), l, g_in), (("w_out",), l, g_out)], below)
    return down, G, [(("ffn1_w_gate", "ffn1_w_up", "ffn1_w_down"), l, g_ffn1)]


def kernel(x, ffn1_w_gate, ffn1_w_up, ffn1_w_down, ln1_g, ln1_b, w_in, rel_bias, sb_out_g, ca_out_g, w_out, ln2_g, ln2_b, ffn2_w_gate, ffn2_w_up, ffn2_w_down, ln3_g, ln3_b, loss_target, m_ffn1_w_gate, m_ffn1_w_up, m_ffn1_w_down, m_ln1_g, m_ln1_b, m_w_in, m_rel_bias, m_sb_out_g, m_ca_out_g, m_w_out, m_ln2_g, m_ln2_b, m_ffn2_w_gate, m_ffn2_w_up, m_ffn2_w_down, m_ln3_g, m_ln3_b, v_ffn1_w_gate, v_ffn1_w_up, v_ffn1_w_down, v_ln1_g, v_ln1_b, v_w_in, v_rel_bias, v_sb_out_g, v_ca_out_g, v_w_out, v_ln2_g, v_ln2_b, v_ffn2_w_gate, v_ffn2_w_up, v_ffn2_w_down, v_ln3_g, v_ln3_b):
    given = dict(locals())
    P = {n: given[n] for n in WEIGHTS}
    M = {n: given["m_" + n] for n in WEIGHTS}
    V = {n: given["v_" + n] for n in WEIGHTS}
    B, S, D = x.shape
    L = ln1_g.shape[0]

    local = {n: (jnp.swapaxes(P[n], 1, 2) if n in TRANSPOSED else P[n]).astype(BF16) for n in BIG}
    bias = ca_bias(rel_bias)

    xf = x.reshape(B * S, D)
    xb = xf.astype(BF16)
    W, saved = [], []
    first, later = BIG[:2], BIG[2:]
    gathered = _run_plan(_Gather([local[n][0] for n in first]), "gather_weights")
    rest = (later, _Gather([local[n][0] for n in later]))
    for l in range(L):
        W.append(dict(zip(BIG, gathered)))
        next_gather = _Gather([local[n][l + 1] for n in BIG]) if l + 1 < L else None
        xf, xb, sv, gathered = _layer_fwd(xf, xb, W[l], P, bias[l], l, B, S, next_gather, rest)
        rest = None
        saved.append(sv)
    dy, loss_part = loss_head(xf, loss_target.reshape(B * S, D))

    big_g = {n: [None] * L for n in BIG}
    small_g = {n: [None] * L for n in SMALL}
    riders, pending = _Riders(), []
    down = ln_bwd(saved[L - 1]["u3"], dy, ln3_g[L - 1], FFN_RESIDUAL)
    for l in reversed(range(L)):
        dxa, dr, dg, db = down
        small_g["ln3_g"][l], small_g["ln3_b"][l] = dg.sum(0), db.sum(0)
        below = (saved[l - 1]["u3"], ln3_g[l - 1], FFN_RESIDUAL) if l > 0 else None
        down, G, pending = _layer_bwd(dxa, dr, saved[l], W[l], P, l, B, S, riders, pending, below)
        for n, g in G.items():
            small_g[n][l] = g
    dx = down
    riders.landed(pending, _run_plan(riders.plan(pending), "exchange_grads"))
    for n, l, slots, j in riders.received:
        big_g[n][l] = sum_slots(slots, j)

    grads = {}
    for n in BIG:
        g = jnp.stack(big_g[n])
        grads[n] = jnp.swapaxes(g, 1, 2) if n in TRANSPOSED else g
    small_like = [P[n] for n in SMALL]
    packed = _pack([jnp.stack(small_g[n]) for n in SMALL] + [loss_part.sum()])
    total = allreduce_small(packed)
    *small_vals, loss = _unpack(total, small_like + [jnp.zeros((), F32)])
    grads.update(dict(zip(SMALL, small_vals)))

    delta, new_m, new_v = {}, {}, {}
    for n in BIG:
        shape = P[n].shape
        two_d = lambda a: a.reshape(shape[0] * shape[1], shape[2])
        d, m, v = adamw(two_d(P[n]), two_d(grads[n]), two_d(M[n]), two_d(V[n]))
        delta[n], new_m[n], new_v[n] = d.reshape(shape), m.reshape(shape), v.reshape(shape)
    one = jnp.ones((), F32)
    d, m, v = adamw(_pack(small_like + [one]), total, _pack([M[n] for n in SMALL] + [one]), _pack([V[n] for n in SMALL] + [one]))
    for dst, src in ((delta, d), (new_m, m), (new_v, v)):
        dst.update(dict(zip(SMALL, _unpack(src, small_like))))

    return (loss, dx.reshape(B, S, D), *[grads[n] for n in WEIGHTS], *[delta[n] for n in WEIGHTS],
            *[new_m[n] for n in WEIGHTS], *[new_v[n] for n in WEIGHTS])
```

```python
import math

import jax
import jax.numpy as jnp
import numpy as np
from jax import lax
from jax.experimental import pallas as pl
from jax.experimental.pallas import tpu as pltpu

F32 = jnp.float32
BF16 = jnp.bfloat16

HEAD_DIM = 64
CHUNK = 64
N_PREV_CHUNKS = 8
MAX_REL = 128
DEPTH = 4
FFN_RESIDUAL = 0.5
ALPHA = (2 * DEPTH) ** 0.25
LN_EPS = 1e-5
RMS_EPS = 1e-6
ADAM_LR = 0.001
ADAM_B1 = 0.9
ADAM_B2 = 0.999
ADAM_EPS = 1e-08
ADAM_WD = 0.01
ADAM_STEP = 10

N_DEV = 8
LANES = 128
VMEM_LIMIT_BYTES = 56 * 1024 * 1024
ROW_TILE = 512
SB_BLOCK = 128
CA_PAD = CHUNK * N_PREV_CHUNKS
CA_TQ = 2 * CHUNK
CA_WIN = CA_PAD + CA_TQ
FWD_HEADS = 8
BWD_HEADS = 8
NEG = -1e30
SB_SKIP_LOG = -104.0

NN = ((1,), (0,))
NT = ((1,), (1,))
TN = ((0,), (0,))


def _dot(a, b, dims):
    return lax.dot_general(a, b, (dims, ((), ())), preferred_element_type=F32)


def _params(*sem):
    return pltpu.CompilerParams(dimension_semantics=sem, vmem_limit_bytes=VMEM_LIMIT_BYTES)


def _sds(shape, dtype):
    return jax.ShapeDtypeStruct(shape, dtype)


def _row_tile(n, want=ROW_TILE):
    t = min(want, n)
    while n % t:
        t //= 2
    assert t >= 8, (n, want)
    return t


def ffn_up(xb, wg, wu, carry=None):
    T, D = xb.shape
    F = wg.shape[0]
    tm = _row_tile(T)
    fc = _row_tile(F, 256)

    def body(x_ref, wg_ref, wu_ref, sg_ref, t_ref, a_ref):
        x = x_ref[...]
        for j in range(F // fc):
            sl = pl.ds(j * fc, fc)
            hg = _dot(x, wg_ref[sl, :], NT)
            hu = _dot(x, wu_ref[sl, :], NT)
            s = jax.nn.sigmoid(hg)
            sg = hg * s
            sg_ref[:, sl] = sg.astype(BF16)
            t_ref[:, sl] = (hu * (s + sg * (1.0 - s))).astype(BF16)
            a_ref[:, sl] = (sg * hu).astype(BF16)

    row = pl.BlockSpec((tm, F), lambda i: (i, 0))
    w = pl.BlockSpec((F, D), lambda i: (0, 0))
    return _call(body, carry, grid=(T // tm,), name="ffn_up", sem=("parallel",), args=(xb, wg, wu),
                 in_specs=[pl.BlockSpec((tm, D), lambda i: (i, 0)), w, w],
                 out_specs=[row, row, row], out_shape=[_sds((T, F), BF16)] * 3)


def ffn_bwd_mid(dr, wd, da_dhu, da_dhg):
    T, D = dr.shape
    F = wd.shape[0]
    tm = _row_tile(T)
    fc = _row_tile(F, 256)

    def body(dr_ref, wd_ref, sg_ref, t_ref, dhg_ref, dhu_ref):
        dr_ = dr_ref[...]
        for j in range(F // fc):
            sl = pl.ds(j * fc, fc)
            da = _dot(dr_, wd_ref[sl, :], NT)
            dhu_ref[:, sl] = (da * sg_ref[:, sl].astype(F32)).astype(BF16)
            dhg_ref[:, sl] = (da * t_ref[:, sl].astype(F32)).astype(BF16)

    row = pl.BlockSpec((tm, F), lambda i: (i, 0))
    return pl.pallas_call(
        body, grid=(T // tm,), name="ffn_bwd_mid",
        in_specs=[pl.BlockSpec((tm, D), lambda i: (i, 0)), pl.BlockSpec((F, D), lambda i: (0, 0)), row, row],
        out_specs=[row, row], out_shape=[_sds((T, F), BF16)] * 2,
        compiler_params=_params("parallel"),
    )(dr, wd, da_dhu, da_dhg)


def mm_nn(pairs, add=None, out_dtype=F32, name="mm_nn"):
    M = pairs[0][0].shape[0]
    N = pairs[0][1].shape[1]
    tm = _row_tile(M)
    nc = _row_tile(N, 512)
    n_pairs = len(pairs)

    def body(*refs):
        a_refs = refs[:n_pairs]
        b_refs = refs[n_pairs:2 * n_pairs]
        add_ref = refs[2 * n_pairs] if add is not None else None
        o_ref = refs[-1]
        for j in range(N // nc):
            sl = pl.ds(j * nc, nc)
            acc = _dot(a_refs[0][...], b_refs[0][:, sl], NN)
            for a_ref, b_ref in zip(a_refs[1:], b_refs[1:]):
                acc = acc + _dot(a_ref[...], b_ref[:, sl], NN)
            if add_ref is not None:
                acc = acc + add_ref[:, sl]
            o_ref[:, sl] = acc.astype(out_dtype)

    in_specs = [pl.BlockSpec((tm, a.shape[1]), lambda i: (i, 0)) for a, _ in pairs]
    in_specs += [pl.BlockSpec(b.shape, lambda i: (0, 0)) for _, b in pairs]
    args = [a for a, _ in pairs] + [b for _, b in pairs]
    if add is not None:
        in_specs.append(pl.BlockSpec((tm, N), lambda i: (i, 0)))
        args.append(add)
    return pl.pallas_call(
        body, grid=(M // tm,), name=name, in_specs=in_specs,
        out_specs=pl.BlockSpec((tm, N), lambda i: (i, 0)), out_shape=_sds((M, N), out_dtype),
        compiler_params=_params("parallel"),
    )(*args)


def _mm_specs(pairs, tm):
    specs = [pl.BlockSpec((tm, a.shape[1]), lambda i: (i, 0)) for a, _ in pairs]
    specs += [pl.BlockSpec(b.shape, lambda i: (0, 0), pipeline_mode=pl.Buffered(1)) for _, b in pairs]
    return specs, [a for a, _ in pairs] + [b for _, b in pairs]


def _mm_sum(a_refs, b_refs):
    acc = _dot(a_refs[0][...], b_refs[0][...], NN)
    for a_ref, b_ref in zip(a_refs[1:], b_refs[1:]):
        acc = acc + _dot(a_ref[...], b_ref[...], NN)
    return acc


def _ln_fwd_rows(u, g, b):
    mu = jnp.mean(u, axis=-1, keepdims=True)
    xc = u - mu
    var = jnp.mean(xc * xc, axis=-1, keepdims=True)
    return xc * lax.rsqrt(var + LN_EPS) * g + b


def _ln_bwd_rows(u, dy, g):
    mu = jnp.mean(u, axis=-1, keepdims=True)
    xc = u - mu
    var = jnp.mean(xc * xc, axis=-1, keepdims=True)
    rstd = lax.rsqrt(var + LN_EPS)
    xh = xc * rstd
    dxh = dy * g
    m1 = jnp.mean(dxh, axis=-1, keepdims=True)
    m2 = jnp.mean(dxh * xh, axis=-1, keepdims=True)
    return rstd * (dxh - m1 - xh * m2), dy * xh


def mm_nn_ln(pairs, x, g, b, res_scale, name):
    M, N = x.shape
    tm = _row_tile(M)
    n_pairs = len(pairs)

    def body(*refs):
        a_refs, b_refs = refs[:n_pairs], refs[n_pairs:2 * n_pairs]
        x_ref, g_ref, b_ref, u_ref, y_ref, yb_ref = refs[2 * n_pairs:]
        u = ALPHA * x_ref[...] + res_scale * _mm_sum(a_refs, b_refs)
        y = _ln_fwd_rows(u, g_ref[...], b_ref[...])
        u_ref[...] = u
        y_ref[...] = y
        yb_ref[...] = y.astype(BF16)

    in_specs, args = _mm_specs(pairs, tm)
    row = pl.BlockSpec((tm, N), lambda i: (i, 0))
    vec = pl.BlockSpec((1, N), lambda i: (0, 0))
    return pl.pallas_call(
        body, grid=(M // tm,), name=name, in_specs=in_specs + [row, vec, vec],
        out_specs=[row, row, row], out_shape=[_sds((M, N), F32), _sds((M, N), F32), _sds((M, N), BF16)],
        compiler_params=_params("parallel"),
    )(*args, x, g.reshape(1, N), b.reshape(1, N))


def mm_nn_ln_bwd(pairs, add, u, g, branch_scale, name):
    M, N = u.shape
    tm = _row_tile(M)
    n_pairs = len(pairs)

    def body(*refs):
        a_refs, b_refs = refs[:n_pairs], refs[n_pairs:2 * n_pairs]
        add_ref, u_ref, g_ref, dxa_ref, dr_ref, dg_ref, db_ref = refs[2 * n_pairs:]

        @pl.when(pl.program_id(0) == 0)
        def _():
            dg_ref[...] = jnp.zeros_like(dg_ref)
            db_ref[...] = jnp.zeros_like(db_ref)

        dy = _mm_sum(a_refs, b_refs) + add_ref[...]
        du, dyxh = _ln_bwd_rows(u_ref[...], dy, g_ref[...])
        dxa_ref[...] = ALPHA * du
        dr_ref[...] = (branch_scale * du).astype(BF16)
        dg_ref[...] += jnp.sum(dyxh.reshape(tm // 8, 8, N), axis=0)
        db_ref[...] += jnp.sum(dy.reshape(tm // 8, 8, N), axis=0)

    in_specs, args = _mm_specs(pairs, tm)
    row = pl.BlockSpec((tm, N), lambda i: (i, 0))
    acc = pl.BlockSpec((8, N), lambda i: (0, 0))
    return pl.pallas_call(
        body, grid=(M // tm,), name=name, in_specs=in_specs + [row, row, pl.BlockSpec((1, N), lambda i: (0, 0))],
        out_specs=[row, row, acc, acc],
        out_shape=[_sds((M, N), F32), _sds((M, N), BF16), _sds((8, N), F32), _sds((8, N), F32)],
        compiler_params=_params("arbitrary"),
    )(*args, add, u, g.reshape(1, N))


def mm_nt(a, b, out_dtype=F32, name="mm_nt"):
    M, K = a.shape
    N = b.shape[0]
    tm = _row_tile(M)
    nc = _row_tile(N, 512)

    def body(a_ref, b_ref, o_ref):
        a_ = a_ref[...]
        for j in range(N // nc):
            sl = pl.ds(j * nc, nc)
            o_ref[:, sl] = _dot(a_, b_ref[sl, :], NT).astype(out_dtype)

    return pl.pallas_call(
        body, grid=(M // tm,), name=name,
        in_specs=[pl.BlockSpec((tm, K), lambda i: (i, 0)), pl.BlockSpec((N, K), lambda i: (0, 0))],
        out_specs=pl.BlockSpec((tm, N), lambda i: (i, 0)), out_shape=_sds((M, N), out_dtype),
        compiler_params=_params("parallel"),
    )(a, b)


def proj_in(xb, w, B, S):
    D = xb.shape[1]
    N = w.shape[0]
    tm = CA_PAD
    nblk = S // tm
    nc = _row_tile(N, 512)
    assert S % tm == 0

    def body(x_ref, w_ref, o_ref):
        i = pl.program_id(1)

        @pl.when(i == 0)
        def _():
            o_ref[...] = jnp.zeros_like(o_ref)

        @pl.when(i > 0)
        def _():
            x = x_ref[...]
            for j in range(N // nc):
                sl = pl.ds(j * nc, nc)
                o_ref[0, :, sl] = _dot(x, w_ref[sl, :], NT).astype(BF16)

    return pl.pallas_call(
        body, grid=(B, nblk + 1), name="proj_in",
        in_specs=[pl.BlockSpec((tm, D), lambda b, i: (b * nblk + jnp.maximum(i - 1, 0), 0)),
                  pl.BlockSpec((N, D), lambda b, i: (0, 0))],
        out_specs=pl.BlockSpec((1, tm, N), lambda b, i: (b, i, 0)), out_shape=_sds((B, CA_PAD + S, N), BF16),
        compiler_params=_params("parallel", "arbitrary"),
    )(xb, w)


def mm_tn(parts, b, name="mm_tn", carry=None):
    T, N = b.shape
    widths = [a.shape[1] for a in parts]
    M = sum(widths)
    tk = _row_tile(T)
    steps = T // tk
    n_parts = len(parts)

    def body(*refs):
        a_refs, b_ref, o_ref, acc_ref = refs[:n_parts], refs[n_parts], refs[n_parts + 1], refs[n_parts + 2]
        i = pl.program_id(0)

        @pl.when(i == 0)
        def _():
            acc_ref[...] = jnp.zeros_like(acc_ref)

        b_ = b_ref[...]
        row = 0
        for a_ref, width in zip(a_refs, widths):
            mc = _row_tile(width, 256)
            for j in range(width // mc):
                acc_ref[pl.ds(row, mc), :] += _dot(a_ref[:, pl.ds(j * mc, mc)], b_, TN)
                row += mc

        @pl.when(i == steps - 1)
        def _():
            o_ref[...] = acc_ref[...].astype(BF16)

    (out,), got = _call(
        body, carry, grid=(steps,), name=name, sem=("arbitrary",), args=(*parts, b),
        in_specs=[pl.BlockSpec((tk, w), lambda i: (i, 0)) for w in widths] + [pl.BlockSpec((tk, N), lambda i: (i, 0))],
        out_specs=[pl.BlockSpec((M, N), lambda i: (0, 0))], out_shape=[_sds((M, N), BF16)],
        scratch_shapes=[pltpu.VMEM((M, N), F32)])
    return out, got


def ffn_dw(dhg, dhu, a, xb, dr, carry=None):
    T, F = dhg.shape
    D = xb.shape[1]
    tk = _row_tile(T)
    mc = _row_tile(F, 256)
    steps = T // tk
    last = steps - 1

    def body(dhg_ref, dhu_ref, a_ref, xb_ref, dr_ref, o_ref, acc_ref):
        p = pl.program_id(0)
        i = pl.program_id(1)

        @pl.when(i == 0)
        def _():
            acc_ref[...] = jnp.zeros_like(acc_ref)

        def accumulate(lhs_ref, rhs_ref):
            rhs = rhs_ref[...]
            for j in range(F // mc):
                sl = pl.ds(j * mc, mc)
                acc_ref[sl, :] += _dot(lhs_ref[:, sl], rhs, TN)

        pl.when(p == 0)(lambda: accumulate(dhg_ref, xb_ref))
        pl.when(p == 1)(lambda: accumulate(dhu_ref, xb_ref))
        pl.when(p == 2)(lambda: accumulate(a_ref, dr_ref))

        @pl.when(i == last)
        def _():
            o_ref[0] = acc_ref[...].astype(BF16)

    def during(phases, width):
        lo, hi = phases
        return pl.BlockSpec((tk, width), lambda p, i: (jnp.where(p < lo, 0, jnp.where(p > hi, last, i)), 0))

    (out,), got = _call(
        body, carry, grid=(3, steps), name="ffn_dw", sem=("arbitrary", "arbitrary"), args=(dhg, dhu, a, xb, dr),
        in_specs=[during((0, 0), F), during((1, 1), F), during((2, 2), F), during((0, 1), D), during((2, 2), D)],
        out_specs=[pl.BlockSpec((1, F, D), lambda p, i: (p, 0, 0))], out_shape=[_sds((3, F, D), BF16)],
        scratch_shapes=[pltpu.VMEM((F, D), F32)])
    return out, got


def ln_bwd(u, dy, g, branch_scale):
    T, D = u.shape
    tm = _row_tile(T)

    def body(u_ref, dy_ref, g_ref, dxa_ref, dr_ref, dg_ref, db_ref):
        @pl.when(pl.program_id(0) == 0)
        def _():
            dg_ref[...] = jnp.zeros_like(dg_ref)
            db_ref[...] = jnp.zeros_like(db_ref)

        dy_ = dy_ref[...]
        du, dyxh = _ln_bwd_rows(u_ref[...], dy_, g_ref[...])
        dxa_ref[...] = ALPHA * du
        dr_ref[...] = (branch_scale * du).astype(BF16)
        dg_ref[...] += jnp.sum(dyxh.reshape(tm // 8, 8, D), axis=0)
        db_ref[...] += jnp.sum(dy_.reshape(tm // 8, 8, D), axis=0)

    row = pl.BlockSpec((tm, D), lambda i: (i, 0))
    acc = pl.BlockSpec((8, D), lambda i: (0, 0))
    dxa, dr, dg, db = pl.pallas_call(
        body, grid=(T // tm,), name="ln_bwd", in_specs=[row, row, pl.BlockSpec((1, D), lambda i: (0, 0))],
        out_specs=[row, row, acc, acc],
        out_shape=[_sds((T, D), F32), _sds((T, D), BF16), _sds((8, D), F32), _sds((8, D), F32)],
        compiler_params=_params("arbitrary"),
    )(u, dy, g.reshape(1, D))
    return dxa, dr, dg, db


def loss_head(y, target):
    T, D = y.shape
    tm = _row_tile(T)

    def body(y_ref, t_ref, dy_ref, l_ref):
        @pl.when(pl.program_id(0) == 0)
        def _():
            l_ref[...] = jnp.zeros_like(l_ref)

        e = y_ref[...] - t_ref[...]
        dy_ref[...] = e * (1.0 / D)
        l_ref[...] += jnp.sum((e * e).reshape(tm // 8, 8, D), axis=0) * (0.5 / D)

    row = pl.BlockSpec((tm, D), lambda i: (i, 0))
    return pl.pallas_call(
        body, grid=(T // tm,), name="loss_head", in_specs=[row, row],
        out_specs=[row, pl.BlockSpec((8, D), lambda i: (0, 0))],
        out_shape=[_sds((T, D), F32), _sds((8, D), F32)],
        compiler_params=_params("arbitrary"),
    )(y, target)


def adamw(w, g, m, v):
    R, C = w.shape
    tr = R
    for cand in (512, 256, 128, 64, 32, 16, 8):
        if R % cand == 0:
            tr = cand
            break
    c1 = 1.0 - ADAM_B1 ** ADAM_STEP
    c2 = 1.0 - ADAM_B2 ** ADAM_STEP

    def body(w_ref, g_ref, m_ref, v_ref, d_ref, mo_ref, vo_ref):
        g_ = g_ref[...]
        m_ = ADAM_B1 * m_ref[...] + (1.0 - ADAM_B1) * g_
        v_ = ADAM_B2 * v_ref[...] + (1.0 - ADAM_B2) * (g_ * g_)
        m_hat = m_ / c1
        v_hat = v_ / c2
        d_ref[...] = -ADAM_LR * (m_hat / (jnp.sqrt(v_hat) + ADAM_EPS) + ADAM_WD * w_ref[...])
        mo_ref[...] = m_
        vo_ref[...] = v_

    blk = pl.BlockSpec((tr, C), lambda i: (i, 0))
    return pl.pallas_call(
        body, grid=(R // tr,), name="adamw", in_specs=[blk] * 4, out_specs=[blk] * 3,
        out_shape=[_sds((R, C), F32)] * 3, compiler_params=_params("parallel"),
    )(w, g, m, v)


def _lane_lo(rows):
    return lax.broadcasted_iota(jnp.int32, (rows, LANES), 1) < HEAD_DIM


def _pair_mean(x, lo):
    s0 = jnp.sum(jnp.where(lo, x, 0.0), axis=-1, keepdims=True)
    s1 = jnp.sum(jnp.where(lo, 0.0, x), axis=-1, keepdims=True)
    return jnp.where(lo, s0, s1) * (1.0 / HEAD_DIM)


def _rms_fwd(o, gain, lo):
    r = lax.rsqrt(_pair_mean(o * o, lo) + RMS_EPS)
    return (o * r * gain).astype(BF16)


def _rms_bwd(o, dm, gain, lo):
    r = lax.rsqrt(_pair_mean(o * o, lo) + RMS_EPS)
    oh = o * r
    dg = jnp.sum(dm * oh, axis=0, keepdims=True)
    doh = dm * gain
    do = r * (doh - oh * _pair_mean(doh * oh, lo))
    return do, dg


def _split_heads(x, lo):
    zero = jnp.zeros_like(x)
    return [jnp.where(lo, x, zero), jnp.where(lo, zero, x)]


def _merge_pairs(per_head, lo):
    return [jnp.where(lo, per_head[2 * p], per_head[2 * p + 1]) for p in range(len(per_head) // 2)]


def _pair_cols(h):
    return slice((h // 2) * LANES, (h // 2 + 1) * LANES)


def _split_dot(x, tri):
    n = x.shape[0]
    hi = x.astype(BF16)
    lo = (x - hi.astype(F32)).astype(BF16)
    both = _dot(jnp.concatenate([hi, lo], axis=0), tri, NN)
    return both[:n] + both[n:]


def _log_keep(z):
    return -(jnp.maximum(z, 0.0) + jnp.log(1.0 + jnp.exp(-jnp.abs(z))))


def _attn_dims(h3p, gain, heads):
    B, SP, C = h3p.shape
    HD = gain.shape[1]
    hs = min(heads, HD // HEAD_DIM)
    W = hs * HEAD_DIM
    assert C == 6 * HD and W % LANES == 0 and HD % W == 0
    return B, SP - CA_PAD, HD, hs, W, HD // W


def sb_fwd(h3p, gain, carry=None):
    B, S, HD, hs, W, ngrp = _attn_dims(h3p, gain, FWD_HEADS)
    tb = min(SB_BLOCK, S)
    nq = S // tb
    off = CA_PAD // tb
    scale = 1.0 / math.sqrt(HEAD_DIM)

    def body(q_ref, k_ref, v_ref, g_ref, o_ref, m_ref, c_ref):
        qi = pl.program_id(2)
        lo = _lane_lo(tb)
        row = lax.broadcasted_iota(jnp.int32, (tb, tb), 0)
        col = lax.broadcasted_iota(jnp.int32, (tb, tb), 1)
        rev_incl = (row >= col).astype(BF16)
        lane = lax.broadcasted_iota(jnp.int32, (tb, nq), 1)
        qm = []
        for p in range(hs // 2):
            qm += _split_heads((q_ref[0, :, p * LANES:(p + 1) * LANES].astype(F32) * scale).astype(BF16), lo)

        def blocks(kb, carries_, mask):
            ks = pl.ds(pl.multiple_of(kb * tb + CA_PAD, tb), tb)
            hh = range(hs)
            zs = [_dot(qm[h], k_ref[0, ks, _pair_cols(h)], NT) for h in hh]
            lks = [_log_keep(z) for z in zs]
            if mask is not None:
                lks = [jnp.where(mask, lk, 0.0) for lk in lks]
            cums = [_split_dot(lk, rev_incl) for lk in lks]
            ws = []
            for h in hh:
                logw = zs[h] + cums[h] + carries_[h]
                if mask is not None:
                    logw = jnp.where(mask, logw, NEG)
                ws.append(jnp.exp(logw).astype(BF16))
            pvs = [_dot(ws[h], v_ref[0, ks, _pair_cols(h)], NN) for h in hh]
            return [(pvs[h], cums[h][:, 0:1]) for h in hh]

        diag = blocks(qi, [jnp.zeros((tb, 1), F32)] * hs, col < row)
        accs = _merge_pairs([d[0] for d in diag], lo)
        carries = [d[1] for d in diag]
        cars = [jnp.where(lane == qi, 0.0, NEG)] * hs

        def cond(st):
            kb, carries_, _, _ = st
            top = carries_[0]
            for c in carries_[1:]:
                top = jnp.maximum(top, c)
            return jnp.logical_and(kb >= 0, jnp.max(top) > SB_SKIP_LOG)

        def step(st):
            kb, carries_, accs_, cars_ = st
            out = blocks(kb, carries_, None)
            pv = _merge_pairs([o[0] for o in out], lo)
            return (kb - 1, [c + o[1] for c, o in zip(carries_, out)], [a + p for a, p in zip(accs_, pv)],
                    [jnp.where(lane == kb, c, cs) for c, cs in zip(carries_, cars_)])

        _, _, accs, cars = lax.while_loop(cond, step, (qi - 1, carries, accs, cars))
        for p, acc in enumerate(accs):
            cols = slice(p * LANES, (p + 1) * LANES)
            o_ref[0, :, cols] = acc
            m_ref[0, :, cols] = _rms_fwd(acc, g_ref[:, cols], lo)
        for h in range(hs):
            c_ref[0, h] = cars[h]

    qspec = pl.BlockSpec((1, tb, W), lambda g, b, i: (b, i + off, g))
    ospec = pl.BlockSpec((1, tb, W), lambda g, b, i: (b, i, g))
    return _call(
        body, carry, grid=(ngrp, B, nq), name="sb_fwd", sem=("parallel", "parallel", "arbitrary"),
        args=(h3p, h3p, h3p, gain),
        in_specs=[qspec, pl.BlockSpec((1, CA_PAD + S, W), lambda g, b, i: (b, 0, ngrp + g)),
                  pl.BlockSpec((1, CA_PAD + S, W), lambda g, b, i: (b, 0, 2 * ngrp + g)),
                  pl.BlockSpec((1, W), lambda g, b, i: (0, g))],
        out_specs=[ospec, ospec, pl.BlockSpec((1, hs, tb, nq), lambda g, b, i: (b, g, i, 0))],
        out_shape=[_sds((B, S, HD), F32), _sds((B, S, HD), BF16), _sds((B, HD // HEAD_DIM, S, nq), F32)])


def sb_bwd(h3p, o, dmix, gain, cars, carry=None):
    B, S, HD, hs, W, ngrp = _attn_dims(h3p, gain, BWD_HEADS)
    tb = min(SB_BLOCK, S)
    nq = S // tb
    off = CA_PAD // tb
    scale = 1.0 / math.sqrt(HEAD_DIM)

    def body(q_ref, k_ref, v_ref, o_ref, dm_ref, g_ref, c_ref, dq_ref, dk_ref, dv_ref, dg_ref, dk_acc, dv_acc):
        qi = pl.program_id(2)

        @pl.when(qi == 0)
        def _():
            dk_acc[...] = jnp.zeros_like(dk_acc)
            dv_acc[...] = jnp.zeros_like(dv_acc)

        lo = _lane_lo(tb)
        row = lax.broadcasted_iota(jnp.int32, (tb, tb), 0)
        col = lax.broadcasted_iota(jnp.int32, (tb, tb), 1)
        rev_incl = (row >= col).astype(BF16)
        fwd_incl = (row <= col).astype(BF16)
        lane = lax.broadcasted_iota(jnp.int32, (tb, nq), 1)
        below = lax.broadcasted_iota(jnp.int32, (1, nq), 1) < qi
        qm, dom, cars_, seen = [], [], [], None
        for p in range(hs // 2):
            cols = slice(p * LANES, (p + 1) * LANES)
            qm += _split_heads((q_ref[0, :, cols].astype(F32) * scale).astype(BF16), lo)
            do, dg = _rms_bwd(o_ref[0, :, cols], dm_ref[0, :, cols], g_ref[:, cols], lo)
            dg_ref[0, 0, :, cols] = dg
            dom += _split_heads(do.astype(BF16), lo)
        for h in range(hs):
            cars_.append(c_ref[0, h])
            visited = jnp.logical_and(jnp.max(cars_[h], axis=0, keepdims=True) > SB_SKIP_LOG, below)
            n = jnp.sum(visited.astype(jnp.int32), axis=1, keepdims=True)
            seen = n if seen is None else jnp.maximum(seen, n)
        first = qi - jnp.max(seen)

        def blocks(kb, gsums, dqs, mask):
            ks = pl.ds(pl.multiple_of(kb * tb + CA_PAD, tb), tb)
            ko = pl.ds(pl.multiple_of(kb * tb, tb), tb)
            hh = range(hs)
            kk = [k_ref[0, ks, p * LANES:(p + 1) * LANES] for p in range(hs // 2)]
            vv = [v_ref[0, ks, p * LANES:(p + 1) * LANES] for p in range(hs // 2)]
            zs = [_dot(qm[h], kk[h // 2], NT) for h in hh]
            dws = [_dot(dom[h], vv[h // 2], NT) for h in hh]
            raw = [_log_keep(z) for z in zs]
            lks = raw if mask is None else [jnp.where(mask, lk, 0.0) for lk in raw]
            cums = [_split_dot(lk, rev_incl) for lk in lks]
            ws = []
            for h in hh:
                carry = jnp.sum(jnp.where(lane == kb, cars_[h], 0.0), axis=1, keepdims=True)
                logw = zs[h] + cums[h] + carry
                if mask is not None:
                    logw = jnp.where(mask, logw, NEG)
                ws.append(jnp.exp(logw))
            gws = [ws[h] * dws[h] for h in hh]
            gcums = [_split_dot(gws[h], fwd_incl) + gsums[h] for h in hh]
            dzb = []
            for h in hh:
                dz = gws[h] - jnp.exp(zs[h] + raw[h]) * gcums[h]
                if mask is not None:
                    dz = jnp.where(mask, dz, 0.0)
                dzb.append(dz.astype(BF16))
            wb = [w.astype(BF16) for w in ws]
            new_dq = [dqs[h] + _dot(dzb[h], kk[h // 2], NN) for h in hh]
            for p in range(hs // 2):
                cols = slice(p * LANES, (p + 1) * LANES)
                dk_acc[ko, cols] += _dot(dzb[2 * p], qm[2 * p], TN) + _dot(dzb[2 * p + 1], qm[2 * p + 1], TN)
                dv_acc[ko, cols] += _dot(wb[2 * p], dom[2 * p], TN) + _dot(wb[2 * p + 1], dom[2 * p + 1], TN)
            return [g[:, tb - 1:tb] for g in gcums], new_dq

        def step(kb, st):
            return blocks(kb, st[0], st[1], None)

        init = ([jnp.zeros((tb, 1), F32)] * hs, [jnp.zeros((tb, LANES), F32)] * hs)
        gsum, dq = lax.fori_loop(first, qi, step, init)
        _, dq = blocks(qi, gsum, dq, col < row)
        for p, d in enumerate(_merge_pairs(dq, lo)):
            dq_ref[0, :, p * LANES:(p + 1) * LANES] = (d * scale).astype(BF16)

        @pl.when(qi == nq - 1)
        def _():
            dk_ref[0] = dk_acc[...].astype(BF16)
            dv_ref[0] = dv_acc[...].astype(BF16)

    once = pl.Buffered(1)
    qspec = pl.BlockSpec((1, tb, W), lambda g, b, i: (b, i + off, g))
    ospec = pl.BlockSpec((1, tb, W), lambda g, b, i: (b, i, g))
    kvout = pl.BlockSpec((1, S, W), lambda g, b, i: (b, 0, g), pipeline_mode=once)
    return _call(
        body, carry, grid=(ngrp, B, nq), name="sb_bwd", sem=("parallel", "parallel", "arbitrary"),
        args=(h3p, h3p, h3p, o, dmix, gain, cars),
        in_specs=[qspec, pl.BlockSpec((1, CA_PAD + S, W), lambda g, b, i: (b, 0, ngrp + g), pipeline_mode=once),
                  pl.BlockSpec((1, CA_PAD + S, W), lambda g, b, i: (b, 0, 2 * ngrp + g), pipeline_mode=once),
                  ospec, ospec, pl.BlockSpec((1, W), lambda g, b, i: (0, g)),
                  pl.BlockSpec((1, hs, tb, nq), lambda g, b, i: (b, g, i, 0))],
        out_specs=[ospec, kvout, kvout, pl.BlockSpec((1, 1, 1, W), lambda g, b, i: (b, i, 0, g))],
        out_shape=[_sds((B, S, HD), BF16), _sds((B, S, HD), BF16), _sds((B, S, HD), BF16), _sds((B, nq, 1, HD), F32)],
        scratch_shapes=[pltpu.VMEM((S, W), F32), pltpu.VMEM((S, W), F32)])


def _ca_rel_index():
    width = CA_WIN + CA_TQ
    c = np.arange(width)
    dj = np.where(c < CA_WIN, c, c - width)
    return np.clip(CA_PAD - dj, -MAX_REL, MAX_REL) + MAX_REL, width


def _ca_onehot():
    idx, _ = _ca_rel_index()
    return (idx[:, None] == np.arange(2 * MAX_REL + 1)[None, :]).astype(np.float32)


def ca_bias(rel_bias):
    _, width = _ca_rel_index()
    lead = rel_bias.shape[:-1]
    by_offset = jnp.dot(rel_bias, jnp.asarray(_ca_onehot().T), precision=lax.Precision.HIGHEST)
    tile = jnp.broadcast_to(by_offset[..., None, :], lead + (CA_TQ, width)).reshape(lead + (CA_TQ * width,))
    tile = tile[..., :CA_TQ * (width - 1)].reshape(lead + (CA_TQ, width - 1))[..., :CA_WIN]
    t = np.arange(CA_TQ)[:, None] // CHUNK * CHUNK
    j = np.arange(CA_WIN)[None, :]
    return jnp.where((j >= t) & (j < t + CA_PAD + CHUNK), tile, NEG)


def rel_bias_grad(db):
    H = db.shape[0]
    _, width = _ca_rel_index()
    x = jnp.pad(db, ((0, 0), (0, 0), (0, width - 1 - CA_WIN))).reshape(H, CA_TQ * (width - 1))
    x = jnp.pad(x, ((0, 0), (0, CA_TQ))).reshape(H, CA_TQ, width).sum(axis=1)
    return jnp.dot(x, jnp.asarray(_ca_onehot()), precision=lax.Precision.HIGHEST)


def _ca_scores(qm_h, kk, bias_h, valid):
    return jnp.where(valid, _dot(qm_h, kk, NT) + bias_h, NEG)


def _ca_softmax(s):
    e = jnp.exp(s - jnp.max(s, axis=-1, keepdims=True))
    return e * (1.0 / jnp.sum(e, axis=-1, keepdims=True))


def ca_fwd(h3p, bias, gain):
    B, S, HD, hs, W, ngrp = _attn_dims(h3p, gain, FWD_HEADS)
    scale = 1.0 / math.sqrt(HEAD_DIM)
    off = CA_PAD // CA_TQ

    def body(q_ref, k_ref, v_ref, b_ref, g_ref, o_ref, m_ref):
        q0 = pl.program_id(2) * CA_TQ
        ks = pl.ds(pl.multiple_of(q0, CA_TQ), CA_WIN)
        lo = _lane_lo(CA_TQ)
        valid = lax.broadcasted_iota(jnp.int32, (CA_TQ, CA_WIN), 1) + q0 >= CA_PAD
        qm, kk, vv = [], [], []
        for p in range(hs // 2):
            cols = slice(p * LANES, (p + 1) * LANES)
            qm += _split_heads((q_ref[0, :, cols].astype(F32) * scale).astype(BF16), lo)
            kk.append(k_ref[0, ks, cols])
            vv.append(v_ref[0, ks, cols])
        ss = [_ca_scores(qm[h], kk[h // 2], b_ref[h], valid) for h in range(hs)]
        ps = [_ca_softmax(s).astype(BF16) for s in ss]
        pv = [_dot(ps[h], vv[h // 2], NN) for h in range(hs)]
        for p, o in enumerate(_merge_pairs(pv, lo)):
            cols = slice(p * LANES, (p + 1) * LANES)
            o_ref[0, :, cols] = o
            m_ref[0, :, cols] = _rms_fwd(o, g_ref[:, cols], lo)

    ospec = pl.BlockSpec((1, CA_TQ, W), lambda g, b, i: (b, i, g))
    return pl.pallas_call(
        body, grid=(ngrp, B, S // CA_TQ), name="ca_fwd",
        in_specs=[pl.BlockSpec((1, CA_TQ, W), lambda g, b, i: (b, i + off, 3 * ngrp + g)),
                  pl.BlockSpec((1, CA_PAD + S, W), lambda g, b, i: (b, 0, 4 * ngrp + g)),
                  pl.BlockSpec((1, CA_PAD + S, W), lambda g, b, i: (b, 0, 5 * ngrp + g)),
                  pl.BlockSpec((hs, CA_TQ, CA_WIN), lambda g, b, i: (g, 0, 0)),
                  pl.BlockSpec((1, W), lambda g, b, i: (0, g))],
        out_specs=[ospec, ospec], out_shape=[_sds((B, S, HD), F32), _sds((B, S, HD), BF16)],
        compiler_params=_params("parallel", "parallel", "arbitrary"),
    )(h3p, h3p, h3p, bias, gain)


def ca_bwd(h3p, bias, o, dmix, gain, carry=None):
    B, S, HD, hs, W, ngrp = _attn_dims(h3p, gain, BWD_HEADS)
    scale = 1.0 / math.sqrt(HEAD_DIM)
    nq = S // CA_TQ
    off = CA_PAD // CA_TQ

    def body(q_ref, k_ref, v_ref, b_ref, o_ref, dm_ref, g_ref, dq_ref, dk_ref, dv_ref, dg_ref, db_ref, dk_acc, dv_acc):
        bi = pl.program_id(1)
        qi = pl.program_id(2)

        @pl.when(qi == 0)
        def _():
            dk_acc[...] = jnp.zeros_like(dk_acc)
            dv_acc[...] = jnp.zeros_like(dv_acc)

        @pl.when(jnp.logical_and(qi == 0, bi == 0))
        def _():
            db_ref[...] = jnp.zeros_like(db_ref)

        q0 = qi * CA_TQ
        ks = pl.ds(pl.multiple_of(q0, CA_TQ), CA_WIN)
        lo = _lane_lo(CA_TQ)
        valid = lax.broadcasted_iota(jnp.int32, (CA_TQ, CA_WIN), 1) + q0 >= CA_PAD
        qm, dom, kk, vv = [], [], [], []
        for p in range(hs // 2):
            cols = slice(p * LANES, (p + 1) * LANES)
            qm += _split_heads((q_ref[0, :, cols].astype(F32) * scale).astype(BF16), lo)
            do, dg = _rms_bwd(o_ref[0, :, cols], dm_ref[0, :, cols], g_ref[:, cols], lo)
            dg_ref[0, 0, :, cols] = dg
            dom += _split_heads(do.astype(BF16), lo)
            kk.append(k_ref[0, ks, cols])
            vv.append(v_ref[0, ks, cols])
        hh = range(hs)
        ss = [_ca_scores(qm[h], kk[h // 2], b_ref[h], valid) for h in hh]
        dps = [_dot(dom[h], vv[h // 2], NT) for h in hh]
        ps = [_ca_softmax(s) for s in ss]
        dss = [ps[h] * (dps[h] - jnp.sum(ps[h] * dps[h], axis=-1, keepdims=True)) for h in hh]
        for h in hh:
            db_ref[h] += dss[h]
        dsb = [d.astype(BF16) for d in dss]
        pb = [p_.astype(BF16) for p_ in ps]
        dq = [_dot(dsb[h], kk[h // 2], NN) for h in hh]
        for p in range(hs // 2):
            cols = slice(p * LANES, (p + 1) * LANES)
            dk_acc[ks, cols] += _dot(dsb[2 * p], qm[2 * p], TN) + _dot(dsb[2 * p + 1], qm[2 * p + 1], TN)
            dv_acc[ks, cols] += _dot(pb[2 * p], dom[2 * p], TN) + _dot(pb[2 * p + 1], dom[2 * p + 1], TN)
        for p, d in enumerate(_merge_pairs(dq, lo)):
            dq_ref[0, :, p * LANES:(p + 1) * LANES] = (d * scale).astype(BF16)

        @pl.when(qi == nq - 1)
        def _():
            dk_ref[0] = dk_acc[CA_PAD:, :].astype(BF16)
            dv_ref[0] = dv_acc[CA_PAD:, :].astype(BF16)

    once = pl.Buffered(1)
    ospec = pl.BlockSpec((1, CA_TQ, W), lambda g, b, i: (b, i, g))
    kvout = pl.BlockSpec((1, S, W), lambda g, b, i: (b, 0, g), pipeline_mode=once)
    bspec = pl.BlockSpec((hs, CA_TQ, CA_WIN), lambda g, b, i: (g, 0, 0))
    return _call(
        body, carry, grid=(ngrp, B, nq), name="ca_bwd", sem=("parallel", "arbitrary", "arbitrary"),
        args=(h3p, h3p, h3p, bias, o, dmix, gain),
        in_specs=[pl.BlockSpec((1, CA_TQ, W), lambda g, b, i: (b, i + off, 3 * ngrp + g)),
                  pl.BlockSpec((1, CA_PAD + S, W), lambda g, b, i: (b, 0, 4 * ngrp + g), pipeline_mode=once),
                  pl.BlockSpec((1, CA_PAD + S, W), lambda g, b, i: (b, 0, 5 * ngrp + g), pipeline_mode=once),
                  bspec, ospec, pl.BlockSpec((1, CA_TQ, W), lambda g, b, i: (b, i, ngrp + g)),
                  pl.BlockSpec((1, W), lambda g, b, i: (0, g))],
        out_specs=[ospec, kvout, kvout, pl.BlockSpec((1, 1, 1, W), lambda g, b, i: (b, i, 0, g)), bspec],
        out_shape=[_sds((B, S, HD), BF16), _sds((B, S, HD), BF16), _sds((B, S, HD), BF16),
                   _sds((B, nq, 1, HD), F32), _sds((HD // HEAD_DIM, CA_TQ, CA_WIN), F32)],
        scratch_shapes=[pltpu.VMEM((CA_PAD + S, W), F32), pltpu.VMEM((CA_PAD + S, W), F32)])


_ANY = pl.BlockSpec(memory_space=pl.ANY)
_MESH = pl.DeviceIdType.MESH


def _mesh_pos():
    return lax.axis_index("x"), lax.axis_index("y"), lax.axis_index("c")


def _dev_index(p):
    return 4 * p[0] + 2 * p[1] + p[2]


def _flip(pos, k):
    return tuple(1 - v if (k >> (2 - a)) & 1 else v for a, v in enumerate(pos))


class _Plan:
    def __init__(self, operands, n_matrices):
        self.operands = list(operands)
        self.n_ops = len(self.operands)
        self.nm = n_matrices
        self.scratch = [pltpu.SemaphoreType.DMA((7, self.nm)), pltpu.SemaphoreType.DMA((7, self.nm)),
                        pltpu.SemaphoreType.DMA((self.nm,))]


class _Gather(_Plan):
    def __init__(self, mats):
        super().__init__(mats, len(mats))
        self.rows = [m.shape[0] for m in mats]
        self.out_shape = [_sds((N_DEV * m.shape[0], m.shape[1]), m.dtype) for m in mats]

    def begin(self, ins, outs, sems):
        nm, rows = self.nm, self.rows
        send_sems, recv_sems, local_sems = sems
        x, y, c = _mesh_pos()
        me, sibling = (x, y, c), (x, y, 1 - c)
        chips = [(1 - x, y), (x, 1 - y), (1 - x, 1 - y)]

        def block(m, p):
            start = pl.multiple_of(_dev_index(p) * rows[m], 16)
            return outs[m].at[pl.ds(start, rows[m]), :]

        def copy(k, m, blk, to, src=None):
            return pltpu.make_async_remote_copy(
                src_ref=block(m, blk) if src is None else src, dst_ref=block(m, blk),
                send_sem=send_sems.at[k, m], recv_sem=recv_sems.at[k, m], device_id=to, device_id_type=_MESH)

        def mine():
            return [pltpu.make_async_copy(ins[m], block(m, me), local_sems.at[m]) for m in range(nm)]

        def first():
            own = [copy(0, m, me, sibling, src=ins[m]) for m in range(nm)]
            return own + [copy(1 + j, m, me, (*chip, c), src=ins[m]) for j, chip in enumerate(chips) for m in range(nm)]

        def start():
            for cp in mine() + first():
                cp.start()

        def passed():
            return [copy(4 + j, m, (*chip, c), sibling) for j, chip in enumerate(chips) for m in range(nm)]

        def pass_on():
            for j, chip in enumerate(chips):
                for m in range(nm):
                    copy(1 + j, m, (*chip, c), me).wait_recv()
                    copy(4 + j, m, (*chip, c), sibling).start()

        def finish():
            for m in range(nm):
                copy(0, m, sibling, me).wait_recv()
            for j, chip in enumerate(chips):
                for m in range(nm):
                    copy(4 + j, m, (*chip, 1 - c), me).wait_recv()
            for cp in first() + passed():
                cp.wait_send()
            for cp in mine():
                cp.wait()

        return start, pass_on, finish


class _Exchange(_Plan):
    def __init__(self, grads):
        self.where = [(i, j) for i, g in enumerate(grads) for j in (range(g.shape[0]) if g.ndim == 3 else [None])]
        super().__init__(grads, len(self.where))
        self.rows = [grads[i].shape[-2] // N_DEV for i, _ in self.where]
        self.out_shape = [_sds(g.shape[:-2] + (N_DEV, g.shape[-2] // N_DEV, g.shape[-1]), g.dtype) for g in grads]

    def begin(self, ins, outs, sems):
        nm, rows = self.nm, self.rows
        send_sems, recv_sems, local_sems = sems
        me = _mesh_pos()
        my = _dev_index(me)

        def piece(m, idx):
            i, j = self.where[m]
            rows_ = pl.ds(pl.multiple_of(idx * rows[m], 16), rows[m])
            return ins[i].at[rows_, :] if j is None else ins[i].at[j, rows_, :]

        def landing(m, slot):
            i, j = self.where[m]
            return outs[i].at[slot] if j is None else outs[i].at[j, slot]

        def copy(k, m, src_idx, slot, to):
            return pltpu.make_async_remote_copy(
                src_ref=piece(m, src_idx), dst_ref=landing(m, slot),
                send_sem=send_sems.at[k - 1, m], recv_sem=recv_sems.at[k - 1, m], device_id=to, device_id_type=_MESH)

        def mine():
            return [pltpu.make_async_copy(piece(m, my), landing(m, my), local_sems.at[m]) for m in range(nm)]

        def sends():
            return [copy(k, m, _dev_index(_flip(me, k)), my, _flip(me, k)) for k in range(1, N_DEV) for m in range(nm)]

        def start():
            for cp in mine() + sends():
                cp.start()

        def finish():
            for k in range(1, N_DEV):
                peer = _flip(me, k)
                for m in range(nm):
                    copy(k, m, my, _dev_index(peer), peer).wait_recv()
            for cp in sends():
                cp.wait_send()
            for cp in mine():
                cp.wait()

        return start, None, finish


def _run_plan(plan, name):
    nm = plan.n_ops

    def body(*refs):
        start, middle, finish = plan.begin(refs[:nm], refs[nm:2 * nm], refs[2 * nm:])
        start()
        if middle is not None:
            middle()
        finish()

    return pl.pallas_call(body, name=name, in_specs=[_ANY] * nm, out_specs=[_ANY] * nm, out_shape=plan.out_shape,
                          scratch_shapes=plan.scratch)(*plan.operands)


def _call(body, carry, *, grid, in_specs, out_specs, out_shape, args, name, sem, scratch_shapes=()):
    if carry is None:
        outs = pl.pallas_call(body, grid=grid, name=name, in_specs=in_specs, out_specs=out_specs, out_shape=out_shape,
                              scratch_shapes=list(scratch_shapes), compiler_params=_params(*sem))(*args)
        return outs, None
    n_in, n_out, n_sc, nm = len(in_specs), len(out_specs), len(scratch_shapes), carry.n_ops

    def carrier(*refs):
        refs = list(refs)
        own_in, refs = refs[:n_in], refs[n_in:]
        c_in, refs = refs[:nm], refs[nm:]
        own_out, refs = refs[:n_out], refs[n_out:]
        c_out, refs = refs[:nm], refs[nm:]
        own_sc, c_sc = refs[:n_sc], refs[n_sc:]
        start, middle, finish = carry.begin(c_in, c_out, c_sc)
        step, steps = 0, 1
        for axis, n in enumerate(grid):
            step = step * n + pl.program_id(axis)
            steps *= n
        pl.when(step == 0)(start)
        body(*own_in, *own_out, *own_sc)
        if middle is not None:
            pl.when(step == max(steps - 1 - steps // 4, 0))(middle)
        pl.when(step == steps - 1)(finish)

    outs = pl.pallas_call(
        carrier, grid=grid, name=name + "_carry", in_specs=list(in_specs) + [_ANY] * nm,
        out_specs=list(out_specs) + [_ANY] * nm, out_shape=list(out_shape) + carry.out_shape,
        scratch_shapes=list(scratch_shapes) + carry.scratch,
        compiler_params=_params(*["arbitrary"] * len(grid)),
    )(*args, *carry.operands)
    return outs[:n_out], outs[n_out:]


def sum_slots(r, j=None):
    n, R, D = r.shape[-3:]
    tc = _row_tile(D, 256)

    def body(r_ref, o_ref):
        slot = (lambda s: r_ref[s]) if j is None else (lambda s: r_ref[0, s])
        acc = slot(0).astype(F32)
        for s in range(1, n):
            acc = acc + slot(s).astype(F32)
        o_ref[...] = acc

    spec = (pl.BlockSpec((n, R, tc), lambda i: (0, 0, i)) if j is None
            else pl.BlockSpec((1, n, R, tc), lambda i: (j, 0, 0, i)))
    return pl.pallas_call(
        body, grid=(D // tc,), name="sum_slots", in_specs=[spec],
        out_specs=pl.BlockSpec((R, tc), lambda i: (0, i)), out_shape=_sds((R, D), F32),
        compiler_params=_params("parallel"),
    )(r)


def allreduce_small(v):
    R, C = v.shape

    def body(v_ref, o_ref, buf, send_sems, recv_sems):
        me = _mesh_pos()
        my = _dev_index(me)
        buf[my] = v_ref[...]

        def copy(k, slot, to):
            return pltpu.make_async_remote_copy(
                src_ref=v_ref, dst_ref=buf.at[slot], send_sem=send_sems.at[k - 1], recv_sem=recv_sems.at[k - 1],
                device_id=to, device_id_type=_MESH)

        sends = []
        for k in range(1, N_DEV):
            cp = copy(k, my, _flip(me, k))
            cp.start()
            sends.append(cp)
        for k in range(1, N_DEV):
            peer = _flip(me, k)
            copy(k, _dev_index(peer), peer).wait_recv()
        acc = buf[0]
        for s in range(1, N_DEV):
            acc = acc + buf[s]
        o_ref[...] = acc
        for cp in sends:
            cp.wait_send()

    vm = pl.BlockSpec(memory_space=pltpu.VMEM)
    return pl.pallas_call(
        body, name="allreduce_small", in_specs=[vm], out_specs=vm, out_shape=_sds((R, C), F32),
        scratch_shapes=[pltpu.VMEM((N_DEV, R, C), F32), pltpu.SemaphoreType.DMA((7,)), pltpu.SemaphoreType.DMA((7,))],
    )(v)


WEIGHTS = ["ffn1_w_gate", "ffn1_w_up", "ffn1_w_down", "ln1_g", "ln1_b", "w_in", "rel_bias", "sb_out_g", "ca_out_g",
           "w_out", "ln2_g", "ln2_b", "ffn2_w_gate", "ffn2_w_up", "ffn2_w_down", "ln3_g", "ln3_b"]
BIG = ["ffn1_w_gate", "ffn1_w_up", "ffn2_w_gate", "ffn2_w_up", "w_in", "ffn1_w_down", "ffn2_w_down", "w_out"]
TRANSPOSED = BIG[:5]
SMALL = [n for n in WEIGHTS if n not in BIG]


def _pack(vals):
    flat = jnp.concatenate([v.reshape(-1).astype(F32) for v in vals])
    pad = -flat.shape[0] % (8 * 128)
    return jnp.pad(flat, (0, pad)).reshape(-1, 128)


def _unpack(packed, like):
    flat = packed.reshape(-1)
    out, off = [], 0
    for v in like:
        out.append(flat[off:off + v.size].reshape(v.shape))
        off += v.size
    return out


def _row_blocks(w, n):
    r = w.shape[0] // n
    return [w[i * r:(i + 1) * r] for i in range(n)]


def _layer_fwd(x, xb, W, P, bias, l, B, S, next_gather, rest=None):
    T = B * S
    sv = {"xb": xb, "bias": bias}
    (sv["sg1"], sv["t1"], sv["a1"]), got = ffn_up(xb, W["ffn1_w_gate"], W["ffn1_w_up"], carry=rest and rest[1])
    if rest:
        W.update(zip(rest[0], got))
    sv["u1"], x1, sv["x1b"] = mm_nn_ln([(sv["a1"], W["ffn1_w_down"])], x, P["ln1_g"][l], P["ln1_b"][l], FFN_RESIDUAL,
                                       "ffn_down_ln")

    sv["h"] = proj_in(sv["x1b"], W["w_in"], B, S)
    sv["gA"] = P["sb_out_g"][l].reshape(1, -1)
    sv["gB"] = P["ca_out_g"][l].reshape(1, -1)
    (sv["oa"], ma, sv["cars"]), gathered = sb_fwd(sv["h"], sv["gA"], carry=next_gather)
    sv["ob"], mb = ca_fwd(sv["h"], bias, sv["gB"])
    sv["ma"], sv["mb"] = ma.reshape(T, -1), mb.reshape(T, -1)
    sv["u2"], x2, sv["x2b"] = mm_nn_ln(list(zip([sv["ma"], sv["mb"]], _row_blocks(W["w_out"], 2))), x1,
                                       P["ln2_g"][l], P["ln2_b"][l], 1.0, "proj_out_ln")

    (sv["sg2"], sv["t2"], sv["a2"]), _ = ffn_up(sv["x2b"], W["ffn2_w_gate"], W["ffn2_w_up"])
    sv["u3"], x3, x3b = mm_nn_ln([(sv["a2"], W["ffn2_w_down"])], x2, P["ln3_g"][l], P["ln3_b"][l], FFN_RESIDUAL,
                                 "ffn_down_ln")
    return x3, x3b, sv, gathered


class _Riders:
    def __init__(self):
        self.received = []

    def plan(self, entries):
        return _Exchange([e[2] for e in entries]) if entries else None

    def landed(self, entries, slots):
        for (names, layer, _), r in zip(entries, slots or []):
            for j, name in enumerate(names):
                self.received.append((name, layer, r, j if r.ndim == 4 else None))


def _ffn_bwd(dr, dxa, xb, sg, t, a, wg, wu, wd, riders, ride, below):
    dhg, dhu = ffn_bwd_mid(dr, wd, sg, t)
    pairs = [(dhg, wg), (dhu, wu)]
    if below is None:
        down = mm_nn(pairs, add=dxa, name="ffn_dx")
    else:
        down = mm_nn_ln_bwd(pairs, dxa, *below, name="ffn_dx_ln")
    grads, slots = ffn_dw(dhg, dhu, a, xb, dr, carry=riders.plan(ride))
    riders.landed(ride, slots)
    return down, grads


def _layer_bwd(dxa, dr, sv, W, P, l, B, S, riders, pending, below):
    T = B * S
    G = {}
    (dxa, dr, dg, db), g_ffn2 = _ffn_bwd(
        dr, dxa, sv["x2b"], sv["sg2"], sv["t2"], sv["a2"], W["ffn2_w_gate"], W["ffn2_w_up"], W["ffn2_w_down"],
        riders, pending, (sv["u2"], P["ln2_g"][l], 1.0))
    G["ln2_g"], G["ln2_b"] = dg.sum(0), db.sum(0)
    dmix = mm_nt(dr, W["w_out"], name="proj_out_dx").reshape(B, S, -1)
    g_out = mm_tn([sv["ma"], sv["mb"]], dr, name="proj_out_dw")[0]
    ride = [(("ffn2_w_gate", "ffn2_w_up", "ffn2_w_down"), l, g_ffn2)]
    (dqa, dka, dva, dga), slots = sb_bwd(sv["h"], sv["oa"], dmix, sv["gA"], sv["cars"], carry=riders.plan(ride))
    riders.landed(ride, slots)
    (dqb, dkb, dvb, dgb, dbias), _ = ca_bwd(sv["h"], sv["bias"], sv["ob"], dmix, sv["gB"])
    G["sb_out_g"] = dga.sum((0, 1, 2))
    G["ca_out_g"] = dgb.sum((0, 1, 2))
    G["rel_bias"] = rel_bias_grad(dbias)
    dh = [t.reshape(T, -1) for t in (dqa, dka, dva, dqb, dkb, dvb)]
    dxa, dr, dg, db = mm_nn_ln_bwd(list(zip(dh, _row_blocks(W["w_in"], 6))), dxa, sv["u1"], P["ln1_g"][l], FFN_RESIDUAL,
                                   name="proj_in_dx_ln")
    G["ln1_g"], G["ln1_b"] = dg.sum(0), db.sum(0)
    g_in = mm_tn(dh, sv["x1b"], name="proj_in_dw")[0]

    down, g_ffn1 = _ffn_bwd(dr, dxa, sv["xb"], sv["sg1"], sv["t1"], sv["a1"], W["ffn1_w_gate"], W["ffn1_w_up"],
                            W["ffn1_w_down"], riders, [(("w_in",), l, g_in), (("w_out",), l, g_out)], below)
    return down, G, [(("ffn1_w_gate", "ffn1_w_up", "ffn1_w_down"), l, g_ffn1)]


def kernel(x, ffn1_w_gate, ffn1_w_up, ffn1_w_down, ln1_g, ln1_b, w_in, rel_bias, sb_out_g, ca_out_g, w_out, ln2_g, ln2_b, ffn2_w_gate, ffn2_w_up, ffn2_w_down, ln3_g, ln3_b, loss_target, m_ffn1_w_gate, m_ffn1_w_up, m_ffn1_w_down, m_ln1_g, m_ln1_b, m_w_in, m_rel_bias, m_sb_out_g, m_ca_out_g, m_w_out, m_ln2_g, m_ln2_b, m_ffn2_w_gate, m_ffn2_w_up, m_ffn2_w_down, m_ln3_g, m_ln3_b, v_ffn1_w_gate, v_ffn1_w_up, v_ffn1_w_down, v_ln1_g, v_ln1_b, v_w_in, v_rel_bias, v_sb_out_g, v_ca_out_g, v_w_out, v_ln2_g, v_ln2_b, v_ffn2_w_gate, v_ffn2_w_up, v_ffn2_w_down, v_ln3_g, v_ln3_b):
    given = dict(locals())
    P = {n: given[n] for n in WEIGHTS}
    M = {n: given["m_" + n] for n in WEIGHTS}
    V = {n: given["v_" + n] for n in WEIGHTS}
    B, S, D = x.shape
    L = ln1_g.shape[0]

    local = {n: (jnp.swapaxes(P[n], 1, 2) if n in TRANSPOSED else P[n]).astype(BF16) for n in BIG}
    bias = ca_bias(rel_bias)

    xf = x.reshape(B * S, D)
    xb = xf.astype(BF16)
    W, saved = [], []
    first, later = BIG[:2], BIG[2:]
    gathered = _run_plan(_Gather([local[n][0] for n in first]), "gather_weights")
    rest = (later, _Gather([local[n][0] for n in later]))
    for l in range(L):
        W.append(dict(zip(BIG, gathered)))
        next_gather = _Gather([local[n][l + 1] for n in BIG]) if l + 1 < L else None
        xf, xb, sv, gathered = _layer_fwd(xf, xb, W[l], P, bias[l], l, B, S, next_gather, rest)
        rest = None
        saved.append(sv)
    dy, loss_part = loss_head(xf, loss_target.reshape(B * S, D))

    big_g = {n: [None] * L for n in BIG}
    small_g = {n: [None] * L for n in SMALL}
    riders, pending = _Riders(), []
    down = ln_bwd(saved[L - 1]["u3"], dy, ln3_g[L - 1], FFN_RESIDUAL)
    for l in reversed(range(L)):
        dxa, dr, dg, db = down
        small_g["ln3_g"][l], small_g["ln3_b"][l] = dg.sum(0), db.sum(0)
        below = (saved[l - 1]["u3"], ln3_g[l - 1], FFN_RESIDUAL) if l > 0 else None
        down, G, pending = _layer_bwd(dxa, dr, saved[l], W[l], P, l, B, S, riders, pending, below)
        for n, g in G.items():
            small_g[n][l] = g
    dx = down
    riders.landed(pending, _run_plan(riders.plan(pending), "exchange_grads"))
    for n, l, slots, j in riders.received:
        big_g[n][l] = sum_slots(slots, j)

    grads = {}
    for n in BIG:
        g = jnp.stack(big_g[n])
        grads[n] = jnp.swapaxes(g, 1, 2) if n in TRANSPOSED else g
    small_like = [P[n] for n in SMALL]
    packed = _pack([jnp.stack(small_g[n]) for n in SMALL] + [loss_part.sum()])
    total = allreduce_small(packed)
    *small_vals, loss = _unpack(total, small_like + [jnp.zeros((), F32)])
    grads.update(dict(zip(SMALL, small_vals)))

    delta, new_m, new_v = {}, {}, {}
    for n in BIG:
        shape = P[n].shape
        two_d = lambda a: a.reshape(shape[0] * shape[1], shape[2])
        d, m, v = adamw(two_d(P[n]), two_d(grads[n]), two_d(M[n]), two_d(V[n]))
        delta[n], new_m[n], new_v[n] = d.reshape(shape), m.reshape(shape), v.reshape(shape)
    one = jnp.ones((), F32)
    d, m, v = adamw(_pack(small_like + [one]), total, _pack([M[n] for n in SMALL] + [one]), _pack([V[n] for n in SMALL] + [one]))
    for dst, src in ((delta, d), (new_m, m), (new_v, v)):
        dst.update(dict(zip(SMALL, _unpack(src, small_like))))

    return (loss, dx.reshape(B, S, D), *[grads[n] for n in WEIGHTS], *[delta[n] for n in WEIGHTS],
            *[new_m[n] for n in WEIGHTS], *[new_v[n] for n in WEIGHTS])
```

```python
import math

import jax
import jax.numpy as jnp
import numpy as np
from jax import lax
from jax.experimental import pallas as pl
from jax.experimental.pallas import tpu as pltpu

F32 = jnp.float32
BF16 = jnp.bfloat16

HEAD_DIM = 64
CHUNK = 64
N_PREV_CHUNKS = 8
MAX_REL = 128
DEPTH = 4
FFN_RESIDUAL = 0.5
ALPHA = (2 * DEPTH) ** 0.25
LN_EPS = 1e-5
RMS_EPS = 1e-6
ADAM_LR = 0.001
ADAM_B1 = 0.9
ADAM_B2 = 0.999
ADAM_EPS = 1e-08
ADAM_WD = 0.01
ADAM_STEP = 10

N_DEV = 8
LANES = 128
VMEM_LIMIT_BYTES = 56 * 1024 * 1024
ROW_TILE = 512
SB_BLOCK = 128
CA_PAD = CHUNK * N_PREV_CHUNKS
CA_TQ = 2 * CHUNK
CA_WIN = CA_PAD + CA_TQ
FWD_HEADS = 8
BWD_HEADS = 8
NEG = -1e30
SB_SKIP_LOG = -104.0

NN = ((1,), (0,))
NT = ((1,), (1,))
TN = ((0,), (0,))


def _dot(a, b, dims):
    return lax.dot_general(a, b, (dims, ((), ())), preferred_element_type=F32)


def _params(*sem):
    return pltpu.CompilerParams(dimension_semantics=sem, vmem_limit_bytes=VMEM_LIMIT_BYTES)


def _sds(shape, dtype):
    return jax.ShapeDtypeStruct(shape, dtype)


def _row_tile(n, want=ROW_TILE):
    t = min(want, n)
    while n % t:
        t //= 2
    assert t >= 8, (n, want)
    return t


def ffn_up(xb, wg, wu, carry=None):
    T, D = xb.shape
    F = wg.shape[0]
    tm = _row_tile(T)
    fc = _row_tile(F, 256)

    def body(x_ref, wg_ref, wu_ref, sg_ref, t_ref, a_ref):
        x = x_ref[...]
        for j in range(F // fc):
            sl = pl.ds(j * fc, fc)
            hg = _dot(x, wg_ref[sl, :], NT)
            hu = _dot(x, wu_ref[sl, :], NT)
            s = jax.nn.sigmoid(hg)
            sg = hg * s
            a = sg * hu
            sg_ref[:, sl] = sg.astype(BF16)
            t_ref[:, sl] = (a + s * (hu - a)).astype(BF16)
            a_ref[:, sl] = a.astype(BF16)

    row = pl.BlockSpec((tm, F), lambda i: (i, 0))
    w = pl.BlockSpec((F, D), lambda i: (0, 0))
    return _call(body, carry, grid=(T // tm,), name="ffn_up", sem=("parallel",), args=(xb, wg, wu),
                 in_specs=[pl.BlockSpec((tm, D), lambda i: (i, 0)), w, w],
                 out_specs=[row, row, row], out_shape=[_sds((T, F), BF16)] * 3)


def ffn_bwd_mid(dr, wd, da_dhu, da_dhg):
    T, D = dr.shape
    F = wd.shape[0]
    tm = _row_tile(T)
    fc = _row_tile(F, 256)

    def body(dr_ref, wd_ref, sg_ref, t_ref, dhg_ref, dhu_ref):
        dr_ = dr_ref[...]
        for j in range(F // fc):
            sl = pl.ds(j * fc, fc)
            da = _dot(dr_, wd_ref[sl, :], NT)
            dhu_ref[:, sl] = (da * sg_ref[:, sl].astype(F32)).astype(BF16)
            dhg_ref[:, sl] = (da * t_ref[:, sl].astype(F32)).astype(BF16)

    row = pl.BlockSpec((tm, F), lambda i: (i, 0))
    return pl.pallas_call(
        body, grid=(T // tm,), name="ffn_bwd_mid",
        in_specs=[pl.BlockSpec((tm, D), lambda i: (i, 0)), pl.BlockSpec((F, D), lambda i: (0, 0)), row, row],
        out_specs=[row, row], out_shape=[_sds((T, F), BF16)] * 2,
        compiler_params=_params("parallel"),
    )(dr, wd, da_dhu, da_dhg)


def mm_nn(pairs, add=None, out_dtype=F32, name="mm_nn"):
    M = pairs[0][0].shape[0]
    N = pairs[0][1].shape[1]
    tm = _row_tile(M)
    nc = _row_tile(N, 512)
    n_pairs = len(pairs)

    def body(*refs):
        a_refs = refs[:n_pairs]
        b_refs = refs[n_pairs:2 * n_pairs]
        add_ref = refs[2 * n_pairs] if add is not None else None
        o_ref = refs[-1]
        for j in range(N // nc):
            sl = pl.ds(j * nc, nc)
            acc = _dot(a_refs[0][...], b_refs[0][:, sl], NN)
            for a_ref, b_ref in zip(a_refs[1:], b_refs[1:]):
                acc = acc + _dot(a_ref[...], b_ref[:, sl], NN)
            if add_ref is not None:
                acc = acc + add_ref[:, sl]
            o_ref[:, sl] = acc.astype(out_dtype)

    in_specs = [pl.BlockSpec((tm, a.shape[1]), lambda i: (i, 0)) for a, _ in pairs]
    in_specs += [pl.BlockSpec(b.shape, lambda i: (0, 0)) for _, b in pairs]
    args = [a for a, _ in pairs] + [b for _, b in pairs]
    if add is not None:
        in_specs.append(pl.BlockSpec((tm, N), lambda i: (i, 0)))
        args.append(add)
    return pl.pallas_call(
        body, grid=(M // tm,), name=name, in_specs=in_specs,
        out_specs=pl.BlockSpec((tm, N), lambda i: (i, 0)), out_shape=_sds((M, N), out_dtype),
        compiler_params=_params("parallel"),
    )(*args)


def _mm_specs(pairs, tm):
    specs = [pl.BlockSpec((tm, a.shape[1]), lambda i: (i, 0)) for a, _ in pairs]
    specs += [pl.BlockSpec(b.shape, lambda i: (0, 0), pipeline_mode=pl.Buffered(1)) for _, b in pairs]
    return specs, [a for a, _ in pairs] + [b for _, b in pairs]


def _mm_sum(a_refs, b_refs):
    acc = _dot(a_refs[0][...], b_refs[0][...], NN)
    for a_ref, b_ref in zip(a_refs[1:], b_refs[1:]):
        acc = acc + _dot(a_ref[...], b_ref[...], NN)
    return acc


def _ln_fwd_rows(u, g, b):
    mu = jnp.mean(u, axis=-1, keepdims=True)
    xc = u - mu
    var = jnp.mean(xc * xc, axis=-1, keepdims=True)
    return xc * lax.rsqrt(var + LN_EPS) * g + b


def _ln_bwd_rows(u, dy, g):
    mu = jnp.mean(u, axis=-1, keepdims=True)
    xc = u - mu
    var = jnp.mean(xc * xc, axis=-1, keepdims=True)
    rstd = lax.rsqrt(var + LN_EPS)
    xh = xc * rstd
    dxh = dy * g
    m1 = jnp.mean(dxh, axis=-1, keepdims=True)
    m2 = jnp.mean(dxh * xh, axis=-1, keepdims=True)
    return rstd * (dxh - m1 - xh * m2), dy * xh


def mm_nn_ln(pairs, x, g, b, res_scale, name):
    M, N = x.shape
    tm = _row_tile(M)
    n_pairs = len(pairs)

    def body(*refs):
        a_refs, b_refs = refs[:n_pairs], refs[n_pairs:2 * n_pairs]
        x_ref, g_ref, b_ref, u_ref, y_ref, yb_ref = refs[2 * n_pairs:]
        u = ALPHA * x_ref[...] + res_scale * _mm_sum(a_refs, b_refs)
        y = _ln_fwd_rows(u, g_ref[...], b_ref[...])
        u_ref[...] = u
        y_ref[...] = y
        yb_ref[...] = y.astype(BF16)

    in_specs, args = _mm_specs(pairs, tm)
    row = pl.BlockSpec((tm, N), lambda i: (i, 0))
    vec = pl.BlockSpec((1, N), lambda i: (0, 0))
    return pl.pallas_call(
        body, grid=(M // tm,), name=name, in_specs=in_specs + [row, vec, vec],
        out_specs=[row, row, row], out_shape=[_sds((M, N), F32), _sds((M, N), F32), _sds((M, N), BF16)],
        compiler_params=_params("parallel"),
    )(*args, x, g.reshape(1, N), b.reshape(1, N))


def mm_nn_ln_bwd(pairs, add, u, g, branch_scale, name):
    M, N = u.shape
    tm = _row_tile(M)
    n_pairs = len(pairs)

    def body(*refs):
        a_refs, b_refs = refs[:n_pairs], refs[n_pairs:2 * n_pairs]
        add_ref, u_ref, g_ref, dxa_ref, dr_ref, dg_ref, db_ref = refs[2 * n_pairs:]

        @pl.when(pl.program_id(0) == 0)
        def _():
            dg_ref[...] = jnp.zeros_like(dg_ref)
            db_ref[...] = jnp.zeros_like(db_ref)

        dy = _mm_sum(a_refs, b_refs) + add_ref[...]
        du, dyxh = _ln_bwd_rows(u_ref[...], dy, g_ref[...])
        dxa_ref[...] = ALPHA * du
        dr_ref[...] = (branch_scale * du).astype(BF16)
        dg_ref[...] += jnp.sum(dyxh.reshape(tm // 8, 8, N), axis=0)
        db_ref[...] += jnp.sum(dy.reshape(tm // 8, 8, N), axis=0)

    in_specs, args = _mm_specs(pairs, tm)
    row = pl.BlockSpec((tm, N), lambda i: (i, 0))
    acc = pl.BlockSpec((8, N), lambda i: (0, 0))
    return pl.pallas_call(
        body, grid=(M // tm,), name=name, in_specs=in_specs + [row, row, pl.BlockSpec((1, N), lambda i: (0, 0))],
        out_specs=[row, row, acc, acc],
        out_shape=[_sds((M, N), F32), _sds((M, N), BF16), _sds((8, N), F32), _sds((8, N), F32)],
        compiler_params=_params("arbitrary"),
    )(*args, add, u, g.reshape(1, N))


def mm_nt(a, b, out_dtype=F32, name="mm_nt"):
    M, K = a.shape
    N = b.shape[0]
    tm = _row_tile(M)
    nc = _row_tile(N, 512)

    def body(a_ref, b_ref, o_ref):
        a_ = a_ref[...]
        for j in range(N // nc):
            sl = pl.ds(j * nc, nc)
            o_ref[:, sl] = _dot(a_, b_ref[sl, :], NT).astype(out_dtype)

    return pl.pallas_call(
        body, grid=(M // tm,), name=name,
        in_specs=[pl.BlockSpec((tm, K), lambda i: (i, 0)), pl.BlockSpec((N, K), lambda i: (0, 0))],
        out_specs=pl.BlockSpec((tm, N), lambda i: (i, 0)), out_shape=_sds((M, N), out_dtype),
        compiler_params=_params("parallel"),
    )(a, b)


def proj_in(xb, w, B, S):
    D = xb.shape[1]
    N = w.shape[0]
    tm = CA_PAD
    nblk = S // tm
    nc = _row_tile(N, 512)
    assert S % tm == 0

    def body(x_ref, w_ref, o_ref):
        i = pl.program_id(1)

        @pl.when(i == 0)
        def _():
            o_ref[...] = jnp.zeros_like(o_ref)

        @pl.when(i > 0)
        def _():
            x = x_ref[...]
            for j in range(N // nc):
                sl = pl.ds(j * nc, nc)
                o_ref[0, :, sl] = _dot(x, w_ref[sl, :], NT).astype(BF16)

    return pl.pallas_call(
        body, grid=(B, nblk + 1), name="proj_in",
        in_specs=[pl.BlockSpec((tm, D), lambda b, i: (b * nblk + jnp.maximum(i - 1, 0), 0)),
                  pl.BlockSpec((N, D), lambda b, i: (0, 0))],
        out_specs=pl.BlockSpec((1, tm, N), lambda b, i: (b, i, 0)), out_shape=_sds((B, CA_PAD + S, N), BF16),
        compiler_params=_params("parallel", "arbitrary"),
    )(xb, w)


def mm_tn(parts, b, name="mm_tn", carry=None):
    T, N = b.shape
    widths = [a.shape[1] for a in parts]
    M = sum(widths)
    tk = _row_tile(T)
    steps = T // tk
    n_parts = len(parts)

    def body(*refs):
        a_refs, b_ref, o_ref, acc_ref = refs[:n_parts], refs[n_parts], refs[n_parts + 1], refs[n_parts + 2]
        i = pl.program_id(0)

        @pl.when(i == 0)
        def _():
            acc_ref[...] = jnp.zeros_like(acc_ref)

        b_ = b_ref[...]
        row = 0
        for a_ref, width in zip(a_refs, widths):
            mc = _row_tile(width, 256)
            for j in range(width // mc):
                acc_ref[pl.ds(row, mc), :] += _dot(a_ref[:, pl.ds(j * mc, mc)], b_, TN)
                row += mc

        @pl.when(i == steps - 1)
        def _():
            o_ref[...] = acc_ref[...].astype(BF16)

    (out,), got = _call(
        body, carry, grid=(steps,), name=name, sem=("arbitrary",), args=(*parts, b),
        in_specs=[pl.BlockSpec((tk, w), lambda i: (i, 0)) for w in widths] + [pl.BlockSpec((tk, N), lambda i: (i, 0))],
        out_specs=[pl.BlockSpec((M, N), lambda i: (0, 0))], out_shape=[_sds((M, N), BF16)],
        scratch_shapes=[pltpu.VMEM((M, N), F32)])
    return out, got


def ffn_dw(dhg, dhu, a, xb, dr, carry=None):
    T, F = dhg.shape
    D = xb.shape[1]
    tk = _row_tile(T)
    mc = _row_tile(F, 256)
    steps = T // tk
    last = steps - 1

    def body(dhg_ref, dhu_ref, a_ref, xb_ref, dr_ref, o_ref, acc_ref):
        p = pl.program_id(0)
        i = pl.program_id(1)

        @pl.when(i == 0)
        def _():
            acc_ref[...] = jnp.zeros_like(acc_ref)

        def accumulate(lhs_ref, rhs_ref):
            rhs = rhs_ref[...]
            for j in range(F // mc):
                sl = pl.ds(j * mc, mc)
                acc_ref[sl, :] += _dot(lhs_ref[:, sl], rhs, TN)

        pl.when(p == 0)(lambda: accumulate(dhg_ref, xb_ref))
        pl.when(p == 1)(lambda: accumulate(dhu_ref, xb_ref))
        pl.when(p == 2)(lambda: accumulate(a_ref, dr_ref))

        @pl.when(i == last)
        def _():
            o_ref[0] = acc_ref[...].astype(BF16)

    def during(phases, width):
        lo, hi = phases
        return pl.BlockSpec((tk, width), lambda p, i: (jnp.where(p < lo, 0, jnp.where(p > hi, last, i)), 0))

    (out,), got = _call(
        body, carry, grid=(3, steps), name="ffn_dw", sem=("arbitrary", "arbitrary"), args=(dhg, dhu, a, xb, dr),
        in_specs=[during((0, 0), F), during((1, 1), F), during((2, 2), F), during((0, 1), D), during((2, 2), D)],
        out_specs=[pl.BlockSpec((1, F, D), lambda p, i: (p, 0, 0))], out_shape=[_sds((3, F, D), BF16)],
        scratch_shapes=[pltpu.VMEM((F, D), F32)])
    return out, got


def ln_bwd(u, dy, g, branch_scale):
    T, D = u.shape
    tm = _row_tile(T)

    def body(u_ref, dy_ref, g_ref, dxa_ref, dr_ref, dg_ref, db_ref):
        @pl.when(pl.program_id(0) == 0)
        def _():
            dg_ref[...] = jnp.zeros_like(dg_ref)
            db_ref[...] = jnp.zeros_like(db_ref)

        dy_ = dy_ref[...]
        du, dyxh = _ln_bwd_rows(u_ref[...], dy_, g_ref[...])
        dxa_ref[...] = ALPHA * du
        dr_ref[...] = (branch_scale * du).astype(BF16)
        dg_ref[...] += jnp.sum(dyxh.reshape(tm // 8, 8, D), axis=0)
        db_ref[...] += jnp.sum(dy_.reshape(tm // 8, 8, D), axis=0)

    row = pl.BlockSpec((tm, D), lambda i: (i, 0))
    acc = pl.BlockSpec((8, D), lambda i: (0, 0))
    dxa, dr, dg, db = pl.pallas_call(
        body, grid=(T // tm,), name="ln_bwd", in_specs=[row, row, pl.BlockSpec((1, D), lambda i: (0, 0))],
        out_specs=[row, row, acc, acc],
        out_shape=[_sds((T, D), F32), _sds((T, D), BF16), _sds((8, D), F32), _sds((8, D), F32)],
        compiler_params=_params("arbitrary"),
    )(u, dy, g.reshape(1, D))
    return dxa, dr, dg, db


def loss_head(y, target):
    T, D = y.shape
    tm = _row_tile(T)

    def body(y_ref, t_ref, dy_ref, l_ref):
        @pl.when(pl.program_id(0) == 0)
        def _():
            l_ref[...] = jnp.zeros_like(l_ref)

        e = y_ref[...] - t_ref[...]
        dy_ref[...] = e * (1.0 / D)
        l_ref[...] += jnp.sum((e * e).reshape(tm // 8, 8, D), axis=0) * (0.5 / D)

    row = pl.BlockSpec((tm, D), lambda i: (i, 0))
    return pl.pallas_call(
        body, grid=(T // tm,), name="loss_head", in_specs=[row, row],
        out_specs=[row, pl.BlockSpec((8, D), lambda i: (0, 0))],
        out_shape=[_sds((T, D), F32), _sds((8, D), F32)],
        compiler_params=_params("arbitrary"),
    )(y, target)


def adamw(w, g, m, v):
    R, C = w.shape
    tr = R
    for cand in (512, 256, 128, 64, 32, 16, 8):
        if R % cand == 0:
            tr = cand
            break
    c1 = 1.0 - ADAM_B1 ** ADAM_STEP
    c2 = 1.0 - ADAM_B2 ** ADAM_STEP

    def body(w_ref, g_ref, m_ref, v_ref, d_ref, mo_ref, vo_ref):
        g_ = g_ref[...]
        m_ = ADAM_B1 * m_ref[...] + (1.0 - ADAM_B1) * g_
        v_ = ADAM_B2 * v_ref[...] + (1.0 - ADAM_B2) * (g_ * g_)
        m_hat = m_ / c1
        v_hat = v_ / c2
        d_ref[...] = -ADAM_LR * (m_hat / (jnp.sqrt(v_hat) + ADAM_EPS) + ADAM_WD * w_ref[...])
        mo_ref[...] = m_
        vo_ref[...] = v_

    blk = pl.BlockSpec((tr, C), lambda i: (i, 0))
    return pl.pallas_call(
        body, grid=(R // tr,), name="adamw", in_specs=[blk] * 4, out_specs=[blk] * 3,
        out_shape=[_sds((R, C), F32)] * 3, compiler_params=_params("parallel"),
    )(w, g, m, v)


def _lane_lo(rows):
    return lax.broadcasted_iota(jnp.int32, (rows, LANES), 1) < HEAD_DIM


def _pair_mean(x, lo):
    s0 = jnp.sum(jnp.where(lo, x, 0.0), axis=-1, keepdims=True)
    s1 = jnp.sum(jnp.where(lo, 0.0, x), axis=-1, keepdims=True)
    return jnp.where(lo, s0, s1) * (1.0 / HEAD_DIM)


def _rms_fwd(o, gain, lo):
    r = lax.rsqrt(_pair_mean(o * o, lo) + RMS_EPS)
    return (o * r * gain).astype(BF16)


def _rms_bwd(o, dm, gain, lo):
    r = lax.rsqrt(_pair_mean(o * o, lo) + RMS_EPS)
    oh = o * r
    dg = jnp.sum(dm * oh, axis=0, keepdims=True)
    doh = dm * gain
    do = r * (doh - oh * _pair_mean(doh * oh, lo))
    return do, dg


def _split_heads(x, lo):
    zero = jnp.zeros_like(x)
    return [jnp.where(lo, x, zero), jnp.where(lo, zero, x)]


def _merge_pairs(per_head, lo):
    return [jnp.where(lo, per_head[2 * p], per_head[2 * p + 1]) for p in range(len(per_head) // 2)]


def _pair_cols(h):
    return slice((h // 2) * LANES, (h // 2 + 1) * LANES)


def _split_dot(x, tri):
    n = x.shape[0]
    hi = x.astype(BF16)
    lo = (x - hi.astype(F32)).astype(BF16)
    both = _dot(jnp.concatenate([hi, lo], axis=0), tri, NN)
    return both[:n] + both[n:]


def _log_keep(z):
    return -(jnp.maximum(z, 0.0) + jnp.log(1.0 + jnp.exp(-jnp.abs(z))))


def _attn_dims(h3p, gain, heads):
    B, SP, C = h3p.shape
    HD = gain.shape[1]
    hs = min(heads, HD // HEAD_DIM)
    W = hs * HEAD_DIM
    assert C == 6 * HD and W % LANES == 0 and HD % W == 0
    return B, SP - CA_PAD, HD, hs, W, HD // W


def sb_fwd(h3p, gain, carry=None):
    B, S, HD, hs, W, ngrp = _attn_dims(h3p, gain, FWD_HEADS)
    tb = min(SB_BLOCK, S)
    nq = S // tb
    off = CA_PAD // tb
    scale = 1.0 / math.sqrt(HEAD_DIM)

    def body(q_ref, k_ref, v_ref, g_ref, o_ref, m_ref, c_ref):
        qi = pl.program_id(2)
        lo = _lane_lo(tb)
        row = lax.broadcasted_iota(jnp.int32, (tb, tb), 0)
        col = lax.broadcasted_iota(jnp.int32, (tb, tb), 1)
        rev_incl = (row >= col).astype(BF16)
        lane = lax.broadcasted_iota(jnp.int32, (tb, nq), 1)
        qm = []
        for p in range(hs // 2):
            qm += _split_heads((q_ref[0, :, p * LANES:(p + 1) * LANES].astype(F32) * scale).astype(BF16), lo)

        def blocks(kb, carries_, mask):
            ks = pl.ds(pl.multiple_of(kb * tb + CA_PAD, tb), tb)
            hh = range(hs)
            zs = [_dot(qm[h], k_ref[0, ks, _pair_cols(h)], NT) for h in hh]
            lks = [_log_keep(z) for z in zs]
            if mask is not None:
                lks = [jnp.where(mask, lk, 0.0) for lk in lks]
            cums = [_split_dot(lk, rev_incl) for lk in lks]
            ws = []
            for h in hh:
                logw = zs[h] + cums[h] + carries_[h]
                if mask is not None:
                    logw = jnp.where(mask, logw, NEG)
                ws.append(jnp.exp(logw).astype(BF16))
            pvs = [_dot(ws[h], v_ref[0, ks, _pair_cols(h)], NN) for h in hh]
            return [(pvs[h], cums[h][:, 0:1]) for h in hh]

        diag = blocks(qi, [jnp.zeros((tb, 1), F32)] * hs, col < row)
        accs = _merge_pairs([d[0] for d in diag], lo)
        carries = [d[1] for d in diag]
        cars = [jnp.where(lane == qi, 0.0, NEG)] * hs

        def cond(st):
            kb, carries_, _, _ = st
            top = carries_[0]
            for c in carries_[1:]:
                top = jnp.maximum(top, c)
            return jnp.logical_and(kb >= 0, jnp.max(top) > SB_SKIP_LOG)

        def step(st):
            kb, carries_, accs_, cars_ = st
            out = blocks(kb, carries_, None)
            pv = _merge_pairs([o[0] for o in out], lo)
            return (kb - 1, [c + o[1] for c, o in zip(carries_, out)], [a + p for a, p in zip(accs_, pv)],
                    [jnp.where(lane == kb, c, cs) for c, cs in zip(carries_, cars_)])

        _, _, accs, cars = lax.while_loop(cond, step, (qi - 1, carries, accs, cars))
        for p, acc in enumerate(accs):
            cols = slice(p * LANES, (p + 1) * LANES)
            o_ref[0, :, cols] = acc
            m_ref[0, :, cols] = _rms_fwd(acc, g_ref[:, cols], lo)
        for h in range(hs):
            c_ref[0, h] = cars[h]

    qspec = pl.BlockSpec((1, tb, W), lambda g, b, i: (b, i + off, g))
    ospec = pl.BlockSpec((1, tb, W), lambda g, b, i: (b, i, g))
    return _call(
        body, carry, grid=(ngrp, B, nq), name="sb_fwd", sem=("parallel", "parallel", "arbitrary"),
        args=(h3p, h3p, h3p, gain),
        in_specs=[qspec, pl.BlockSpec((1, CA_PAD + S, W), lambda g, b, i: (b, 0, ngrp + g)),
                  pl.BlockSpec((1, CA_PAD + S, W), lambda g, b, i: (b, 0, 2 * ngrp + g)),
                  pl.BlockSpec((1, W), lambda g, b, i: (0, g))],
        out_specs=[ospec, ospec, pl.BlockSpec((1, hs, tb, nq), lambda g, b, i: (b, g, i, 0))],
        out_shape=[_sds((B, S, HD), F32), _sds((B, S, HD), BF16), _sds((B, HD // HEAD_DIM, S, nq), F32)])


def sb_bwd(h3p, o, dmix, gain, cars, carry=None):
    B, S, HD, hs, W, ngrp = _attn_dims(h3p, gain, BWD_HEADS)
    tb = min(SB_BLOCK, S)
    nq = S // tb
    off = CA_PAD // tb
    scale = 1.0 / math.sqrt(HEAD_DIM)

    def body(q_ref, k_ref, v_ref, o_ref, dm_ref, g_ref, c_ref, dq_ref, dk_ref, dv_ref, dg_ref, dk_acc, dv_acc):
        qi = pl.program_id(2)

        @pl.when(qi == 0)
        def _():
            dk_acc[...] = jnp.zeros_like(dk_acc)
            dv_acc[...] = jnp.zeros_like(dv_acc)

        lo = _lane_lo(tb)
        row = lax.broadcasted_iota(jnp.int32, (tb, tb), 0)
        col = lax.broadcasted_iota(jnp.int32, (tb, tb), 1)
        rev_incl = (row >= col).astype(BF16)
        fwd_incl = (row <= col).astype(BF16)
        lane = lax.broadcasted_iota(jnp.int32, (tb, nq), 1)
        below = lax.broadcasted_iota(jnp.int32, (1, nq), 1) < qi
        qm, dom, cars_, seen = [], [], [], None
        for p in range(hs // 2):
            cols = slice(p * LANES, (p + 1) * LANES)
            qm += _split_heads((q_ref[0, :, cols].astype(F32) * scale).astype(BF16), lo)
            do, dg = _rms_bwd(o_ref[0, :, cols], dm_ref[0, :, cols], g_ref[:, cols], lo)
            dg_ref[0, 0, :, cols] = dg
            dom += _split_heads(do.astype(BF16), lo)
        for h in range(hs):
            cars_.append(c_ref[0, h])
            visited = jnp.logical_and(jnp.max(cars_[h], axis=0, keepdims=True) > SB_SKIP_LOG, below)
            n = jnp.sum(visited.astype(jnp.int32), axis=1, keepdims=True)
            seen = n if seen is None else jnp.maximum(seen, n)
        first = qi - jnp.max(seen)

        def blocks(kb, gsums, dqs, mask):
            ks = pl.ds(pl.multiple_of(kb * tb + CA_PAD, tb), tb)
            ko = pl.ds(pl.multiple_of(kb * tb, tb), tb)
            hh = range(hs)
            kk = [k_ref[0, ks, p * LANES:(p + 1) * LANES] for p in range(hs // 2)]
            vv = [v_ref[0, ks, p * LANES:(p + 1) * LANES] for p in range(hs // 2)]
            zs = [_dot(qm[h], kk[h // 2], NT) for h in hh]
            dws = [_dot(dom[h], vv[h // 2], NT) for h in hh]
            raw = [_log_keep(z) for z in zs]
            lks = raw if mask is None else [jnp.where(mask, lk, 0.0) for lk in raw]
            cums = [_split_dot(lk, rev_incl) for lk in lks]
            ws = []
            for h in hh:
                carry = jnp.sum(jnp.where(lane == kb, cars_[h], 0.0), axis=1, keepdims=True)
                logw = zs[h] + cums[h] + carry
                if mask is not None:
                    logw = jnp.where(mask, logw, NEG)
                ws.append(jnp.exp(logw))
            gws = [ws[h] * dws[h] for h in hh]
            gcums = [_split_dot(gws[h], fwd_incl) + gsums[h] for h in hh]
            dzb = []
            for h in hh:
                dz = gws[h] - jnp.exp(zs[h] + raw[h]) * gcums[h]
                if mask is not None:
                    dz = jnp.where(mask, dz, 0.0)
                dzb.append(dz.astype(BF16))
            wb = [w.astype(BF16) for w in ws]
            new_dq = [dqs[h] + _dot(dzb[h], kk[h // 2], NN) for h in hh]
            for p in range(hs // 2):
                cols = slice(p * LANES, (p + 1) * LANES)
                dk_acc[ko, cols] += _dot(dzb[2 * p], qm[2 * p], TN) + _dot(dzb[2 * p + 1], qm[2 * p + 1], TN)
                dv_acc[ko, cols] += _dot(wb[2 * p], dom[2 * p], TN) + _dot(wb[2 * p + 1], dom[2 * p + 1], TN)
            return [g[:, tb - 1:tb] for g in gcums], new_dq

        def step(kb, st):
            return blocks(kb, st[0], st[1], None)

        init = ([jnp.zeros((tb, 1), F32)] * hs, [jnp.zeros((tb, LANES), F32)] * hs)
        gsum, dq = lax.fori_loop(first, qi, step, init)
        _, dq = blocks(qi, gsum, dq, col < row)
        for p, d in enumerate(_merge_pairs(dq, lo)):
            dq_ref[0, :, p * LANES:(p + 1) * LANES] = (d * scale).astype(BF16)

        @pl.when(qi == nq - 1)
        def _():
            dk_ref[0] = dk_acc[...].astype(BF16)
            dv_ref[0] = dv_acc[...].astype(BF16)

    once = pl.Buffered(1)
    qspec = pl.BlockSpec((1, tb, W), lambda g, b, i: (b, i + off, g))
    ospec = pl.BlockSpec((1, tb, W), lambda g, b, i: (b, i, g))
    kvout = pl.BlockSpec((1, S, W), lambda g, b, i: (b, 0, g), pipeline_mode=once)
    return _call(
        body, carry, grid=(ngrp, B, nq), name="sb_bwd", sem=("parallel", "parallel", "arbitrary"),
        args=(h3p, h3p, h3p, o, dmix, gain, cars),
        in_specs=[qspec, pl.BlockSpec((1, CA_PAD + S, W), lambda g, b, i: (b, 0, ngrp + g), pipeline_mode=once),
                  pl.BlockSpec((1, CA_PAD + S, W), lambda g, b, i: (b, 0, 2 * ngrp + g), pipeline_mode=once),
                  ospec, ospec, pl.BlockSpec((1, W), lambda g, b, i: (0, g)),
                  pl.BlockSpec((1, hs, tb, nq), lambda g, b, i: (b, g, i, 0))],
        out_specs=[ospec, kvout, kvout, pl.BlockSpec((1, 1, 1, W), lambda g, b, i: (b, i, 0, g))],
        out_shape=[_sds((B, S, HD), BF16), _sds((B, S, HD), BF16), _sds((B, S, HD), BF16), _sds((B, nq, 1, HD), F32)],
        scratch_shapes=[pltpu.VMEM((S, W), F32), pltpu.VMEM((S, W), F32)])


def _ca_rel_index():
    width = CA_WIN + CA_TQ
    c = np.arange(width)
    dj = np.where(c < CA_WIN, c, c - width)
    return np.clip(CA_PAD - dj, -MAX_REL, MAX_REL) + MAX_REL, width


def _ca_onehot():
    idx, _ = _ca_rel_index()
    return (idx[:, None] == np.arange(2 * MAX_REL + 1)[None, :]).astype(np.float32)


def ca_bias(rel_bias):
    _, width = _ca_rel_index()
    lead = rel_bias.shape[:-1]
    by_offset = jnp.dot(rel_bias, jnp.asarray(_ca_onehot().T), precision=lax.Precision.HIGHEST)
    tile = jnp.broadcast_to(by_offset[..., None, :], lead + (CA_TQ, width)).reshape(lead + (CA_TQ * width,))
    tile = tile[..., :CA_TQ * (width - 1)].reshape(lead + (CA_TQ, width - 1))[..., :CA_WIN]
    t = np.arange(CA_TQ)[:, None] // CHUNK * CHUNK
    j = np.arange(CA_WIN)[None, :]
    return jnp.where((j >= t) & (j < t + CA_PAD + CHUNK), tile, NEG)


def rel_bias_grad(db):
    H = db.shape[0]
    _, width = _ca_rel_index()
    x = jnp.pad(db, ((0, 0), (0, 0), (0, width - 1 - CA_WIN))).reshape(H, CA_TQ * (width - 1))
    x = jnp.pad(x, ((0, 0), (0, CA_TQ))).reshape(H, CA_TQ, width).sum(axis=1)
    return jnp.dot(x, jnp.asarray(_ca_onehot()), precision=lax.Precision.HIGHEST)


def _ca_scores(qm_h, kk, bias_h, valid):
    return jnp.where(valid, _dot(qm_h, kk, NT) + bias_h, NEG)


def _ca_softmax(s):
    e = jnp.exp(s - jnp.max(s, axis=-1, keepdims=True))
    return e * (1.0 / jnp.sum(e, axis=-1, keepdims=True))


def ca_fwd(h3p, bias, gain, carry=None):
    B, S, HD, hs, W, ngrp = _attn_dims(h3p, gain, FWD_HEADS)
    scale = 1.0 / math.sqrt(HEAD_DIM)
    off = CA_PAD // CA_TQ

    def body(q_ref, k_ref, v_ref, b_ref, g_ref, o_ref, m_ref):
        q0 = pl.program_id(2) * CA_TQ
        ks = pl.ds(pl.multiple_of(q0, CA_TQ), CA_WIN)
        lo = _lane_lo(CA_TQ)
        valid = lax.broadcasted_iota(jnp.int32, (CA_TQ, CA_WIN), 1) + q0 >= CA_PAD
        qm, kk, vv = [], [], []
        for p in range(hs // 2):
            cols = slice(p * LANES, (p + 1) * LANES)
            qm += _split_heads((q_ref[0, :, cols].astype(F32) * scale).astype(BF16), lo)
            kk.append(k_ref[0, ks, cols])
            vv.append(v_ref[0, ks, cols])
        ss = [_ca_scores(qm[h], kk[h // 2], b_ref[h], valid) for h in range(hs)]
        ps = [_ca_softmax(s).astype(BF16) for s in ss]
        pv = [_dot(ps[h], vv[h // 2], NN) for h in range(hs)]
        for p, o in enumerate(_merge_pairs(pv, lo)):
            cols = slice(p * LANES, (p + 1) * LANES)
            o_ref[0, :, cols] = o
            m_ref[0, :, cols] = _rms_fwd(o, g_ref[:, cols], lo)

    ospec = pl.BlockSpec((1, CA_TQ, W), lambda g, b, i: (b, i, g))
    return _call(
        body, carry, grid=(ngrp, B, S // CA_TQ), name="ca_fwd", sem=("parallel", "parallel", "arbitrary"),
        args=(h3p, h3p, h3p, bias, gain),
        in_specs=[pl.BlockSpec((1, CA_TQ, W), lambda g, b, i: (b, i + off, 3 * ngrp + g)),
                  pl.BlockSpec((1, CA_PAD + S, W), lambda g, b, i: (b, 0, 4 * ngrp + g)),
                  pl.BlockSpec((1, CA_PAD + S, W), lambda g, b, i: (b, 0, 5 * ngrp + g)),
                  pl.BlockSpec((hs, CA_TQ, CA_WIN), lambda g, b, i: (g, 0, 0)),
                  pl.BlockSpec((1, W), lambda g, b, i: (0, g))],
        out_specs=[ospec, ospec], out_shape=[_sds((B, S, HD), F32), _sds((B, S, HD), BF16)])


def ca_bwd(h3p, bias, o, dmix, gain, carry=None):
    B, S, HD, hs, W, ngrp = _attn_dims(h3p, gain, BWD_HEADS)
    scale = 1.0 / math.sqrt(HEAD_DIM)
    nq = S // CA_TQ
    off = CA_PAD // CA_TQ

    def body(q_ref, k_ref, v_ref, b_ref, o_ref, dm_ref, g_ref, dq_ref, dk_ref, dv_ref, dg_ref, db_ref, dk_acc, dv_acc):
        bi = pl.program_id(1)
        qi = pl.program_id(2)

        @pl.when(qi == 0)
        def _():
            dk_acc[...] = jnp.zeros_like(dk_acc)
            dv_acc[...] = jnp.zeros_like(dv_acc)

        @pl.when(jnp.logical_and(qi == 0, bi == 0))
        def _():
            db_ref[...] = jnp.zeros_like(db_ref)

        q0 = qi * CA_TQ
        ks = pl.ds(pl.multiple_of(q0, CA_TQ), CA_WIN)
        lo = _lane_lo(CA_TQ)
        valid = lax.broadcasted_iota(jnp.int32, (CA_TQ, CA_WIN), 1) + q0 >= CA_PAD
        qm, dom, kk, vv = [], [], [], []
        for p in range(hs // 2):
            cols = slice(p * LANES, (p + 1) * LANES)
            qm += _split_heads((q_ref[0, :, cols].astype(F32) * scale).astype(BF16), lo)
            do, dg = _rms_bwd(o_ref[0, :, cols], dm_ref[0, :, cols], g_ref[:, cols], lo)
            dg_ref[0, 0, :, cols] = dg
            dom += _split_heads(do.astype(BF16), lo)
            kk.append(k_ref[0, ks, cols])
            vv.append(v_ref[0, ks, cols])
        hh = range(hs)
        ss = [_ca_scores(qm[h], kk[h // 2], b_ref[h], valid) for h in hh]
        dps = [_dot(dom[h], vv[h // 2], NT) for h in hh]
        ps = [_ca_softmax(s) for s in ss]
        dss = [ps[h] * (dps[h] - jnp.sum(ps[h] * dps[h], axis=-1, keepdims=True)) for h in hh]
        for h in hh:
            db_ref[h] += dss[h]
        dsb = [d.astype(BF16) for d in dss]
        pb = [p_.astype(BF16) for p_ in ps]
        dq = [_dot(dsb[h], kk[h // 2], NN) for h in hh]
        for p in range(hs // 2):
            cols = slice(p * LANES, (p + 1) * LANES)
            dk_acc[ks, cols] += _dot(dsb[2 * p], qm[2 * p], TN) + _dot(dsb[2 * p + 1], qm[2 * p + 1], TN)
            dv_acc[ks, cols] += _dot(pb[2 * p], dom[2 * p], TN) + _dot(pb[2 * p + 1], dom[2 * p + 1], TN)
        for p, d in enumerate(_merge_pairs(dq, lo)):
            dq_ref[0, :, p * LANES:(p + 1) * LANES] = (d * scale).astype(BF16)

        @pl.when(qi == nq - 1)
        def _():
            dk_ref[0] = dk_acc[CA_PAD:, :].astype(BF16)
            dv_ref[0] = dv_acc[CA_PAD:, :].astype(BF16)

    once = pl.Buffered(1)
    ospec = pl.BlockSpec((1, CA_TQ, W), lambda g, b, i: (b, i, g))
    kvout = pl.BlockSpec((1, S, W), lambda g, b, i: (b, 0, g), pipeline_mode=once)
    bspec = pl.BlockSpec((hs, CA_TQ, CA_WIN), lambda g, b, i: (g, 0, 0))
    return _call(
        body, carry, grid=(ngrp, B, nq), name="ca_bwd", sem=("parallel", "arbitrary", "arbitrary"),
        args=(h3p, h3p, h3p, bias, o, dmix, gain),
        in_specs=[pl.BlockSpec((1, CA_TQ, W), lambda g, b, i: (b, i + off, 3 * ngrp + g)),
                  pl.BlockSpec((1, CA_PAD + S, W), lambda g, b, i: (b, 0, 4 * ngrp + g), pipeline_mode=once),
                  pl.BlockSpec((1, CA_PAD + S, W), lambda g, b, i: (b, 0, 5 * ngrp + g), pipeline_mode=once),
                  bspec, ospec, pl.BlockSpec((1, CA_TQ, W), lambda g, b, i: (b, i, ngrp + g)),
                  pl.BlockSpec((1, W), lambda g, b, i: (0, g))],
        out_specs=[ospec, kvout, kvout, pl.BlockSpec((1, 1, 1, W), lambda g, b, i: (b, i, 0, g)), bspec],
        out_shape=[_sds((B, S, HD), BF16), _sds((B, S, HD), BF16), _sds((B, S, HD), BF16),
                   _sds((B, nq, 1, HD), F32), _sds((HD // HEAD_DIM, CA_TQ, CA_WIN), F32)],
        scratch_shapes=[pltpu.VMEM((CA_PAD + S, W), F32), pltpu.VMEM((CA_PAD + S, W), F32)])


_ANY = pl.BlockSpec(memory_space=pl.ANY)
_MESH = pl.DeviceIdType.MESH


def _mesh_pos():
    return lax.axis_index("x"), lax.axis_index("y"), lax.axis_index("c")


def _dev_index(p):
    return 4 * p[0] + 2 * p[1] + p[2]


def _flip(pos, k):
    return tuple(1 - v if (k >> (2 - a)) & 1 else v for a, v in enumerate(pos))


class _Plan:
    def __init__(self, operands, n_matrices):
        self.operands = list(operands)
        self.n_ops = len(self.operands)
        self.nm = n_matrices
        self.scratch = [pltpu.SemaphoreType.DMA((7, self.nm)), pltpu.SemaphoreType.DMA((7, self.nm)),
                        pltpu.SemaphoreType.DMA((self.nm,))]


class _Gather(_Plan):
    def __init__(self, mats):
        super().__init__(mats, len(mats))
        self.rows = [m.shape[0] for m in mats]
        self.out_shape = [_sds((N_DEV * m.shape[0], m.shape[1]), m.dtype) for m in mats]

    def begin(self, ins, outs, sems):
        nm, rows = self.nm, self.rows
        send_sems, recv_sems, local_sems = sems
        x, y, c = _mesh_pos()
        me, sibling = (x, y, c), (x, y, 1 - c)
        chips = [(1 - x, y), (x, 1 - y), (1 - x, 1 - y)]

        def block(m, p):
            start = pl.multiple_of(_dev_index(p) * rows[m], 16)
            return outs[m].at[pl.ds(start, rows[m]), :]

        def copy(k, m, blk, to, src=None):
            return pltpu.make_async_remote_copy(
                src_ref=block(m, blk) if src is None else src, dst_ref=block(m, blk),
                send_sem=send_sems.at[k, m], recv_sem=recv_sems.at[k, m], device_id=to, device_id_type=_MESH)

        def mine():
            return [pltpu.make_async_copy(ins[m], block(m, me), local_sems.at[m]) for m in range(nm)]

        def first():
            own = [copy(0, m, me, sibling, src=ins[m]) for m in range(nm)]
            return own + [copy(1 + j, m, me, (*chip, c), src=ins[m]) for j, chip in enumerate(chips) for m in range(nm)]

        def start():
            for cp in mine() + first():
                cp.start()

        def passed():
            return [copy(4 + j, m, (*chip, c), sibling) for j, chip in enumerate(chips) for m in range(nm)]

        def pass_on():
            for j, chip in enumerate(chips):
                for m in range(nm):
                    copy(1 + j, m, (*chip, c), me).wait_recv()
                    copy(4 + j, m, (*chip, c), sibling).start()

        def finish():
            for m in range(nm):
                copy(0, m, sibling, me).wait_recv()
            for j, chip in enumerate(chips):
                for m in range(nm):
                    copy(4 + j, m, (*chip, 1 - c), me).wait_recv()
            for cp in first() + passed():
                cp.wait_send()
            for cp in mine():
                cp.wait()

        return start, pass_on, finish


class _Exchange(_Plan):
    def __init__(self, grads):
        self.where = [(i, j) for i, g in enumerate(grads) for j in (range(g.shape[0]) if g.ndim == 3 else [None])]
        super().__init__(grads, len(self.where))
        self.rows = [grads[i].shape[-2] // N_DEV for i, _ in self.where]
        self.out_shape = [_sds(g.shape[:-2] + (N_DEV, g.shape[-2] // N_DEV, g.shape[-1]), g.dtype) for g in grads]

    def begin(self, ins, outs, sems):
        nm, rows = self.nm, self.rows
        send_sems, recv_sems, local_sems = sems
        me = _mesh_pos()
        my = _dev_index(me)

        def piece(m, idx):
            i, j = self.where[m]
            rows_ = pl.ds(pl.multiple_of(idx * rows[m], 16), rows[m])
            return ins[i].at[rows_, :] if j is None else ins[i].at[j, rows_, :]

        def landing(m, slot):
            i, j = self.where[m]
            return outs[i].at[slot] if j is None else outs[i].at[j, slot]

        def copy(k, m, src_idx, slot, to):
            return pltpu.make_async_remote_copy(
                src_ref=piece(m, src_idx), dst_ref=landing(m, slot),
                send_sem=send_sems.at[k - 1, m], recv_sem=recv_sems.at[k - 1, m], device_id=to, device_id_type=_MESH)

        def mine():
            return [pltpu.make_async_copy(piece(m, my), landing(m, my), local_sems.at[m]) for m in range(nm)]

        def sends():
            return [copy(k, m, _dev_index(_flip(me, k)), my, _flip(me, k)) for k in range(1, N_DEV) for m in range(nm)]

        def start():
            for cp in mine() + sends():
                cp.start()

        def finish():
            for k in range(1, N_DEV):
                peer = _flip(me, k)
                for m in range(nm):
                    copy(k, m, my, _dev_index(peer), peer).wait_recv()
            for cp in sends():
                cp.wait_send()
            for cp in mine():
                cp.wait()

        return start, None, finish


def _run_plan(plan, name):
    nm = plan.n_ops

    def body(*refs):
        start, middle, finish = plan.begin(refs[:nm], refs[nm:2 * nm], refs[2 * nm:])
        start()
        if middle is not None:
            middle()
        finish()

    return pl.pallas_call(body, name=name, in_specs=[_ANY] * nm, out_specs=[_ANY] * nm, out_shape=plan.out_shape,
                          scratch_shapes=plan.scratch)(*plan.operands)


def _call(body, carry, *, grid, in_specs, out_specs, out_shape, args, name, sem, scratch_shapes=()):
    if carry is None:
        outs = pl.pallas_call(body, grid=grid, name=name, in_specs=in_specs, out_specs=out_specs, out_shape=out_shape,
                              scratch_shapes=list(scratch_shapes), compiler_params=_params(*sem))(*args)
        return outs, None
    n_in, n_out, n_sc, nm = len(in_specs), len(out_specs), len(scratch_shapes), carry.n_ops

    def carrier(*refs):
        refs = list(refs)
        own_in, refs = refs[:n_in], refs[n_in:]
        c_in, refs = refs[:nm], refs[nm:]
        own_out, refs = refs[:n_out], refs[n_out:]
        c_out, refs = refs[:nm], refs[nm:]
        own_sc, c_sc = refs[:n_sc], refs[n_sc:]
        start, middle, finish = carry.begin(c_in, c_out, c_sc)
        step, steps = 0, 1
        for axis, n in enumerate(grid):
            step = step * n + pl.program_id(axis)
            steps *= n
        pl.when(step == 0)(start)
        body(*own_in, *own_out, *own_sc)

        @pl.when(step == steps - 1)
        def _():
            if middle is not None:
                middle()
            finish()

    outs = pl.pallas_call(
        carrier, grid=grid, name=name + "_carry", in_specs=list(in_specs) + [_ANY] * nm,
        out_specs=list(out_specs) + [_ANY] * nm, out_shape=list(out_shape) + carry.out_shape,
        scratch_shapes=list(scratch_shapes) + carry.scratch,
        compiler_params=_params(*["arbitrary"] * len(grid)),
    )(*args, *carry.operands)
    return outs[:n_out], outs[n_out:]


def sum_slots(r, j=None):
    n, R, D = r.shape[-3:]
    tc = _row_tile(D, 256)

    def body(r_ref, o_ref):
        slot = (lambda s: r_ref[s]) if j is None else (lambda s: r_ref[0, s])
        acc = slot(0).astype(F32)
        for s in range(1, n):
            acc = acc + slot(s).astype(F32)
        o_ref[...] = acc

    spec = (pl.BlockSpec((n, R, tc), lambda i: (0, 0, i)) if j is None
            else pl.BlockSpec((1, n, R, tc), lambda i: (j, 0, 0, i)))
    return pl.pallas_call(
        body, grid=(D // tc,), name="sum_slots", in_specs=[spec],
        out_specs=pl.BlockSpec((R, tc), lambda i: (0, i)), out_shape=_sds((R, D), F32),
        compiler_params=_params("parallel"),
    )(r)


def allreduce_small(v):
    R, C = v.shape

    def body(v_ref, o_ref, buf, send_sems, recv_sems):
        me = _mesh_pos()
        my = _dev_index(me)
        buf[my] = v_ref[...]

        def copy(k, slot, to):
            return pltpu.make_async_remote_copy(
                src_ref=v_ref, dst_ref=buf.at[slot], send_sem=send_sems.at[k - 1], recv_sem=recv_sems.at[k - 1],
                device_id=to, device_id_type=_MESH)

        sends = []
        for k in range(1, N_DEV):
            cp = copy(k, my, _flip(me, k))
            cp.start()
            sends.append(cp)
        for k in range(1, N_DEV):
            peer = _flip(me, k)
            copy(k, _dev_index(peer), peer).wait_recv()
        acc = buf[0]
        for s in range(1, N_DEV):
            acc = acc + buf[s]
        o_ref[...] = acc
        for cp in sends:
            cp.wait_send()

    vm = pl.BlockSpec(memory_space=pltpu.VMEM)
    return pl.pallas_call(
        body, name="allreduce_small", in_specs=[vm], out_specs=vm, out_shape=_sds((R, C), F32),
        scratch_shapes=[pltpu.VMEM((N_DEV, R, C), F32), pltpu.SemaphoreType.DMA((7,)), pltpu.SemaphoreType.DMA((7,))],
    )(v)


WEIGHTS = ["ffn1_w_gate", "ffn1_w_up", "ffn1_w_down", "ln1_g", "ln1_b", "w_in", "rel_bias", "sb_out_g", "ca_out_g",
           "w_out", "ln2_g", "ln2_b", "ffn2_w_gate", "ffn2_w_up", "ffn2_w_down", "ln3_g", "ln3_b"]
BIG = ["ffn1_w_gate", "ffn1_w_up", "ffn2_w_gate", "ffn2_w_up", "w_in", "ffn1_w_down", "ffn2_w_down", "w_out"]
TRANSPOSED = BIG[:5]
SMALL = [n for n in WEIGHTS if n not in BIG]


def _pack(vals):
    flat = jnp.concatenate([v.reshape(-1).astype(F32) for v in vals])
    pad = -flat.shape[0] % (8 * 128)
    return jnp.pad(flat, (0, pad)).reshape(-1, 128)


def _unpack(packed, like):
    flat = packed.reshape(-1)
    out, off = [], 0
    for v in like:
        out.append(flat[off:off + v.size].reshape(v.shape))
        off += v.size
    return out


def _row_blocks(w, n):
    r = w.shape[0] // n
    return [w[i * r:(i + 1) * r] for i in range(n)]


class _Arrivals:
    def __init__(self):
        self.by_kernel = {}

    def ride(self, kernel_name, weights, names, local, layer):
        self.by_kernel[kernel_name] = (weights, names, _Gather([local[n][layer] for n in names]))

    def plan(self, kernel_name):
        return self.by_kernel[kernel_name][2] if kernel_name in self.by_kernel else None

    def landed(self, kernel_name, outs):
        if kernel_name in self.by_kernel:
            weights, names, _ = self.by_kernel[kernel_name]
            weights.update(zip(names, outs))


def _layer_fwd(x, xb, W, P, bias, l, B, S, arrivals):
    T = B * S
    sv = {"xb": xb, "bias": bias}
    (sv["sg1"], sv["t1"], sv["a1"]), got = ffn_up(xb, W["ffn1_w_gate"], W["ffn1_w_up"], carry=arrivals.plan("ffn_up"))
    arrivals.landed("ffn_up", got)
    sv["u1"], x1, sv["x1b"] = mm_nn_ln([(sv["a1"], W["ffn1_w_down"])], x, P["ln1_g"][l], P["ln1_b"][l], FFN_RESIDUAL,
                                       "ffn_down_ln")

    sv["h"] = proj_in(sv["x1b"], W["w_in"], B, S)
    sv["gA"] = P["sb_out_g"][l].reshape(1, -1)
    sv["gB"] = P["ca_out_g"][l].reshape(1, -1)
    (sv["oa"], ma, sv["cars"]), got = sb_fwd(sv["h"], sv["gA"], carry=arrivals.plan("sb_fwd"))
    arrivals.landed("sb_fwd", got)
    (sv["ob"], mb), got = ca_fwd(sv["h"], bias, sv["gB"], carry=arrivals.plan("ca_fwd"))
    arrivals.landed("ca_fwd", got)
    sv["ma"], sv["mb"] = ma.reshape(T, -1), mb.reshape(T, -1)
    sv["u2"], x2, sv["x2b"] = mm_nn_ln(list(zip([sv["ma"], sv["mb"]], _row_blocks(W["w_out"], 2))), x1,
                                       P["ln2_g"][l], P["ln2_b"][l], 1.0, "proj_out_ln")

    (sv["sg2"], sv["t2"], sv["a2"]), _ = ffn_up(sv["x2b"], W["ffn2_w_gate"], W["ffn2_w_up"])
    sv["u3"], x3, x3b = mm_nn_ln([(sv["a2"], W["ffn2_w_down"])], x2, P["ln3_g"][l], P["ln3_b"][l], FFN_RESIDUAL,
                                 "ffn_down_ln")
    return x3, x3b, sv


class _Riders:
    def __init__(self):
        self.received = []

    def plan(self, entries):
        return _Exchange([e[2] for e in entries]) if entries else None

    def landed(self, entries, slots):
        for (names, layer, _), r in zip(entries, slots or []):
            for j, name in enumerate(names):
                self.received.append((name, layer, r, j if r.ndim == 4 else None))


def _ffn_bwd(dr, dxa, xb, sg, t, a, wg, wu, wd, riders, ride, below):
    dhg, dhu = ffn_bwd_mid(dr, wd, sg, t)
    pairs = [(dhg, wg), (dhu, wu)]
    if below is None:
        down = mm_nn(pairs, add=dxa, name="ffn_dx")
    else:
        down = mm_nn_ln_bwd(pairs, dxa, *below, name="ffn_dx_ln")
    grads, slots = ffn_dw(dhg, dhu, a, xb, dr, carry=riders.plan(ride))
    riders.landed(ride, slots)
    return down, grads


def _layer_bwd(dxa, dr, sv, W, P, l, B, S, riders, pending, below):
    T = B * S
    G = {}
    (dxa, dr, dg, db), g_ffn2 = _ffn_bwd(
        dr, dxa, sv["x2b"], sv["sg2"], sv["t2"], sv["a2"], W["ffn2_w_gate"], W["ffn2_w_up"], W["ffn2_w_down"],
        riders, pending, (sv["u2"], P["ln2_g"][l], 1.0))
    G["ln2_g"], G["ln2_b"] = dg.sum(0), db.sum(0)
    dmix = mm_nt(dr, W["w_out"], name="proj_out_dx").reshape(B, S, -1)
    g_out = mm_tn([sv["ma"], sv["mb"]], dr, name="proj_out_dw")[0]
    ride = [(("ffn2_w_gate", "ffn2_w_up", "ffn2_w_down"), l, g_ffn2)]
    (dqa, dka, dva, dga), slots = sb_bwd(sv["h"], sv["oa"], dmix, sv["gA"], sv["cars"], carry=riders.plan(ride))
    riders.landed(ride, slots)
    (dqb, dkb, dvb, dgb, dbias), _ = ca_bwd(sv["h"], sv["bias"], sv["ob"], dmix, sv["gB"])
    G["sb_out_g"] = dga.sum((0, 1, 2))
    G["ca_out_g"] = dgb.sum((0, 1, 2))
    G["rel_bias"] = rel_bias_grad(dbias)
    dh = [t.reshape(T, -1) for t in (dqa, dka, dva, dqb, dkb, dvb)]
    dxa, dr, dg, db = mm_nn_ln_bwd(list(zip(dh, _row_blocks(W["w_in"], 6))), dxa, sv["u1"], P["ln1_g"][l], FFN_RESIDUAL,
                                   name="proj_in_dx_ln")
    G["ln1_g"], G["ln1_b"] = dg.sum(0), db.sum(0)
    g_in = mm_tn(dh, sv["x1b"], name="proj_in_dw")[0]

    down, g_ffn1 = _ffn_bwd(dr, dxa, sv["xb"], sv["sg1"], sv["t1"], sv["a1"], W["ffn1_w_gate"], W["ffn1_w_up"],
                            W["ffn1_w_down"], riders, [(("w_in",), l, g_in), (("w_out",), l, g_out)], below)
    return down, G, [(("ffn1_w_gate", "ffn1_w_up", "ffn1_w_down"), l, g_ffn1)]


def kernel(x, ffn1_w_gate, ffn1_w_up, ffn1_w_down, ln1_g, ln1_b, w_in, rel_bias, sb_out_g, ca_out_g, w_out, ln2_g, ln2_b, ffn2_w_gate, ffn2_w_up, ffn2_w_down, ln3_g, ln3_b, loss_target, m_ffn1_w_gate, m_ffn1_w_up, m_ffn1_w_down, m_ln1_g, m_ln1_b, m_w_in, m_rel_bias, m_sb_out_g, m_ca_out_g, m_w_out, m_ln2_g, m_ln2_b, m_ffn2_w_gate, m_ffn2_w_up, m_ffn2_w_down, m_ln3_g, m_ln3_b, v_ffn1_w_gate, v_ffn1_w_up, v_ffn1_w_down, v_ln1_g, v_ln1_b, v_w_in, v_rel_bias, v_sb_out_g, v_ca_out_g, v_w_out, v_ln2_g, v_ln2_b, v_ffn2_w_gate, v_ffn2_w_up, v_ffn2_w_down, v_ln3_g, v_ln3_b):
    given = dict(locals())
    P = {n: given[n] for n in WEIGHTS}
    M = {n: given["m_" + n] for n in WEIGHTS}
    V = {n: given["v_" + n] for n in WEIGHTS}
    B, S, D = x.shape
    L = ln1_g.shape[0]

    local = {n: (jnp.swapaxes(P[n], 1, 2) if n in TRANSPOSED else P[n]).astype(BF16) for n in BIG}
    bias = ca_bias(rel_bias)

    xf = x.reshape(B * S, D)
    xb = xf.astype(BF16)
    W, saved = [{} for _ in range(L)], []
    first = ["ffn1_w_gate", "ffn1_w_up"]
    W[0].update(zip(first, _run_plan(_Gather([local[n][0] for n in first]), "gather_weights")))
    for l in range(L):
        arrivals = _Arrivals()
        if l == 0:
            arrivals.ride("ffn_up", W[0], ["ffn1_w_down", "w_in", "w_out"], local, 0)
            arrivals.ride("ca_fwd", W[0], ["ffn2_w_gate", "ffn2_w_up", "ffn2_w_down"], local, 0)
            if L > 1:
                arrivals.ride("sb_fwd", W[1], BIG, local, 1)
        elif l + 1 < L:
            arrivals.ride("sb_fwd", W[l + 1], BIG[:5], local, l + 1)
            arrivals.ride("ca_fwd", W[l + 1], BIG[5:], local, l + 1)
        xf, xb, sv = _layer_fwd(xf, xb, W[l], P, bias[l], l, B, S, arrivals)
        saved.append(sv)
    dy, loss_part = loss_head(xf, loss_target.reshape(B * S, D))

    big_g = {n: [None] * L for n in BIG}
    small_g = {n: [None] * L for n in SMALL}
    riders, pending = _Riders(), []
    down = ln_bwd(saved[L - 1]["u3"], dy, ln3_g[L - 1], FFN_RESIDUAL)
    for l in reversed(range(L)):
        dxa, dr, dg, db = down
        small_g["ln3_g"][l], small_g["ln3_b"][l] = dg.sum(0), db.sum(0)
        below = (saved[l - 1]["u3"], ln3_g[l - 1], FFN_RESIDUAL) if l > 0 else None
        down, G, pending = _layer_bwd(dxa, dr, saved[l], W[l], P, l, B, S, riders, pending, below)
        for n, g in G.items():
            small_g[n][l] = g
    dx = down
    riders.landed(pending, _run_plan(riders.plan(pending), "exchange_grads"))
    for n, l, slots, j in riders.received:
        big_g[n][l] = sum_slots(slots, j)

    grads = {}
    for n in BIG:
        g = jnp.stack(big_g[n])
        grads[n] = jnp.swapaxes(g, 1, 2) if n in TRANSPOSED else g
    small_like = [P[n] for n in SMALL]
    packed = _pack([jnp.stack(small_g[n]) for n in SMALL] + [loss_part.sum()])
    total = allreduce_small(packed)
    *small_vals, loss = _unpack(total, small_like + [jnp.zeros((), F32)])
    grads.update(dict(zip(SMALL, small_vals)))

    delta, new_m, new_v = {}, {}, {}
    for n in BIG:
        shape = P[n].shape
        two_d = lambda a: a.reshape(shape[0] * shape[1], shape[2])
        d, m, v = adamw(two_d(P[n]), two_d(grads[n]), two_d(M[n]), two_d(V[n]))
        delta[n], new_m[n], new_v[n] = d.reshape(shape), m.reshape(shape), v.reshape(shape)
    one = jnp.ones((), F32)
    d, m, v = adamw(_pack(small_like + [one]), total, _pack([M[n] for n in SMALL] + [one]), _pack([V[n] for n in SMALL] + [one]))
    for dst, src in ((delta, d), (new_m, m), (new_v, v)):
        dst.update(dict(zip(SMALL, _unpack(src, small_like))))

    return (loss, dx.reshape(B, S, D), *[grads[n] for n in WEIGHTS], *[delta[n] for n in WEIGHTS],
            *[new_m[n] for n in WEIGHTS], *[new_v[n] for n in WEIGHTS])
```

```python
import math

import jax
import jax.numpy as jnp
import numpy as np
from jax import lax
from jax.experimental import pallas as pl
from jax.experimental.pallas import tpu as pltpu

F32 = jnp.float32
BF16 = jnp.bfloat16

HEAD_DIM = 64
CHUNK = 64
N_PREV_CHUNKS = 8
MAX_REL = 128
DEPTH = 4
FFN_RESIDUAL = 0.5
ALPHA = (2 * DEPTH) ** 0.25
LN_EPS = 1e-5
RMS_EPS = 1e-6
ADAM_LR = 0.001
ADAM_B1 = 0.9
ADAM_B2 = 0.999
ADAM_EPS = 1e-08
ADAM_WD = 0.01
ADAM_STEP = 10

N_DEV = 8
LANES = 128
VMEM_LIMIT_BYTES = 56 * 1024 * 1024
ROW_TILE = 512
SB_BLOCK = 128
CA_PAD = CHUNK * N_PREV_CHUNKS
CA_TQ = 2 * CHUNK
CA_WIN = CA_PAD + CA_TQ
FWD_HEADS = 8
BWD_HEADS = 8
NEG = -1e30
SB_SKIP_LOG = -104.0

NN = ((1,), (0,))
NT = ((1,), (1,))
TN = ((0,), (0,))


def _dot(a, b, dims):
    return lax.dot_general(a, b, (dims, ((), ())), preferred_element_type=F32)


def _params(*sem):
    return pltpu.CompilerParams(dimension_semantics=sem, vmem_limit_bytes=VMEM_LIMIT_BYTES)


def _sds(shape, dtype):
    return jax.ShapeDtypeStruct(shape, dtype)


def _row_tile(n, want=ROW_TILE):
    t = min(want, n)
    while n % t:
        t //= 2
    assert t >= 8, (n, want)
    return t


def ffn_up(xb, wg, wu, carry=None):
    T, D = xb.shape
    F = wg.shape[0]
    tm = _row_tile(T)
    fc = _row_tile(F, 256)

    def body(x_ref, wg_ref, wu_ref, sg_ref, t_ref, a_ref):
        x = x_ref[...]
        for j in range(F // fc):
            sl = pl.ds(j * fc, fc)
            hg = _dot(x, wg_ref[sl, :], NT)
            hu = _dot(x, wu_ref[sl, :], NT)
            s = jax.nn.sigmoid(hg)
            sg = hg * s
            a = sg * hu
            sg_ref[:, sl] = sg.astype(BF16)
            t_ref[:, sl] = (a + s * (hu - a)).astype(BF16)
            a_ref[:, sl] = a.astype(BF16)

    row = pl.BlockSpec((tm, F), lambda i: (i, 0))
    w = pl.BlockSpec((F, D), lambda i: (0, 0))
    return _call(body, carry, grid=(T // tm,), name="ffn_up", sem=("parallel",), args=(xb, wg, wu),
                 in_specs=[pl.BlockSpec((tm, D), lambda i: (i, 0)), w, w],
                 out_specs=[row, row, row], out_shape=[_sds((T, F), BF16)] * 3)


def ffn_bwd_mid(dr, wd, da_dhu, da_dhg):
    T, D = dr.shape
    F = wd.shape[0]
    tm = _row_tile(T)
    fc = _row_tile(F, 256)

    def body(dr_ref, wd_ref, sg_ref, t_ref, dhg_ref, dhu_ref):
        dr_ = dr_ref[...]
        for j in range(F // fc):
            sl = pl.ds(j * fc, fc)
            da = _dot(dr_, wd_ref[sl, :], NT)
            dhu_ref[:, sl] = (da * sg_ref[:, sl].astype(F32)).astype(BF16)
            dhg_ref[:, sl] = (da * t_ref[:, sl].astype(F32)).astype(BF16)

    row = pl.BlockSpec((tm, F), lambda i: (i, 0))
    return pl.pallas_call(
        body, grid=(T // tm,), name="ffn_bwd_mid",
        in_specs=[pl.BlockSpec((tm, D), lambda i: (i, 0)), pl.BlockSpec((F, D), lambda i: (0, 0)), row, row],
        out_specs=[row, row], out_shape=[_sds((T, F), BF16)] * 2,
        compiler_params=_params("parallel"),
    )(dr, wd, da_dhu, da_dhg)


def mm_nn(pairs, add=None, out_dtype=F32, name="mm_nn"):
    M = pairs[0][0].shape[0]
    N = pairs[0][1].shape[1]
    tm = _row_tile(M)
    nc = _row_tile(N, 512)
    n_pairs = len(pairs)

    def body(*refs):
        a_refs = refs[:n_pairs]
        b_refs = refs[n_pairs:2 * n_pairs]
        add_ref = refs[2 * n_pairs] if add is not None else None
        o_ref = refs[-1]
        for j in range(N // nc):
            sl = pl.ds(j * nc, nc)
            acc = _dot(a_refs[0][...], b_refs[0][:, sl], NN)
            for a_ref, b_ref in zip(a_refs[1:], b_refs[1:]):
                acc = acc + _dot(a_ref[...], b_ref[:, sl], NN)
            if add_ref is not None:
                acc = acc + add_ref[:, sl]
            o_ref[:, sl] = acc.astype(out_dtype)

    in_specs = [pl.BlockSpec((tm, a.shape[1]), lambda i: (i, 0)) for a, _ in pairs]
    in_specs += [pl.BlockSpec(b.shape, lambda i: (0, 0)) for _, b in pairs]
    args = [a for a, _ in pairs] + [b for _, b in pairs]
    if add is not None:
        in_specs.append(pl.BlockSpec((tm, N), lambda i: (i, 0)))
        args.append(add)
    return pl.pallas_call(
        body, grid=(M // tm,), name=name, in_specs=in_specs,
        out_specs=pl.BlockSpec((tm, N), lambda i: (i, 0)), out_shape=_sds((M, N), out_dtype),
        compiler_params=_params("parallel"),
    )(*args)


def _mm_specs(pairs, tm):
    specs = [pl.BlockSpec((tm, a.shape[1]), lambda i: (i, 0)) for a, _ in pairs]
    specs += [pl.BlockSpec(b.shape, lambda i: (0, 0), pipeline_mode=pl.Buffered(1)) for _, b in pairs]
    return specs, [a for a, _ in pairs] + [b for _, b in pairs]


def _mm_sum(a_refs, b_refs):
    acc = _dot(a_refs[0][...], b_refs[0][...], NN)
    for a_ref, b_ref in zip(a_refs[1:], b_refs[1:]):
        acc = acc + _dot(a_ref[...], b_ref[...], NN)
    return acc


def _ln_fwd_rows(u, g, b):
    mu = jnp.mean(u, axis=-1, keepdims=True)
    xc = u - mu
    var = jnp.mean(xc * xc, axis=-1, keepdims=True)
    return xc * lax.rsqrt(var + LN_EPS) * g + b


def _ln_bwd_rows(u, dy, g):
    mu = jnp.mean(u, axis=-1, keepdims=True)
    xc = u - mu
    var = jnp.mean(xc * xc, axis=-1, keepdims=True)
    rstd = lax.rsqrt(var + LN_EPS)
    xh = xc * rstd
    dxh = dy * g
    m1 = jnp.mean(dxh, axis=-1, keepdims=True)
    m2 = jnp.mean(dxh * xh, axis=-1, keepdims=True)
    return rstd * (dxh - m1 - xh * m2), dy * xh


def mm_nn_ln(pairs, x, g, b, res_scale, name):
    M, N = x.shape
    tm = _row_tile(M)
    n_pairs = len(pairs)

    def body(*refs):
        a_refs, b_refs = refs[:n_pairs], refs[n_pairs:2 * n_pairs]
        x_ref, g_ref, b_ref, u_ref, y_ref, yb_ref = refs[2 * n_pairs:]
        u = ALPHA * x_ref[...] + res_scale * _mm_sum(a_refs, b_refs)
        y = _ln_fwd_rows(u, g_ref[...], b_ref[...])
        u_ref[...] = u
        y_ref[...] = y
        yb_ref[...] = y.astype(BF16)

    in_specs, args = _mm_specs(pairs, tm)
    row = pl.BlockSpec((tm, N), lambda i: (i, 0))
    vec = pl.BlockSpec((1, N), lambda i: (0, 0))
    return pl.pallas_call(
        body, grid=(M // tm,), name=name, in_specs=in_specs + [row, vec, vec],
        out_specs=[row, row, row], out_shape=[_sds((M, N), F32), _sds((M, N), F32), _sds((M, N), BF16)],
        compiler_params=_params("parallel"),
    )(*args, x, g.reshape(1, N), b.reshape(1, N))


def mm_nn_ln_bwd(pairs, add, u, g, branch_scale, name):
    M, N = u.shape
    tm = _row_tile(M)
    n_pairs = len(pairs)

    def body(*refs):
        a_refs, b_refs = refs[:n_pairs], refs[n_pairs:2 * n_pairs]
        add_ref, u_ref, g_ref, dxa_ref, dr_ref, dg_ref, db_ref = refs[2 * n_pairs:]

        @pl.when(pl.program_id(0) == 0)
        def _():
            dg_ref[...] = jnp.zeros_like(dg_ref)
            db_ref[...] = jnp.zeros_like(db_ref)

        dy = _mm_sum(a_refs, b_refs) + add_ref[...]
        du, dyxh = _ln_bwd_rows(u_ref[...], dy, g_ref[...])
        dxa_ref[...] = ALPHA * du
        dr_ref[...] = (branch_scale * du).astype(BF16)
        dg_ref[...] += jnp.sum(dyxh.reshape(tm // 8, 8, N), axis=0)
        db_ref[...] += jnp.sum(dy.reshape(tm // 8, 8, N), axis=0)

    in_specs, args = _mm_specs(pairs, tm)
    row = pl.BlockSpec((tm, N), lambda i: (i, 0))
    acc = pl.BlockSpec((8, N), lambda i: (0, 0))
    return pl.pallas_call(
        body, grid=(M // tm,), name=name, in_specs=in_specs + [row, row, pl.BlockSpec((1, N), lambda i: (0, 0))],
        out_specs=[row, row, acc, acc],
        out_shape=[_sds((M, N), F32), _sds((M, N), BF16), _sds((8, N), F32), _sds((8, N), F32)],
        compiler_params=_params("arbitrary"),
    )(*args, add, u, g.reshape(1, N))


def mm_nt(a, b, out_dtype=F32, name="mm_nt"):
    M, K = a.shape
    N = b.shape[0]
    tm = _row_tile(M)
    nc = _row_tile(N, 512)

    def body(a_ref, b_ref, o_ref):
        a_ = a_ref[...]
        for j in range(N // nc):
            sl = pl.ds(j * nc, nc)
            o_ref[:, sl] = _dot(a_, b_ref[sl, :], NT).astype(out_dtype)

    return pl.pallas_call(
        body, grid=(M // tm,), name=name,
        in_specs=[pl.BlockSpec((tm, K), lambda i: (i, 0)), pl.BlockSpec((N, K), lambda i: (0, 0))],
        out_specs=pl.BlockSpec((tm, N), lambda i: (i, 0)), out_shape=_sds((M, N), out_dtype),
        compiler_params=_params("parallel"),
    )(a, b)


def proj_in(xb, w, B, S):
    D = xb.shape[1]
    N = w.shape[0]
    tm = CA_PAD
    nblk = S // tm
    nc = _row_tile(N, 512)
    assert S % tm == 0

    def body(x_ref, w_ref, o_ref):
        i = pl.program_id(1)

        @pl.when(i == 0)
        def _():
            o_ref[...] = jnp.zeros_like(o_ref)

        @pl.when(i > 0)
        def _():
            x = x_ref[...]
            for j in range(N // nc):
                sl = pl.ds(j * nc, nc)
                o_ref[0, :, sl] = _dot(x, w_ref[sl, :], NT).astype(BF16)

    return pl.pallas_call(
        body, grid=(B, nblk + 1), name="proj_in",
        in_specs=[pl.BlockSpec((tm, D), lambda b, i: (b * nblk + jnp.maximum(i - 1, 0), 0)),
                  pl.BlockSpec((N, D), lambda b, i: (0, 0))],
        out_specs=pl.BlockSpec((1, tm, N), lambda b, i: (b, i, 0)), out_shape=_sds((B, CA_PAD + S, N), BF16),
        compiler_params=_params("parallel", "arbitrary"),
    )(xb, w)


def mm_tn(parts, b, name="mm_tn", carry=None):
    T, N = b.shape
    widths = [a.shape[1] for a in parts]
    M = sum(widths)
    tk = _row_tile(T)
    steps = T // tk
    n_parts = len(parts)

    def body(*refs):
        a_refs, b_ref, o_ref, acc_ref = refs[:n_parts], refs[n_parts], refs[n_parts + 1], refs[n_parts + 2]
        i = pl.program_id(0)

        @pl.when(i == 0)
        def _():
            acc_ref[...] = jnp.zeros_like(acc_ref)

        b_ = b_ref[...]
        row = 0
        for a_ref, width in zip(a_refs, widths):
            mc = _row_tile(width, 256)
            for j in range(width // mc):
                acc_ref[pl.ds(row, mc), :] += _dot(a_ref[:, pl.ds(j * mc, mc)], b_, TN)
                row += mc

        @pl.when(i == steps - 1)
        def _():
            o_ref[...] = acc_ref[...].astype(BF16)

    (out,), got = _call(
        body, carry, grid=(steps,), name=name, sem=("arbitrary",), args=(*parts, b),
        in_specs=[pl.BlockSpec((tk, w), lambda i: (i, 0)) for w in widths] + [pl.BlockSpec((tk, N), lambda i: (i, 0))],
        out_specs=[pl.BlockSpec((M, N), lambda i: (0, 0))], out_shape=[_sds((M, N), BF16)],
        scratch_shapes=[pltpu.VMEM((M, N), F32)])
    return out, got


def ffn_dw(dhg, dhu, a, xb, dr, carry=None):
    T, F = dhg.shape
    D = xb.shape[1]
    tk = _row_tile(T)
    mc = _row_tile(F, 256)
    steps = T // tk
    last = steps - 1

    def body(dhg_ref, dhu_ref, a_ref, xb_ref, dr_ref, o_ref, acc_ref):
        p = pl.program_id(0)
        i = pl.program_id(1)

        @pl.when(i == 0)
        def _():
            acc_ref[...] = jnp.zeros_like(acc_ref)

        def accumulate(lhs_ref, rhs_ref):
            rhs = rhs_ref[...]
            for j in range(F // mc):
                sl = pl.ds(j * mc, mc)
                acc_ref[sl, :] += _dot(lhs_ref[:, sl], rhs, TN)

        pl.when(p == 0)(lambda: accumulate(dhg_ref, xb_ref))
        pl.when(p == 1)(lambda: accumulate(dhu_ref, xb_ref))
        pl.when(p == 2)(lambda: accumulate(a_ref, dr_ref))

        @pl.when(i == last)
        def _():
            o_ref[0] = acc_ref[...].astype(BF16)

    def during(phases, width):
        lo, hi = phases
        return pl.BlockSpec((tk, width), lambda p, i: (jnp.where(p < lo, 0, jnp.where(p > hi, last, i)), 0))

    (out,), got = _call(
        body, carry, grid=(3, steps), name="ffn_dw", sem=("arbitrary", "arbitrary"), args=(dhg, dhu, a, xb, dr),
        in_specs=[during((0, 0), F), during((1, 1), F), during((2, 2), F), during((0, 1), D), during((2, 2), D)],
        out_specs=[pl.BlockSpec((1, F, D), lambda p, i: (p, 0, 0))], out_shape=[_sds((3, F, D), BF16)],
        scratch_shapes=[pltpu.VMEM((F, D), F32)])
    return out, got


def ln_bwd(u, dy, g, branch_scale):
    T, D = u.shape
    tm = _row_tile(T)

    def body(u_ref, dy_ref, g_ref, dxa_ref, dr_ref, dg_ref, db_ref):
        @pl.when(pl.program_id(0) == 0)
        def _():
            dg_ref[...] = jnp.zeros_like(dg_ref)
            db_ref[...] = jnp.zeros_like(db_ref)

        dy_ = dy_ref[...]
        du, dyxh = _ln_bwd_rows(u_ref[...], dy_, g_ref[...])
        dxa_ref[...] = ALPHA * du
        dr_ref[...] = (branch_scale * du).astype(BF16)
        dg_ref[...] += jnp.sum(dyxh.reshape(tm // 8, 8, D), axis=0)
        db_ref[...] += jnp.sum(dy_.reshape(tm // 8, 8, D), axis=0)

    row = pl.BlockSpec((tm, D), lambda i: (i, 0))
    acc = pl.BlockSpec((8, D), lambda i: (0, 0))
    dxa, dr, dg, db = pl.pallas_call(
        body, grid=(T // tm,), name="ln_bwd", in_specs=[row, row, pl.BlockSpec((1, D), lambda i: (0, 0))],
        out_specs=[row, row, acc, acc],
        out_shape=[_sds((T, D), F32), _sds((T, D), BF16), _sds((8, D), F32), _sds((8, D), F32)],
        compiler_params=_params("arbitrary"),
    )(u, dy, g.reshape(1, D))
    return dxa, dr, dg, db


def loss_head(y, target):
    T, D = y.shape
    tm = _row_tile(T)

    def body(y_ref, t_ref, dy_ref, l_ref):
        @pl.when(pl.program_id(0) == 0)
        def _():
            l_ref[...] = jnp.zeros_like(l_ref)

        e = y_ref[...] - t_ref[...]
        dy_ref[...] = e * (1.0 / D)
        l_ref[...] += jnp.sum((e * e).reshape(tm // 8, 8, D), axis=0) * (0.5 / D)

    row = pl.BlockSpec((tm, D), lambda i: (i, 0))
    return pl.pallas_call(
        body, grid=(T // tm,), name="loss_head", in_specs=[row, row],
        out_specs=[row, pl.BlockSpec((8, D), lambda i: (0, 0))],
        out_shape=[_sds((T, D), F32), _sds((8, D), F32)],
        compiler_params=_params("arbitrary"),
    )(y, target)


def adamw(w, g, m, v):
    R, C = w.shape
    tr = R
    for cand in (512, 256, 128, 64, 32, 16, 8):
        if R % cand == 0:
            tr = cand
            break
    c1 = 1.0 - ADAM_B1 ** ADAM_STEP
    c2 = 1.0 - ADAM_B2 ** ADAM_STEP

    def body(w_ref, g_ref, m_ref, v_ref, d_ref, mo_ref, vo_ref):
        g_ = g_ref[...]
        m_ = ADAM_B1 * m_ref[...] + (1.0 - ADAM_B1) * g_
        v_ = ADAM_B2 * v_ref[...] + (1.0 - ADAM_B2) * (g_ * g_)
        m_hat = m_ / c1
        v_hat = v_ / c2
        d_ref[...] = -ADAM_LR * (m_hat / (jnp.sqrt(v_hat) + ADAM_EPS) + ADAM_WD * w_ref[...])
        mo_ref[...] = m_
        vo_ref[...] = v_

    blk = pl.BlockSpec((tr, C), lambda i: (i, 0))
    return pl.pallas_call(
        body, grid=(R // tr,), name="adamw", in_specs=[blk] * 4, out_specs=[blk] * 3,
        out_shape=[_sds((R, C), F32)] * 3, compiler_params=_params("parallel"),
    )(w, g, m, v)


def _lane_lo(rows):
    return lax.broadcasted_iota(jnp.int32, (rows, LANES), 1) < HEAD_DIM


def _pair_mean(x, lo):
    s0 = jnp.sum(jnp.where(lo, x, 0.0), axis=-1, keepdims=True)
    s1 = jnp.sum(jnp.where(lo, 0.0, x), axis=-1, keepdims=True)
    return jnp.where(lo, s0, s1) * (1.0 / HEAD_DIM)


def _rms_fwd(o, gain, lo):
    r = lax.rsqrt(_pair_mean(o * o, lo) + RMS_EPS)
    return (o * r * gain).astype(BF16)


def _rms_bwd(o, dm, gain, lo):
    r = lax.rsqrt(_pair_mean(o * o, lo) + RMS_EPS)
    oh = o * r
    dg = jnp.sum(dm * oh, axis=0, keepdims=True)
    doh = dm * gain
    do = r * (doh - oh * _pair_mean(doh * oh, lo))
    return do, dg


def _split_heads(x, lo):
    zero = jnp.zeros_like(x)
    return [jnp.where(lo, x, zero), jnp.where(lo, zero, x)]


def _merge_pairs(per_head, lo):
    return [jnp.where(lo, per_head[2 * p], per_head[2 * p + 1]) for p in range(len(per_head) // 2)]


def _pair_cols(h):
    return slice((h // 2) * LANES, (h // 2 + 1) * LANES)


def _split_dot(x, tri):
    n = x.shape[0]
    hi = x.astype(BF16)
    lo = (x - hi.astype(F32)).astype(BF16)
    both = _dot(jnp.concatenate([hi, lo], axis=0), tri, NN)
    return both[:n] + both[n:]


def _log_keep(z):
    return -(jnp.maximum(z, 0.0) + jnp.log(1.0 + jnp.exp(-jnp.abs(z))))


def _attn_dims(h3p, gain, heads):
    B, SP, C = h3p.shape
    HD = gain.shape[1]
    hs = min(heads, HD // HEAD_DIM)
    W = hs * HEAD_DIM
    assert C == 6 * HD and W % LANES == 0 and HD % W == 0
    return B, SP - CA_PAD, HD, hs, W, HD // W


def sb_fwd(h3p, gain, carry=None):
    B, S, HD, hs, W, ngrp = _attn_dims(h3p, gain, FWD_HEADS)
    tb = min(SB_BLOCK, S)
    nq = S // tb
    off = CA_PAD // tb
    scale = 1.0 / math.sqrt(HEAD_DIM)

    def body(q_ref, k_ref, v_ref, g_ref, o_ref, m_ref, c_ref):
        qi = pl.program_id(2)
        lo = _lane_lo(tb)
        row = lax.broadcasted_iota(jnp.int32, (tb, tb), 0)
        col = lax.broadcasted_iota(jnp.int32, (tb, tb), 1)
        rev_incl = (row >= col).astype(BF16)
        lane = lax.broadcasted_iota(jnp.int32, (tb, nq), 1)
        qm = []
        for p in range(hs // 2):
            qm += _split_heads((q_ref[0, :, p * LANES:(p + 1) * LANES].astype(F32) * scale).astype(BF16), lo)

        def blocks(kb, carries_, mask):
            ks = pl.ds(pl.multiple_of(kb * tb + CA_PAD, tb), tb)
            hh = range(hs)
            zs = [_dot(qm[h], k_ref[0, ks, _pair_cols(h)], NT) for h in hh]
            lks = [_log_keep(z) for z in zs]
            if mask is not None:
                lks = [jnp.where(mask, lk, 0.0) for lk in lks]
            cums = [_split_dot(lk, rev_incl) for lk in lks]
            ws = []
            for h in hh:
                logw = zs[h] + cums[h] + carries_[h]
                if mask is not None:
                    logw = jnp.where(mask, logw, NEG)
                ws.append(jnp.exp(logw).astype(BF16))
            pvs = [_dot(ws[h], v_ref[0, ks, _pair_cols(h)], NN) for h in hh]
            return [(pvs[h], cums[h][:, 0:1]) for h in hh]

        diag = blocks(qi, [jnp.zeros((tb, 1), F32)] * hs, col < row)
        accs = _merge_pairs([d[0] for d in diag], lo)
        carries = [d[1] for d in diag]
        cars = [jnp.where(lane == qi, 0.0, NEG)] * hs

        def cond(st):
            kb, carries_, _, _ = st
            top = carries_[0]
            for c in carries_[1:]:
                top = jnp.maximum(top, c)
            return jnp.logical_and(kb >= 0, jnp.max(top) > SB_SKIP_LOG)

        def step(st):
            kb, carries_, accs_, cars_ = st
            out = blocks(kb, carries_, None)
            pv = _merge_pairs([o[0] for o in out], lo)
            return (kb - 1, [c + o[1] for c, o in zip(carries_, out)], [a + p for a, p in zip(accs_, pv)],
                    [jnp.where(lane == kb, c, cs) for c, cs in zip(carries_, cars_)])

        _, _, accs, cars = lax.while_loop(cond, step, (qi - 1, carries, accs, cars))
        for p, acc in enumerate(accs):
            cols = slice(p * LANES, (p + 1) * LANES)
            o_ref[0, :, cols] = acc
            m_ref[0, :, cols] = _rms_fwd(acc, g_ref[:, cols], lo)
        for h in range(hs):
            c_ref[0, h] = cars[h]

    qspec = pl.BlockSpec((1, tb, W), lambda g, b, i: (b, i + off, g))
    ospec = pl.BlockSpec((1, tb, W), lambda g, b, i: (b, i, g))
    return _call(
        body, carry, grid=(ngrp, B, nq), name="sb_fwd", sem=("parallel", "parallel", "arbitrary"),
        args=(h3p, h3p, h3p, gain),
        in_specs=[qspec, pl.BlockSpec((1, CA_PAD + S, W), lambda g, b, i: (b, 0, ngrp + g)),
                  pl.BlockSpec((1, CA_PAD + S, W), lambda g, b, i: (b, 0, 2 * ngrp + g)),
                  pl.BlockSpec((1, W), lambda g, b, i: (0, g))],
        out_specs=[ospec, ospec, pl.BlockSpec((1, hs, tb, nq), lambda g, b, i: (b, g, i, 0))],
        out_shape=[_sds((B, S, HD), F32), _sds((B, S, HD), BF16), _sds((B, HD // HEAD_DIM, S, nq), F32)])


def sb_bwd(h3p, o, dmix, gain, cars, carry=None):
    B, S, HD, hs, W, ngrp = _attn_dims(h3p, gain, BWD_HEADS)
    tb = min(SB_BLOCK, S)
    nq = S // tb
    off = CA_PAD // tb
    scale = 1.0 / math.sqrt(HEAD_DIM)

    def body(q_ref, k_ref, v_ref, o_ref, dm_ref, g_ref, c_ref, dq_ref, dk_ref, dv_ref, dg_ref, dk_acc, dv_acc):
        qi = pl.program_id(2)

        @pl.when(qi == 0)
        def _():
            dk_acc[...] = jnp.zeros_like(dk_acc)
            dv_acc[...] = jnp.zeros_like(dv_acc)

        lo = _lane_lo(tb)
        row = lax.broadcasted_iota(jnp.int32, (tb, tb), 0)
        col = lax.broadcasted_iota(jnp.int32, (tb, tb), 1)
        rev_incl = (row >= col).astype(BF16)
        fwd_incl = (row <= col).astype(BF16)
        lane = lax.broadcasted_iota(jnp.int32, (tb, nq), 1)
        below = lax.broadcasted_iota(jnp.int32, (1, nq), 1) < qi
        qm, dom, cars_, seen = [], [], [], None
        for p in range(hs // 2):
            cols = slice(p * LANES, (p + 1) * LANES)
            qm += _split_heads((q_ref[0, :, cols].astype(F32) * scale).astype(BF16), lo)
            do, dg = _rms_bwd(o_ref[0, :, cols], dm_ref[0, :, cols], g_ref[:, cols], lo)
            dg_ref[0, 0, :, cols] = dg
            dom += _split_heads(do.astype(BF16), lo)
        for h in range(hs):
            cars_.append(c_ref[0, h])
            visited = jnp.logical_and(jnp.max(cars_[h], axis=0, keepdims=True) > SB_SKIP_LOG, below)
            n = jnp.sum(visited.astype(jnp.int32), axis=1, keepdims=True)
            seen = n if seen is None else jnp.maximum(seen, n)
        first = qi - jnp.max(seen)

        def blocks(kb, gsums, dqs, mask):
            ks = pl.ds(pl.multiple_of(kb * tb + CA_PAD, tb), tb)
            ko = pl.ds(pl.multiple_of(kb * tb, tb), tb)
            hh = range(hs)
            kk = [k_ref[0, ks, p * LANES:(p + 1) * LANES] for p in range(hs // 2)]
            vv = [v_ref[0, ks, p * LANES:(p + 1) * LANES] for p in range(hs // 2)]
            zs = [_dot(qm[h], kk[h // 2], NT) for h in hh]
            dws = [_dot(dom[h], vv[h // 2], NT) for h in hh]
            raw = [_log_keep(z) for z in zs]
            lks = raw if mask is None else [jnp.where(mask, lk, 0.0) for lk in raw]
            cums = [_split_dot(lk, rev_incl) for lk in lks]
            ws = []
            for h in hh:
                carry = jnp.sum(jnp.where(lane == kb, cars_[h], 0.0), axis=1, keepdims=True)
                logw = zs[h] + cums[h] + carry
                if mask is not None:
                    logw = jnp.where(mask, logw, NEG)
                ws.append(jnp.exp(logw))
            gws = [ws[h] * dws[h] for h in hh]
            gcums = [_split_dot(gws[h], fwd_incl) + gsums[h] for h in hh]
            dzb = []
            for h in hh:
                dz = gws[h] - jnp.exp(zs[h] + raw[h]) * gcums[h]
                if mask is not None:
                    dz = jnp.where(mask, dz, 0.0)
                dzb.append(dz.astype(BF16))
            wb = [w.astype(BF16) for w in ws]
            new_dq = [dqs[h] + _dot(dzb[h], kk[h // 2], NN) for h in hh]
            for p in range(hs // 2):
                cols = slice(p * LANES, (p + 1) * LANES)
                dk_acc[ko, cols] += _dot(dzb[2 * p], qm[2 * p], TN) + _dot(dzb[2 * p + 1], qm[2 * p + 1], TN)
                dv_acc[ko, cols] += _dot(wb[2 * p], dom[2 * p], TN) + _dot(wb[2 * p + 1], dom[2 * p + 1], TN)
            return [g[:, tb - 1:tb] for g in gcums], new_dq

        def step(kb, st):
            return blocks(kb, st[0], st[1], None)

        init = ([jnp.zeros((tb, 1), F32)] * hs, [jnp.zeros((tb, LANES), F32)] * hs)
        gsum, dq = lax.fori_loop(first, qi, step, init)
        _, dq = blocks(qi, gsum, dq, col < row)
        for p, d in enumerate(_merge_pairs(dq, lo)):
            dq_ref[0, :, p * LANES:(p + 1) * LANES] = (d * scale).astype(BF16)

        @pl.when(qi == nq - 1)
        def _():
            dk_ref[0] = dk_acc[...].astype(BF16)
            dv_ref[0] = dv_acc[...].astype(BF16)

    once = pl.Buffered(1)
    qspec = pl.BlockSpec((1, tb, W), lambda g, b, i: (b, i + off, g))
    ospec = pl.BlockSpec((1, tb, W), lambda g, b, i: (b, i, g))
    kvout = pl.BlockSpec((1, S, W), lambda g, b, i: (b, 0, g), pipeline_mode=once)
    return _call(
        body, carry, grid=(ngrp, B, nq), name="sb_bwd", sem=("parallel", "parallel", "arbitrary"),
        args=(h3p, h3p, h3p, o, dmix, gain, cars),
        in_specs=[qspec, pl.BlockSpec((1, CA_PAD + S, W), lambda g, b, i: (b, 0, ngrp + g), pipeline_mode=once),
                  pl.BlockSpec((1, CA_PAD + S, W), lambda g, b, i: (b, 0, 2 * ngrp + g), pipeline_mode=once),
                  ospec, ospec, pl.BlockSpec((1, W), lambda g, b, i: (0, g)),
                  pl.BlockSpec((1, hs, tb, nq), lambda g, b, i: (b, g, i, 0))],
        out_specs=[ospec, kvout, kvout, pl.BlockSpec((1, 1, 1, W), lambda g, b, i: (b, i, 0, g))],
        out_shape=[_sds((B, S, HD), BF16), _sds((B, S, HD), BF16), _sds((B, S, HD), BF16), _sds((B, nq, 1, HD), F32)],
        scratch_shapes=[pltpu.VMEM((S, W), F32), pltpu.VMEM((S, W), F32)])


def _ca_rel_index():
    width = CA_WIN + CA_TQ
    c = np.arange(width)
    dj = np.where(c < CA_WIN, c, c - width)
    return np.clip(CA_PAD - dj, -MAX_REL, MAX_REL) + MAX_REL, width


def _ca_onehot():
    idx, _ = _ca_rel_index()
    return (idx[:, None] == np.arange(2 * MAX_REL + 1)[None, :]).astype(np.float32)


def ca_bias(rel_bias):
    _, width = _ca_rel_index()
    lead = rel_bias.shape[:-1]
    by_offset = jnp.dot(rel_bias, jnp.asarray(_ca_onehot().T), precision=lax.Precision.HIGHEST)
    tile = jnp.broadcast_to(by_offset[..., None, :], lead + (CA_TQ, width)).reshape(lead + (CA_TQ * width,))
    tile = tile[..., :CA_TQ * (width - 1)].reshape(lead + (CA_TQ, width - 1))[..., :CA_WIN]
    t = np.arange(CA_TQ)[:, None] // CHUNK * CHUNK
    j = np.arange(CA_WIN)[None, :]
    return jnp.where((j >= t) & (j < t + CA_PAD + CHUNK), tile, NEG)


def rel_bias_grad(db):
    H = db.shape[0]
    _, width = _ca_rel_index()
    x = jnp.pad(db, ((0, 0), (0, 0), (0, width - 1 - CA_WIN))).reshape(H, CA_TQ * (width - 1))
    x = jnp.pad(x, ((0, 0), (0, CA_TQ))).reshape(H, CA_TQ, width).sum(axis=1)
    return jnp.dot(x, jnp.asarray(_ca_onehot()), precision=lax.Precision.HIGHEST)


def _ca_scores(qm_h, kk, bias_h, valid):
    return jnp.where(valid, _dot(qm_h, kk, NT) + bias_h, NEG)


def _ca_softmax(s):
    e = jnp.exp(s - jnp.max(s, axis=-1, keepdims=True))
    return e * (1.0 / jnp.sum(e, axis=-1, keepdims=True))


def ca_fwd(h3p, bias, gain, carry=None):
    B, S, HD, hs, W, ngrp = _attn_dims(h3p, gain, FWD_HEADS)
    scale = 1.0 / math.sqrt(HEAD_DIM)
    off = CA_PAD // CA_TQ

    def body(q_ref, k_ref, v_ref, b_ref, g_ref, o_ref, m_ref):
        q0 = pl.program_id(2) * CA_TQ
        ks = pl.ds(pl.multiple_of(q0, CA_TQ), CA_WIN)
        lo = _lane_lo(CA_TQ)
        valid = lax.broadcasted_iota(jnp.int32, (CA_TQ, CA_WIN), 1) + q0 >= CA_PAD
        qm, kk, vv = [], [], []
        for p in range(hs // 2):
            cols = slice(p * LANES, (p + 1) * LANES)
            qm += _split_heads((q_ref[0, :, cols].astype(F32) * scale).astype(BF16), lo)
            kk.append(k_ref[0, ks, cols])
            vv.append(v_ref[0, ks, cols])
        ss = [_ca_scores(qm[h], kk[h // 2], b_ref[h], valid) for h in range(hs)]
        ps = [_ca_softmax(s).astype(BF16) for s in ss]
        pv = [_dot(ps[h], vv[h // 2], NN) for h in range(hs)]
        for p, o in enumerate(_merge_pairs(pv, lo)):
            cols = slice(p * LANES, (p + 1) * LANES)
            o_ref[0, :, cols] = o
            m_ref[0, :, cols] = _rms_fwd(o, g_ref[:, cols], lo)

    ospec = pl.BlockSpec((1, CA_TQ, W), lambda g, b, i: (b, i, g))
    return _call(
        body, carry, grid=(ngrp, B, S // CA_TQ), name="ca_fwd", sem=("parallel", "parallel", "arbitrary"),
        args=(h3p, h3p, h3p, bias, gain),
        in_specs=[pl.BlockSpec((1, CA_TQ, W), lambda g, b, i: (b, i + off, 3 * ngrp + g)),
                  pl.BlockSpec((1, CA_PAD + S, W), lambda g, b, i: (b, 0, 4 * ngrp + g)),
                  pl.BlockSpec((1, CA_PAD + S, W), lambda g, b, i: (b, 0, 5 * ngrp + g)),
                  pl.BlockSpec((hs, CA_TQ, CA_WIN), lambda g, b, i: (g, 0, 0)),
                  pl.BlockSpec((1, W), lambda g, b, i: (0, g))],
        out_specs=[ospec, ospec], out_shape=[_sds((B, S, HD), F32), _sds((B, S, HD), BF16)])


def ca_bwd(h3p, bias, o, dmix, gain, carry=None):
    B, S, HD, hs, W, ngrp = _attn_dims(h3p, gain, BWD_HEADS)
    scale = 1.0 / math.sqrt(HEAD_DIM)
    nq = S // CA_TQ
    off = CA_PAD // CA_TQ

    def body(q_ref, k_ref, v_ref, b_ref, o_ref, dm_ref, g_ref, dq_ref, dk_ref, dv_ref, dg_ref, db_ref, dk_acc, dv_acc):
        bi = pl.program_id(1)
        qi = pl.program_id(2)

        @pl.when(qi == 0)
        def _():
            dk_acc[...] = jnp.zeros_like(dk_acc)
            dv_acc[...] = jnp.zeros_like(dv_acc)

        @pl.when(jnp.logical_and(qi == 0, bi == 0))
        def _():
            db_ref[...] = jnp.zeros_like(db_ref)

        q0 = qi * CA_TQ
        ks = pl.ds(pl.multiple_of(q0, CA_TQ), CA_WIN)
        lo = _lane_lo(CA_TQ)
        valid = lax.broadcasted_iota(jnp.int32, (CA_TQ, CA_WIN), 1) + q0 >= CA_PAD
        qm, dom, kk, vv = [], [], [], []
        for p in range(hs // 2):
            cols = slice(p * LANES, (p + 1) * LANES)
            qm += _split_heads((q_ref[0, :, cols].astype(F32) * scale).astype(BF16), lo)
            do, dg = _rms_bwd(o_ref[0, :, cols], dm_ref[0, :, cols], g_ref[:, cols], lo)
            dg_ref[0, 0, :, cols] = dg
            dom += _split_heads(do.astype(BF16), lo)
            kk.append(k_ref[0, ks, cols])
            vv.append(v_ref[0, ks, cols])
        hh = range(hs)
        ss = [_ca_scores(qm[h], kk[h // 2], b_ref[h], valid) for h in hh]
        dps = [_dot(dom[h], vv[h // 2], NT) for h in hh]
        ps = [_ca_softmax(s) for s in ss]
        dss = [ps[h] * (dps[h] - jnp.sum(ps[h] * dps[h], axis=-1, keepdims=True)) for h in hh]
        for h in hh:
            db_ref[h] += dss[h]
        dsb = [d.astype(BF16) for d in dss]
        pb = [p_.astype(BF16) for p_ in ps]
        dq = [_dot(dsb[h], kk[h // 2], NN) for h in hh]
        for p in range(hs // 2):
            cols = slice(p * LANES, (p + 1) * LANES)
            dk_acc[ks, cols] += _dot(dsb[2 * p], qm[2 * p], TN) + _dot(dsb[2 * p + 1], qm[2 * p + 1], TN)
            dv_acc[ks, cols] += _dot(pb[2 * p], dom[2 * p], TN) + _dot(pb[2 * p + 1], dom[2 * p + 1], TN)
        for p, d in enumerate(_merge_pairs(dq, lo)):
            dq_ref[0, :, p * LANES:(p + 1) * LANES] = (d * scale).astype(BF16)

        @pl.when(qi == nq - 1)
        def _():
            dk_ref[0] = dk_acc[CA_PAD:, :].astype(BF16)
            dv_ref[0] = dv_acc[CA_PAD:, :].astype(BF16)

    once = pl.Buffered(1)
    ospec = pl.BlockSpec((1, CA_TQ, W), lambda g, b, i: (b, i, g))
    kvout = pl.BlockSpec((1, S, W), lambda g, b, i: (b, 0, g), pipeline_mode=once)
    bspec = pl.BlockSpec((hs, CA_TQ, CA_WIN), lambda g, b, i: (g, 0, 0))
    return _call(
        body, carry, grid=(ngrp, B, nq), name="ca_bwd", sem=("parallel", "arbitrary", "arbitrary"),
        args=(h3p, h3p, h3p, bias, o, dmix, gain),
        in_specs=[pl.BlockSpec((1, CA_TQ, W), lambda g, b, i: (b, i + off, 3 * ngrp + g)),
                  pl.BlockSpec((1, CA_PAD + S, W), lambda g, b, i: (b, 0, 4 * ngrp + g), pipeline_mode=once),
                  pl.BlockSpec((1, CA_PAD + S, W), lambda g, b, i: (b, 0, 5 * ngrp + g), pipeline_mode=once),
                  bspec, ospec, pl.BlockSpec((1, CA_TQ, W), lambda g, b, i: (b, i, ngrp + g)),
                  pl.BlockSpec((1, W), lambda g, b, i: (0, g))],
        out_specs=[ospec, kvout, kvout, pl.BlockSpec((1, 1, 1, W), lambda g, b, i: (b, i, 0, g)), bspec],
        out_shape=[_sds((B, S, HD), BF16), _sds((B, S, HD), BF16), _sds((B, S, HD), BF16),
                   _sds((B, nq, 1, HD), F32), _sds((HD // HEAD_DIM, CA_TQ, CA_WIN), F32)],
        scratch_shapes=[pltpu.VMEM((CA_PAD + S, W), F32), pltpu.VMEM((CA_PAD + S, W), F32)])


_ANY = pl.BlockSpec(memory_space=pl.ANY)
_MESH = pl.DeviceIdType.MESH


def _mesh_pos():
    return lax.axis_index("x"), lax.axis_index("y"), lax.axis_index("c")


def _dev_index(p):
    return 4 * p[0] + 2 * p[1] + p[2]


def _flip(pos, k):
    return tuple(1 - v if (k >> (2 - a)) & 1 else v for a, v in enumerate(pos))


class _Plan:
    def __init__(self, operands, n_matrices):
        self.operands = list(operands)
        self.n_ops = len(self.operands)
        self.nm = n_matrices
        self.scratch = [pltpu.SemaphoreType.DMA((7, self.nm)), pltpu.SemaphoreType.DMA((7, self.nm)),
                        pltpu.SemaphoreType.DMA((self.nm,))]


class _Gather(_Plan):
    def __init__(self, mats):
        super().__init__(mats, len(mats))
        self.rows = [m.shape[0] for m in mats]
        self.out_shape = [_sds((N_DEV * m.shape[0], m.shape[1]), m.dtype) for m in mats]

    def begin(self, ins, outs, sems):
        nm, rows = self.nm, self.rows
        send_sems, recv_sems, local_sems = sems
        x, y, c = _mesh_pos()
        me, sibling = (x, y, c), (x, y, 1 - c)
        chips = [(1 - x, y), (x, 1 - y), (1 - x, 1 - y)]

        def block(m, p):
            start = pl.multiple_of(_dev_index(p) * rows[m], 16)
            return outs[m].at[pl.ds(start, rows[m]), :]

        def copy(k, m, blk, to, src=None):
            return pltpu.make_async_remote_copy(
                src_ref=block(m, blk) if src is None else src, dst_ref=block(m, blk),
                send_sem=send_sems.at[k, m], recv_sem=recv_sems.at[k, m], device_id=to, device_id_type=_MESH)

        def mine():
            return [pltpu.make_async_copy(ins[m], block(m, me), local_sems.at[m]) for m in range(nm)]

        def first():
            own = [copy(0, m, me, sibling, src=ins[m]) for m in range(nm)]
            return own + [copy(1 + j, m, me, (*chip, c), src=ins[m]) for j, chip in enumerate(chips) for m in range(nm)]

        def start():
            for cp in mine() + first():
                cp.start()

        def passed():
            return [copy(4 + j, m, (*chip, c), sibling) for j, chip in enumerate(chips) for m in range(nm)]

        def pass_on():
            for j, chip in enumerate(chips):
                for m in range(nm):
                    copy(1 + j, m, (*chip, c), me).wait_recv()
                    copy(4 + j, m, (*chip, c), sibling).start()

        def finish():
            for m in range(nm):
                copy(0, m, sibling, me).wait_recv()
            for j, chip in enumerate(chips):
                for m in range(nm):
                    copy(4 + j, m, (*chip, 1 - c), me).wait_recv()
            for cp in first() + passed():
                cp.wait_send()
            for cp in mine():
                cp.wait()

        return start, pass_on, finish


class _Exchange(_Plan):
    def __init__(self, grads):
        self.where = [(i, j) for i, g in enumerate(grads) for j in (range(g.shape[0]) if g.ndim == 3 else [None])]
        super().__init__(grads, len(self.where))
        self.rows = [grads[i].shape[-2] // N_DEV for i, _ in self.where]
        self.out_shape = [_sds(g.shape[:-2] + (N_DEV, g.shape[-2] // N_DEV, g.shape[-1]), g.dtype) for g in grads]

    def begin(self, ins, outs, sems):
        nm, rows = self.nm, self.rows
        send_sems, recv_sems, local_sems = sems
        me = _mesh_pos()
        my = _dev_index(me)

        def piece(m, idx):
            i, j = self.where[m]
            rows_ = pl.ds(pl.multiple_of(idx * rows[m], 16), rows[m])
            return ins[i].at[rows_, :] if j is None else ins[i].at[j, rows_, :]

        def landing(m, slot):
            i, j = self.where[m]
            return outs[i].at[slot] if j is None else outs[i].at[j, slot]

        def copy(k, m, src_idx, slot, to):
            return pltpu.make_async_remote_copy(
                src_ref=piece(m, src_idx), dst_ref=landing(m, slot),
                send_sem=send_sems.at[k - 1, m], recv_sem=recv_sems.at[k - 1, m], device_id=to, device_id_type=_MESH)

        def mine():
            return [pltpu.make_async_copy(piece(m, my), landing(m, my), local_sems.at[m]) for m in range(nm)]

        def sends():
            return [copy(k, m, _dev_index(_flip(me, k)), my, _flip(me, k)) for k in range(1, N_DEV) for m in range(nm)]

        def start():
            for cp in mine() + sends():
                cp.start()

        def finish():
            for k in range(1, N_DEV):
                peer = _flip(me, k)
                for m in range(nm):
                    copy(k, m, my, _dev_index(peer), peer).wait_recv()
            for cp in sends():
                cp.wait_send()
            for cp in mine():
                cp.wait()

        return start, None, finish


def _run_plan(plan, name):
    nm = plan.n_ops

    def body(*refs):
        start, middle, finish = plan.begin(refs[:nm], refs[nm:2 * nm], refs[2 * nm:])
        start()
        if middle is not None:
            middle()
        finish()

    return pl.pallas_call(body, name=name, in_specs=[_ANY] * nm, out_specs=[_ANY] * nm, out_shape=plan.out_shape,
                          scratch_shapes=plan.scratch)(*plan.operands)


def _call(body, carry, *, grid, in_specs, out_specs, out_shape, args, name, sem, scratch_shapes=()):
    if carry is None:
        outs = pl.pallas_call(body, grid=grid, name=name, in_specs=in_specs, out_specs=out_specs, out_shape=out_shape,
                              scratch_shapes=list(scratch_shapes), compiler_params=_params(*sem))(*args)
        return outs, None
    n_in, n_out, n_sc, nm = len(in_specs), len(out_specs), len(scratch_shapes), carry.n_ops

    def carrier(*refs):
        refs = list(refs)
        own_in, refs = refs[:n_in], refs[n_in:]
        c_in, refs = refs[:nm], refs[nm:]
        own_out, refs = refs[:n_out], refs[n_out:]
        c_out, refs = refs[:nm], refs[nm:]
        own_sc, c_sc = refs[:n_sc], refs[n_sc:]
        start, middle, finish = carry.begin(c_in, c_out, c_sc)
        step, steps = 0, 1
        for axis, n in enumerate(grid):
            step = step * n + pl.program_id(axis)
            steps *= n
        pl.when(step == 0)(start)
        body(*own_in, *own_out, *own_sc)

        @pl.when(step == steps - 1)
        def _():
            if middle is not None:
                middle()
            finish()

    outs = pl.pallas_call(
        carrier, grid=grid, name=name + "_carry", in_specs=list(in_specs) + [_ANY] * nm,
        out_specs=list(out_specs) + [_ANY] * nm, out_shape=list(out_shape) + carry.out_shape,
        scratch_shapes=list(scratch_shapes) + carry.scratch,
        compiler_params=_params(*["arbitrary"] * len(grid)),
    )(*args, *carry.operands)
    return outs[:n_out], outs[n_out:]


def sum_slots(r, j=None):
    n, R, D = r.shape[-3:]
    tc = _row_tile(D, 256)

    def body(r_ref, o_ref):
        slot = (lambda s: r_ref[s]) if j is None else (lambda s: r_ref[0, s])
        acc = slot(0).astype(F32)
        for s in range(1, n):
            acc = acc + slot(s).astype(F32)
        o_ref[...] = acc

    spec = (pl.BlockSpec((n, R, tc), lambda i: (0, 0, i)) if j is None
            else pl.BlockSpec((1, n, R, tc), lambda i: (j, 0, 0, i)))
    return pl.pallas_call(
        body, grid=(D // tc,), name="sum_slots", in_specs=[spec],
        out_specs=pl.BlockSpec((R, tc), lambda i: (0, i)), out_shape=_sds((R, D), F32),
        compiler_params=_params("parallel"),
    )(r)


def allreduce_small(v):
    R, C = v.shape

    def body(v_ref, o_ref, buf, send_sems, recv_sems):
        me = _mesh_pos()
        my = _dev_index(me)
        buf[my] = v_ref[...]

        def copy(k, slot, to):
            return pltpu.make_async_remote_copy(
                src_ref=v_ref, dst_ref=buf.at[slot], send_sem=send_sems.at[k - 1], recv_sem=recv_sems.at[k - 1],
                device_id=to, device_id_type=_MESH)

        sends = []
        for k in range(1, N_DEV):
            cp = copy(k, my, _flip(me, k))
            cp.start()
            sends.append(cp)
        for k in range(1, N_DEV):
            peer = _flip(me, k)
            copy(k, _dev_index(peer), peer).wait_recv()
        acc = buf[0]
        for s in range(1, N_DEV):
            acc = acc + buf[s]
        o_ref[...] = acc
        for cp in sends:
            cp.wait_send()

    vm = pl.BlockSpec(memory_space=pltpu.VMEM)
    return pl.pallas_call(
        body, name="allreduce_small", in_specs=[vm], out_specs=vm, out_shape=_sds((R, C), F32),
        scratch_shapes=[pltpu.VMEM((N_DEV, R, C), F32), pltpu.SemaphoreType.DMA((7,)), pltpu.SemaphoreType.DMA((7,))],
    )(v)


WEIGHTS = ["ffn1_w_gate", "ffn1_w_up", "ffn1_w_down", "ln1_g", "ln1_b", "w_in", "rel_bias", "sb_out_g", "ca_out_g",
           "w_out", "ln2_g", "ln2_b", "ffn2_w_gate", "ffn2_w_up", "ffn2_w_down", "ln3_g", "ln3_b"]
BIG = ["ffn1_w_gate", "ffn1_w_up", "ffn2_w_gate", "ffn2_w_up", "w_in", "ffn1_w_down", "ffn2_w_down", "w_out"]
TRANSPOSED = BIG[:5]
SMALL = [n for n in WEIGHTS if n not in BIG]


def _pack(vals):
    flat = jnp.concatenate([v.reshape(-1).astype(F32) for v in vals])
    pad = -flat.shape[0] % (8 * 128)
    return jnp.pad(flat, (0, pad)).reshape(-1, 128)


def _unpack(packed, like):
    flat = packed.reshape(-1)
    out, off = [], 0
    for v in like:
        out.append(flat[off:off + v.size].reshape(v.shape))
        off += v.size
    return out


def _row_blocks(w, n):
    r = w.shape[0] // n
    return [w[i * r:(i + 1) * r] for i in range(n)]


class _Arrivals:
    def __init__(self):
        self.by_kernel = {}

    def ride(self, kernel_name, weights, names, local, layer):
        self.by_kernel[kernel_name] = (weights, names, _Gather([local[n][layer] for n in names]))

    def plan(self, kernel_name):
        return self.by_kernel[kernel_name][2] if kernel_name in self.by_kernel else None

    def landed(self, kernel_name, outs):
        if kernel_name in self.by_kernel:
            weights, names, _ = self.by_kernel[kernel_name]
            weights.update(zip(names, outs))


def _layer_fwd(x, xb, W, P, bias, l, B, S, arrivals):
    T = B * S
    sv = {"xb": xb, "bias": bias}
    (sv["sg1"], sv["t1"], sv["a1"]), got = ffn_up(xb, W["ffn1_w_gate"], W["ffn1_w_up"], carry=arrivals.plan("ffn_up"))
    arrivals.landed("ffn_up", got)
    sv["u1"], x1, sv["x1b"] = mm_nn_ln([(sv["a1"], W["ffn1_w_down"])], x, P["ln1_g"][l], P["ln1_b"][l], FFN_RESIDUAL,
                                       "ffn_down_ln")

    sv["h"] = proj_in(sv["x1b"], W["w_in"], B, S)
    sv["gA"] = P["sb_out_g"][l].reshape(1, -1)
    sv["gB"] = P["ca_out_g"][l].reshape(1, -1)
    (sv["oa"], ma, sv["cars"]), got = sb_fwd(sv["h"], sv["gA"], carry=arrivals.plan("sb_fwd"))
    arrivals.landed("sb_fwd", got)
    (sv["ob"], mb), got = ca_fwd(sv["h"], bias, sv["gB"], carry=arrivals.plan("ca_fwd"))
    arrivals.landed("ca_fwd", got)
    sv["ma"], sv["mb"] = ma.reshape(T, -1), mb.reshape(T, -1)
    sv["u2"], x2, sv["x2b"] = mm_nn_ln(list(zip([sv["ma"], sv["mb"]], _row_blocks(W["w_out"], 2))), x1,
                                       P["ln2_g"][l], P["ln2_b"][l], 1.0, "proj_out_ln")

    (sv["sg2"], sv["t2"], sv["a2"]), _ = ffn_up(sv["x2b"], W["ffn2_w_gate"], W["ffn2_w_up"])
    sv["u3"], x3, x3b = mm_nn_ln([(sv["a2"], W["ffn2_w_down"])], x2, P["ln3_g"][l], P["ln3_b"][l], FFN_RESIDUAL,
                                 "ffn_down_ln")
    return x3, x3b, sv


class _Riders:
    def __init__(self):
        self.received = []

    def plan(self, entries):
        return _Exchange([e[2] for e in entries]) if entries else None

    def landed(self, entries, slots):
        for (names, layer, _), r in zip(entries, slots or []):
            for j, name in enumerate(names):
                self.received.append((name, layer, r, j if r.ndim == 4 else None))


def _ffn_bwd(dr, dxa, xb, sg, t, a, wg, wu, wd, riders, ride, below, names, layer):
    dhg, dhu = ffn_bwd_mid(dr, wd, sg, t)
    pairs = [(dhg, wg), (dhu, wu)]
    if below is not None:
        down = mm_nn_ln_bwd(pairs, dxa, *below, name="ffn_dx_ln")
        grads, slots = ffn_dw(dhg, dhu, a, xb, dr, carry=riders.plan(ride))
        riders.landed(ride, slots)
        return down, [(names, layer, grads)]
    down = mm_nn(pairs, add=dxa, name="ffn_dx")
    for name, lhs, rhs in zip(names, (dhg, dhu, a), (xb, xb, dr)):
        g, slots = mm_tn([lhs], rhs, name="ffn_dw_last", carry=riders.plan(ride))
        riders.landed(ride, slots)
        ride = [((name,), layer, g)]
    return down, ride


def _layer_bwd(dxa, dr, sv, W, P, l, B, S, riders, pending, below):
    T = B * S
    G = {}
    (dxa, dr, dg, db), ride = _ffn_bwd(
        dr, dxa, sv["x2b"], sv["sg2"], sv["t2"], sv["a2"], W["ffn2_w_gate"], W["ffn2_w_up"], W["ffn2_w_down"],
        riders, pending, (sv["u2"], P["ln2_g"][l], 1.0), ("ffn2_w_gate", "ffn2_w_up", "ffn2_w_down"), l)
    G["ln2_g"], G["ln2_b"] = dg.sum(0), db.sum(0)
    dmix = mm_nt(dr, W["w_out"], name="proj_out_dx").reshape(B, S, -1)
    g_out = mm_tn([sv["ma"], sv["mb"]], dr, name="proj_out_dw")[0]
    (dqa, dka, dva, dga), slots = sb_bwd(sv["h"], sv["oa"], dmix, sv["gA"], sv["cars"], carry=riders.plan(ride))
    riders.landed(ride, slots)
    (dqb, dkb, dvb, dgb, dbias), _ = ca_bwd(sv["h"], sv["bias"], sv["ob"], dmix, sv["gB"])
    G["sb_out_g"] = dga.sum((0, 1, 2))
    G["ca_out_g"] = dgb.sum((0, 1, 2))
    G["rel_bias"] = rel_bias_grad(dbias)
    dh = [t.reshape(T, -1) for t in (dqa, dka, dva, dqb, dkb, dvb)]
    dxa, dr, dg, db = mm_nn_ln_bwd(list(zip(dh, _row_blocks(W["w_in"], 6))), dxa, sv["u1"], P["ln1_g"][l], FFN_RESIDUAL,
                                   name="proj_in_dx_ln")
    G["ln1_g"], G["ln1_b"] = dg.sum(0), db.sum(0)
    g_in = mm_tn(dh, sv["x1b"], name="proj_in_dw")[0]

    down, pending = _ffn_bwd(dr, dxa, sv["xb"], sv["sg1"], sv["t1"], sv["a1"], W["ffn1_w_gate"], W["ffn1_w_up"],
                             W["ffn1_w_down"], riders, [(("w_in",), l, g_in), (("w_out",), l, g_out)], below,
                             ("ffn1_w_gate", "ffn1_w_up", "ffn1_w_down"), l)
    return down, G, pending


def kernel(x, ffn1_w_gate, ffn1_w_up, ffn1_w_down, ln1_g, ln1_b, w_in, rel_bias, sb_out_g, ca_out_g, w_out, ln2_g, ln2_b, ffn2_w_gate, ffn2_w_up, ffn2_w_down, ln3_g, ln3_b, loss_target, m_ffn1_w_gate, m_ffn1_w_up, m_ffn1_w_down, m_ln1_g, m_ln1_b, m_w_in, m_rel_bias, m_sb_out_g, m_ca_out_g, m_w_out, m_ln2_g, m_ln2_b, m_ffn2_w_gate, m_ffn2_w_up, m_ffn2_w_down, m_ln3_g, m_ln3_b, v_ffn1_w_gate, v_ffn1_w_up, v_ffn1_w_down, v_ln1_g, v_ln1_b, v_w_in, v_rel_bias, v_sb_out_g, v_ca_out_g, v_w_out, v_ln2_g, v_ln2_b, v_ffn2_w_gate, v_ffn2_w_up, v_ffn2_w_down, v_ln3_g, v_ln3_b):
    given = dict(locals())
    P = {n: given[n] for n in WEIGHTS}
    M = {n: given["m_" + n] for n in WEIGHTS}
    V = {n: given["v_" + n] for n in WEIGHTS}
    B, S, D = x.shape
    L = ln1_g.shape[0]

    local = {n: (jnp.swapaxes(P[n], 1, 2) if n in TRANSPOSED else P[n]).astype(BF16) for n in BIG}
    bias = ca_bias(rel_bias)

    xf = x.reshape(B * S, D)
    xb = xf.astype(BF16)
    W, saved = [{} for _ in range(L)], []
    first = ["ffn1_w_gate", "ffn1_w_up"]
    W[0].update(zip(first, _run_plan(_Gather([local[n][0] for n in first]), "gather_weights")))
    for l in range(L):
        arrivals = _Arrivals()
        if l == 0:
            arrivals.ride("ffn_up", W[0], ["ffn1_w_down", "w_in", "w_out"], local, 0)
            arrivals.ride("ca_fwd", W[0], ["ffn2_w_gate", "ffn2_w_up", "ffn2_w_down"], local, 0)
            if L > 1:
                arrivals.ride("sb_fwd", W[1], BIG, local, 1)
        elif l + 1 < L:
            arrivals.ride("sb_fwd", W[l + 1], BIG[:5], local, l + 1)
            arrivals.ride("ca_fwd", W[l + 1], BIG[5:], local, l + 1)
        xf, xb, sv = _layer_fwd(xf, xb, W[l], P, bias[l], l, B, S, arrivals)
        saved.append(sv)
    dy, loss_part = loss_head(xf, loss_target.reshape(B * S, D))

    big_g = {n: [None] * L for n in BIG}
    small_g = {n: [None] * L for n in SMALL}
    riders, pending = _Riders(), []
    down = ln_bwd(saved[L - 1]["u3"], dy, ln3_g[L - 1], FFN_RESIDUAL)
    for l in reversed(range(L)):
        dxa, dr, dg, db = down
        small_g["ln3_g"][l], small_g["ln3_b"][l] = dg.sum(0), db.sum(0)
        below = (saved[l - 1]["u3"], ln3_g[l - 1], FFN_RESIDUAL) if l > 0 else None
        down, G, pending = _layer_bwd(dxa, dr, saved[l], W[l], P, l, B, S, riders, pending, below)
        for n, g in G.items():
            small_g[n][l] = g
    dx = down
    riders.landed(pending, _run_plan(riders.plan(pending), "exchange_grads"))
    for n, l, slots, j in riders.received:
        big_g[n][l] = sum_slots(slots, j)

    grads = {}
    for n in BIG:
        g = jnp.stack(big_g[n])
        grads[n] = jnp.swapaxes(g, 1, 2) if n in TRANSPOSED else g
    small_like = [P[n] for n in SMALL]
    packed = _pack([jnp.stack(small_g[n]) for n in SMALL] + [loss_part.sum()])
    total = allreduce_small(packed)
    *small_vals, loss = _unpack(total, small_like + [jnp.zeros((), F32)])
    grads.update(dict(zip(SMALL, small_vals)))

    delta, new_m, new_v = {}, {}, {}
    for n in BIG:
        shape = P[n].shape
        two_d = lambda a: a.reshape(shape[0] * shape[1], shape[2])
        d, m, v = adamw(two_d(P[n]), two_d(grads[n]), two_d(M[n]), two_d(V[n]))
        delta[n], new_m[n], new_v[n] = d.reshape(shape), m.reshape(shape), v.reshape(shape)
    one = jnp.ones((), F32)
    d, m, v = adamw(_pack(small_like + [one]), total, _pack([M[n] for n in SMALL] + [one]), _pack([V[n] for n in SMALL] + [one]))
    for dst, src in ((delta, d), (new_m, m), (new_v, v)):
        dst.update(dict(zip(SMALL, _unpack(src, small_like))))

    return (loss, dx.reshape(B, S, D), *[grads[n] for n in WEIGHTS], *[delta[n] for n in WEIGHTS],
            *[new_m[n] for n in WEIGHTS], *[new_v[n] for n in WEIGHTS])
```

```python
import math

import jax
import jax.numpy as jnp
import numpy as np
from jax import lax
from jax.experimental import pallas as pl
from jax.experimental.pallas import tpu as pltpu

F32 = jnp.float32
BF16 = jnp.bfloat16

HEAD_DIM = 64
CHUNK = 64
N_PREV_CHUNKS = 8
MAX_REL = 128
DEPTH = 4
FFN_RESIDUAL = 0.5
ALPHA = (2 * DEPTH) ** 0.25
LN_EPS = 1e-5
RMS_EPS = 1e-6
ADAM_LR = 0.001
ADAM_B1 = 0.9
ADAM_B2 = 0.999
ADAM_EPS = 1e-08
ADAM_WD = 0.01
ADAM_STEP = 10

N_DEV = 8
LANES = 128
VMEM_LIMIT_BYTES = 56 * 1024 * 1024
ROW_TILE = 512
SB_BLOCK = 128
CA_PAD = CHUNK * N_PREV_CHUNKS
CA_TQ = 2 * CHUNK
CA_WIN = CA_PAD + CA_TQ
FWD_HEADS = 8
BWD_HEADS = 8
NEG = -1e30
SB_SKIP_LOG = -104.0

NN = ((1,), (0,))
NT = ((1,), (1,))
TN = ((0,), (0,))


def _dot(a, b, dims):
    return lax.dot_general(a, b, (dims, ((), ())), preferred_element_type=F32)


def _params(*sem):
    return pltpu.CompilerParams(dimension_semantics=sem, vmem_limit_bytes=VMEM_LIMIT_BYTES)


def _sds(shape, dtype):
    return jax.ShapeDtypeStruct(shape, dtype)


def _row_tile(n, want=ROW_TILE):
    t = min(want, n)
    while n % t:
        t //= 2
    assert t >= 8, (n, want)
    return t


def ffn_up(xb, wg, wu, carry=None):
    T, D = xb.shape
    F = wg.shape[0]
    tm = _row_tile(T)
    fc = _row_tile(F, 256)

    def body(x_ref, wg_ref, wu_ref, sg_ref, t_ref, a_ref):
        x = x_ref[...]
        for j in range(F // fc):
            sl = pl.ds(j * fc, fc)
            hg = _dot(x, wg_ref[sl, :], NT)
            hu = _dot(x, wu_ref[sl, :], NT)
            s = jax.nn.sigmoid(hg)
            sg = hg * s
            a = sg * hu
            sg_ref[:, sl] = sg.astype(BF16)
            t_ref[:, sl] = (a + s * (hu - a)).astype(BF16)
            a_ref[:, sl] = a.astype(BF16)

    row = pl.BlockSpec((tm, F), lambda i: (i, 0))
    w = pl.BlockSpec((F, D), lambda i: (0, 0))
    return _call(body, carry, grid=(T // tm,), name="ffn_up", sem=("parallel",), args=(xb, wg, wu),
                 in_specs=[pl.BlockSpec((tm, D), lambda i: (i, 0)), w, w],
                 out_specs=[row, row, row], out_shape=[_sds((T, F), BF16)] * 3)


def ffn_bwd_mid(dr, wd, da_dhu, da_dhg):
    T, D = dr.shape
    F = wd.shape[0]
    tm = _row_tile(T)
    fc = _row_tile(F, 256)

    def body(dr_ref, wd_ref, sg_ref, t_ref, dhg_ref, dhu_ref):
        dr_ = dr_ref[...]
        for j in range(F // fc):
            sl = pl.ds(j * fc, fc)
            da = _dot(dr_, wd_ref[sl, :], NT)
            dhu_ref[:, sl] = (da * sg_ref[:, sl].astype(F32)).astype(BF16)
            dhg_ref[:, sl] = (da * t_ref[:, sl].astype(F32)).astype(BF16)

    row = pl.BlockSpec((tm, F), lambda i: (i, 0))
    return pl.pallas_call(
        body, grid=(T // tm,), name="ffn_bwd_mid",
        in_specs=[pl.BlockSpec((tm, D), lambda i: (i, 0)), pl.BlockSpec((F, D), lambda i: (0, 0)), row, row],
        out_specs=[row, row], out_shape=[_sds((T, F), BF16)] * 2,
        compiler_params=_params("parallel"),
    )(dr, wd, da_dhu, da_dhg)


def mm_nn(pairs, add=None, out_dtype=F32, name="mm_nn"):
    M = pairs[0][0].shape[0]
    N = pairs[0][1].shape[1]
    tm = _row_tile(M)
    nc = _row_tile(N, 512)
    n_pairs = len(pairs)

    def body(*refs):
        a_refs = refs[:n_pairs]
        b_refs = refs[n_pairs:2 * n_pairs]
        add_ref = refs[2 * n_pairs] if add is not None else None
        o_ref = refs[-1]
        for j in range(N // nc):
            sl = pl.ds(j * nc, nc)
            acc = _dot(a_refs[0][...], b_refs[0][:, sl], NN)
            for a_ref, b_ref in zip(a_refs[1:], b_refs[1:]):
                acc = acc + _dot(a_ref[...], b_ref[:, sl], NN)
            if add_ref is not None:
                acc = acc + add_ref[:, sl]
            o_ref[:, sl] = acc.astype(out_dtype)

    in_specs = [pl.BlockSpec((tm, a.shape[1]), lambda i: (i, 0)) for a, _ in pairs]
    in_specs += [pl.BlockSpec(b.shape, lambda i: (0, 0)) for _, b in pairs]
    args = [a for a, _ in pairs] + [b for _, b in pairs]
    if add is not None:
        in_specs.append(pl.BlockSpec((tm, N), lambda i: (i, 0)))
        args.append(add)
    return pl.pallas_call(
        body, grid=(M // tm,), name=name, in_specs=in_specs,
        out_specs=pl.BlockSpec((tm, N), lambda i: (i, 0)), out_shape=_sds((M, N), out_dtype),
        compiler_params=_params("parallel"),
    )(*args)


def _mm_specs(pairs, tm):
    specs = [pl.BlockSpec((tm, a.shape[1]), lambda i: (i, 0)) for a, _ in pairs]
    specs += [pl.BlockSpec(b.shape, lambda i: (0, 0), pipeline_mode=pl.Buffered(1)) for _, b in pairs]
    return specs, [a for a, _ in pairs] + [b for _, b in pairs]


def _mm_sum(a_refs, b_refs):
    acc = _dot(a_refs[0][...], b_refs[0][...], NN)
    for a_ref, b_ref in zip(a_refs[1:], b_refs[1:]):
        acc = acc + _dot(a_ref[...], b_ref[...], NN)
    return acc


def _ln_fwd_rows(u, g, b):
    mu = jnp.mean(u, axis=-1, keepdims=True)
    xc = u - mu
    var = jnp.mean(xc * xc, axis=-1, keepdims=True)
    return xc * lax.rsqrt(var + LN_EPS) * g + b


def _ln_bwd_rows(u, dy, g):
    mu = jnp.mean(u, axis=-1, keepdims=True)
    xc = u - mu
    var = jnp.mean(xc * xc, axis=-1, keepdims=True)
    rstd = lax.rsqrt(var + LN_EPS)
    xh = xc * rstd
    dxh = dy * g
    m1 = jnp.mean(dxh, axis=-1, keepdims=True)
    m2 = jnp.mean(dxh * xh, axis=-1, keepdims=True)
    return rstd * (dxh - m1 - xh * m2), dy * xh


def mm_nn_ln(pairs, x, g, b, res_scale, name):
    M, N = x.shape
    tm = _row_tile(M)
    n_pairs = len(pairs)

    def body(*refs):
        a_refs, b_refs = refs[:n_pairs], refs[n_pairs:2 * n_pairs]
        x_ref, g_ref, b_ref, u_ref, y_ref, yb_ref = refs[2 * n_pairs:]
        u = ALPHA * x_ref[...] + res_scale * _mm_sum(a_refs, b_refs)
        y = _ln_fwd_rows(u, g_ref[...], b_ref[...])
        u_ref[...] = u
        y_ref[...] = y
        yb_ref[...] = y.astype(BF16)

    in_specs, args = _mm_specs(pairs, tm)
    row = pl.BlockSpec((tm, N), lambda i: (i, 0))
    vec = pl.BlockSpec((1, N), lambda i: (0, 0))
    return pl.pallas_call(
        body, grid=(M // tm,), name=name, in_specs=in_specs + [row, vec, vec],
        out_specs=[row, row, row], out_shape=[_sds((M, N), F32), _sds((M, N), F32), _sds((M, N), BF16)],
        compiler_params=_params("parallel"),
    )(*args, x, g.reshape(1, N), b.reshape(1, N))


def mm_nn_ln_bwd(pairs, add, u, g, branch_scale, name):
    M, N = u.shape
    tm = _row_tile(M)
    n_pairs = len(pairs)

    def body(*refs):
        a_refs, b_refs = refs[:n_pairs], refs[n_pairs:2 * n_pairs]
        add_ref, u_ref, g_ref, dxa_ref, dr_ref, dg_ref, db_ref = refs[2 * n_pairs:]

        @pl.when(pl.program_id(0) == 0)
        def _():
            dg_ref[...] = jnp.zeros_like(dg_ref)
            db_ref[...] = jnp.zeros_like(db_ref)

        dy = _mm_sum(a_refs, b_refs) + add_ref[...]
        du, dyxh = _ln_bwd_rows(u_ref[...], dy, g_ref[...])
        dxa_ref[...] = ALPHA * du
        dr_ref[...] = (branch_scale * du).astype(BF16)
        dg_ref[...] += jnp.sum(dyxh.reshape(tm // 8, 8, N), axis=0)
        db_ref[...] += jnp.sum(dy.reshape(tm // 8, 8, N), axis=0)

    in_specs, args = _mm_specs(pairs, tm)
    row = pl.BlockSpec((tm, N), lambda i: (i, 0))
    acc = pl.BlockSpec((8, N), lambda i: (0, 0))
    return pl.pallas_call(
        body, grid=(M // tm,), name=name, in_specs=in_specs + [row, row, pl.BlockSpec((1, N), lambda i: (0, 0))],
        out_specs=[row, row, acc, acc],
        out_shape=[_sds((M, N), F32), _sds((M, N), BF16), _sds((8, N), F32), _sds((8, N), F32)],
        compiler_params=_params("arbitrary"),
    )(*args, add, u, g.reshape(1, N))


def mm_nt(a, b, out_dtype=F32, name="mm_nt"):
    M, K = a.shape
    N = b.shape[0]
    tm = _row_tile(M)
    nc = _row_tile(N, 512)

    def body(a_ref, b_ref, o_ref):
        a_ = a_ref[...]
        for j in range(N // nc):
            sl = pl.ds(j * nc, nc)
            o_ref[:, sl] = _dot(a_, b_ref[sl, :], NT).astype(out_dtype)

    return pl.pallas_call(
        body, grid=(M // tm,), name=name,
        in_specs=[pl.BlockSpec((tm, K), lambda i: (i, 0)), pl.BlockSpec((N, K), lambda i: (0, 0))],
        out_specs=pl.BlockSpec((tm, N), lambda i: (i, 0)), out_shape=_sds((M, N), out_dtype),
        compiler_params=_params("parallel"),
    )(a, b)


def proj_in(xb, w, B, S):
    D = xb.shape[1]
    N = w.shape[0]
    tm = CA_PAD
    nblk = S // tm
    nc = _row_tile(N, 512)
    assert S % tm == 0

    def body(x_ref, w_ref, o_ref):
        i = pl.program_id(1)

        @pl.when(i == 0)
        def _():
            o_ref[...] = jnp.zeros_like(o_ref)

        @pl.when(i > 0)
        def _():
            x = x_ref[...]
            for j in range(N // nc):
                sl = pl.ds(j * nc, nc)
                o_ref[0, :, sl] = _dot(x, w_ref[sl, :], NT).astype(BF16)

    return pl.pallas_call(
        body, grid=(B, nblk + 1), name="proj_in",
        in_specs=[pl.BlockSpec((tm, D), lambda b, i: (b * nblk + jnp.maximum(i - 1, 0), 0)),
                  pl.BlockSpec((N, D), lambda b, i: (0, 0))],
        out_specs=pl.BlockSpec((1, tm, N), lambda b, i: (b, i, 0)), out_shape=_sds((B, CA_PAD + S, N), BF16),
        compiler_params=_params("parallel", "arbitrary"),
    )(xb, w)


def mm_tn(parts, b, name="mm_tn", carry=None):
    T, N = b.shape
    widths = [a.shape[1] for a in parts]
    M = sum(widths)
    tk = _row_tile(T)
    steps = T // tk
    n_parts = len(parts)

    def body(*refs):
        a_refs, b_ref, o_ref, acc_ref = refs[:n_parts], refs[n_parts], refs[n_parts + 1], refs[n_parts + 2]
        i = pl.program_id(0)

        @pl.when(i == 0)
        def _():
            acc_ref[...] = jnp.zeros_like(acc_ref)

        b_ = b_ref[...]
        row = 0
        for a_ref, width in zip(a_refs, widths):
            mc = _row_tile(width, 256)
            for j in range(width // mc):
                acc_ref[pl.ds(row, mc), :] += _dot(a_ref[:, pl.ds(j * mc, mc)], b_, TN)
                row += mc

        @pl.when(i == steps - 1)
        def _():
            o_ref[...] = acc_ref[...].astype(BF16)

    (out,), got = _call(
        body, carry, grid=(steps,), name=name, sem=("arbitrary",), args=(*parts, b),
        in_specs=[pl.BlockSpec((tk, w), lambda i: (i, 0)) for w in widths] + [pl.BlockSpec((tk, N), lambda i: (i, 0))],
        out_specs=[pl.BlockSpec((M, N), lambda i: (0, 0))], out_shape=[_sds((M, N), BF16)],
        scratch_shapes=[pltpu.VMEM((M, N), F32)])
    return out, got


def ffn_dw(dhg, dhu, a, xb, dr, carry=None):
    T, F = dhg.shape
    D = xb.shape[1]
    tk = _row_tile(T)
    mc = _row_tile(F, 256)
    steps = T // tk
    last = steps - 1

    def body(dhg_ref, dhu_ref, a_ref, xb_ref, dr_ref, o_ref, acc_ref):
        p = pl.program_id(0)
        i = pl.program_id(1)

        @pl.when(i == 0)
        def _():
            acc_ref[...] = jnp.zeros_like(acc_ref)

        def accumulate(lhs_ref, rhs_ref):
            rhs = rhs_ref[...]
            for j in range(F // mc):
                sl = pl.ds(j * mc, mc)
                acc_ref[sl, :] += _dot(lhs_ref[:, sl], rhs, TN)

        pl.when(p == 0)(lambda: accumulate(dhg_ref, xb_ref))
        pl.when(p == 1)(lambda: accumulate(dhu_ref, xb_ref))
        pl.when(p == 2)(lambda: accumulate(a_ref, dr_ref))

        @pl.when(i == last)
        def _():
            o_ref[0] = acc_ref[...].astype(BF16)

    def during(phases, width):
        lo, hi = phases
        return pl.BlockSpec((tk, width), lambda p, i: (jnp.where(p < lo, 0, jnp.where(p > hi, last, i)), 0))

    (out,), got = _call(
        body, carry, grid=(3, steps), name="ffn_dw", sem=("arbitrary", "arbitrary"), args=(dhg, dhu, a, xb, dr),
        in_specs=[during((0, 0), F), during((1, 1), F), during((2, 2), F), during((0, 1), D), during((2, 2), D)],
        out_specs=[pl.BlockSpec((1, F, D), lambda p, i: (p, 0, 0))], out_shape=[_sds((3, F, D), BF16)],
        scratch_shapes=[pltpu.VMEM((F, D), F32)])
    return out, got


def ln_bwd(u, dy, g, branch_scale):
    T, D = u.shape
    tm = _row_tile(T)

    def body(u_ref, dy_ref, g_ref, dxa_ref, dr_ref, dg_ref, db_ref):
        @pl.when(pl.program_id(0) == 0)
        def _():
            dg_ref[...] = jnp.zeros_like(dg_ref)
            db_ref[...] = jnp.zeros_like(db_ref)

        dy_ = dy_ref[...]
        du, dyxh = _ln_bwd_rows(u_ref[...], dy_, g_ref[...])
        dxa_ref[...] = ALPHA * du
        dr_ref[...] = (branch_scale * du).astype(BF16)
        dg_ref[...] += jnp.sum(dyxh.reshape(tm // 8, 8, D), axis=0)
        db_ref[...] += jnp.sum(dy_.reshape(tm // 8, 8, D), axis=0)

    row = pl.BlockSpec((tm, D), lambda i: (i, 0))
    acc = pl.BlockSpec((8, D), lambda i: (0, 0))
    dxa, dr, dg, db = pl.pallas_call(
        body, grid=(T // tm,), name="ln_bwd", in_specs=[row, row, pl.BlockSpec((1, D), lambda i: (0, 0))],
        out_specs=[row, row, acc, acc],
        out_shape=[_sds((T, D), F32), _sds((T, D), BF16), _sds((8, D), F32), _sds((8, D), F32)],
        compiler_params=_params("arbitrary"),
    )(u, dy, g.reshape(1, D))
    return dxa, dr, dg, db


def loss_head(y, target):
    T, D = y.shape
    tm = _row_tile(T)

    def body(y_ref, t_ref, dy_ref, l_ref):
        @pl.when(pl.program_id(0) == 0)
        def _():
            l_ref[...] = jnp.zeros_like(l_ref)

        e = y_ref[...] - t_ref[...]
        dy_ref[...] = e * (1.0 / D)
        l_ref[...] += jnp.sum((e * e).reshape(tm // 8, 8, D), axis=0) * (0.5 / D)

    row = pl.BlockSpec((tm, D), lambda i: (i, 0))
    return pl.pallas_call(
        body, grid=(T // tm,), name="loss_head", in_specs=[row, row],
        out_specs=[row, pl.BlockSpec((8, D), lambda i: (0, 0))],
        out_shape=[_sds((T, D), F32), _sds((8, D), F32)],
        compiler_params=_params("arbitrary"),
    )(y, target)


def adamw(w, g, m, v):
    R, C = w.shape
    tr = R
    for cand in (512, 256, 128, 64, 32, 16, 8):
        if R % cand == 0:
            tr = cand
            break
    c1 = 1.0 - ADAM_B1 ** ADAM_STEP
    c2 = 1.0 - ADAM_B2 ** ADAM_STEP

    def body(w_ref, g_ref, m_ref, v_ref, d_ref, mo_ref, vo_ref):
        g_ = g_ref[...]
        m_ = ADAM_B1 * m_ref[...] + (1.0 - ADAM_B1) * g_
        v_ = ADAM_B2 * v_ref[...] + (1.0 - ADAM_B2) * (g_ * g_)
        m_hat = m_ / c1
        v_hat = v_ / c2
        d_ref[...] = -ADAM_LR * (m_hat / (jnp.sqrt(v_hat) + ADAM_EPS) + ADAM_WD * w_ref[...])
        mo_ref[...] = m_
        vo_ref[...] = v_

    blk = pl.BlockSpec((tr, C), lambda i: (i, 0))
    return pl.pallas_call(
        body, grid=(R // tr,), name="adamw", in_specs=[blk] * 4, out_specs=[blk] * 3,
        out_shape=[_sds((R, C), F32)] * 3, compiler_params=_params("parallel"),
    )(w, g, m, v)


def _lane_lo(rows):
    return lax.broadcasted_iota(jnp.int32, (rows, LANES), 1) < HEAD_DIM


def _pair_mean(x, lo):
    s0 = jnp.sum(jnp.where(lo, x, 0.0), axis=-1, keepdims=True)
    s1 = jnp.sum(jnp.where(lo, 0.0, x), axis=-1, keepdims=True)
    return jnp.where(lo, s0, s1) * (1.0 / HEAD_DIM)


def _rms_fwd(o, gain, lo):
    r = lax.rsqrt(_pair_mean(o * o, lo) + RMS_EPS)
    return (o * r * gain).astype(BF16)


def _rms_bwd(o, dm, gain, lo):
    r = lax.rsqrt(_pair_mean(o * o, lo) + RMS_EPS)
    oh = o * r
    dg = jnp.sum(dm * oh, axis=0, keepdims=True)
    doh = dm * gain
    do = r * (doh - oh * _pair_mean(doh * oh, lo))
    return do, dg


def _split_heads(x, lo):
    zero = jnp.zeros_like(x)
    return [jnp.where(lo, x, zero), jnp.where(lo, zero, x)]


def _merge_pairs(per_head, lo):
    return [jnp.where(lo, per_head[2 * p], per_head[2 * p + 1]) for p in range(len(per_head) // 2)]


def _pair_cols(h):
    return slice((h // 2) * LANES, (h // 2 + 1) * LANES)


def _split_dot(x, tri):
    n = x.shape[0]
    hi = x.astype(BF16)
    lo = (x - hi.astype(F32)).astype(BF16)
    both = _dot(jnp.concatenate([hi, lo], axis=0), tri, NN)
    return both[:n] + both[n:]


def _log_keep(z):
    return -(jnp.maximum(z, 0.0) + jnp.log(1.0 + jnp.exp(-jnp.abs(z))))


def _attn_dims(h3p, gain, heads):
    B, SP, C = h3p.shape
    HD = gain.shape[1]
    hs = min(heads, HD // HEAD_DIM)
    W = hs * HEAD_DIM
    assert C == 6 * HD and W % LANES == 0 and HD % W == 0
    return B, SP - CA_PAD, HD, hs, W, HD // W


def sb_fwd(h3p, gain, carry=None):
    B, S, HD, hs, W, ngrp = _attn_dims(h3p, gain, FWD_HEADS)
    tb = min(SB_BLOCK, S)
    nq = S // tb
    off = CA_PAD // tb
    scale = 1.0 / math.sqrt(HEAD_DIM)

    def body(q_ref, k_ref, v_ref, g_ref, o_ref, m_ref, c_ref):
        qi = pl.program_id(2)
        lo = _lane_lo(tb)
        row = lax.broadcasted_iota(jnp.int32, (tb, tb), 0)
        col = lax.broadcasted_iota(jnp.int32, (tb, tb), 1)
        rev_incl = (row >= col).astype(BF16)
        lane = lax.broadcasted_iota(jnp.int32, (tb, nq), 1)
        qm = []
        for p in range(hs // 2):
            qm += _split_heads((q_ref[0, :, p * LANES:(p + 1) * LANES].astype(F32) * scale).astype(BF16), lo)

        def blocks(kb, carries_, mask):
            ks = pl.ds(pl.multiple_of(kb * tb + CA_PAD, tb), tb)
            hh = range(hs)
            zs = [_dot(qm[h], k_ref[0, ks, _pair_cols(h)], NT) for h in hh]
            lks = [_log_keep(z) for z in zs]
            if mask is not None:
                lks = [jnp.where(mask, lk, 0.0) for lk in lks]
            cums = [_split_dot(lk, rev_incl) for lk in lks]
            ws = []
            for h in hh:
                logw = zs[h] + cums[h] + carries_[h]
                if mask is not None:
                    logw = jnp.where(mask, logw, NEG)
                ws.append(jnp.exp(logw).astype(BF16))
            pvs = [_dot(ws[h], v_ref[0, ks, _pair_cols(h)], NN) for h in hh]
            return [(pvs[h], cums[h][:, 0:1]) for h in hh]

        diag = blocks(qi, [jnp.zeros((tb, 1), F32)] * hs, col < row)
        accs = _merge_pairs([d[0] for d in diag], lo)
        carries = [d[1] for d in diag]
        cars = [jnp.where(lane == qi, 0.0, NEG)] * hs

        def cond(st):
            kb, carries_, _, _ = st
            top = carries_[0]
            for c in carries_[1:]:
                top = jnp.maximum(top, c)
            return jnp.logical_and(kb >= 0, jnp.max(top) > SB_SKIP_LOG)

        def step(st):
            kb, carries_, accs_, cars_ = st
            out = blocks(kb, carries_, None)
            pv = _merge_pairs([o[0] for o in out], lo)
            return (kb - 1, [c + o[1] for c, o in zip(carries_, out)], [a + p for a, p in zip(accs_, pv)],
                    [jnp.where(lane == kb, c, cs) for c, cs in zip(carries_, cars_)])

        _, _, accs, cars = lax.while_loop(cond, step, (qi - 1, carries, accs, cars))
        for p, acc in enumerate(accs):
            cols = slice(p * LANES, (p + 1) * LANES)
            o_ref[0, :, cols] = acc
            m_ref[0, :, cols] = _rms_fwd(acc, g_ref[:, cols], lo)
        for h in range(hs):
            c_ref[0, h] = cars[h]

    qspec = pl.BlockSpec((1, tb, W), lambda g, b, i: (b, i + off, g))
    ospec = pl.BlockSpec((1, tb, W), lambda g, b, i: (b, i, g))
    return _call(
        body, carry, grid=(ngrp, B, nq), name="sb_fwd", sem=("parallel", "parallel", "arbitrary"),
        args=(h3p, h3p, h3p, gain),
        in_specs=[qspec, pl.BlockSpec((1, CA_PAD + S, W), lambda g, b, i: (b, 0, ngrp + g)),
                  pl.BlockSpec((1, CA_PAD + S, W), lambda g, b, i: (b, 0, 2 * ngrp + g)),
                  pl.BlockSpec((1, W), lambda g, b, i: (0, g))],
        out_specs=[ospec, ospec, pl.BlockSpec((1, hs, tb, nq), lambda g, b, i: (b, g, i, 0))],
        out_shape=[_sds((B, S, HD), F32), _sds((B, S, HD), BF16), _sds((B, HD // HEAD_DIM, S, nq), F32)])


def sb_bwd(h3p, o, dmix, gain, cars, carry=None):
    B, S, HD, hs, W, ngrp = _attn_dims(h3p, gain, BWD_HEADS)
    tb = min(SB_BLOCK, S)
    nq = S // tb
    off = CA_PAD // tb
    scale = 1.0 / math.sqrt(HEAD_DIM)

    def body(q_ref, k_ref, v_ref, o_ref, dm_ref, g_ref, c_ref, dq_ref, dk_ref, dv_ref, dg_ref, dk_acc, dv_acc):
        qi = pl.program_id(2)

        @pl.when(qi == 0)
        def _():
            dk_acc[...] = jnp.zeros_like(dk_acc)
            dv_acc[...] = jnp.zeros_like(dv_acc)

        lo = _lane_lo(tb)
        row = lax.broadcasted_iota(jnp.int32, (tb, tb), 0)
        col = lax.broadcasted_iota(jnp.int32, (tb, tb), 1)
        rev_incl = (row >= col).astype(BF16)
        fwd_incl = (row <= col).astype(BF16)
        lane = lax.broadcasted_iota(jnp.int32, (tb, nq), 1)
        below = lax.broadcasted_iota(jnp.int32, (1, nq), 1) < qi
        qm, dom, cars_, seen = [], [], [], None
        for p in range(hs // 2):
            cols = slice(p * LANES, (p + 1) * LANES)
            qm += _split_heads((q_ref[0, :, cols].astype(F32) * scale).astype(BF16), lo)
            do, dg = _rms_bwd(o_ref[0, :, cols], dm_ref[0, :, cols], g_ref[:, cols], lo)
            dg_ref[0, 0, :, cols] = dg
            dom += _split_heads(do.astype(BF16), lo)
        for h in range(hs):
            cars_.append(c_ref[0, h])
            visited = jnp.logical_and(jnp.max(cars_[h], axis=0, keepdims=True) > SB_SKIP_LOG, below)
            n = jnp.sum(visited.astype(jnp.int32), axis=1, keepdims=True)
            seen = n if seen is None else jnp.maximum(seen, n)
        first = qi - jnp.max(seen)

        def blocks(kb, gsums, dqs, mask):
            ks = pl.ds(pl.multiple_of(kb * tb + CA_PAD, tb), tb)
            ko = pl.ds(pl.multiple_of(kb * tb, tb), tb)
            hh = range(hs)
            kk = [k_ref[0, ks, p * LANES:(p + 1) * LANES] for p in range(hs // 2)]
            vv = [v_ref[0, ks, p * LANES:(p + 1) * LANES] for p in range(hs // 2)]
            zs = [_dot(qm[h], kk[h // 2], NT) for h in hh]
            dws = [_dot(dom[h], vv[h // 2], NT) for h in hh]
            raw = [_log_keep(z) for z in zs]
            lks = raw if mask is None else [jnp.where(mask, lk, 0.0) for lk in raw]
            cums = [_split_dot(lk, rev_incl) for lk in lks]
            ws = []
            for h in hh:
                carry = jnp.sum(jnp.where(lane == kb, cars_[h], 0.0), axis=1, keepdims=True)
                logw = zs[h] + cums[h] + carry
                if mask is not None:
                    logw = jnp.where(mask, logw, NEG)
                ws.append(jnp.exp(logw))
            gws = [ws[h] * dws[h] for h in hh]
            gcums = [_split_dot(gws[h], fwd_incl) + gsums[h] for h in hh]
            dzb = []
            for h in hh:
                dz = gws[h] - jnp.exp(zs[h] + raw[h]) * gcums[h]
                if mask is not None:
                    dz = jnp.where(mask, dz, 0.0)
                dzb.append(dz.astype(BF16))
            wb = [w.astype(BF16) for w in ws]
            new_dq = [dqs[h] + _dot(dzb[h], kk[h // 2], NN) for h in hh]
            for p in range(hs // 2):
                cols = slice(p * LANES, (p + 1) * LANES)
                dk_acc[ko, cols] += _dot(dzb[2 * p], qm[2 * p], TN) + _dot(dzb[2 * p + 1], qm[2 * p + 1], TN)
                dv_acc[ko, cols] += _dot(wb[2 * p], dom[2 * p], TN) + _dot(wb[2 * p + 1], dom[2 * p + 1], TN)
            return [g[:, tb - 1:tb] for g in gcums], new_dq

        def step(kb, st):
            return blocks(kb, st[0], st[1], None)

        init = ([jnp.zeros((tb, 1), F32)] * hs, [jnp.zeros((tb, LANES), F32)] * hs)
        gsum, dq = lax.fori_loop(first, qi, step, init)
        _, dq = blocks(qi, gsum, dq, col < row)
        for p, d in enumerate(_merge_pairs(dq, lo)):
            dq_ref[0, :, p * LANES:(p + 1) * LANES] = (d * scale).astype(BF16)

        @pl.when(qi == nq - 1)
        def _():
            dk_ref[0] = dk_acc[...].astype(BF16)
            dv_ref[0] = dv_acc[...].astype(BF16)

    once = pl.Buffered(1)
    qspec = pl.BlockSpec((1, tb, W), lambda g, b, i: (b, i + off, g))
    ospec = pl.BlockSpec((1, tb, W), lambda g, b, i: (b, i, g))
    kvout = pl.BlockSpec((1, S, W), lambda g, b, i: (b, 0, g), pipeline_mode=once)
    return _call(
        body, carry, grid=(ngrp, B, nq), name="sb_bwd", sem=("parallel", "parallel", "arbitrary"),
        args=(h3p, h3p, h3p, o, dmix, gain, cars),
        in_specs=[qspec, pl.BlockSpec((1, CA_PAD + S, W), lambda g, b, i: (b, 0, ngrp + g), pipeline_mode=once),
                  pl.BlockSpec((1, CA_PAD + S, W), lambda g, b, i: (b, 0, 2 * ngrp + g), pipeline_mode=once),
                  ospec, ospec, pl.BlockSpec((1, W), lambda g, b, i: (0, g)),
                  pl.BlockSpec((1, hs, tb, nq), lambda g, b, i: (b, g, i, 0))],
        out_specs=[ospec, kvout, kvout, pl.BlockSpec((1, 1, 1, W), lambda g, b, i: (b, i, 0, g))],
        out_shape=[_sds((B, S, HD), BF16), _sds((B, S, HD), BF16), _sds((B, S, HD), BF16), _sds((B, nq, 1, HD), F32)],
        scratch_shapes=[pltpu.VMEM((S, W), F32), pltpu.VMEM((S, W), F32)])


def _ca_rel_index():
    width = CA_WIN + CA_TQ
    c = np.arange(width)
    dj = np.where(c < CA_WIN, c, c - width)
    return np.clip(CA_PAD - dj, -MAX_REL, MAX_REL) + MAX_REL, width


def _ca_onehot():
    idx, _ = _ca_rel_index()
    return (idx[:, None] == np.arange(2 * MAX_REL + 1)[None, :]).astype(np.float32)


def ca_bias(rel_bias):
    _, width = _ca_rel_index()
    lead = rel_bias.shape[:-1]
    by_offset = jnp.dot(rel_bias, jnp.asarray(_ca_onehot().T), precision=lax.Precision.HIGHEST)
    tile = jnp.broadcast_to(by_offset[..., None, :], lead + (CA_TQ, width)).reshape(lead + (CA_TQ * width,))
    tile = tile[..., :CA_TQ * (width - 1)].reshape(lead + (CA_TQ, width - 1))[..., :CA_WIN]
    t = np.arange(CA_TQ)[:, None] // CHUNK * CHUNK
    j = np.arange(CA_WIN)[None, :]
    return jnp.where((j >= t) & (j < t + CA_PAD + CHUNK), tile, NEG)


def rel_bias_grad(db):
    H = db.shape[0]
    _, width = _ca_rel_index()
    x = jnp.pad(db, ((0, 0), (0, 0), (0, width - 1 - CA_WIN))).reshape(H, CA_TQ * (width - 1))
    x = jnp.pad(x, ((0, 0), (0, CA_TQ))).reshape(H, CA_TQ, width).sum(axis=1)
    return jnp.dot(x, jnp.asarray(_ca_onehot()), precision=lax.Precision.HIGHEST)


def _ca_scores(qm_h, kk, bias_h, valid):
    return jnp.where(valid, _dot(qm_h, kk, NT) + bias_h, NEG)


def _ca_softmax(s):
    e = jnp.exp(s - jnp.max(s, axis=-1, keepdims=True))
    return e * (1.0 / jnp.sum(e, axis=-1, keepdims=True))


def ca_fwd(h3p, bias, gain, carry=None):
    B, S, HD, hs, W, ngrp = _attn_dims(h3p, gain, FWD_HEADS)
    scale = 1.0 / math.sqrt(HEAD_DIM)
    off = CA_PAD // CA_TQ

    def body(q_ref, k_ref, v_ref, b_ref, g_ref, o_ref, m_ref):
        q0 = pl.program_id(2) * CA_TQ
        ks = pl.ds(pl.multiple_of(q0, CA_TQ), CA_WIN)
        lo = _lane_lo(CA_TQ)
        valid = lax.broadcasted_iota(jnp.int32, (CA_TQ, CA_WIN), 1) + q0 >= CA_PAD
        qm, kk, vv = [], [], []
        for p in range(hs // 2):
            cols = slice(p * LANES, (p + 1) * LANES)
            qm += _split_heads((q_ref[0, :, cols].astype(F32) * scale).astype(BF16), lo)
            kk.append(k_ref[0, ks, cols])
            vv.append(v_ref[0, ks, cols])
        ss = [_ca_scores(qm[h], kk[h // 2], b_ref[h], valid) for h in range(hs)]
        ps = [_ca_softmax(s).astype(BF16) for s in ss]
        pv = [_dot(ps[h], vv[h // 2], NN) for h in range(hs)]
        for p, o in enumerate(_merge_pairs(pv, lo)):
            cols = slice(p * LANES, (p + 1) * LANES)
            o_ref[0, :, cols] = o
            m_ref[0, :, cols] = _rms_fwd(o, g_ref[:, cols], lo)

    ospec = pl.BlockSpec((1, CA_TQ, W), lambda g, b, i: (b, i, g))
    return _call(
        body, carry, grid=(ngrp, B, S // CA_TQ), name="ca_fwd", sem=("parallel", "parallel", "arbitrary"),
        args=(h3p, h3p, h3p, bias, gain),
        in_specs=[pl.BlockSpec((1, CA_TQ, W), lambda g, b, i: (b, i + off, 3 * ngrp + g)),
                  pl.BlockSpec((1, CA_PAD + S, W), lambda g, b, i: (b, 0, 4 * ngrp + g)),
                  pl.BlockSpec((1, CA_PAD + S, W), lambda g, b, i: (b, 0, 5 * ngrp + g)),
                  pl.BlockSpec((hs, CA_TQ, CA_WIN), lambda g, b, i: (g, 0, 0)),
                  pl.BlockSpec((1, W), lambda g, b, i: (0, g))],
        out_specs=[ospec, ospec], out_shape=[_sds((B, S, HD), F32), _sds((B, S, HD), BF16)])


def ca_bwd(h3p, bias, o, dmix, gain, carry=None):
    B, S, HD, hs, W, ngrp = _attn_dims(h3p, gain, BWD_HEADS)
    scale = 1.0 / math.sqrt(HEAD_DIM)
    nq = S // CA_TQ
    off = CA_PAD // CA_TQ

    def body(q_ref, k_ref, v_ref, b_ref, o_ref, dm_ref, g_ref, dq_ref, dk_ref, dv_ref, dg_ref, db_ref, dk_acc, dv_acc):
        bi = pl.program_id(1)
        qi = pl.program_id(2)

        @pl.when(qi == 0)
        def _():
            dk_acc[...] = jnp.zeros_like(dk_acc)
            dv_acc[...] = jnp.zeros_like(dv_acc)

        @pl.when(jnp.logical_and(qi == 0, bi == 0))
        def _():
            db_ref[...] = jnp.zeros_like(db_ref)

        q0 = qi * CA_TQ
        ks = pl.ds(pl.multiple_of(q0, CA_TQ), CA_WIN)
        lo = _lane_lo(CA_TQ)
        valid = lax.broadcasted_iota(jnp.int32, (CA_TQ, CA_WIN), 1) + q0 >= CA_PAD
        qm, dom, kk, vv = [], [], [], []
        for p in range(hs // 2):
            cols = slice(p * LANES, (p + 1) * LANES)
            qm += _split_heads((q_ref[0, :, cols].astype(F32) * scale).astype(BF16), lo)
            do, dg = _rms_bwd(o_ref[0, :, cols], dm_ref[0, :, cols], g_ref[:, cols], lo)
            dg_ref[0, 0, :, cols] = dg
            dom += _split_heads(do.astype(BF16), lo)
            kk.append(k_ref[0, ks, cols])
            vv.append(v_ref[0, ks, cols])
        hh = range(hs)
        ss = [_ca_scores(qm[h], kk[h // 2], b_ref[h], valid) for h in hh]
        dps = [_dot(dom[h], vv[h // 2], NT) for h in hh]
        ps = [_ca_softmax(s) for s in ss]
        dss = [ps[h] * (dps[h] - jnp.sum(ps[h] * dps[h], axis=-1, keepdims=True)) for h in hh]
        for h in hh:
            db_ref[h] += dss[h]
        dsb = [d.astype(BF16) for d in dss]
        pb = [p_.astype(BF16) for p_ in ps]
        dq = [_dot(dsb[h], kk[h // 2], NN) for h in hh]
        for p in range(hs // 2):
            cols = slice(p * LANES, (p + 1) * LANES)
            dk_acc[ks, cols] += _dot(dsb[2 * p], qm[2 * p], TN) + _dot(dsb[2 * p + 1], qm[2 * p + 1], TN)
            dv_acc[ks, cols] += _dot(pb[2 * p], dom[2 * p], TN) + _dot(pb[2 * p + 1], dom[2 * p + 1], TN)
        for p, d in enumerate(_merge_pairs(dq, lo)):
            dq_ref[0, :, p * LANES:(p + 1) * LANES] = (d * scale).astype(BF16)

        @pl.when(qi == nq - 1)
        def _():
            dk_ref[0] = dk_acc[CA_PAD:, :].astype(BF16)
            dv_ref[0] = dv_acc[CA_PAD:, :].astype(BF16)

    once = pl.Buffered(1)
    ospec = pl.BlockSpec((1, CA_TQ, W), lambda g, b, i: (b, i, g))
    kvout = pl.BlockSpec((1, S, W), lambda g, b, i: (b, 0, g), pipeline_mode=once)
    bspec = pl.BlockSpec((hs, CA_TQ, CA_WIN), lambda g, b, i: (g, 0, 0))
    return _call(
        body, carry, grid=(ngrp, B, nq), name="ca_bwd", sem=("parallel", "arbitrary", "arbitrary"),
        args=(h3p, h3p, h3p, bias, o, dmix, gain),
        in_specs=[pl.BlockSpec((1, CA_TQ, W), lambda g, b, i: (b, i + off, 3 * ngrp + g)),
                  pl.BlockSpec((1, CA_PAD + S, W), lambda g, b, i: (b, 0, 4 * ngrp + g), pipeline_mode=once),
                  pl.BlockSpec((1, CA_PAD + S, W), lambda g, b, i: (b, 0, 5 * ngrp + g), pipeline_mode=once),
                  bspec, ospec, pl.BlockSpec((1, CA_TQ, W), lambda g, b, i: (b, i, ngrp + g)),
                  pl.BlockSpec((1, W), lambda g, b, i: (0, g))],
        out_specs=[ospec, kvout, kvout, pl.BlockSpec((1, 1, 1, W), lambda g, b, i: (b, i, 0, g)), bspec],
        out_shape=[_sds((B, S, HD), BF16), _sds((B, S, HD), BF16), _sds((B, S, HD), BF16),
                   _sds((B, nq, 1, HD), F32), _sds((HD // HEAD_DIM, CA_TQ, CA_WIN), F32)],
        scratch_shapes=[pltpu.VMEM((CA_PAD + S, W), F32), pltpu.VMEM((CA_PAD + S, W), F32)])


_ANY = pl.BlockSpec(memory_space=pl.ANY)
_MESH = pl.DeviceIdType.MESH


def _mesh_pos():
    return lax.axis_index("x"), lax.axis_index("y"), lax.axis_index("c")


def _dev_index(p):
    return 4 * p[0] + 2 * p[1] + p[2]


def _flip(pos, k):
    return tuple(1 - v if (k >> (2 - a)) & 1 else v for a, v in enumerate(pos))


class _Plan:
    def __init__(self, operands, n_matrices):
        self.operands = list(operands)
        self.n_ops = len(self.operands)
        self.nm = n_matrices
        self.scratch = [pltpu.SemaphoreType.DMA((7, self.nm)), pltpu.SemaphoreType.DMA((7, self.nm)),
                        pltpu.SemaphoreType.DMA((self.nm,))]


class _Gather(_Plan):
    def __init__(self, mats):
        super().__init__(mats, len(mats))
        self.rows = [m.shape[0] for m in mats]
        self.out_shape = [_sds((N_DEV * m.shape[0], m.shape[1]), m.dtype) for m in mats]

    def begin(self, ins, outs, sems):
        nm, rows = self.nm, self.rows
        send_sems, recv_sems, local_sems = sems
        x, y, c = _mesh_pos()
        me, sibling = (x, y, c), (x, y, 1 - c)
        chips = [(1 - x, y), (x, 1 - y), (1 - x, 1 - y)]

        def block(m, p):
            start = pl.multiple_of(_dev_index(p) * rows[m], 16)
            return outs[m].at[pl.ds(start, rows[m]), :]

        def copy(k, m, blk, to, src=None):
            return pltpu.make_async_remote_copy(
                src_ref=block(m, blk) if src is None else src, dst_ref=block(m, blk),
                send_sem=send_sems.at[k, m], recv_sem=recv_sems.at[k, m], device_id=to, device_id_type=_MESH)

        def mine():
            return [pltpu.make_async_copy(ins[m], block(m, me), local_sems.at[m]) for m in range(nm)]

        def first():
            own = [copy(0, m, me, sibling, src=ins[m]) for m in range(nm)]
            return own + [copy(1 + j, m, me, (*chip, c), src=ins[m]) for j, chip in enumerate(chips) for m in range(nm)]

        def start():
            for cp in mine() + first():
                cp.start()

        def passed():
            return [copy(4 + j, m, (*chip, c), sibling) for j, chip in enumerate(chips) for m in range(nm)]

        def pass_on():
            for j, chip in enumerate(chips):
                for m in range(nm):
                    copy(1 + j, m, (*chip, c), me).wait_recv()
                    copy(4 + j, m, (*chip, c), sibling).start()

        def finish():
            for m in range(nm):
                copy(0, m, sibling, me).wait_recv()
            for j, chip in enumerate(chips):
                for m in range(nm):
                    copy(4 + j, m, (*chip, 1 - c), me).wait_recv()
            for cp in first() + passed():
                cp.wait_send()
            for cp in mine():
                cp.wait()

        return start, pass_on, finish


class _Exchange(_Plan):
    def __init__(self, grads):
        self.where = [(i, j) for i, g in enumerate(grads) for j in (range(g.shape[0]) if g.ndim == 3 else [None])]
        super().__init__(grads, len(self.where))
        self.rows = [grads[i].shape[-2] // N_DEV for i, _ in self.where]
        self.out_shape = [_sds(g.shape[:-2] + (N_DEV, g.shape[-2] // N_DEV, g.shape[-1]), g.dtype) for g in grads]

    def begin(self, ins, outs, sems):
        nm, rows = self.nm, self.rows
        send_sems, recv_sems, local_sems = sems
        me = _mesh_pos()
        my = _dev_index(me)

        def piece(m, idx):
            i, j = self.where[m]
            rows_ = pl.ds(pl.multiple_of(idx * rows[m], 16), rows[m])
            return ins[i].at[rows_, :] if j is None else ins[i].at[j, rows_, :]

        def landing(m, slot):
            i, j = self.where[m]
            return outs[i].at[slot] if j is None else outs[i].at[j, slot]

        def copy(k, m, src_idx, slot, to):
            return pltpu.make_async_remote_copy(
                src_ref=piece(m, src_idx), dst_ref=landing(m, slot),
                send_sem=send_sems.at[k - 1, m], recv_sem=recv_sems.at[k - 1, m], device_id=to, device_id_type=_MESH)

        def mine():
            return [pltpu.make_async_copy(piece(m, my), landing(m, my), local_sems.at[m]) for m in range(nm)]

        def sends():
            return [copy(k, m, _dev_index(_flip(me, k)), my, _flip(me, k)) for k in range(1, N_DEV) for m in range(nm)]

        def start():
            for cp in mine() + sends():
                cp.start()

        def finish():
            for k in range(1, N_DEV):
                peer = _flip(me, k)
                for m in range(nm):
                    copy(k, m, my, _dev_index(peer), peer).wait_recv()
            for cp in sends():
                cp.wait_send()
            for cp in mine():
                cp.wait()

        return start, None, finish


def _run_plan(plan, name):
    nm = plan.n_ops

    def body(*refs):
        start, middle, finish = plan.begin(refs[:nm], refs[nm:2 * nm], refs[2 * nm:])
        start()
        if middle is not None:
            middle()
        finish()

    return pl.pallas_call(body, name=name, in_specs=[_ANY] * nm, out_specs=[_ANY] * nm, out_shape=plan.out_shape,
                          scratch_shapes=plan.scratch)(*plan.operands)


def _call(body, carry, *, grid, in_specs, out_specs, out_shape, args, name, sem, scratch_shapes=()):
    if carry is None:
        outs = pl.pallas_call(body, grid=grid, name=name, in_specs=in_specs, out_specs=out_specs, out_shape=out_shape,
                              scratch_shapes=list(scratch_shapes), compiler_params=_params(*sem))(*args)
        return outs, None
    n_in, n_out, n_sc, nm = len(in_specs), len(out_specs), len(scratch_shapes), carry.n_ops

    def carrier(*refs):
        refs = list(refs)
        own_in, refs = refs[:n_in], refs[n_in:]
        c_in, refs = refs[:nm], refs[nm:]
        own_out, refs = refs[:n_out], refs[n_out:]
        c_out, refs = refs[:nm], refs[nm:]
        own_sc, c_sc = refs[:n_sc], refs[n_sc:]
        start, middle, finish = carry.begin(c_in, c_out, c_sc)
        step, steps = 0, 1
        for axis, n in enumerate(grid):
            step = step * n + pl.program_id(axis)
            steps *= n
        pl.when(step == 0)(start)
        body(*own_in, *own_out, *own_sc)

        @pl.when(step == steps - 1)
        def _():
            if middle is not None:
                middle()
            finish()

    outs = pl.pallas_call(
        carrier, grid=grid, name=name + "_carry", in_specs=list(in_specs) + [_ANY] * nm,
        out_specs=list(out_specs) + [_ANY] * nm, out_shape=list(out_shape) + carry.out_shape,
        scratch_shapes=list(scratch_shapes) + carry.scratch,
        compiler_params=_params(*["arbitrary"] * len(grid)),
    )(*args, *carry.operands)
    return outs[:n_out], outs[n_out:]


def sum_slots(r, j=None):
    n, R, D = r.shape[-3:]
    tc = _row_tile(D, 256)

    def body(r_ref, o_ref):
        slot = (lambda s: r_ref[s]) if j is None else (lambda s: r_ref[0, s])
        acc = slot(0).astype(F32)
        for s in range(1, n):
            acc = acc + slot(s).astype(F32)
        o_ref[...] = acc

    spec = (pl.BlockSpec((n, R, tc), lambda i: (0, 0, i)) if j is None
            else pl.BlockSpec((1, n, R, tc), lambda i: (j, 0, 0, i)))
    return pl.pallas_call(
        body, grid=(D // tc,), name="sum_slots", in_specs=[spec],
        out_specs=pl.BlockSpec((R, tc), lambda i: (0, i)), out_shape=_sds((R, D), F32),
        compiler_params=_params("parallel"),
    )(r)


def allreduce_small(v):
    R, C = v.shape

    def body(v_ref, o_ref, buf, send_sems, recv_sems):
        me = _mesh_pos()
        my = _dev_index(me)
        buf[my] = v_ref[...]

        def copy(k, slot, to):
            return pltpu.make_async_remote_copy(
                src_ref=v_ref, dst_ref=buf.at[slot], send_sem=send_sems.at[k - 1], recv_sem=recv_sems.at[k - 1],
                device_id=to, device_id_type=_MESH)

        sends = []
        for k in range(1, N_DEV):
            cp = copy(k, my, _flip(me, k))
            cp.start()
            sends.append(cp)
        for k in range(1, N_DEV):
            peer = _flip(me, k)
            copy(k, _dev_index(peer), peer).wait_recv()
        acc = buf[0]
        for s in range(1, N_DEV):
            acc = acc + buf[s]
        o_ref[...] = acc
        for cp in sends:
            cp.wait_send()

    vm = pl.BlockSpec(memory_space=pltpu.VMEM)
    return pl.pallas_call(
        body, name="allreduce_small", in_specs=[vm], out_specs=vm, out_shape=_sds((R, C), F32),
        scratch_shapes=[pltpu.VMEM((N_DEV, R, C), F32), pltpu.SemaphoreType.DMA((7,)), pltpu.SemaphoreType.DMA((7,))],
    )(v)


WEIGHTS = ["ffn1_w_gate", "ffn1_w_up", "ffn1_w_down", "ln1_g", "ln1_b", "w_in", "rel_bias", "sb_out_g", "ca_out_g",
           "w_out", "ln2_g", "ln2_b", "ffn2_w_gate", "ffn2_w_up", "ffn2_w_down", "ln3_g", "ln3_b"]
BIG = ["ffn1_w_gate", "ffn1_w_up", "ffn2_w_gate", "ffn2_w_up", "w_in", "ffn1_w_down", "ffn2_w_down", "w_out"]
TRANSPOSED = BIG[:5]
SMALL = [n for n in WEIGHTS if n not in BIG]


def _pack(vals):
    flat = jnp.concatenate([v.reshape(-1).astype(F32) for v in vals])
    pad = -flat.shape[0] % (8 * 128)
    return jnp.pad(flat, (0, pad)).reshape(-1, 128)


def _unpack(packed, like):
    flat = packed.reshape(-1)
    out, off = [], 0
    for v in like:
        out.append(flat[off:off + v.size].reshape(v.shape))
        off += v.size
    return out


def _row_blocks(w, n):
    r = w.shape[0] // n
    return [w[i * r:(i + 1) * r] for i in range(n)]


class _Arrivals:
    def __init__(self):
        self.by_kernel = {}

    def ride(self, kernel_name, weights, names, local, layer):
        self.by_kernel[kernel_name] = (weights, names, _Gather([local[n][layer] for n in names]))

    def plan(self, kernel_name):
        return self.by_kernel[kernel_name][2] if kernel_name in self.by_kernel else None

    def landed(self, kernel_name, outs):
        if kernel_name in self.by_kernel:
            weights, names, _ = self.by_kernel[kernel_name]
            weights.update(zip(names, outs))


def _layer_fwd(x, xb, W, P, bias, l, B, S, arrivals):
    T = B * S
    sv = {"xb": xb, "bias": bias}
    (sv["sg1"], sv["t1"], sv["a1"]), got = ffn_up(xb, W["ffn1_w_gate"], W["ffn1_w_up"], carry=arrivals.plan("ffn_up"))
    arrivals.landed("ffn_up", got)
    sv["u1"], x1, sv["x1b"] = mm_nn_ln([(sv["a1"], W["ffn1_w_down"])], x, P["ln1_g"][l], P["ln1_b"][l], FFN_RESIDUAL,
                                       "ffn_down_ln")

    sv["h"] = proj_in(sv["x1b"], W["w_in"], B, S)
    sv["gA"] = P["sb_out_g"][l].reshape(1, -1)
    sv["gB"] = P["ca_out_g"][l].reshape(1, -1)
    (sv["oa"], ma, sv["cars"]), got = sb_fwd(sv["h"], sv["gA"], carry=arrivals.plan("sb_fwd"))
    arrivals.landed("sb_fwd", got)
    (sv["ob"], mb), got = ca_fwd(sv["h"], bias, sv["gB"], carry=arrivals.plan("ca_fwd"))
    arrivals.landed("ca_fwd", got)
    sv["ma"], sv["mb"] = ma.reshape(T, -1), mb.reshape(T, -1)
    sv["u2"], x2, sv["x2b"] = mm_nn_ln(list(zip([sv["ma"], sv["mb"]], _row_blocks(W["w_out"], 2))), x1,
                                       P["ln2_g"][l], P["ln2_b"][l], 1.0, "proj_out_ln")

    (sv["sg2"], sv["t2"], sv["a2"]), _ = ffn_up(sv["x2b"], W["ffn2_w_gate"], W["ffn2_w_up"])
    sv["u3"], x3, x3b = mm_nn_ln([(sv["a2"], W["ffn2_w_down"])], x2, P["ln3_g"][l], P["ln3_b"][l], FFN_RESIDUAL,
                                 "ffn_down_ln")
    return x3, x3b, sv


class _Riders:
    def __init__(self):
        self.received = []

    def plan(self, entries):
        return _Exchange([e[2] for e in entries]) if entries else None

    def landed(self, entries, slots):
        for (names, layer, _), r in zip(entries, slots or []):
            for j, name in enumerate(names):
                self.received.append((name, layer, r, j if r.ndim == 4 else None))


def _ffn_bwd(dr, dxa, xb, sg, t, a, wg, wu, wd, riders, ride, below, names, layer):
    dhg, dhu = ffn_bwd_mid(dr, wd, sg, t)
    pairs = [(dhg, wg), (dhu, wu)]
    if below is not None:
        down = mm_nn_ln_bwd(pairs, dxa, *below, name="ffn_dx_ln")
        grads, slots = ffn_dw(dhg, dhu, a, xb, dr, carry=riders.plan(ride))
        riders.landed(ride, slots)
        return down, [(names, layer, grads)]
    down = mm_nn(pairs, add=dxa, name="ffn_dx")
    for name, lhs, rhs in zip(names, (dhg, dhu, a), (xb, xb, dr)):
        g, slots = mm_tn([lhs], rhs, name="ffn_dw_last", carry=riders.plan(ride))
        riders.landed(ride, slots)
        ride = [((name,), layer, g)]
    return down, ride


def _layer_bwd(dxa, dr, sv, W, P, l, B, S, riders, pending, below):
    T = B * S
    G = {}
    (dxa, dr, dg, db), ride = _ffn_bwd(
        dr, dxa, sv["x2b"], sv["sg2"], sv["t2"], sv["a2"], W["ffn2_w_gate"], W["ffn2_w_up"], W["ffn2_w_down"],
        riders, [], (sv["u2"], P["ln2_g"][l], 1.0), ("ffn2_w_gate", "ffn2_w_up", "ffn2_w_down"), l)
    G["ln2_g"], G["ln2_b"] = dg.sum(0), db.sum(0)
    dmix = mm_nt(dr, W["w_out"], name="proj_out_dx").reshape(B, S, -1)
    g_out = mm_tn([sv["ma"], sv["mb"]], dr, name="proj_out_dw")[0]
    (dqa, dka, dva, dga), slots = sb_bwd(sv["h"], sv["oa"], dmix, sv["gA"], sv["cars"], carry=riders.plan(pending))
    riders.landed(pending, slots)
    (dqb, dkb, dvb, dgb, dbias), slots = ca_bwd(sv["h"], sv["bias"], sv["ob"], dmix, sv["gB"], carry=riders.plan(ride))
    riders.landed(ride, slots)
    G["sb_out_g"] = dga.sum((0, 1, 2))
    G["ca_out_g"] = dgb.sum((0, 1, 2))
    G["rel_bias"] = rel_bias_grad(dbias)
    dh = [t.reshape(T, -1) for t in (dqa, dka, dva, dqb, dkb, dvb)]
    dxa, dr, dg, db = mm_nn_ln_bwd(list(zip(dh, _row_blocks(W["w_in"], 6))), dxa, sv["u1"], P["ln1_g"][l], FFN_RESIDUAL,
                                   name="proj_in_dx_ln")
    G["ln1_g"], G["ln1_b"] = dg.sum(0), db.sum(0)
    g_in = mm_tn(dh, sv["x1b"], name="proj_in_dw")[0]

    proj = [(("w_in",), l, g_in), (("w_out",), l, g_out)]
    down, pending = _ffn_bwd(dr, dxa, sv["xb"], sv["sg1"], sv["t1"], sv["a1"], W["ffn1_w_gate"], W["ffn1_w_up"],
                             W["ffn1_w_down"], riders, [] if below is not None else proj, below,
                             ("ffn1_w_gate", "ffn1_w_up", "ffn1_w_down"), l)
    return down, G, pending + proj if below is not None else pending


def kernel(x, ffn1_w_gate, ffn1_w_up, ffn1_w_down, ln1_g, ln1_b, w_in, rel_bias, sb_out_g, ca_out_g, w_out, ln2_g, ln2_b, ffn2_w_gate, ffn2_w_up, ffn2_w_down, ln3_g, ln3_b, loss_target, m_ffn1_w_gate, m_ffn1_w_up, m_ffn1_w_down, m_ln1_g, m_ln1_b, m_w_in, m_rel_bias, m_sb_out_g, m_ca_out_g, m_w_out, m_ln2_g, m_ln2_b, m_ffn2_w_gate, m_ffn2_w_up, m_ffn2_w_down, m_ln3_g, m_ln3_b, v_ffn1_w_gate, v_ffn1_w_up, v_ffn1_w_down, v_ln1_g, v_ln1_b, v_w_in, v_rel_bias, v_sb_out_g, v_ca_out_g, v_w_out, v_ln2_g, v_ln2_b, v_ffn2_w_gate, v_ffn2_w_up, v_ffn2_w_down, v_ln3_g, v_ln3_b):
    given = dict(locals())
    P = {n: given[n] for n in WEIGHTS}
    M = {n: given["m_" + n] for n in WEIGHTS}
    V = {n: given["v_" + n] for n in WEIGHTS}
    B, S, D = x.shape
    L = ln1_g.shape[0]

    local = {n: (jnp.swapaxes(P[n], 1, 2) if n in TRANSPOSED else P[n]).astype(BF16) for n in BIG}
    bias = ca_bias(rel_bias)

    xf = x.reshape(B * S, D)
    xb = xf.astype(BF16)
    W, saved = [{} for _ in range(L)], []
    first = ["ffn1_w_gate", "ffn1_w_up"]
    W[0].update(zip(first, _run_plan(_Gather([local[n][0] for n in first]), "gather_weights")))
    for l in range(L):
        arrivals = _Arrivals()
        if l == 0:
            arrivals.ride("ffn_up", W[0], ["ffn1_w_down", "w_in", "w_out"], local, 0)
            arrivals.ride("ca_fwd", W[0], ["ffn2_w_gate", "ffn2_w_up", "ffn2_w_down"], local, 0)
            if L > 1:
                arrivals.ride("sb_fwd", W[1], BIG, local, 1)
        elif l + 1 < L:
            arrivals.ride("sb_fwd", W[l + 1], BIG[:5], local, l + 1)
            arrivals.ride("ca_fwd", W[l + 1], BIG[5:], local, l + 1)
        xf, xb, sv = _layer_fwd(xf, xb, W[l], P, bias[l], l, B, S, arrivals)
        saved.append(sv)
    dy, loss_part = loss_head(xf, loss_target.reshape(B * S, D))

    big_g = {n: [None] * L for n in BIG}
    small_g = {n: [None] * L for n in SMALL}
    riders, pending = _Riders(), []
    down = ln_bwd(saved[L - 1]["u3"], dy, ln3_g[L - 1], FFN_RESIDUAL)
    for l in reversed(range(L)):
        dxa, dr, dg, db = down
        small_g["ln3_g"][l], small_g["ln3_b"][l] = dg.sum(0), db.sum(0)
        below = (saved[l - 1]["u3"], ln3_g[l - 1], FFN_RESIDUAL) if l > 0 else None
        down, G, pending = _layer_bwd(dxa, dr, saved[l], W[l], P, l, B, S, riders, pending, below)
        for n, g in G.items():
            small_g[n][l] = g
    dx = down
    riders.landed(pending, _run_plan(riders.plan(pending), "exchange_grads"))
    for n, l, slots, j in riders.received:
        big_g[n][l] = sum_slots(slots, j)

    grads = {}
    for n in BIG:
        g = jnp.stack(big_g[n])
        grads[n] = jnp.swapaxes(g, 1, 2) if n in TRANSPOSED else g
    small_like = [P[n] for n in SMALL]
    packed = _pack([jnp.stack(small_g[n]) for n in SMALL] + [loss_part.sum()])
    total = allreduce_small(packed)
    *small_vals, loss = _unpack(total, small_like + [jnp.zeros((), F32)])
    grads.update(dict(zip(SMALL, small_vals)))

    delta, new_m, new_v = {}, {}, {}
    for n in BIG:
        shape = P[n].shape
        two_d = lambda a: a.reshape(shape[0] * shape[1], shape[2])
        d, m, v = adamw(two_d(P[n]), two_d(grads[n]), two_d(M[n]), two_d(V[n]))
        delta[n], new_m[n], new_v[n] = d.reshape(shape), m.reshape(shape), v.reshape(shape)
    one = jnp.ones((), F32)
    d, m, v = adamw(_pack(small_like + [one]), total, _pack([M[n] for n in SMALL] + [one]), _pack([V[n] for n in SMALL] + [one]))
    for dst, src in ((delta, d), (new_m, m), (new_v, v)):
        dst.update(dict(zip(SMALL, _unpack(src, small_like))))

    return (loss, dx.reshape(B, S, D), *[grads[n] for n in WEIGHTS], *[delta[n] for n in WEIGHTS],
            *[new_m[n] for n in WEIGHTS], *[new_v[n] for n in WEIGHTS])
```

```python
import math

import jax
import jax.numpy as jnp
import numpy as np
from jax import lax
from jax.experimental import pallas as pl
from jax.experimental.pallas import tpu as pltpu

F32 = jnp.float32
BF16 = jnp.bfloat16

HEAD_DIM = 64
CHUNK = 64
N_PREV_CHUNKS = 8
MAX_REL = 128
DEPTH = 4
FFN_RESIDUAL = 0.5
ALPHA = (2 * DEPTH) ** 0.25
LN_EPS = 1e-5
RMS_EPS = 1e-6
ADAM_LR = 0.001
ADAM_B1 = 0.9
ADAM_B2 = 0.999
ADAM_EPS = 1e-08
ADAM_WD = 0.01
ADAM_STEP = 10

N_DEV = 8
LANES = 128
VMEM_LIMIT_BYTES = 56 * 1024 * 1024
ROW_TILE = 512
SB_BLOCK = 128
CA_PAD = CHUNK * N_PREV_CHUNKS
CA_TQ = 2 * CHUNK
CA_WIN = CA_PAD + CA_TQ
CA_NEAR = CA_WIN - (CA_PAD - MAX_REL)
FWD_HEADS = 8
BWD_HEADS = 8
NEG = -1e30
SB_SKIP_LOG = -104.0

NN = ((1,), (0,))
NT = ((1,), (1,))
TN = ((0,), (0,))


def _dot(a, b, dims):
    return lax.dot_general(a, b, (dims, ((), ())), preferred_element_type=F32)


def _params(*sem):
    return pltpu.CompilerParams(dimension_semantics=sem, vmem_limit_bytes=VMEM_LIMIT_BYTES)


def _sds(shape, dtype):
    return jax.ShapeDtypeStruct(shape, dtype)


def _row_tile(n, want=ROW_TILE):
    t = min(want, n)
    while n % t:
        t //= 2
    assert t >= 8, (n, want)
    return t


def ffn_up(xb, wg, wu, carry=None):
    T, D = xb.shape
    F = wg.shape[0]
    tm = _row_tile(T)
    fc = _row_tile(F, 256)

    def body(x_ref, wg_ref, wu_ref, sg_ref, t_ref, a_ref):
        x = x_ref[...]
        for j in range(F // fc):
            sl = pl.ds(j * fc, fc)
            hg = _dot(x, wg_ref[sl, :], NT)
            hu = _dot(x, wu_ref[sl, :], NT)
            s = jax.nn.sigmoid(hg)
            sg = hg * s
            a = sg * hu
            sg_ref[:, sl] = sg.astype(BF16)
            t_ref[:, sl] = (a + s * (hu - a)).astype(BF16)
            a_ref[:, sl] = a.astype(BF16)

    row = pl.BlockSpec((tm, F), lambda i: (i, 0))
    w = pl.BlockSpec((F, D), lambda i: (0, 0))
    return _call(body, carry, grid=(T // tm,), name="ffn_up", sem=("parallel",), args=(xb, wg, wu),
                 in_specs=[pl.BlockSpec((tm, D), lambda i: (i, 0)), w, w],
                 out_specs=[row, row, row], out_shape=[_sds((T, F), BF16)] * 3)


def ffn_bwd_mid(dr, wd, da_dhu, da_dhg):
    T, D = dr.shape
    F = wd.shape[0]
    tm = _row_tile(T)
    fc = _row_tile(F, 256)

    def body(dr_ref, wd_ref, sg_ref, t_ref, dhg_ref, dhu_ref):
        dr_ = dr_ref[...]
        for j in range(F // fc):
            sl = pl.ds(j * fc, fc)
            da = _dot(dr_, wd_ref[sl, :], NT)
            dhu_ref[:, sl] = (da * sg_ref[:, sl].astype(F32)).astype(BF16)
            dhg_ref[:, sl] = (da * t_ref[:, sl].astype(F32)).astype(BF16)

    row = pl.BlockSpec((tm, F), lambda i: (i, 0))
    return pl.pallas_call(
        body, grid=(T // tm,), name="ffn_bwd_mid",
        in_specs=[pl.BlockSpec((tm, D), lambda i: (i, 0)), pl.BlockSpec((F, D), lambda i: (0, 0)), row, row],
        out_specs=[row, row], out_shape=[_sds((T, F), BF16)] * 2,
        compiler_params=_params("parallel"),
    )(dr, wd, da_dhu, da_dhg)


def mm_nn(pairs, add=None, out_dtype=F32, name="mm_nn"):
    M = pairs[0][0].shape[0]
    N = pairs[0][1].shape[1]
    tm = _row_tile(M)
    nc = _row_tile(N, 512)
    n_pairs = len(pairs)

    def body(*refs):
        a_refs = refs[:n_pairs]
        b_refs = refs[n_pairs:2 * n_pairs]
        add_ref = refs[2 * n_pairs] if add is not None else None
        o_ref = refs[-1]
        for j in range(N // nc):
            sl = pl.ds(j * nc, nc)
            acc = _dot(a_refs[0][...], b_refs[0][:, sl], NN)
            for a_ref, b_ref in zip(a_refs[1:], b_refs[1:]):
                acc = acc + _dot(a_ref[...], b_ref[:, sl], NN)
            if add_ref is not None:
                acc = acc + add_ref[:, sl]
            o_ref[:, sl] = acc.astype(out_dtype)

    in_specs = [pl.BlockSpec((tm, a.shape[1]), lambda i: (i, 0)) for a, _ in pairs]
    in_specs += [pl.BlockSpec(b.shape, lambda i: (0, 0)) for _, b in pairs]
    args = [a for a, _ in pairs] + [b for _, b in pairs]
    if add is not None:
        in_specs.append(pl.BlockSpec((tm, N), lambda i: (i, 0)))
        args.append(add)
    return pl.pallas_call(
        body, grid=(M // tm,), name=name, in_specs=in_specs,
        out_specs=pl.BlockSpec((tm, N), lambda i: (i, 0)), out_shape=_sds((M, N), out_dtype),
        compiler_params=_params("parallel"),
    )(*args)


def _mm_specs(pairs, tm):
    specs = [pl.BlockSpec((tm, a.shape[1]), lambda i: (i, 0)) for a, _ in pairs]
    specs += [pl.BlockSpec(b.shape, lambda i: (0, 0), pipeline_mode=pl.Buffered(1)) for _, b in pairs]
    return specs, [a for a, _ in pairs] + [b for _, b in pairs]


def _mm_sum(a_refs, b_refs):
    acc = _dot(a_refs[0][...], b_refs[0][...], NN)
    for a_ref, b_ref in zip(a_refs[1:], b_refs[1:]):
        acc = acc + _dot(a_ref[...], b_ref[...], NN)
    return acc


def _ln_fwd_rows(u, g, b):
    mu = jnp.mean(u, axis=-1, keepdims=True)
    xc = u - mu
    var = jnp.mean(xc * xc, axis=-1, keepdims=True)
    return xc * lax.rsqrt(var + LN_EPS) * g + b


def _ln_bwd_rows(u, dy, g):
    mu = jnp.mean(u, axis=-1, keepdims=True)
    xc = u - mu
    var = jnp.mean(xc * xc, axis=-1, keepdims=True)
    rstd = lax.rsqrt(var + LN_EPS)
    xh = xc * rstd
    dxh = dy * g
    m1 = jnp.mean(dxh, axis=-1, keepdims=True)
    m2 = jnp.mean(dxh * xh, axis=-1, keepdims=True)
    return rstd * (dxh - m1 - xh * m2), dy * xh


def mm_nn_ln(pairs, x, g, b, res_scale, name):
    M, N = x.shape
    tm = _row_tile(M)
    n_pairs = len(pairs)

    def body(*refs):
        a_refs, b_refs = refs[:n_pairs], refs[n_pairs:2 * n_pairs]
        x_ref, g_ref, b_ref, u_ref, y_ref, yb_ref = refs[2 * n_pairs:]
        u = ALPHA * x_ref[...] + res_scale * _mm_sum(a_refs, b_refs)
        y = _ln_fwd_rows(u, g_ref[...], b_ref[...])
        u_ref[...] = u
        y_ref[...] = y
        yb_ref[...] = y.astype(BF16)

    in_specs, args = _mm_specs(pairs, tm)
    row = pl.BlockSpec((tm, N), lambda i: (i, 0))
    vec = pl.BlockSpec((1, N), lambda i: (0, 0))
    return pl.pallas_call(
        body, grid=(M // tm,), name=name, in_specs=in_specs + [row, vec, vec],
        out_specs=[row, row, row], out_shape=[_sds((M, N), F32), _sds((M, N), F32), _sds((M, N), BF16)],
        compiler_params=_params("parallel"),
    )(*args, x, g.reshape(1, N), b.reshape(1, N))


def mm_nn_ln_bwd(pairs, add, u, g, branch_scale, name):
    M, N = u.shape
    tm = _row_tile(M)
    n_pairs = len(pairs)

    def body(*refs):
        a_refs, b_refs = refs[:n_pairs], refs[n_pairs:2 * n_pairs]
        add_ref, u_ref, g_ref, dxa_ref, dr_ref, dg_ref, db_ref = refs[2 * n_pairs:]

        @pl.when(pl.program_id(0) == 0)
        def _():
            dg_ref[...] = jnp.zeros_like(dg_ref)
            db_ref[...] = jnp.zeros_like(db_ref)

        dy = _mm_sum(a_refs, b_refs) + add_ref[...]
        du, dyxh = _ln_bwd_rows(u_ref[...], dy, g_ref[...])
        dxa_ref[...] = ALPHA * du
        dr_ref[...] = (branch_scale * du).astype(BF16)
        dg_ref[...] += jnp.sum(dyxh.reshape(tm // 8, 8, N), axis=0)
        db_ref[...] += jnp.sum(dy.reshape(tm // 8, 8, N), axis=0)

    in_specs, args = _mm_specs(pairs, tm)
    row = pl.BlockSpec((tm, N), lambda i: (i, 0))
    acc = pl.BlockSpec((8, N), lambda i: (0, 0))
    return pl.pallas_call(
        body, grid=(M // tm,), name=name, in_specs=in_specs + [row, row, pl.BlockSpec((1, N), lambda i: (0, 0))],
        out_specs=[row, row, acc, acc],
        out_shape=[_sds((M, N), F32), _sds((M, N), BF16), _sds((8, N), F32), _sds((8, N), F32)],
        compiler_params=_params("arbitrary"),
    )(*args, add, u, g.reshape(1, N))


def mm_nt(a, b, out_dtype=F32, name="mm_nt"):
    M, K = a.shape
    N = b.shape[0]
    tm = _row_tile(M)
    nc = _row_tile(N, 512)

    def body(a_ref, b_ref, o_ref):
        a_ = a_ref[...]
        for j in range(N // nc):
            sl = pl.ds(j * nc, nc)
            o_ref[:, sl] = _dot(a_, b_ref[sl, :], NT).astype(out_dtype)

    return pl.pallas_call(
        body, grid=(M // tm,), name=name,
        in_specs=[pl.BlockSpec((tm, K), lambda i: (i, 0)), pl.BlockSpec((N, K), lambda i: (0, 0))],
        out_specs=pl.BlockSpec((tm, N), lambda i: (i, 0)), out_shape=_sds((M, N), out_dtype),
        compiler_params=_params("parallel"),
    )(a, b)


def proj_in(xb, w, B, S):
    D = xb.shape[1]
    N = w.shape[0]
    tm = CA_PAD
    nblk = S // tm
    nc = _row_tile(N, 512)
    assert S % tm == 0

    def body(x_ref, w_ref, o_ref):
        i = pl.program_id(1)

        @pl.when(i == 0)
        def _():
            o_ref[...] = jnp.zeros_like(o_ref)

        @pl.when(i > 0)
        def _():
            x = x_ref[...]
            for j in range(N // nc):
                sl = pl.ds(j * nc, nc)
                o_ref[0, :, sl] = _dot(x, w_ref[sl, :], NT).astype(BF16)

    return pl.pallas_call(
        body, grid=(B, nblk + 1), name="proj_in",
        in_specs=[pl.BlockSpec((tm, D), lambda b, i: (b * nblk + jnp.maximum(i - 1, 0), 0)),
                  pl.BlockSpec((N, D), lambda b, i: (0, 0))],
        out_specs=pl.BlockSpec((1, tm, N), lambda b, i: (b, i, 0)), out_shape=_sds((B, CA_PAD + S, N), BF16),
        compiler_params=_params("parallel", "arbitrary"),
    )(xb, w)


def mm_tn(parts, b, name="mm_tn", carry=None):
    T, N = b.shape
    widths = [a.shape[1] for a in parts]
    M = sum(widths)
    tk = _row_tile(T)
    steps = T // tk
    n_parts = len(parts)

    def body(*refs):
        a_refs, b_ref, o_ref, acc_ref = refs[:n_parts], refs[n_parts], refs[n_parts + 1], refs[n_parts + 2]
        i = pl.program_id(0)

        @pl.when(i == 0)
        def _():
            acc_ref[...] = jnp.zeros_like(acc_ref)

        b_ = b_ref[...]
        row = 0
        for a_ref, width in zip(a_refs, widths):
            mc = _row_tile(width, 256)
            for j in range(width // mc):
                acc_ref[pl.ds(row, mc), :] += _dot(a_ref[:, pl.ds(j * mc, mc)], b_, TN)
                row += mc

        @pl.when(i == steps - 1)
        def _():
            o_ref[...] = acc_ref[...].astype(BF16)

    (out,), got = _call(
        body, carry, grid=(steps,), name=name, sem=("arbitrary",), args=(*parts, b),
        in_specs=[pl.BlockSpec((tk, w), lambda i: (i, 0)) for w in widths] + [pl.BlockSpec((tk, N), lambda i: (i, 0))],
        out_specs=[pl.BlockSpec((M, N), lambda i: (0, 0))], out_shape=[_sds((M, N), BF16)],
        scratch_shapes=[pltpu.VMEM((M, N), F32)])
    return out, got


def ffn_dw(dhg, dhu, a, xb, dr, carry=None):
    T, F = dhg.shape
    D = xb.shape[1]
    tk = _row_tile(T)
    mc = _row_tile(F, 256)
    steps = T // tk
    last = steps - 1

    def body(dhg_ref, dhu_ref, a_ref, xb_ref, dr_ref, o_ref, acc_ref):
        p = pl.program_id(0)
        i = pl.program_id(1)

        @pl.when(i == 0)
        def _():
            acc_ref[...] = jnp.zeros_like(acc_ref)

        def accumulate(lhs_ref, rhs_ref):
            rhs = rhs_ref[...]
            for j in range(F // mc):
                sl = pl.ds(j * mc, mc)
                acc_ref[sl, :] += _dot(lhs_ref[:, sl], rhs, TN)

        pl.when(p == 0)(lambda: accumulate(dhg_ref, xb_ref))
        pl.when(p == 1)(lambda: accumulate(dhu_ref, xb_ref))
        pl.when(p == 2)(lambda: accumulate(a_ref, dr_ref))

        @pl.when(i == last)
        def _():
            o_ref[0] = acc_ref[...].astype(BF16)

    def during(phases, width):
        lo, hi = phases
        return pl.BlockSpec((tk, width), lambda p, i: (jnp.where(p < lo, 0, jnp.where(p > hi, last, i)), 0))

    (out,), got = _call(
        body, carry, grid=(3, steps), name="ffn_dw", sem=("arbitrary", "arbitrary"), args=(dhg, dhu, a, xb, dr),
        in_specs=[during((0, 0), F), during((1, 1), F), during((2, 2), F), during((0, 1), D), during((2, 2), D)],
        out_specs=[pl.BlockSpec((1, F, D), lambda p, i: (p, 0, 0))], out_shape=[_sds((3, F, D), BF16)],
        scratch_shapes=[pltpu.VMEM((F, D), F32)])
    return out, got


def ln_bwd(u, dy, g, branch_scale):
    T, D = u.shape
    tm = _row_tile(T)

    def body(u_ref, dy_ref, g_ref, dxa_ref, dr_ref, dg_ref, db_ref):
        @pl.when(pl.program_id(0) == 0)
        def _():
            dg_ref[...] = jnp.zeros_like(dg_ref)
            db_ref[...] = jnp.zeros_like(db_ref)

        dy_ = dy_ref[...]
        du, dyxh = _ln_bwd_rows(u_ref[...], dy_, g_ref[...])
        dxa_ref[...] = ALPHA * du
        dr_ref[...] = (branch_scale * du).astype(BF16)
        dg_ref[...] += jnp.sum(dyxh.reshape(tm // 8, 8, D), axis=0)
        db_ref[...] += jnp.sum(dy_.reshape(tm // 8, 8, D), axis=0)

    row = pl.BlockSpec((tm, D), lambda i: (i, 0))
    acc = pl.BlockSpec((8, D), lambda i: (0, 0))
    dxa, dr, dg, db = pl.pallas_call(
        body, grid=(T // tm,), name="ln_bwd", in_specs=[row, row, pl.BlockSpec((1, D), lambda i: (0, 0))],
        out_specs=[row, row, acc, acc],
        out_shape=[_sds((T, D), F32), _sds((T, D), BF16), _sds((8, D), F32), _sds((8, D), F32)],
        compiler_params=_params("arbitrary"),
    )(u, dy, g.reshape(1, D))
    return dxa, dr, dg, db


def loss_head(y, target):
    T, D = y.shape
    tm = _row_tile(T)

    def body(y_ref, t_ref, dy_ref, l_ref):
        @pl.when(pl.program_id(0) == 0)
        def _():
            l_ref[...] = jnp.zeros_like(l_ref)

        e = y_ref[...] - t_ref[...]
        dy_ref[...] = e * (1.0 / D)
        l_ref[...] += jnp.sum((e * e).reshape(tm // 8, 8, D), axis=0) * (0.5 / D)

    row = pl.BlockSpec((tm, D), lambda i: (i, 0))
    return pl.pallas_call(
        body, grid=(T // tm,), name="loss_head", in_specs=[row, row],
        out_specs=[row, pl.BlockSpec((8, D), lambda i: (0, 0))],
        out_shape=[_sds((T, D), F32), _sds((8, D), F32)],
        compiler_params=_params("arbitrary"),
    )(y, target)


def adamw(w, g, m, v):
    R, C = w.shape
    tr = R
    for cand in (512, 256, 128, 64, 32, 16, 8):
        if R % cand == 0:
            tr = cand
            break
    c1 = 1.0 - ADAM_B1 ** ADAM_STEP
    c2 = 1.0 - ADAM_B2 ** ADAM_STEP

    def body(w_ref, g_ref, m_ref, v_ref, d_ref, mo_ref, vo_ref):
        g_ = g_ref[...]
        m_ = ADAM_B1 * m_ref[...] + (1.0 - ADAM_B1) * g_
        v_ = ADAM_B2 * v_ref[...] + (1.0 - ADAM_B2) * (g_ * g_)
        m_hat = m_ / c1
        v_hat = v_ / c2
        d_ref[...] = -ADAM_LR * (m_hat / (jnp.sqrt(v_hat) + ADAM_EPS) + ADAM_WD * w_ref[...])
        mo_ref[...] = m_
        vo_ref[...] = v_

    blk = pl.BlockSpec((tr, C), lambda i: (i, 0))
    return pl.pallas_call(
        body, grid=(R // tr,), name="adamw", in_specs=[blk] * 4, out_specs=[blk] * 3,
        out_shape=[_sds((R, C), F32)] * 3, compiler_params=_params("parallel"),
    )(w, g, m, v)


def _lane_lo(rows):
    return lax.broadcasted_iota(jnp.int32, (rows, LANES), 1) < HEAD_DIM


def _pair_mean(x, lo):
    s0 = jnp.sum(jnp.where(lo, x, 0.0), axis=-1, keepdims=True)
    s1 = jnp.sum(jnp.where(lo, 0.0, x), axis=-1, keepdims=True)
    return jnp.where(lo, s0, s1) * (1.0 / HEAD_DIM)


def _rms_fwd(o, gain, lo):
    r = lax.rsqrt(_pair_mean(o * o, lo) + RMS_EPS)
    return (o * r * gain).astype(BF16)


def _rms_bwd(o, dm, gain, lo):
    r = lax.rsqrt(_pair_mean(o * o, lo) + RMS_EPS)
    oh = o * r
    dg = jnp.sum(dm * oh, axis=0, keepdims=True)
    doh = dm * gain
    do = r * (doh - oh * _pair_mean(doh * oh, lo))
    return do, dg


def _split_heads(x, lo):
    zero = jnp.zeros_like(x)
    return [jnp.where(lo, x, zero), jnp.where(lo, zero, x)]


def _merge_pairs(per_head, lo):
    return [jnp.where(lo, per_head[2 * p], per_head[2 * p + 1]) for p in range(len(per_head) // 2)]


def _pair_cols(h):
    return slice((h // 2) * LANES, (h // 2 + 1) * LANES)


def _split_dot(x, tri):
    n = x.shape[0]
    hi = x.astype(BF16)
    lo = (x - hi.astype(F32)).astype(BF16)
    both = _dot(jnp.concatenate([hi, lo], axis=0), tri, NN)
    return both[:n] + both[n:]


def _log_keep(z):
    return -(jnp.maximum(z, 0.0) + jnp.log(1.0 + jnp.exp(-jnp.abs(z))))


def _attn_dims(h3p, gain, heads):
    B, SP, C = h3p.shape
    HD = gain.shape[1]
    hs = min(heads, HD // HEAD_DIM)
    W = hs * HEAD_DIM
    assert C == 6 * HD and W % LANES == 0 and HD % W == 0
    return B, SP - CA_PAD, HD, hs, W, HD // W


def sb_fwd(h3p, gain, carry=None):
    B, S, HD, hs, W, ngrp = _attn_dims(h3p, gain, FWD_HEADS)
    tb = min(SB_BLOCK, S)
    nq = S // tb
    off = CA_PAD // tb
    scale = 1.0 / math.sqrt(HEAD_DIM)

    def body(q_ref, k_ref, v_ref, g_ref, o_ref, m_ref, c_ref):
        qi = pl.program_id(2)
        lo = _lane_lo(tb)
        row = lax.broadcasted_iota(jnp.int32, (tb, tb), 0)
        col = lax.broadcasted_iota(jnp.int32, (tb, tb), 1)
        rev_incl = (row >= col).astype(BF16)
        lane = lax.broadcasted_iota(jnp.int32, (tb, nq), 1)
        qm = []
        for p in range(hs // 2):
            qm += _split_heads((q_ref[0, :, p * LANES:(p + 1) * LANES].astype(F32) * scale).astype(BF16), lo)

        def blocks(kb, carries_, mask):
            ks = pl.ds(pl.multiple_of(kb * tb + CA_PAD, tb), tb)
            hh = range(hs)
            zs = [_dot(qm[h], k_ref[0, ks, _pair_cols(h)], NT) for h in hh]
            lks = [_log_keep(z) for z in zs]
            if mask is not None:
                lks = [jnp.where(mask, lk, 0.0) for lk in lks]
            cums = [_split_dot(lk, rev_incl) for lk in lks]
            ws = []
            for h in hh:
                logw = zs[h] + cums[h] + carries_[h]
                if mask is not None:
                    logw = jnp.where(mask, logw, NEG)
                ws.append(jnp.exp(logw).astype(BF16))
            pvs = [_dot(ws[h], v_ref[0, ks, _pair_cols(h)], NN) for h in hh]
            return [(pvs[h], cums[h][:, 0:1]) for h in hh]

        diag = blocks(qi, [jnp.zeros((tb, 1), F32)] * hs, col < row)
        accs = _merge_pairs([d[0] for d in diag], lo)
        carries = [d[1] for d in diag]
        cars = [jnp.where(lane == qi, 0.0, NEG)] * hs

        def cond(st):
            kb, carries_, _, _ = st
            top = carries_[0]
            for c in carries_[1:]:
                top = jnp.maximum(top, c)
            return jnp.logical_and(kb >= 0, jnp.max(top) > SB_SKIP_LOG)

        def step(st):
            kb, carries_, accs_, cars_ = st
            out = blocks(kb, carries_, None)
            pv = _merge_pairs([o[0] for o in out], lo)
            return (kb - 1, [c + o[1] for c, o in zip(carries_, out)], [a + p for a, p in zip(accs_, pv)],
                    [jnp.where(lane == kb, c, cs) for c, cs in zip(carries_, cars_)])

        _, _, accs, cars = lax.while_loop(cond, step, (qi - 1, carries, accs, cars))
        for p, acc in enumerate(accs):
            cols = slice(p * LANES, (p + 1) * LANES)
            o_ref[0, :, cols] = acc
            m_ref[0, :, cols] = _rms_fwd(acc, g_ref[:, cols], lo)
        for h in range(hs):
            c_ref[0, h] = cars[h]

    qspec = pl.BlockSpec((1, tb, W), lambda g, b, i: (b, i + off, g))
    ospec = pl.BlockSpec((1, tb, W), lambda g, b, i: (b, i, g))
    return _call(
        body, carry, grid=(ngrp, B, nq), name="sb_fwd", sem=("parallel", "parallel", "arbitrary"),
        args=(h3p, h3p, h3p, gain),
        in_specs=[qspec, pl.BlockSpec((1, CA_PAD + S, W), lambda g, b, i: (b, 0, ngrp + g)),
                  pl.BlockSpec((1, CA_PAD + S, W), lambda g, b, i: (b, 0, 2 * ngrp + g)),
                  pl.BlockSpec((1, W), lambda g, b, i: (0, g))],
        out_specs=[ospec, ospec, pl.BlockSpec((1, hs, tb, nq), lambda g, b, i: (b, g, i, 0))],
        out_shape=[_sds((B, S, HD), F32), _sds((B, S, HD), BF16), _sds((B, HD // HEAD_DIM, S, nq), F32)])


def sb_bwd(h3p, o, dmix, gain, cars, carry=None):
    B, S, HD, hs, W, ngrp = _attn_dims(h3p, gain, BWD_HEADS)
    tb = min(SB_BLOCK, S)
    nq = S // tb
    off = CA_PAD // tb
    scale = 1.0 / math.sqrt(HEAD_DIM)

    def body(q_ref, k_ref, v_ref, o_ref, dm_ref, g_ref, c_ref, dq_ref, dk_ref, dv_ref, dg_ref, dk_acc, dv_acc):
        qi = pl.program_id(2)

        @pl.when(qi == 0)
        def _():
            dk_acc[...] = jnp.zeros_like(dk_acc)
            dv_acc[...] = jnp.zeros_like(dv_acc)

        lo = _lane_lo(tb)
        row = lax.broadcasted_iota(jnp.int32, (tb, tb), 0)
        col = lax.broadcasted_iota(jnp.int32, (tb, tb), 1)
        rev_incl = (row >= col).astype(BF16)
        fwd_incl = (row <= col).astype(BF16)
        lane = lax.broadcasted_iota(jnp.int32, (tb, nq), 1)
        below = lax.broadcasted_iota(jnp.int32, (1, nq), 1) < qi
        qm, dom, cars_, seen = [], [], [], None
        for p in range(hs // 2):
            cols = slice(p * LANES, (p + 1) * LANES)
            qm += _split_heads((q_ref[0, :, cols].astype(F32) * scale).astype(BF16), lo)
            do, dg = _rms_bwd(o_ref[0, :, cols], dm_ref[0, :, cols], g_ref[:, cols], lo)
            dg_ref[0, 0, :, cols] = dg
            dom += _split_heads(do.astype(BF16), lo)
        for h in range(hs):
            cars_.append(c_ref[0, h])
            visited = jnp.logical_and(jnp.max(cars_[h], axis=0, keepdims=True) > SB_SKIP_LOG, below)
            n = jnp.sum(visited.astype(jnp.int32), axis=1, keepdims=True)
            seen = n if seen is None else jnp.maximum(seen, n)
        first = qi - jnp.max(seen)

        def blocks(kb, gsums, dqs, mask):
            ks = pl.ds(pl.multiple_of(kb * tb + CA_PAD, tb), tb)
            ko = pl.ds(pl.multiple_of(kb * tb, tb), tb)
            hh = range(hs)
            kk = [k_ref[0, ks, p * LANES:(p + 1) * LANES] for p in range(hs // 2)]
            vv = [v_ref[0, ks, p * LANES:(p + 1) * LANES] for p in range(hs // 2)]
            zs = [_dot(qm[h], kk[h // 2], NT) for h in hh]
            dws = [_dot(dom[h], vv[h // 2], NT) for h in hh]
            raw = [_log_keep(z) for z in zs]
            lks = raw if mask is None else [jnp.where(mask, lk, 0.0) for lk in raw]
            cums = [_split_dot(lk, rev_incl) for lk in lks]
            ws = []
            for h in hh:
                carry = jnp.sum(jnp.where(lane == kb, cars_[h], 0.0), axis=1, keepdims=True)
                logw = zs[h] + cums[h] + carry
                if mask is not None:
                    logw = jnp.where(mask, logw, NEG)
                ws.append(jnp.exp(logw))
            gws = [ws[h] * dws[h] for h in hh]
            gcums = [_split_dot(gws[h], fwd_incl) + gsums[h] for h in hh]
            dzb = []
            for h in hh:
                dz = gws[h] - jnp.exp(zs[h] + raw[h]) * gcums[h]
                if mask is not None:
                    dz = jnp.where(mask, dz, 0.0)
                dzb.append(dz.astype(BF16))
            wb = [w.astype(BF16) for w in ws]
            new_dq = [dqs[h] + _dot(dzb[h], kk[h // 2], NN) for h in hh]
            for p in range(hs // 2):
                cols = slice(p * LANES, (p + 1) * LANES)
                dk_acc[ko, cols] += _dot(dzb[2 * p], qm[2 * p], TN) + _dot(dzb[2 * p + 1], qm[2 * p + 1], TN)
                dv_acc[ko, cols] += _dot(wb[2 * p], dom[2 * p], TN) + _dot(wb[2 * p + 1], dom[2 * p + 1], TN)
            return [g[:, tb - 1:tb] for g in gcums], new_dq

        def step(kb, st):
            return blocks(kb, st[0], st[1], None)

        init = ([jnp.zeros((tb, 1), F32)] * hs, [jnp.zeros((tb, LANES), F32)] * hs)
        gsum, dq = lax.fori_loop(first, qi, step, init)
        _, dq = blocks(qi, gsum, dq, col < row)
        for p, d in enumerate(_merge_pairs(dq, lo)):
            dq_ref[0, :, p * LANES:(p + 1) * LANES] = (d * scale).astype(BF16)

        @pl.when(qi == nq - 1)
        def _():
            dk_ref[0] = dk_acc[...].astype(BF16)
            dv_ref[0] = dv_acc[...].astype(BF16)

    once = pl.Buffered(1)
    qspec = pl.BlockSpec((1, tb, W), lambda g, b, i: (b, i + off, g))
    ospec = pl.BlockSpec((1, tb, W), lambda g, b, i: (b, i, g))
    kvout = pl.BlockSpec((1, S, W), lambda g, b, i: (b, 0, g), pipeline_mode=once)
    return _call(
        body, carry, grid=(ngrp, B, nq), name="sb_bwd", sem=("parallel", "parallel", "arbitrary"),
        args=(h3p, h3p, h3p, o, dmix, gain, cars),
        in_specs=[qspec, pl.BlockSpec((1, CA_PAD + S, W), lambda g, b, i: (b, 0, ngrp + g), pipeline_mode=once),
                  pl.BlockSpec((1, CA_PAD + S, W), lambda g, b, i: (b, 0, 2 * ngrp + g), pipeline_mode=once),
                  ospec, ospec, pl.BlockSpec((1, W), lambda g, b, i: (0, g)),
                  pl.BlockSpec((1, hs, tb, nq), lambda g, b, i: (b, g, i, 0))],
        out_specs=[ospec, kvout, kvout, pl.BlockSpec((1, 1, 1, W), lambda g, b, i: (b, i, 0, g))],
        out_shape=[_sds((B, S, HD), BF16), _sds((B, S, HD), BF16), _sds((B, S, HD), BF16), _sds((B, nq, 1, HD), F32)],
        scratch_shapes=[pltpu.VMEM((S, W), F32), pltpu.VMEM((S, W), F32)])


def _ca_rel_index():
    width = CA_WIN + CA_TQ
    c = np.arange(width)
    dj = np.where(c < CA_WIN, c, c - width)
    return np.clip(CA_PAD - dj, -MAX_REL, MAX_REL) + MAX_REL, width


def _ca_onehot():
    idx, _ = _ca_rel_index()
    return (idx[:, None] == np.arange(2 * MAX_REL + 1)[None, :]).astype(np.float32)


def ca_bias(rel_bias):
    _, width = _ca_rel_index()
    lead = rel_bias.shape[:-1]
    by_offset = jnp.dot(rel_bias, jnp.asarray(_ca_onehot().T), precision=lax.Precision.HIGHEST)
    tile = jnp.broadcast_to(by_offset[..., None, :], lead + (CA_TQ, width)).reshape(lead + (CA_TQ * width,))
    tile = tile[..., :CA_TQ * (width - 1)].reshape(lead + (CA_TQ, width - 1))[..., :CA_WIN]
    t = np.arange(CA_TQ)[:, None] // CHUNK * CHUNK
    j = np.arange(CA_WIN)[None, :]
    return jnp.where((j >= t) & (j < t + CA_PAD + CHUNK), tile, NEG)


def rel_bias_grad(db_near):
    H = db_near.shape[0]
    _, width = _ca_rel_index()
    db = jnp.pad(db_near, ((0, 0), (0, 0), (CA_WIN - CA_NEAR, 0)))
    x = jnp.pad(db, ((0, 0), (0, 0), (0, width - 1 - CA_WIN))).reshape(H, CA_TQ * (width - 1))
    x = jnp.pad(x, ((0, 0), (0, CA_TQ))).reshape(H, CA_TQ, width).sum(axis=1)
    grad = jnp.dot(x, jnp.asarray(_ca_onehot()), precision=lax.Precision.HIGHEST)
    return grad.at[:, 2 * MAX_REL].add(-db_near.sum((1, 2)))


def _ca_scores(qm_h, kk, bias_h, valid):
    return jnp.where(valid, _dot(qm_h, kk, NT) + bias_h, NEG)


def _ca_softmax(s):
    e = jnp.exp(s - jnp.max(s, axis=-1, keepdims=True))
    return e * (1.0 / jnp.sum(e, axis=-1, keepdims=True))


def ca_fwd(h3p, bias, gain, carry=None):
    B, S, HD, hs, W, ngrp = _attn_dims(h3p, gain, FWD_HEADS)
    scale = 1.0 / math.sqrt(HEAD_DIM)
    off = CA_PAD // CA_TQ

    def body(q_ref, k_ref, v_ref, b_ref, g_ref, o_ref, m_ref):
        q0 = pl.program_id(2) * CA_TQ
        ks = pl.ds(pl.multiple_of(q0, CA_TQ), CA_WIN)
        lo = _lane_lo(CA_TQ)
        valid = lax.broadcasted_iota(jnp.int32, (CA_TQ, CA_WIN), 1) + q0 >= CA_PAD
        qm, kk, vv = [], [], []
        for p in range(hs // 2):
            cols = slice(p * LANES, (p + 1) * LANES)
            qm += _split_heads((q_ref[0, :, cols].astype(F32) * scale).astype(BF16), lo)
            kk.append(k_ref[0, ks, cols])
            vv.append(v_ref[0, ks, cols])
        ss = [_ca_scores(qm[h], kk[h // 2], b_ref[h], valid) for h in range(hs)]
        ps = [_ca_softmax(s).astype(BF16) for s in ss]
        pv = [_dot(ps[h], vv[h // 2], NN) for h in range(hs)]
        for p, o in enumerate(_merge_pairs(pv, lo)):
            cols = slice(p * LANES, (p + 1) * LANES)
            o_ref[0, :, cols] = o
            m_ref[0, :, cols] = _rms_fwd(o, g_ref[:, cols], lo)

    ospec = pl.BlockSpec((1, CA_TQ, W), lambda g, b, i: (b, i, g))
    return _call(
        body, carry, grid=(ngrp, B, S // CA_TQ), name="ca_fwd", sem=("parallel", "parallel", "arbitrary"),
        args=(h3p, h3p, h3p, bias, gain),
        in_specs=[pl.BlockSpec((1, CA_TQ, W), lambda g, b, i: (b, i + off, 3 * ngrp + g)),
                  pl.BlockSpec((1, CA_PAD + S, W), lambda g, b, i: (b, 0, 4 * ngrp + g)),
                  pl.BlockSpec((1, CA_PAD + S, W), lambda g, b, i: (b, 0, 5 * ngrp + g)),
                  pl.BlockSpec((hs, CA_TQ, CA_WIN), lambda g, b, i: (g, 0, 0)),
                  pl.BlockSpec((1, W), lambda g, b, i: (0, g))],
        out_specs=[ospec, ospec], out_shape=[_sds((B, S, HD), F32), _sds((B, S, HD), BF16)])


def ca_bwd(h3p, bias, o, dmix, gain, carry=None):
    B, S, HD, hs, W, ngrp = _attn_dims(h3p, gain, BWD_HEADS)
    scale = 1.0 / math.sqrt(HEAD_DIM)
    nq = S // CA_TQ
    off = CA_PAD // CA_TQ

    def body(q_ref, k_ref, v_ref, b_ref, o_ref, dm_ref, g_ref, dq_ref, dk_ref, dv_ref, dg_ref, db_ref, dk_acc, dv_acc):
        bi = pl.program_id(1)
        qi = pl.program_id(2)

        @pl.when(qi == 0)
        def _():
            dk_acc[...] = jnp.zeros_like(dk_acc)
            dv_acc[...] = jnp.zeros_like(dv_acc)

        @pl.when(jnp.logical_and(qi == 0, bi == 0))
        def _():
            db_ref[...] = jnp.zeros_like(db_ref)

        q0 = qi * CA_TQ
        ks = pl.ds(pl.multiple_of(q0, CA_TQ), CA_WIN)
        lo = _lane_lo(CA_TQ)
        valid = lax.broadcasted_iota(jnp.int32, (CA_TQ, CA_WIN), 1) + q0 >= CA_PAD
        qm, dom, kk, vv = [], [], [], []
        for p in range(hs // 2):
            cols = slice(p * LANES, (p + 1) * LANES)
            qm += _split_heads((q_ref[0, :, cols].astype(F32) * scale).astype(BF16), lo)
            do, dg = _rms_bwd(o_ref[0, :, cols], dm_ref[0, :, cols], g_ref[:, cols], lo)
            dg_ref[0, 0, :, cols] = dg
            dom += _split_heads(do.astype(BF16), lo)
            kk.append(k_ref[0, ks, cols])
            vv.append(v_ref[0, ks, cols])
        dq = []
        for h0 in range(0, hs, 4):
            hh = range(h0, min(h0 + 4, hs))
            ss = {h: _ca_scores(qm[h], kk[h // 2], b_ref[h], valid) for h in hh}
            dps = {h: _dot(dom[h], vv[h // 2], NT) for h in hh}
            ps = {h: _ca_softmax(ss[h]) for h in hh}
            dss = {h: ps[h] * (dps[h] - jnp.sum(ps[h] * dps[h], axis=-1, keepdims=True)) for h in hh}
            for h in hh:
                db_ref[h] += dss[h][:, CA_WIN - CA_NEAR:]
            dsb = {h: dss[h].astype(BF16) for h in hh}
            pb = {h: ps[h].astype(BF16) for h in hh}
            dq += [_dot(dsb[h], kk[h // 2], NN) for h in hh]
            for p in range(hh[0] // 2, (hh[-1] + 1) // 2):
                cols = slice(p * LANES, (p + 1) * LANES)
                dk_acc[ks, cols] += _dot(dsb[2 * p], qm[2 * p], TN) + _dot(dsb[2 * p + 1], qm[2 * p + 1], TN)
                dv_acc[ks, cols] += _dot(pb[2 * p], dom[2 * p], TN) + _dot(pb[2 * p + 1], dom[2 * p + 1], TN)
        for p, d in enumerate(_merge_pairs(dq, lo)):
            dq_ref[0, :, p * LANES:(p + 1) * LANES] = (d * scale).astype(BF16)

        @pl.when(qi == nq - 1)
        def _():
            dk_ref[0] = dk_acc[CA_PAD:, :].astype(BF16)
            dv_ref[0] = dv_acc[CA_PAD:, :].astype(BF16)

    once = pl.Buffered(1)
    ospec = pl.BlockSpec((1, CA_TQ, W), lambda g, b, i: (b, i, g))
    kvout = pl.BlockSpec((1, S, W), lambda g, b, i: (b, 0, g), pipeline_mode=once)
    bspec = pl.BlockSpec((hs, CA_TQ, CA_WIN), lambda g, b, i: (g, 0, 0))
    return _call(
        body, carry, grid=(ngrp, B, nq), name="ca_bwd", sem=("parallel", "arbitrary", "arbitrary"),
        args=(h3p, h3p, h3p, bias, o, dmix, gain),
        in_specs=[pl.BlockSpec((1, CA_TQ, W), lambda g, b, i: (b, i + off, 3 * ngrp + g)),
                  pl.BlockSpec((1, CA_PAD + S, W), lambda g, b, i: (b, 0, 4 * ngrp + g), pipeline_mode=once),
                  pl.BlockSpec((1, CA_PAD + S, W), lambda g, b, i: (b, 0, 5 * ngrp + g), pipeline_mode=once),
                  bspec, ospec, pl.BlockSpec((1, CA_TQ, W), lambda g, b, i: (b, i, ngrp + g)),
                  pl.BlockSpec((1, W), lambda g, b, i: (0, g))],
        out_specs=[ospec, kvout, kvout, pl.BlockSpec((1, 1, 1, W), lambda g, b, i: (b, i, 0, g)),
                   pl.BlockSpec((hs, CA_TQ, CA_NEAR), lambda g, b, i: (g, 0, 0))],
        out_shape=[_sds((B, S, HD), BF16), _sds((B, S, HD), BF16), _sds((B, S, HD), BF16),
                   _sds((B, nq, 1, HD), F32), _sds((HD // HEAD_DIM, CA_TQ, CA_NEAR), F32)],
        scratch_shapes=[pltpu.VMEM((CA_PAD + S, W), F32), pltpu.VMEM((CA_PAD + S, W), F32)])


_ANY = pl.BlockSpec(memory_space=pl.ANY)
_MESH = pl.DeviceIdType.MESH


def _mesh_pos():
    return lax.axis_index("x"), lax.axis_index("y"), lax.axis_index("c")


def _dev_index(p):
    return 4 * p[0] + 2 * p[1] + p[2]


def _flip(pos, k):
    return tuple(1 - v if (k >> (2 - a)) & 1 else v for a, v in enumerate(pos))


class _Plan:
    def __init__(self, operands, n_matrices):
        self.operands = list(operands)
        self.n_ops = len(self.operands)
        self.nm = n_matrices
        self.scratch = [pltpu.SemaphoreType.DMA((7, self.nm)), pltpu.SemaphoreType.DMA((7, self.nm)),
                        pltpu.SemaphoreType.DMA((self.nm,))]


class _Gather(_Plan):
    def __init__(self, mats):
        super().__init__(mats, len(mats))
        self.rows = [m.shape[0] for m in mats]
        self.out_shape = [_sds((N_DEV * m.shape[0], m.shape[1]), m.dtype) for m in mats]

    def begin(self, ins, outs, sems):
        nm, rows = self.nm, self.rows
        send_sems, recv_sems, local_sems = sems
        x, y, c = _mesh_pos()
        me, sibling = (x, y, c), (x, y, 1 - c)
        chips = [(1 - x, y), (x, 1 - y), (1 - x, 1 - y)]

        def block(m, p):
            start = pl.multiple_of(_dev_index(p) * rows[m], 16)
            return outs[m].at[pl.ds(start, rows[m]), :]

        def copy(k, m, blk, to, src=None):
            return pltpu.make_async_remote_copy(
                src_ref=block(m, blk) if src is None else src, dst_ref=block(m, blk),
                send_sem=send_sems.at[k, m], recv_sem=recv_sems.at[k, m], device_id=to, device_id_type=_MESH)

        def mine():
            return [pltpu.make_async_copy(ins[m], block(m, me), local_sems.at[m]) for m in range(nm)]

        def first():
            own = [copy(0, m, me, sibling, src=ins[m]) for m in range(nm)]
            return own + [copy(1 + j, m, me, (*chip, c), src=ins[m]) for j, chip in enumerate(chips) for m in range(nm)]

        def start():
            for cp in mine() + first():
                cp.start()

        def passed():
            return [copy(4 + j, m, (*chip, c), sibling) for j, chip in enumerate(chips) for m in range(nm)]

        def pass_on():
            for j, chip in enumerate(chips):
                for m in range(nm):
                    copy(1 + j, m, (*chip, c), me).wait_recv()
                    copy(4 + j, m, (*chip, c), sibling).start()

        def finish():
            for m in range(nm):
                copy(0, m, sibling, me).wait_recv()
            for j, chip in enumerate(chips):
                for m in range(nm):
                    copy(4 + j, m, (*chip, 1 - c), me).wait_recv()
            for cp in first() + passed():
                cp.wait_send()
            for cp in mine():
                cp.wait()

        return start, pass_on, finish


class _Exchange(_Plan):
    def __init__(self, grads):
        self.where = [(i, j) for i, g in enumerate(grads) for j in (range(g.shape[0]) if g.ndim == 3 else [None])]
        super().__init__(grads, len(self.where))
        self.rows = [grads[i].shape[-2] // N_DEV for i, _ in self.where]
        self.out_shape = [_sds(g.shape[:-2] + (N_DEV, g.shape[-2] // N_DEV, g.shape[-1]), g.dtype) for g in grads]

    def begin(self, ins, outs, sems):
        nm, rows = self.nm, self.rows
        send_sems, recv_sems, local_sems = sems
        me = _mesh_pos()
        my = _dev_index(me)

        def piece(m, idx):
            i, j = self.where[m]
            rows_ = pl.ds(pl.multiple_of(idx * rows[m], 16), rows[m])
            return ins[i].at[rows_, :] if j is None else ins[i].at[j, rows_, :]

        def landing(m, slot):
            i, j = self.where[m]
            return outs[i].at[slot] if j is None else outs[i].at[j, slot]

        def copy(k, m, src_idx, slot, to):
            return pltpu.make_async_remote_copy(
                src_ref=piece(m, src_idx), dst_ref=landing(m, slot),
                send_sem=send_sems.at[k - 1, m], recv_sem=recv_sems.at[k - 1, m], device_id=to, device_id_type=_MESH)

        def mine():
            return [pltpu.make_async_copy(piece(m, my), landing(m, my), local_sems.at[m]) for m in range(nm)]

        def sends():
            return [copy(k, m, _dev_index(_flip(me, k)), my, _flip(me, k)) for k in range(1, N_DEV) for m in range(nm)]

        def start():
            for cp in mine() + sends():
                cp.start()

        def finish():
            for k in range(1, N_DEV):
                peer = _flip(me, k)
                for m in range(nm):
                    copy(k, m, my, _dev_index(peer), peer).wait_recv()
            for cp in sends():
                cp.wait_send()
            for cp in mine():
                cp.wait()

        return start, None, finish


def _run_plan(plan, name):
    nm = plan.n_ops

    def body(*refs):
        start, middle, finish = plan.begin(refs[:nm], refs[nm:2 * nm], refs[2 * nm:])
        start()
        if middle is not None:
            middle()
        finish()

    return pl.pallas_call(body, name=name, in_specs=[_ANY] * nm, out_specs=[_ANY] * nm, out_shape=plan.out_shape,
                          scratch_shapes=plan.scratch)(*plan.operands)


def _call(body, carry, *, grid, in_specs, out_specs, out_shape, args, name, sem, scratch_shapes=()):
    if carry is None:
        outs = pl.pallas_call(body, grid=grid, name=name, in_specs=in_specs, out_specs=out_specs, out_shape=out_shape,
                              scratch_shapes=list(scratch_shapes), compiler_params=_params(*sem))(*args)
        return outs, None
    n_in, n_out, n_sc, nm = len(in_specs), len(out_specs), len(scratch_shapes), carry.n_ops

    def carrier(*refs):
        refs = list(refs)
        own_in, refs = refs[:n_in], refs[n_in:]
        c_in, refs = refs[:nm], refs[nm:]
        own_out, refs = refs[:n_out], refs[n_out:]
        c_out, refs = refs[:nm], refs[nm:]
        own_sc, c_sc = refs[:n_sc], refs[n_sc:]
        start, middle, finish = carry.begin(c_in, c_out, c_sc)
        step, steps = 0, 1
        for axis, n in enumerate(grid):
            step = step * n + pl.program_id(axis)
            steps *= n
        pl.when(step == 0)(start)
        body(*own_in, *own_out, *own_sc)

        @pl.when(step == steps - 1)
        def _():
            if middle is not None:
                middle()
            finish()

    outs = pl.pallas_call(
        carrier, grid=grid, name=name + "_carry", in_specs=list(in_specs) + [_ANY] * nm,
        out_specs=list(out_specs) + [_ANY] * nm, out_shape=list(out_shape) + carry.out_shape,
        scratch_shapes=list(scratch_shapes) + carry.scratch,
        compiler_params=_params(*["arbitrary"] * len(grid)),
    )(*args, *carry.operands)
    return outs[:n_out], outs[n_out:]


def sum_slots(r, j=None):
    n, R, D = r.shape[-3:]
    tc = _row_tile(D, 256)

    def body(r_ref, o_ref):
        slot = (lambda s: r_ref[s]) if j is None else (lambda s: r_ref[0, s])
        acc = slot(0).astype(F32)
        for s in range(1, n):
            acc = acc + slot(s).astype(F32)
        o_ref[...] = acc

    spec = (pl.BlockSpec((n, R, tc), lambda i: (0, 0, i)) if j is None
            else pl.BlockSpec((1, n, R, tc), lambda i: (j, 0, 0, i)))
    return pl.pallas_call(
        body, grid=(D // tc,), name="sum_slots", in_specs=[spec],
        out_specs=pl.BlockSpec((R, tc), lambda i: (0, i)), out_shape=_sds((R, D), F32),
        compiler_params=_params("parallel"),
    )(r)


def allreduce_small(v):
    R, C = v.shape

    def body(v_ref, o_ref, buf, send_sems, recv_sems):
        me = _mesh_pos()
        my = _dev_index(me)
        buf[my] = v_ref[...]

        def copy(k, slot, to):
            return pltpu.make_async_remote_copy(
                src_ref=v_ref, dst_ref=buf.at[slot], send_sem=send_sems.at[k - 1], recv_sem=recv_sems.at[k - 1],
                device_id=to, device_id_type=_MESH)

        sends = []
        for k in range(1, N_DEV):
            cp = copy(k, my, _flip(me, k))
            cp.start()
            sends.append(cp)
        for k in range(1, N_DEV):
            peer = _flip(me, k)
            copy(k, _dev_index(peer), peer).wait_recv()
        acc = buf[0]
        for s in range(1, N_DEV):
            acc = acc + buf[s]
        o_ref[...] = acc
        for cp in sends:
            cp.wait_send()

    vm = pl.BlockSpec(memory_space=pltpu.VMEM)
    return pl.pallas_call(
        body, name="allreduce_small", in_specs=[vm], out_specs=vm, out_shape=_sds((R, C), F32),
        scratch_shapes=[pltpu.VMEM((N_DEV, R, C), F32), pltpu.SemaphoreType.DMA((7,)), pltpu.SemaphoreType.DMA((7,))],
    )(v)


WEIGHTS = ["ffn1_w_gate", "ffn1_w_up", "ffn1_w_down", "ln1_g", "ln1_b", "w_in", "rel_bias", "sb_out_g", "ca_out_g",
           "w_out", "ln2_g", "ln2_b", "ffn2_w_gate", "ffn2_w_up", "ffn2_w_down", "ln3_g", "ln3_b"]
BIG = ["ffn1_w_gate", "ffn1_w_up", "ffn2_w_gate", "ffn2_w_up", "w_in", "ffn1_w_down", "ffn2_w_down", "w_out"]
TRANSPOSED = BIG[:5]
SMALL = [n for n in WEIGHTS if n not in BIG]


def _pack(vals):
    flat = jnp.concatenate([v.reshape(-1).astype(F32) for v in vals])
    pad = -flat.shape[0] % (8 * 128)
    return jnp.pad(flat, (0, pad)).reshape(-1, 128)


def _unpack(packed, like):
    flat = packed.reshape(-1)
    out, off = [], 0
    for v in like:
        out.append(flat[off:off + v.size].reshape(v.shape))
        off += v.size
    return out


def _row_blocks(w, n):
    r = w.shape[0] // n
    return [w[i * r:(i + 1) * r] for i in range(n)]


class _Arrivals:
    def __init__(self):
        self.by_kernel = {}

    def ride(self, kernel_name, weights, names, local, layer):
        self.by_kernel[kernel_name] = (weights, names, _Gather([local[n][layer] for n in names]))

    def plan(self, kernel_name):
        return self.by_kernel[kernel_name][2] if kernel_name in self.by_kernel else None

    def landed(self, kernel_name, outs):
        if kernel_name in self.by_kernel:
            weights, names, _ = self.by_kernel[kernel_name]
            weights.update(zip(names, outs))


def _layer_fwd(x, xb, W, P, bias, l, B, S, arrivals):
    T = B * S
    sv = {"xb": xb, "bias": bias}
    (sv["sg1"], sv["t1"], sv["a1"]), got = ffn_up(xb, W["ffn1_w_gate"], W["ffn1_w_up"], carry=arrivals.plan("ffn_up"))
    arrivals.landed("ffn_up", got)
    sv["u1"], x1, sv["x1b"] = mm_nn_ln([(sv["a1"], W["ffn1_w_down"])], x, P["ln1_g"][l], P["ln1_b"][l], FFN_RESIDUAL,
                                       "ffn_down_ln")

    sv["h"] = proj_in(sv["x1b"], W["w_in"], B, S)
    sv["gA"] = P["sb_out_g"][l].reshape(1, -1)
    sv["gB"] = P["ca_out_g"][l].reshape(1, -1)
    (sv["oa"], ma, sv["cars"]), got = sb_fwd(sv["h"], sv["gA"], carry=arrivals.plan("sb_fwd"))
    arrivals.landed("sb_fwd", got)
    (sv["ob"], mb), got = ca_fwd(sv["h"], bias, sv["gB"], carry=arrivals.plan("ca_fwd"))
    arrivals.landed("ca_fwd", got)
    sv["ma"], sv["mb"] = ma.reshape(T, -1), mb.reshape(T, -1)
    sv["u2"], x2, sv["x2b"] = mm_nn_ln(list(zip([sv["ma"], sv["mb"]], _row_blocks(W["w_out"], 2))), x1,
                                       P["ln2_g"][l], P["ln2_b"][l], 1.0, "proj_out_ln")

    (sv["sg2"], sv["t2"], sv["a2"]), _ = ffn_up(sv["x2b"], W["ffn2_w_gate"], W["ffn2_w_up"])
    sv["u3"], x3, x3b = mm_nn_ln([(sv["a2"], W["ffn2_w_down"])], x2, P["ln3_g"][l], P["ln3_b"][l], FFN_RESIDUAL,
                                 "ffn_down_ln")
    return x3, x3b, sv


class _Riders:
    def __init__(self):
        self.received = []

    def plan(self, entries):
        return _Exchange([e[2] for e in entries]) if entries else None

    def landed(self, entries, slots):
        for (names, layer, _), r in zip(entries, slots or []):
            for j, name in enumerate(names):
                self.received.append((name, layer, r, j if r.ndim == 4 else None))


def _ffn_bwd(dr, dxa, xb, sg, t, a, wg, wu, wd, riders, ride, below, names, layer):
    dhg, dhu = ffn_bwd_mid(dr, wd, sg, t)
    pairs = [(dhg, wg), (dhu, wu)]
    if below is not None:
        down = mm_nn_ln_bwd(pairs, dxa, *below, name="ffn_dx_ln")
        grads, slots = ffn_dw(dhg, dhu, a, xb, dr, carry=riders.plan(ride))
        riders.landed(ride, slots)
        return down, [(names, layer, grads)]
    down = mm_nn(pairs, add=dxa, name="ffn_dx")
    for name, lhs, rhs in zip(names, (dhg, dhu, a), (xb, xb, dr)):
        g, slots = mm_tn([lhs], rhs, name="ffn_dw_last", carry=riders.plan(ride))
        riders.landed(ride, slots)
        ride = [((name,), layer, g)]
    return down, ride


def _layer_bwd(dxa, dr, sv, W, P, l, B, S, riders, pending, below):
    T = B * S
    G = {}
    (dxa, dr, dg, db), ride = _ffn_bwd(
        dr, dxa, sv["x2b"], sv["sg2"], sv["t2"], sv["a2"], W["ffn2_w_gate"], W["ffn2_w_up"], W["ffn2_w_down"],
        riders, [], (sv["u2"], P["ln2_g"][l], 1.0), ("ffn2_w_gate", "ffn2_w_up", "ffn2_w_down"), l)
    G["ln2_g"], G["ln2_b"] = dg.sum(0), db.sum(0)
    dmix = mm_nt(dr, W["w_out"], name="proj_out_dx").reshape(B, S, -1)
    g_out = mm_tn([sv["ma"], sv["mb"]], dr, name="proj_out_dw")[0]
    (dqa, dka, dva, dga), slots = sb_bwd(sv["h"], sv["oa"], dmix, sv["gA"], sv["cars"], carry=riders.plan(pending))
    riders.landed(pending, slots)
    (dqb, dkb, dvb, dgb, dbias), slots = ca_bwd(sv["h"], sv["bias"], sv["ob"], dmix, sv["gB"], carry=riders.plan(ride))
    riders.landed(ride, slots)
    G["sb_out_g"] = dga.sum((0, 1, 2))
    G["ca_out_g"] = dgb.sum((0, 1, 2))
    G["rel_bias"] = rel_bias_grad(dbias)
    dh = [t.reshape(T, -1) for t in (dqa, dka, dva, dqb, dkb, dvb)]
    dxa, dr, dg, db = mm_nn_ln_bwd(list(zip(dh, _row_blocks(W["w_in"], 6))), dxa, sv["u1"], P["ln1_g"][l], FFN_RESIDUAL,
                                   name="proj_in_dx_ln")
    G["ln1_g"], G["ln1_b"] = dg.sum(0), db.sum(0)
    g_in = mm_tn(dh, sv["x1b"], name="proj_in_dw")[0]

    proj = [(("w_in",), l, g_in), (("w_out",), l, g_out)]
    down, pending = _ffn_bwd(dr, dxa, sv["xb"], sv["sg1"], sv["t1"], sv["a1"], W["ffn1_w_gate"], W["ffn1_w_up"],
                             W["ffn1_w_down"], riders, [] if below is not None else proj, below,
                             ("ffn1_w_gate", "ffn1_w_up", "ffn1_w_down"), l)
    return down, G, pending + proj if below is not None else pending


def kernel(x, ffn1_w_gate, ffn1_w_up, ffn1_w_down, ln1_g, ln1_b, w_in, rel_bias, sb_out_g, ca_out_g, w_out, ln2_g, ln2_b, ffn2_w_gate, ffn2_w_up, ffn2_w_down, ln3_g, ln3_b, loss_target, m_ffn1_w_gate, m_ffn1_w_up, m_ffn1_w_down, m_ln1_g, m_ln1_b, m_w_in, m_rel_bias, m_sb_out_g, m_ca_out_g, m_w_out, m_ln2_g, m_ln2_b, m_ffn2_w_gate, m_ffn2_w_up, m_ffn2_w_down, m_ln3_g, m_ln3_b, v_ffn1_w_gate, v_ffn1_w_up, v_ffn1_w_down, v_ln1_g, v_ln1_b, v_w_in, v_rel_bias, v_sb_out_g, v_ca_out_g, v_w_out, v_ln2_g, v_ln2_b, v_ffn2_w_gate, v_ffn2_w_up, v_ffn2_w_down, v_ln3_g, v_ln3_b):
    given = dict(locals())
    P = {n: given[n] for n in WEIGHTS}
    M = {n: given["m_" + n] for n in WEIGHTS}
    V = {n: given["v_" + n] for n in WEIGHTS}
    B, S, D = x.shape
    L = ln1_g.shape[0]

    local = {n: (jnp.swapaxes(P[n], 1, 2) if n in TRANSPOSED else P[n]).astype(BF16) for n in BIG}
    bias = ca_bias(rel_bias)

    xf = x.reshape(B * S, D)
    xb = xf.astype(BF16)
    W, saved = [{} for _ in range(L)], []
    first = ["ffn1_w_gate", "ffn1_w_up"]
    W[0].update(zip(first, _run_plan(_Gather([local[n][0] for n in first]), "gather_weights")))
    for l in range(L):
        arrivals = _Arrivals()
        if l == 0:
            arrivals.ride("ffn_up", W[0], ["ffn1_w_down", "w_in", "w_out"], local, 0)
            arrivals.ride("ca_fwd", W[0], ["ffn2_w_gate", "ffn2_w_up", "ffn2_w_down"], local, 0)
            if L > 1:
                arrivals.ride("sb_fwd", W[1], BIG, local, 1)
        elif l + 1 < L:
            arrivals.ride("sb_fwd", W[l + 1], BIG[:5], local, l + 1)
            arrivals.ride("ca_fwd", W[l + 1], BIG[5:], local, l + 1)
        xf, xb, sv = _layer_fwd(xf, xb, W[l], P, bias[l], l, B, S, arrivals)
        saved.append(sv)
    dy, loss_part = loss_head(xf, loss_target.reshape(B * S, D))

    big_g = {n: [None] * L for n in BIG}
    small_g = {n: [None] * L for n in SMALL}
    riders, pending = _Riders(), []
    down = ln_bwd(saved[L - 1]["u3"], dy, ln3_g[L - 1], FFN_RESIDUAL)
    for l in reversed(range(L)):
        dxa, dr, dg, db = down
        small_g["ln3_g"][l], small_g["ln3_b"][l] = dg.sum(0), db.sum(0)
        below = (saved[l - 1]["u3"], ln3_g[l - 1], FFN_RESIDUAL) if l > 0 else None
        down, G, pending = _layer_bwd(dxa, dr, saved[l], W[l], P, l, B, S, riders, pending, below)
        for n, g in G.items():
            small_g[n][l] = g
    dx = down
    riders.landed(pending, _run_plan(riders.plan(pending), "exchange_grads"))
    for n, l, slots, j in riders.received:
        big_g[n][l] = sum_slots(slots, j)

    grads = {}
    for n in BIG:
        g = jnp.stack(big_g[n])
        grads[n] = jnp.swapaxes(g, 1, 2) if n in TRANSPOSED else g
    small_like = [P[n] for n in SMALL]
    packed = _pack([jnp.stack(small_g[n]) for n in SMALL] + [loss_part.sum()])
    total = allreduce_small(packed)
    *small_vals, loss = _unpack(total, small_like + [jnp.zeros((), F32)])
    grads.update(dict(zip(SMALL, small_vals)))

    delta, new_m, new_v = {}, {}, {}
    for n in BIG:
        shape = P[n].shape
        two_d = lambda a: a.reshape(shape[0] * shape[1], shape[2])
        d, m, v = adamw(two_d(P[n]), two_d(grads[n]), two_d(M[n]), two_d(V[n]))
        delta[n], new_m[n], new_v[n] = d.reshape(shape), m.reshape(shape), v.reshape(shape)
    one = jnp.ones((), F32)
    d, m, v = adamw(_pack(small_like + [one]), total, _pack([M[n] for n in SMALL] + [one]), _pack([V[n] for n in SMALL] + [one]))
    for dst, src in ((delta, d), (new_m, m), (new_v, v)):
        dst.update(dict(zip(SMALL, _unpack(src, small_like))))

    return (loss, dx.reshape(B, S, D), *[grads[n] for n in WEIGHTS], *[delta[n] for n in WEIGHTS],
            *[new_m[n] for n in WEIGHTS], *[new_v[n] for n in WEIGHTS])
```

```python
import math

import jax
import jax.numpy as jnp
import numpy as np
from jax import lax
from jax.experimental import pallas as pl
from jax.experimental.pallas import tpu as pltpu

F32 = jnp.float32
BF16 = jnp.bfloat16

HEAD_DIM = 64
CHUNK = 64
N_PREV_CHUNKS = 8
MAX_REL = 128
DEPTH = 4
FFN_RESIDUAL = 0.5
ALPHA = (2 * DEPTH) ** 0.25
LN_EPS = 1e-5
RMS_EPS = 1e-6
ADAM_LR = 0.001
ADAM_B1 = 0.9
ADAM_B2 = 0.999
ADAM_EPS = 1e-08
ADAM_WD = 0.01
ADAM_STEP = 10

N_DEV = 8
LANES = 128
VMEM_LIMIT_BYTES = 56 * 1024 * 1024
ROW_TILE = 512
SB_BLOCK = 128
CA_PAD = CHUNK * N_PREV_CHUNKS
CA_TQ = 2 * CHUNK
CA_WIN = CA_PAD + CA_TQ
CA_NEAR = CA_WIN - (CA_PAD - MAX_REL)
FWD_HEADS = 8
BWD_HEADS = 8
NEG = -1e30
SB_SKIP_LOG = -104.0

NN = ((1,), (0,))
NT = ((1,), (1,))
TN = ((0,), (0,))


def _dot(a, b, dims):
    return lax.dot_general(a, b, (dims, ((), ())), preferred_element_type=F32)


def _params(*sem):
    return pltpu.CompilerParams(dimension_semantics=sem, vmem_limit_bytes=VMEM_LIMIT_BYTES)


def _sds(shape, dtype):
    return jax.ShapeDtypeStruct(shape, dtype)


def _row_tile(n, want=ROW_TILE):
    t = min(want, n)
    while n % t:
        t //= 2
    assert t >= 8, (n, want)
    return t


def ffn_up(xb, wg, wu, carry=None):
    T, D = xb.shape
    F = wg.shape[0]
    tm = _row_tile(T)
    fc = _row_tile(F, 256)

    def body(x_ref, wg_ref, wu_ref, sg_ref, t_ref, a_ref):
        x = x_ref[...]
        for j in range(F // fc):
            sl = pl.ds(j * fc, fc)
            hg = _dot(x, wg_ref[sl, :], NT)
            hu = _dot(x, wu_ref[sl, :], NT)
            s = jax.nn.sigmoid(hg)
            sg = hg * s
            a = sg * hu
            sg_ref[:, sl] = sg.astype(BF16)
            t_ref[:, sl] = (a + s * (hu - a)).astype(BF16)
            a_ref[:, sl] = a.astype(BF16)

    row = pl.BlockSpec((tm, F), lambda i: (i, 0))
    w = pl.BlockSpec((F, D), lambda i: (0, 0))
    return _call(body, carry, grid=(T // tm,), name="ffn_up", sem=("parallel",), args=(xb, wg, wu),
                 in_specs=[pl.BlockSpec((tm, D), lambda i: (i, 0)), w, w],
                 out_specs=[row, row, row], out_shape=[_sds((T, F), BF16)] * 3)


def ffn_bwd_mid(dr, wd, da_dhu, da_dhg):
    T, D = dr.shape
    F = wd.shape[0]
    tm = _row_tile(T)
    fc = _row_tile(F, 256)

    def body(dr_ref, wd_ref, sg_ref, t_ref, dhg_ref, dhu_ref):
        dr_ = dr_ref[...]
        for j in range(F // fc):
            sl = pl.ds(j * fc, fc)
            da = _dot(dr_, wd_ref[sl, :], NT)
            dhu_ref[:, sl] = (da * sg_ref[:, sl].astype(F32)).astype(BF16)
            dhg_ref[:, sl] = (da * t_ref[:, sl].astype(F32)).astype(BF16)

    row = pl.BlockSpec((tm, F), lambda i: (i, 0))
    return pl.pallas_call(
        body, grid=(T // tm,), name="ffn_bwd_mid",
        in_specs=[pl.BlockSpec((tm, D), lambda i: (i, 0)), pl.BlockSpec((F, D), lambda i: (0, 0)), row, row],
        out_specs=[row, row], out_shape=[_sds((T, F), BF16)] * 2,
        compiler_params=_params("parallel"),
    )(dr, wd, da_dhu, da_dhg)


def mm_nn(pairs, add=None, out_dtype=F32, name="mm_nn"):
    M = pairs[0][0].shape[0]
    N = pairs[0][1].shape[1]
    tm = _row_tile(M)
    nc = _row_tile(N, 512)
    n_pairs = len(pairs)

    def body(*refs):
        a_refs = refs[:n_pairs]
        b_refs = refs[n_pairs:2 * n_pairs]
        add_ref = refs[2 * n_pairs] if add is not None else None
        o_ref = refs[-1]
        for j in range(N // nc):
            sl = pl.ds(j * nc, nc)
            acc = _dot(a_refs[0][...], b_refs[0][:, sl], NN)
            for a_ref, b_ref in zip(a_refs[1:], b_refs[1:]):
                acc = acc + _dot(a_ref[...], b_ref[:, sl], NN)
            if add_ref is not None:
                acc = acc + add_ref[:, sl]
            o_ref[:, sl] = acc.astype(out_dtype)

    in_specs = [pl.BlockSpec((tm, a.shape[1]), lambda i: (i, 0)) for a, _ in pairs]
    in_specs += [pl.BlockSpec(b.shape, lambda i: (0, 0)) for _, b in pairs]
    args = [a for a, _ in pairs] + [b for _, b in pairs]
    if add is not None:
        in_specs.append(pl.BlockSpec((tm, N), lambda i: (i, 0)))
        args.append(add)
    return pl.pallas_call(
        body, grid=(M // tm,), name=name, in_specs=in_specs,
        out_specs=pl.BlockSpec((tm, N), lambda i: (i, 0)), out_shape=_sds((M, N), out_dtype),
        compiler_params=_params("parallel"),
    )(*args)


def _mm_specs(pairs, tm):
    specs = [pl.BlockSpec((tm, a.shape[1]), lambda i: (i, 0)) for a, _ in pairs]
    specs += [pl.BlockSpec(b.shape, lambda i: (0, 0), pipeline_mode=pl.Buffered(1)) for _, b in pairs]
    return specs, [a for a, _ in pairs] + [b for _, b in pairs]


def _mm_sum(a_refs, b_refs):
    acc = _dot(a_refs[0][...], b_refs[0][...], NN)
    for a_ref, b_ref in zip(a_refs[1:], b_refs[1:]):
        acc = acc + _dot(a_ref[...], b_ref[...], NN)
    return acc


def _ln_fwd_rows(u, g, b):
    mu = jnp.mean(u, axis=-1, keepdims=True)
    xc = u - mu
    var = jnp.mean(xc * xc, axis=-1, keepdims=True)
    return xc * lax.rsqrt(var + LN_EPS) * g + b


def _ln_bwd_rows(u, dy, g):
    mu = jnp.mean(u, axis=-1, keepdims=True)
    xc = u - mu
    var = jnp.mean(xc * xc, axis=-1, keepdims=True)
    rstd = lax.rsqrt(var + LN_EPS)
    xh = xc * rstd
    dxh = dy * g
    m1 = jnp.mean(dxh, axis=-1, keepdims=True)
    m2 = jnp.mean(dxh * xh, axis=-1, keepdims=True)
    return rstd * (dxh - m1 - xh * m2), dy * xh


def mm_nn_ln(pairs, x, g, b, res_scale, name):
    M, N = x.shape
    tm = _row_tile(M)
    n_pairs = len(pairs)

    def body(*refs):
        a_refs, b_refs = refs[:n_pairs], refs[n_pairs:2 * n_pairs]
        x_ref, g_ref, b_ref, u_ref, y_ref, yb_ref = refs[2 * n_pairs:]
        u = ALPHA * x_ref[...] + res_scale * _mm_sum(a_refs, b_refs)
        y = _ln_fwd_rows(u, g_ref[...], b_ref[...])
        u_ref[...] = u
        y_ref[...] = y
        yb_ref[...] = y.astype(BF16)

    in_specs, args = _mm_specs(pairs, tm)
    row = pl.BlockSpec((tm, N), lambda i: (i, 0))
    vec = pl.BlockSpec((1, N), lambda i: (0, 0))
    return pl.pallas_call(
        body, grid=(M // tm,), name=name, in_specs=in_specs + [row, vec, vec],
        out_specs=[row, row, row], out_shape=[_sds((M, N), F32), _sds((M, N), F32), _sds((M, N), BF16)],
        compiler_params=_params("parallel"),
    )(*args, x, g.reshape(1, N), b.reshape(1, N))


def mm_nn_ln_bwd(pairs, add, u, g, branch_scale, name):
    M, N = u.shape
    tm = _row_tile(M)
    n_pairs = len(pairs)

    def body(*refs):
        a_refs, b_refs = refs[:n_pairs], refs[n_pairs:2 * n_pairs]
        add_ref, u_ref, g_ref, dxa_ref, dr_ref, dg_ref, db_ref = refs[2 * n_pairs:]

        @pl.when(pl.program_id(0) == 0)
        def _():
            dg_ref[...] = jnp.zeros_like(dg_ref)
            db_ref[...] = jnp.zeros_like(db_ref)

        dy = _mm_sum(a_refs, b_refs) + add_ref[...]
        du, dyxh = _ln_bwd_rows(u_ref[...], dy, g_ref[...])
        dxa_ref[...] = ALPHA * du
        dr_ref[...] = (branch_scale * du).astype(BF16)
        dg_ref[...] += jnp.sum(dyxh.reshape(tm // 8, 8, N), axis=0)
        db_ref[...] += jnp.sum(dy.reshape(tm // 8, 8, N), axis=0)

    in_specs, args = _mm_specs(pairs, tm)
    row = pl.BlockSpec((tm, N), lambda i: (i, 0))
    acc = pl.BlockSpec((8, N), lambda i: (0, 0))
    return pl.pallas_call(
        body, grid=(M // tm,), name=name, in_specs=in_specs + [row, row, pl.BlockSpec((1, N), lambda i: (0, 0))],
        out_specs=[row, row, acc, acc],
        out_shape=[_sds((M, N), F32), _sds((M, N), BF16), _sds((8, N), F32), _sds((8, N), F32)],
        compiler_params=_params("arbitrary"),
    )(*args, add, u, g.reshape(1, N))


def mm_nt(a, b, out_dtype=F32, name="mm_nt"):
    M, K = a.shape
    N = b.shape[0]
    tm = _row_tile(M)
    nc = _row_tile(N, 512)

    def body(a_ref, b_ref, o_ref):
        a_ = a_ref[...]
        for j in range(N // nc):
            sl = pl.ds(j * nc, nc)
            o_ref[:, sl] = _dot(a_, b_ref[sl, :], NT).astype(out_dtype)

    return pl.pallas_call(
        body, grid=(M // tm,), name=name,
        in_specs=[pl.BlockSpec((tm, K), lambda i: (i, 0)), pl.BlockSpec((N, K), lambda i: (0, 0))],
        out_specs=pl.BlockSpec((tm, N), lambda i: (i, 0)), out_shape=_sds((M, N), out_dtype),
        compiler_params=_params("parallel"),
    )(a, b)


def proj_in(xb, w, B, S):
    D = xb.shape[1]
    N = w.shape[0]
    tm = CA_PAD
    nblk = S // tm
    nc = _row_tile(N, 512)
    assert S % tm == 0

    def body(x_ref, w_ref, o_ref):
        i = pl.program_id(1)

        @pl.when(i == 0)
        def _():
            o_ref[...] = jnp.zeros_like(o_ref)

        @pl.when(i > 0)
        def _():
            x = x_ref[...]
            for j in range(N // nc):
                sl = pl.ds(j * nc, nc)
                o_ref[0, :, sl] = _dot(x, w_ref[sl, :], NT).astype(BF16)

    return pl.pallas_call(
        body, grid=(B, nblk + 1), name="proj_in",
        in_specs=[pl.BlockSpec((tm, D), lambda b, i: (b * nblk + jnp.maximum(i - 1, 0), 0)),
                  pl.BlockSpec((N, D), lambda b, i: (0, 0))],
        out_specs=pl.BlockSpec((1, tm, N), lambda b, i: (b, i, 0)), out_shape=_sds((B, CA_PAD + S, N), BF16),
        compiler_params=_params("parallel", "arbitrary"),
    )(xb, w)


def mm_tn(parts, b, name="mm_tn", carry=None):
    T, N = b.shape
    widths = [a.shape[1] for a in parts]
    M = sum(widths)
    tk = _row_tile(T)
    steps = T // tk
    n_parts = len(parts)

    def body(*refs):
        a_refs, b_ref, o_ref, acc_ref = refs[:n_parts], refs[n_parts], refs[n_parts + 1], refs[n_parts + 2]
        i = pl.program_id(0)

        @pl.when(i == 0)
        def _():
            acc_ref[...] = jnp.zeros_like(acc_ref)

        b_ = b_ref[...]
        row = 0
        for a_ref, width in zip(a_refs, widths):
            mc = _row_tile(width, 256)
            for j in range(width // mc):
                acc_ref[pl.ds(row, mc), :] += _dot(a_ref[:, pl.ds(j * mc, mc)], b_, TN)
                row += mc

        @pl.when(i == steps - 1)
        def _():
            o_ref[...] = acc_ref[...].astype(BF16)

    (out,), got = _call(
        body, carry, grid=(steps,), name=name, sem=("arbitrary",), args=(*parts, b),
        in_specs=[pl.BlockSpec((tk, w), lambda i: (i, 0)) for w in widths] + [pl.BlockSpec((tk, N), lambda i: (i, 0))],
        out_specs=[pl.BlockSpec((M, N), lambda i: (0, 0))], out_shape=[_sds((M, N), BF16)],
        scratch_shapes=[pltpu.VMEM((M, N), F32)])
    return out, got


def ffn_dw(dhg, dhu, a, xb, dr, carry=None):
    T, F = dhg.shape
    D = xb.shape[1]
    tk = _row_tile(T)
    mc = _row_tile(F, 256)
    steps = T // tk
    last = steps - 1

    def body(dhg_ref, dhu_ref, a_ref, xb_ref, dr_ref, o_ref, acc_ref):
        p = pl.program_id(0)
        i = pl.program_id(1)

        @pl.when(i == 0)
        def _():
            acc_ref[...] = jnp.zeros_like(acc_ref)

        def accumulate(lhs_ref, rhs_ref):
            rhs = rhs_ref[...]
            for j in range(F // mc):
                sl = pl.ds(j * mc, mc)
                acc_ref[sl, :] += _dot(lhs_ref[:, sl], rhs, TN)

        pl.when(p == 0)(lambda: accumulate(dhg_ref, xb_ref))
        pl.when(p == 1)(lambda: accumulate(dhu_ref, xb_ref))
        pl.when(p == 2)(lambda: accumulate(a_ref, dr_ref))

        @pl.when(i == last)
        def _():
            o_ref[0] = acc_ref[...].astype(BF16)

    def during(phases, width):
        lo, hi = phases
        return pl.BlockSpec((tk, width), lambda p, i: (jnp.where(p < lo, 0, jnp.where(p > hi, last, i)), 0))

    (out,), got = _call(
        body, carry, grid=(3, steps), name="ffn_dw", sem=("arbitrary", "arbitrary"), args=(dhg, dhu, a, xb, dr),
        in_specs=[during((0, 0), F), during((1, 1), F), during((2, 2), F), during((0, 1), D), during((2, 2), D)],
        out_specs=[pl.BlockSpec((1, F, D), lambda p, i: (p, 0, 0))], out_shape=[_sds((3, F, D), BF16)],
        scratch_shapes=[pltpu.VMEM((F, D), F32)])
    return out, got


def ln_bwd(u, dy, g, branch_scale):
    T, D = u.shape
    tm = _row_tile(T)

    def body(u_ref, dy_ref, g_ref, dxa_ref, dr_ref, dg_ref, db_ref):
        @pl.when(pl.program_id(0) == 0)
        def _():
            dg_ref[...] = jnp.zeros_like(dg_ref)
            db_ref[...] = jnp.zeros_like(db_ref)

        dy_ = dy_ref[...]
        du, dyxh = _ln_bwd_rows(u_ref[...], dy_, g_ref[...])
        dxa_ref[...] = ALPHA * du
        dr_ref[...] = (branch_scale * du).astype(BF16)
        dg_ref[...] += jnp.sum(dyxh.reshape(tm // 8, 8, D), axis=0)
        db_ref[...] += jnp.sum(dy_.reshape(tm // 8, 8, D), axis=0)

    row = pl.BlockSpec((tm, D), lambda i: (i, 0))
    acc = pl.BlockSpec((8, D), lambda i: (0, 0))
    dxa, dr, dg, db = pl.pallas_call(
        body, grid=(T // tm,), name="ln_bwd", in_specs=[row, row, pl.BlockSpec((1, D), lambda i: (0, 0))],
        out_specs=[row, row, acc, acc],
        out_shape=[_sds((T, D), F32), _sds((T, D), BF16), _sds((8, D), F32), _sds((8, D), F32)],
        compiler_params=_params("arbitrary"),
    )(u, dy, g.reshape(1, D))
    return dxa, dr, dg, db


def loss_head(y, target):
    T, D = y.shape
    tm = _row_tile(T)

    def body(y_ref, t_ref, dy_ref, l_ref):
        @pl.when(pl.program_id(0) == 0)
        def _():
            l_ref[...] = jnp.zeros_like(l_ref)

        e = y_ref[...] - t_ref[...]
        dy_ref[...] = e * (1.0 / D)
        l_ref[...] += jnp.sum((e * e).reshape(tm // 8, 8, D), axis=0) * (0.5 / D)

    row = pl.BlockSpec((tm, D), lambda i: (i, 0))
    return pl.pallas_call(
        body, grid=(T // tm,), name="loss_head", in_specs=[row, row],
        out_specs=[row, pl.BlockSpec((8, D), lambda i: (0, 0))],
        out_shape=[_sds((T, D), F32), _sds((8, D), F32)],
        compiler_params=_params("arbitrary"),
    )(y, target)


def adamw(w, g, m, v):
    R, C = w.shape
    tr = R
    for cand in (512, 256, 128, 64, 32, 16, 8):
        if R % cand == 0:
            tr = cand
            break
    c1 = 1.0 - ADAM_B1 ** ADAM_STEP
    c2 = 1.0 - ADAM_B2 ** ADAM_STEP

    def body(w_ref, g_ref, m_ref, v_ref, d_ref, mo_ref, vo_ref):
        g_ = g_ref[...]
        m_ = ADAM_B1 * m_ref[...] + (1.0 - ADAM_B1) * g_
        v_ = ADAM_B2 * v_ref[...] + (1.0 - ADAM_B2) * (g_ * g_)
        m_hat = m_ / c1
        v_hat = v_ / c2
        d_ref[...] = -ADAM_LR * (m_hat / (jnp.sqrt(v_hat) + ADAM_EPS) + ADAM_WD * w_ref[...])
        mo_ref[...] = m_
        vo_ref[...] = v_

    blk = pl.BlockSpec((tr, C), lambda i: (i, 0))
    return pl.pallas_call(
        body, grid=(R // tr,), name="adamw", in_specs=[blk] * 4, out_specs=[blk] * 3,
        out_shape=[_sds((R, C), F32)] * 3, compiler_params=_params("parallel"),
    )(w, g, m, v)


def _lane_lo(rows):
    return lax.broadcasted_iota(jnp.int32, (rows, LANES), 1) < HEAD_DIM


def _pair_mean(x, lo):
    s0 = jnp.sum(jnp.where(lo, x, 0.0), axis=-1, keepdims=True)
    s1 = jnp.sum(jnp.where(lo, 0.0, x), axis=-1, keepdims=True)
    return jnp.where(lo, s0, s1) * (1.0 / HEAD_DIM)


def _rms_fwd(o, gain, lo):
    r = lax.rsqrt(_pair_mean(o * o, lo) + RMS_EPS)
    return (o * r * gain).astype(BF16)


def _rms_bwd(o, dm, gain, lo):
    r = lax.rsqrt(_pair_mean(o * o, lo) + RMS_EPS)
    oh = o * r
    dg = jnp.sum(dm * oh, axis=0, keepdims=True)
    doh = dm * gain
    do = r * (doh - oh * _pair_mean(doh * oh, lo))
    return do, dg


def _split_heads(x, lo):
    zero = jnp.zeros_like(x)
    return [jnp.where(lo, x, zero), jnp.where(lo, zero, x)]


def _merge_pairs(per_head, lo):
    return [jnp.where(lo, per_head[2 * p], per_head[2 * p + 1]) for p in range(len(per_head) // 2)]


def _pair_cols(h):
    return slice((h // 2) * LANES, (h // 2 + 1) * LANES)


def _split_dot(x, tri):
    n = x.shape[0]
    hi = x.astype(BF16)
    lo = (x - hi.astype(F32)).astype(BF16)
    both = _dot(jnp.concatenate([hi, lo], axis=0), tri, NN)
    return both[:n] + both[n:]


def _log_keep(z):
    return -(jnp.maximum(z, 0.0) + jnp.log(1.0 + jnp.exp(-jnp.abs(z))))


def _attn_dims(h3p, gain, heads):
    B, SP, C = h3p.shape
    HD = gain.shape[1]
    hs = min(heads, HD // HEAD_DIM)
    W = hs * HEAD_DIM
    assert C == 6 * HD and W % LANES == 0 and HD % W == 0
    return B, SP - CA_PAD, HD, hs, W, HD // W


def sb_fwd(h3p, gain, carry=None):
    B, S, HD, hs, W, ngrp = _attn_dims(h3p, gain, FWD_HEADS)
    tb = min(SB_BLOCK, S)
    nq = S // tb
    off = CA_PAD // tb
    scale = 1.0 / math.sqrt(HEAD_DIM)

    def body(q_ref, k_ref, v_ref, g_ref, o_ref, m_ref, c_ref):
        qi = pl.program_id(2)
        lo = _lane_lo(tb)
        row = lax.broadcasted_iota(jnp.int32, (tb, tb), 0)
        col = lax.broadcasted_iota(jnp.int32, (tb, tb), 1)
        rev_incl = (row >= col).astype(BF16)
        lane = lax.broadcasted_iota(jnp.int32, (tb, nq), 1)
        qm = []
        for p in range(hs // 2):
            qm += _split_heads((q_ref[0, :, p * LANES:(p + 1) * LANES].astype(F32) * scale).astype(BF16), lo)

        def blocks(kb, carries_, mask):
            ks = pl.ds(pl.multiple_of(kb * tb + CA_PAD, tb), tb)
            hh = range(hs)
            zs = [_dot(qm[h], k_ref[0, ks, _pair_cols(h)], NT) for h in hh]
            lks = [_log_keep(z) for z in zs]
            if mask is not None:
                lks = [jnp.where(mask, lk, 0.0) for lk in lks]
            cums = [_split_dot(lk, rev_incl) for lk in lks]
            ws = []
            for h in hh:
                logw = zs[h] + cums[h] + carries_[h]
                if mask is not None:
                    logw = jnp.where(mask, logw, NEG)
                ws.append(jnp.exp(logw).astype(BF16))
            pvs = [_dot(ws[h], v_ref[0, ks, _pair_cols(h)], NN) for h in hh]
            return [(pvs[h], cums[h][:, 0:1]) for h in hh]

        diag = blocks(qi, [jnp.zeros((tb, 1), F32)] * hs, col < row)
        accs = _merge_pairs([d[0] for d in diag], lo)
        carries = [d[1] for d in diag]
        cars = [jnp.where(lane == qi, 0.0, NEG)] * hs

        def cond(st):
            kb, carries_, _, _ = st
            top = carries_[0]
            for c in carries_[1:]:
                top = jnp.maximum(top, c)
            return jnp.logical_and(kb >= 0, jnp.max(top) > SB_SKIP_LOG)

        def step(st):
            kb, carries_, accs_, cars_ = st
            out = blocks(kb, carries_, None)
            pv = _merge_pairs([o[0] for o in out], lo)
            return (kb - 1, [c + o[1] for c, o in zip(carries_, out)], [a + p for a, p in zip(accs_, pv)],
                    [jnp.where(lane == kb, c, cs) for c, cs in zip(carries_, cars_)])

        _, _, accs, cars = lax.while_loop(cond, step, (qi - 1, carries, accs, cars))
        for p, acc in enumerate(accs):
            cols = slice(p * LANES, (p + 1) * LANES)
            o_ref[0, :, cols] = acc
            m_ref[0, :, cols] = _rms_fwd(acc, g_ref[:, cols], lo)
        for h in range(hs):
            c_ref[0, h] = cars[h]

    qspec = pl.BlockSpec((1, tb, W), lambda g, b, i: (b, i + off, g))
    ospec = pl.BlockSpec((1, tb, W), lambda g, b, i: (b, i, g))
    return _call(
        body, carry, grid=(ngrp, B, nq), name="sb_fwd", sem=("parallel", "parallel", "arbitrary"),
        args=(h3p, h3p, h3p, gain),
        in_specs=[qspec, pl.BlockSpec((1, CA_PAD + S, W), lambda g, b, i: (b, 0, ngrp + g)),
                  pl.BlockSpec((1, CA_PAD + S, W), lambda g, b, i: (b, 0, 2 * ngrp + g)),
                  pl.BlockSpec((1, W), lambda g, b, i: (0, g))],
        out_specs=[ospec, ospec, pl.BlockSpec((1, hs, tb, nq), lambda g, b, i: (b, g, i, 0))],
        out_shape=[_sds((B, S, HD), F32), _sds((B, S, HD), BF16), _sds((B, HD // HEAD_DIM, S, nq), F32)])


def sb_bwd(h3p, o, dmix, gain, cars, carry=None):
    B, S, HD, hs, W, ngrp = _attn_dims(h3p, gain, BWD_HEADS)
    tb = min(SB_BLOCK, S)
    nq = S // tb
    off = CA_PAD // tb
    scale = 1.0 / math.sqrt(HEAD_DIM)

    def body(q_ref, k_ref, v_ref, o_ref, dm_ref, g_ref, c_ref, dq_ref, dk_ref, dv_ref, dg_ref, dk_acc, dv_acc):
        qi = pl.program_id(2)

        @pl.when(qi == 0)
        def _():
            dk_acc[...] = jnp.zeros_like(dk_acc)
            dv_acc[...] = jnp.zeros_like(dv_acc)

        lo = _lane_lo(tb)
        row = lax.broadcasted_iota(jnp.int32, (tb, tb), 0)
        col = lax.broadcasted_iota(jnp.int32, (tb, tb), 1)
        rev_incl = (row >= col).astype(BF16)
        fwd_incl = (row <= col).astype(BF16)
        lane = lax.broadcasted_iota(jnp.int32, (tb, nq), 1)
        below = lax.broadcasted_iota(jnp.int32, (1, nq), 1) < qi
        qm, dom, cars_, seen = [], [], [], None
        for p in range(hs // 2):
            cols = slice(p * LANES, (p + 1) * LANES)
            qm += _split_heads((q_ref[0, :, cols].astype(F32) * scale).astype(BF16), lo)
            do, dg = _rms_bwd(o_ref[0, :, cols], dm_ref[0, :, cols], g_ref[:, cols], lo)
            dg_ref[0, 0, :, cols] = dg
            dom += _split_heads(do.astype(BF16), lo)
        for h in range(hs):
            cars_.append(c_ref[0, h])
            visited = jnp.logical_and(jnp.max(cars_[h], axis=0, keepdims=True) > SB_SKIP_LOG, below)
            n = jnp.sum(visited.astype(jnp.int32), axis=1, keepdims=True)
            seen = n if seen is None else jnp.maximum(seen, n)
        first = qi - jnp.max(seen)

        def blocks(kb, gsums, dqs, mask):
            ks = pl.ds(pl.multiple_of(kb * tb + CA_PAD, tb), tb)
            ko = pl.ds(pl.multiple_of(kb * tb, tb), tb)
            hh = range(hs)
            kk = [k_ref[0, ks, p * LANES:(p + 1) * LANES] for p in range(hs // 2)]
            vv = [v_ref[0, ks, p * LANES:(p + 1) * LANES] for p in range(hs // 2)]
            zs = [_dot(qm[h], kk[h // 2], NT) for h in hh]
            dws = [_dot(dom[h], vv[h // 2], NT) for h in hh]
            raw = [_log_keep(z) for z in zs]
            lks = raw if mask is None else [jnp.where(mask, lk, 0.0) for lk in raw]
            cums = [_split_dot(lk, rev_incl) for lk in lks]
            ws = []
            for h in hh:
                carry = jnp.sum(jnp.where(lane == kb, cars_[h], 0.0), axis=1, keepdims=True)
                logw = zs[h] + cums[h] + carry
                if mask is not None:
                    logw = jnp.where(mask, logw, NEG)
                ws.append(jnp.exp(logw))
            gws = [ws[h] * dws[h] for h in hh]
            gcums = [_split_dot(gws[h], fwd_incl) + gsums[h] for h in hh]
            dzb = []
            for h in hh:
                dz = gws[h] - jnp.exp(zs[h] + raw[h]) * gcums[h]
                if mask is not None:
                    dz = jnp.where(mask, dz, 0.0)
                dzb.append(dz.astype(BF16))
            wb = [w.astype(BF16) for w in ws]
            new_dq = [dqs[h] + _dot(dzb[h], kk[h // 2], NN) for h in hh]
            for p in range(hs // 2):
                cols = slice(p * LANES, (p + 1) * LANES)
                dk_acc[ko, cols] += _dot(dzb[2 * p], qm[2 * p], TN) + _dot(dzb[2 * p + 1], qm[2 * p + 1], TN)
                dv_acc[ko, cols] += _dot(wb[2 * p], dom[2 * p], TN) + _dot(wb[2 * p + 1], dom[2 * p + 1], TN)
            return [g[:, tb - 1:tb] for g in gcums], new_dq

        def step(kb, st):
            return blocks(kb, st[0], st[1], None)

        init = ([jnp.zeros((tb, 1), F32)] * hs, [jnp.zeros((tb, LANES), F32)] * hs)
        gsum, dq = lax.fori_loop(first, qi, step, init)
        _, dq = blocks(qi, gsum, dq, col < row)
        for p, d in enumerate(_merge_pairs(dq, lo)):
            dq_ref[0, :, p * LANES:(p + 1) * LANES] = (d * scale).astype(BF16)

        @pl.when(qi == nq - 1)
        def _():
            dk_ref[0] = dk_acc[...].astype(BF16)
            dv_ref[0] = dv_acc[...].astype(BF16)

    once = pl.Buffered(1)
    qspec = pl.BlockSpec((1, tb, W), lambda g, b, i: (b, i + off, g))
    ospec = pl.BlockSpec((1, tb, W), lambda g, b, i: (b, i, g))
    kvout = pl.BlockSpec((1, S, W), lambda g, b, i: (b, 0, g), pipeline_mode=once)
    return _call(
        body, carry, grid=(ngrp, B, nq), name="sb_bwd", sem=("parallel", "parallel", "arbitrary"),
        args=(h3p, h3p, h3p, o, dmix, gain, cars),
        in_specs=[qspec, pl.BlockSpec((1, CA_PAD + S, W), lambda g, b, i: (b, 0, ngrp + g), pipeline_mode=once),
                  pl.BlockSpec((1, CA_PAD + S, W), lambda g, b, i: (b, 0, 2 * ngrp + g), pipeline_mode=once),
                  ospec, ospec, pl.BlockSpec((1, W), lambda g, b, i: (0, g)),
                  pl.BlockSpec((1, hs, tb, nq), lambda g, b, i: (b, g, i, 0))],
        out_specs=[ospec, kvout, kvout, pl.BlockSpec((1, 1, 1, W), lambda g, b, i: (b, i, 0, g))],
        out_shape=[_sds((B, S, HD), BF16), _sds((B, S, HD), BF16), _sds((B, S, HD), BF16), _sds((B, nq, 1, HD), F32)],
        scratch_shapes=[pltpu.VMEM((S, W), F32), pltpu.VMEM((S, W), F32)])


def _ca_rel_index():
    width = CA_WIN + CA_TQ
    c = np.arange(width)
    dj = np.where(c < CA_WIN, c, c - width)
    return np.clip(CA_PAD - dj, -MAX_REL, MAX_REL) + MAX_REL, width


def _ca_onehot():
    idx, _ = _ca_rel_index()
    return (idx[:, None] == np.arange(2 * MAX_REL + 1)[None, :]).astype(np.float32)


def ca_bias(rel_bias):
    _, width = _ca_rel_index()
    lead = rel_bias.shape[:-1]
    by_offset = jnp.dot(rel_bias, jnp.asarray(_ca_onehot().T), precision=lax.Precision.HIGHEST)
    tile = jnp.broadcast_to(by_offset[..., None, :], lead + (CA_TQ, width)).reshape(lead + (CA_TQ * width,))
    tile = tile[..., :CA_TQ * (width - 1)].reshape(lead + (CA_TQ, width - 1))[..., :CA_WIN]
    t = np.arange(CA_TQ)[:, None] // CHUNK * CHUNK
    j = np.arange(CA_WIN)[None, :]
    return jnp.where((j >= t) & (j < t + CA_PAD + CHUNK), tile, NEG)


def rel_bias_grad(db_near):
    H = db_near.shape[0]
    _, width = _ca_rel_index()
    db = jnp.pad(db_near, ((0, 0), (0, 0), (CA_WIN - CA_NEAR, 0)))
    x = jnp.pad(db, ((0, 0), (0, 0), (0, width - 1 - CA_WIN))).reshape(H, CA_TQ * (width - 1))
    x = jnp.pad(x, ((0, 0), (0, CA_TQ))).reshape(H, CA_TQ, width).sum(axis=1)
    grad = jnp.dot(x, jnp.asarray(_ca_onehot()), precision=lax.Precision.HIGHEST)
    return grad.at[:, 2 * MAX_REL].add(-db_near.sum((1, 2)))


def _ca_scores(qm_h, kk, bias_h, valid):
    return jnp.where(valid, _dot(qm_h, kk, NT) + bias_h, NEG)


def _ca_softmax(s):
    e = jnp.exp(s - jnp.max(s, axis=-1, keepdims=True))
    return e * (1.0 / jnp.sum(e, axis=-1, keepdims=True))


def ca_fwd(h3p, bias, gain, carry=None):
    B, S, HD, hs, W, ngrp = _attn_dims(h3p, gain, FWD_HEADS)
    scale = 1.0 / math.sqrt(HEAD_DIM)
    off = CA_PAD // CA_TQ

    def body(q_ref, k_ref, v_ref, b_ref, g_ref, o_ref, m_ref):
        q0 = pl.program_id(2) * CA_TQ
        ks = pl.ds(pl.multiple_of(q0, CA_TQ), CA_WIN)
        lo = _lane_lo(CA_TQ)
        valid = lax.broadcasted_iota(jnp.int32, (CA_TQ, CA_WIN), 1) + q0 >= CA_PAD
        qm, kk, vv = [], [], []
        for p in range(hs // 2):
            cols = slice(p * LANES, (p + 1) * LANES)
            qm += _split_heads((q_ref[0, :, cols].astype(F32) * scale).astype(BF16), lo)
            kk.append(k_ref[0, ks, cols])
            vv.append(v_ref[0, ks, cols])
        ss = [_ca_scores(qm[h], kk[h // 2], b_ref[h], valid) for h in range(hs)]
        ps = [_ca_softmax(s).astype(BF16) for s in ss]
        pv = [_dot(ps[h], vv[h // 2], NN) for h in range(hs)]
        for p, o in enumerate(_merge_pairs(pv, lo)):
            cols = slice(p * LANES, (p + 1) * LANES)
            o_ref[0, :, cols] = o
            m_ref[0, :, cols] = _rms_fwd(o, g_ref[:, cols], lo)

    ospec = pl.BlockSpec((1, CA_TQ, W), lambda g, b, i: (b, i, g))
    return _call(
        body, carry, grid=(ngrp, B, S // CA_TQ), name="ca_fwd", sem=("parallel", "parallel", "arbitrary"),
        args=(h3p, h3p, h3p, bias, gain),
        in_specs=[pl.BlockSpec((1, CA_TQ, W), lambda g, b, i: (b, i + off, 3 * ngrp + g)),
                  pl.BlockSpec((1, CA_PAD + S, W), lambda g, b, i: (b, 0, 4 * ngrp + g)),
                  pl.BlockSpec((1, CA_PAD + S, W), lambda g, b, i: (b, 0, 5 * ngrp + g)),
                  pl.BlockSpec((hs, CA_TQ, CA_WIN), lambda g, b, i: (g, 0, 0)),
                  pl.BlockSpec((1, W), lambda g, b, i: (0, g))],
        out_specs=[ospec, ospec], out_shape=[_sds((B, S, HD), F32), _sds((B, S, HD), BF16)])


def ca_bwd(h3p, bias, o, dmix, gain, carry=None):
    B, S, HD, hs, W, ngrp = _attn_dims(h3p, gain, BWD_HEADS)
    scale = 1.0 / math.sqrt(HEAD_DIM)
    nq = S // CA_TQ
    off = CA_PAD // CA_TQ

    def body(q_ref, k_ref, v_ref, b_ref, o_ref, dm_ref, g_ref, dq_ref, dk_ref, dv_ref, dg_ref, db_ref, dk_acc, dv_acc):
        bi = pl.program_id(1)
        qi = pl.program_id(2)

        @pl.when(qi == 0)
        def _():
            dk_acc[...] = jnp.zeros_like(dk_acc)
            dv_acc[...] = jnp.zeros_like(dv_acc)

        @pl.when(jnp.logical_and(qi == 0, bi == 0))
        def _():
            db_ref[...] = jnp.zeros_like(db_ref)

        q0 = qi * CA_TQ
        ks = pl.ds(pl.multiple_of(q0, CA_TQ), CA_WIN)
        lo = _lane_lo(CA_TQ)
        valid = lax.broadcasted_iota(jnp.int32, (CA_TQ, CA_WIN), 1) + q0 >= CA_PAD
        qm, dom, kk, vv = [], [], [], []
        for p in range(hs // 2):
            cols = slice(p * LANES, (p + 1) * LANES)
            qm += _split_heads((q_ref[0, :, cols].astype(F32) * scale).astype(BF16), lo)
            do, dg = _rms_bwd(o_ref[0, :, cols], dm_ref[0, :, cols], g_ref[:, cols], lo)
            dg_ref[0, 0, :, cols] = dg
            dom += _split_heads(do.astype(BF16), lo)
            kk.append(k_ref[0, ks, cols])
            vv.append(v_ref[0, ks, cols])
        dq = []
        for h0 in range(0, hs, 4):
            hh = range(h0, min(h0 + 4, hs))
            ss = {h: _ca_scores(qm[h], kk[h // 2], b_ref[h], valid) for h in hh}
            dps = {h: _dot(dom[h], vv[h // 2], NT) for h in hh}
            ps = {h: _ca_softmax(ss[h]) for h in hh}
            dss = {h: ps[h] * (dps[h] - jnp.sum(ps[h] * dps[h], axis=-1, keepdims=True)) for h in hh}
            for h in hh:
                db_ref[h] += dss[h][:, CA_WIN - CA_NEAR:]
            dsb = {h: dss[h].astype(BF16) for h in hh}
            pb = {h: ps[h].astype(BF16) for h in hh}
            dq += [_dot(dsb[h], kk[h // 2], NN) for h in hh]
            for p in range(hh[0] // 2, (hh[-1] + 1) // 2):
                cols = slice(p * LANES, (p + 1) * LANES)
                dk_acc[ks, cols] += _dot(dsb[2 * p], qm[2 * p], TN) + _dot(dsb[2 * p + 1], qm[2 * p + 1], TN)
                dv_acc[ks, cols] += _dot(pb[2 * p], dom[2 * p], TN) + _dot(pb[2 * p + 1], dom[2 * p + 1], TN)
        for p, d in enumerate(_merge_pairs(dq, lo)):
            dq_ref[0, :, p * LANES:(p + 1) * LANES] = (d * scale).astype(BF16)

        @pl.when(qi == nq - 1)
        def _():
            dk_ref[0] = dk_acc[CA_PAD:, :].astype(BF16)
            dv_ref[0] = dv_acc[CA_PAD:, :].astype(BF16)

    once = pl.Buffered(1)
    ospec = pl.BlockSpec((1, CA_TQ, W), lambda g, b, i: (b, i, g))
    kvout = pl.BlockSpec((1, S, W), lambda g, b, i: (b, 0, g), pipeline_mode=once)
    bspec = pl.BlockSpec((hs, CA_TQ, CA_WIN), lambda g, b, i: (g, 0, 0))
    return _call(
        body, carry, grid=(ngrp, B, nq), name="ca_bwd", sem=("parallel", "arbitrary", "arbitrary"),
        args=(h3p, h3p, h3p, bias, o, dmix, gain),
        in_specs=[pl.BlockSpec((1, CA_TQ, W), lambda g, b, i: (b, i + off, 3 * ngrp + g)),
                  pl.BlockSpec((1, CA_PAD + S, W), lambda g, b, i: (b, 0, 4 * ngrp + g), pipeline_mode=once),
                  pl.BlockSpec((1, CA_PAD + S, W), lambda g, b, i: (b, 0, 5 * ngrp + g), pipeline_mode=once),
                  bspec, ospec, pl.BlockSpec((1, CA_TQ, W), lambda g, b, i: (b, i, ngrp + g)),
                  pl.BlockSpec((1, W), lambda g, b, i: (0, g))],
        out_specs=[ospec, kvout, kvout, pl.BlockSpec((1, 1, 1, W), lambda g, b, i: (b, i, 0, g)),
                   pl.BlockSpec((hs, CA_TQ, CA_NEAR), lambda g, b, i: (g, 0, 0))],
        out_shape=[_sds((B, S, HD), BF16), _sds((B, S, HD), BF16), _sds((B, S, HD), BF16),
                   _sds((B, nq, 1, HD), F32), _sds((HD // HEAD_DIM, CA_TQ, CA_NEAR), F32)],
        scratch_shapes=[pltpu.VMEM((CA_PAD + S, W), F32), pltpu.VMEM((CA_PAD + S, W), F32)])


_ANY = pl.BlockSpec(memory_space=pl.ANY)
_MESH = pl.DeviceIdType.MESH


def _mesh_pos():
    return lax.axis_index("x"), lax.axis_index("y"), lax.axis_index("c")


def _dev_index(p):
    return 4 * p[0] + 2 * p[1] + p[2]


def _flip(pos, k):
    return tuple(1 - v if (k >> (2 - a)) & 1 else v for a, v in enumerate(pos))


class _Plan:
    def __init__(self, operands, n_matrices):
        self.operands = list(operands)
        self.n_ops = len(self.operands)
        self.nm = n_matrices
        self.scratch = [pltpu.SemaphoreType.DMA((7, self.nm)), pltpu.SemaphoreType.DMA((7, self.nm)),
                        pltpu.SemaphoreType.DMA((self.nm,))]


class _Gather(_Plan):
    def __init__(self, mats):
        super().__init__(mats, len(mats))
        self.rows = [m.shape[0] for m in mats]
        self.out_shape = [_sds((N_DEV * m.shape[0], m.shape[1]), m.dtype) for m in mats]

    def begin(self, ins, outs, sems):
        nm, rows = self.nm, self.rows
        send_sems, recv_sems, local_sems = sems
        x, y, c = _mesh_pos()
        me, sibling = (x, y, c), (x, y, 1 - c)
        chips = [(1 - x, y), (x, 1 - y), (1 - x, 1 - y)]

        def block(m, p):
            start = pl.multiple_of(_dev_index(p) * rows[m], 16)
            return outs[m].at[pl.ds(start, rows[m]), :]

        def copy(k, m, blk, to, src=None):
            return pltpu.make_async_remote_copy(
                src_ref=block(m, blk) if src is None else src, dst_ref=block(m, blk),
                send_sem=send_sems.at[k, m], recv_sem=recv_sems.at[k, m], device_id=to, device_id_type=_MESH)

        def mine():
            return [pltpu.make_async_copy(ins[m], block(m, me), local_sems.at[m]) for m in range(nm)]

        def first():
            own = [copy(0, m, me, sibling, src=ins[m]) for m in range(nm)]
            return own + [copy(1 + j, m, me, (*chip, c), src=ins[m]) for j, chip in enumerate(chips) for m in range(nm)]

        def start():
            for cp in mine() + first():
                cp.start()

        def passed():
            return [copy(4 + j, m, (*chip, c), sibling) for j, chip in enumerate(chips) for m in range(nm)]

        def pass_on():
            for j, chip in enumerate(chips):
                for m in range(nm):
                    copy(1 + j, m, (*chip, c), me).wait_recv()
                    copy(4 + j, m, (*chip, c), sibling).start()

        def finish():
            for m in range(nm):
                copy(0, m, sibling, me).wait_recv()
            for j, chip in enumerate(chips):
                for m in range(nm):
                    copy(4 + j, m, (*chip, 1 - c), me).wait_recv()
            for cp in first() + passed():
                cp.wait_send()
            for cp in mine():
                cp.wait()

        return start, pass_on, finish


class _Exchange(_Plan):
    def __init__(self, grads):
        self.where = [(i, j) for i, g in enumerate(grads) for j in (range(g.shape[0]) if g.ndim == 3 else [None])]
        super().__init__(grads, len(self.where))
        self.rows = [grads[i].shape[-2] // N_DEV for i, _ in self.where]
        self.out_shape = [_sds(g.shape[:-2] + (N_DEV, g.shape[-2] // N_DEV, g.shape[-1]), g.dtype) for g in grads]

    def begin(self, ins, outs, sems):
        nm, rows = self.nm, self.rows
        send_sems, recv_sems, local_sems = sems
        me = _mesh_pos()
        my = _dev_index(me)

        def piece(m, idx):
            i, j = self.where[m]
            rows_ = pl.ds(pl.multiple_of(idx * rows[m], 16), rows[m])
            return ins[i].at[rows_, :] if j is None else ins[i].at[j, rows_, :]

        def landing(m, slot):
            i, j = self.where[m]
            return outs[i].at[slot] if j is None else outs[i].at[j, slot]

        def copy(k, m, src_idx, slot, to):
            return pltpu.make_async_remote_copy(
                src_ref=piece(m, src_idx), dst_ref=landing(m, slot),
                send_sem=send_sems.at[k - 1, m], recv_sem=recv_sems.at[k - 1, m], device_id=to, device_id_type=_MESH)

        def mine():
            return [pltpu.make_async_copy(piece(m, my), landing(m, my), local_sems.at[m]) for m in range(nm)]

        def sends():
            return [copy(k, m, _dev_index(_flip(me, k)), my, _flip(me, k)) for k in range(1, N_DEV) for m in range(nm)]

        def start():
            for cp in mine() + sends():
                cp.start()

        def finish():
            for k in range(1, N_DEV):
                peer = _flip(me, k)
                for m in range(nm):
                    copy(k, m, my, _dev_index(peer), peer).wait_recv()
            for cp in sends():
                cp.wait_send()
            for cp in mine():
                cp.wait()

        return start, None, finish


def _run_plan(plan, name):
    nm = plan.n_ops

    def body(*refs):
        start, middle, finish = plan.begin(refs[:nm], refs[nm:2 * nm], refs[2 * nm:])
        start()
        if middle is not None:
            middle()
        finish()

    return pl.pallas_call(body, name=name, in_specs=[_ANY] * nm, out_specs=[_ANY] * nm, out_shape=plan.out_shape,
                          scratch_shapes=plan.scratch)(*plan.operands)


def _call(body, carry, *, grid, in_specs, out_specs, out_shape, args, name, sem, scratch_shapes=()):
    if carry is None:
        outs = pl.pallas_call(body, grid=grid, name=name, in_specs=in_specs, out_specs=out_specs, out_shape=out_shape,
                              scratch_shapes=list(scratch_shapes), compiler_params=_params(*sem))(*args)
        return outs, None
    n_in, n_out, n_sc, nm = len(in_specs), len(out_specs), len(scratch_shapes), carry.n_ops

    def carrier(*refs):
        refs = list(refs)
        own_in, refs = refs[:n_in], refs[n_in:]
        c_in, refs = refs[:nm], refs[nm:]
        own_out, refs = refs[:n_out], refs[n_out:]
        c_out, refs = refs[:nm], refs[nm:]
        own_sc, c_sc = refs[:n_sc], refs[n_sc:]
        start, middle, finish = carry.begin(c_in, c_out, c_sc)
        step, steps = 0, 1
        for axis, n in enumerate(grid):
            step = step * n + pl.program_id(axis)
            steps *= n
        pl.when(step == 0)(start)
        body(*own_in, *own_out, *own_sc)

        @pl.when(step == steps - 1)
        def _():
            if middle is not None:
                middle()
            finish()

    outs = pl.pallas_call(
        carrier, grid=grid, name=name + "_carry", in_specs=list(in_specs) + [_ANY] * nm,
        out_specs=list(out_specs) + [_ANY] * nm, out_shape=list(out_shape) + carry.out_shape,
        scratch_shapes=list(scratch_shapes) + carry.scratch,
        compiler_params=_params(*["arbitrary"] * len(grid)),
    )(*args, *carry.operands)
    return outs[:n_out], outs[n_out:]


def sum_slots(r, j=None):
    n, R, D = r.shape[-3:]
    tc = _row_tile(D, 256)

    def body(r_ref, o_ref):
        slot = (lambda s: r_ref[s]) if j is None else (lambda s: r_ref[0, s])
        acc = slot(0).astype(F32)
        for s in range(1, n):
            acc = acc + slot(s).astype(F32)
        o_ref[...] = acc

    spec = (pl.BlockSpec((n, R, tc), lambda i: (0, 0, i)) if j is None
            else pl.BlockSpec((1, n, R, tc), lambda i: (j, 0, 0, i)))
    return pl.pallas_call(
        body, grid=(D // tc,), name="sum_slots", in_specs=[spec],
        out_specs=pl.BlockSpec((R, tc), lambda i: (0, i)), out_shape=_sds((R, D), F32),
        compiler_params=_params("parallel"),
    )(r)


def allreduce_small(v):
    R, C = v.shape

    def body(v_ref, o_ref, buf, send_sems, recv_sems):
        me = _mesh_pos()
        my = _dev_index(me)
        buf[my] = v_ref[...]

        def copy(k, slot, to):
            return pltpu.make_async_remote_copy(
                src_ref=v_ref, dst_ref=buf.at[slot], send_sem=send_sems.at[k - 1], recv_sem=recv_sems.at[k - 1],
                device_id=to, device_id_type=_MESH)

        sends = []
        for k in range(1, N_DEV):
            cp = copy(k, my, _flip(me, k))
            cp.start()
            sends.append(cp)
        for k in range(1, N_DEV):
            peer = _flip(me, k)
            copy(k, _dev_index(peer), peer).wait_recv()
        acc = buf[0]
        for s in range(1, N_DEV):
            acc = acc + buf[s]
        o_ref[...] = acc
        for cp in sends:
            cp.wait_send()

    vm = pl.BlockSpec(memory_space=pltpu.VMEM)
    return pl.pallas_call(
        body, name="allreduce_small", in_specs=[vm], out_specs=vm, out_shape=_sds((R, C), F32),
        scratch_shapes=[pltpu.VMEM((N_DEV, R, C), F32), pltpu.SemaphoreType.DMA((7,)), pltpu.SemaphoreType.DMA((7,))],
    )(v)


WEIGHTS = ["ffn1_w_gate", "ffn1_w_up", "ffn1_w_down", "ln1_g", "ln1_b", "w_in", "rel_bias", "sb_out_g", "ca_out_g",
           "w_out", "ln2_g", "ln2_b", "ffn2_w_gate", "ffn2_w_up", "ffn2_w_down", "ln3_g", "ln3_b"]
BIG = ["ffn1_w_gate", "ffn1_w_up", "ffn2_w_gate", "ffn2_w_up", "w_in", "ffn1_w_down", "ffn2_w_down", "w_out"]
TRANSPOSED = BIG[:5]
SMALL = [n for n in WEIGHTS if n not in BIG]


def _pack(vals):
    flat = jnp.concatenate([v.reshape(-1).astype(F32) for v in vals])
    pad = -flat.shape[0] % (8 * 128)
    return jnp.pad(flat, (0, pad)).reshape(-1, 128)


def _unpack(packed, like):
    flat = packed.reshape(-1)
    out, off = [], 0
    for v in like:
        out.append(flat[off:off + v.size].reshape(v.shape))
        off += v.size
    return out


def _row_blocks(w, n):
    r = w.shape[0] // n
    return [w[i * r:(i + 1) * r] for i in range(n)]


class _Arrivals:
    def __init__(self):
        self.by_kernel = {}

    def ride(self, kernel_name, weights, names, local, layer):
        self.by_kernel[kernel_name] = (weights, names, _Gather([local[n][layer] for n in names]))

    def plan(self, kernel_name):
        return self.by_kernel[kernel_name][2] if kernel_name in self.by_kernel else None

    def landed(self, kernel_name, outs):
        if kernel_name in self.by_kernel:
            weights, names, _ = self.by_kernel[kernel_name]
            weights.update(zip(names, outs))


def _layer_fwd(x, xb, W, P, bias, l, B, S, arrivals):
    T = B * S
    sv = {"xb": xb, "bias": bias}
    (sv["sg1"], sv["t1"], sv["a1"]), got = ffn_up(xb, W["ffn1_w_gate"], W["ffn1_w_up"], carry=arrivals.plan("ffn_up"))
    arrivals.landed("ffn_up", got)
    sv["u1"], x1, sv["x1b"] = mm_nn_ln([(sv["a1"], W["ffn1_w_down"])], x, P["ln1_g"][l], P["ln1_b"][l], FFN_RESIDUAL,
                                       "ffn_down_ln")

    sv["h"] = proj_in(sv["x1b"], W["w_in"], B, S)
    sv["gA"] = P["sb_out_g"][l].reshape(1, -1)
    sv["gB"] = P["ca_out_g"][l].reshape(1, -1)
    (sv["oa"], ma, sv["cars"]), got = sb_fwd(sv["h"], sv["gA"], carry=arrivals.plan("sb_fwd"))
    arrivals.landed("sb_fwd", got)
    (sv["ob"], mb), got = ca_fwd(sv["h"], bias, sv["gB"], carry=arrivals.plan("ca_fwd"))
    arrivals.landed("ca_fwd", got)
    sv["ma"], sv["mb"] = ma.reshape(T, -1), mb.reshape(T, -1)
    sv["u2"], x2, sv["x2b"] = mm_nn_ln(list(zip([sv["ma"], sv["mb"]], _row_blocks(W["w_out"], 2))), x1,
                                       P["ln2_g"][l], P["ln2_b"][l], 1.0, "proj_out_ln")

    (sv["sg2"], sv["t2"], sv["a2"]), _ = ffn_up(sv["x2b"], W["ffn2_w_gate"], W["ffn2_w_up"])
    sv["u3"], x3, x3b = mm_nn_ln([(sv["a2"], W["ffn2_w_down"])], x2, P["ln3_g"][l], P["ln3_b"][l], FFN_RESIDUAL,
                                 "ffn_down_ln")
    return x3, x3b, sv


class _Riders:
    def __init__(self):
        self.received = []

    def plan(self, entries):
        return _Exchange([e[2] for e in entries]) if entries else None

    def landed(self, entries, slots):
        for (names, layer, _), r in zip(entries, slots or []):
            for j, name in enumerate(names):
                self.received.append((name, layer, r, j if r.ndim == 4 else None))


def _ffn_bwd(dr, dxa, xb, sg, t, a, wg, wu, wd, riders, ride, below, names, layer):
    dhg, dhu = ffn_bwd_mid(dr, wd, sg, t)
    pairs = [(dhg, wg), (dhu, wu)]
    if below is not None:
        down = mm_nn_ln_bwd(pairs, dxa, *below, name="ffn_dx_ln")
        grads, slots = ffn_dw(dhg, dhu, a, xb, dr, carry=riders.plan(ride))
        riders.landed(ride, slots)
        return down, [(names, layer, grads)]
    down = mm_nn(pairs, add=dxa, name="ffn_dx")
    for name, lhs, rhs in zip(names, (dhg, dhu, a), (xb, xb, dr)):
        g, slots = mm_tn([lhs], rhs, name="ffn_dw_last", carry=riders.plan(ride))
        riders.landed(ride, slots)
        ride = [((name,), layer, g)]
    return down, ride


def _layer_bwd(dxa, dr, sv, W, P, l, B, S, riders, pending, below):
    T = B * S
    G = {}
    (dxa, dr, dg, db), ride = _ffn_bwd(
        dr, dxa, sv["x2b"], sv["sg2"], sv["t2"], sv["a2"], W["ffn2_w_gate"], W["ffn2_w_up"], W["ffn2_w_down"],
        riders, [], (sv["u2"], P["ln2_g"][l], 1.0), ("ffn2_w_gate", "ffn2_w_up", "ffn2_w_down"), l)
    G["ln2_g"], G["ln2_b"] = dg.sum(0), db.sum(0)
    dmix = mm_nt(dr, W["w_out"], name="proj_out_dx").reshape(B, S, -1)
    g_out = mm_tn([sv["ma"], sv["mb"]], dr, name="proj_out_dw")[0]
    (dqa, dka, dva, dga), slots = sb_bwd(sv["h"], sv["oa"], dmix, sv["gA"], sv["cars"], carry=riders.plan(pending))
    riders.landed(pending, slots)
    (dqb, dkb, dvb, dgb, dbias), slots = ca_bwd(sv["h"], sv["bias"], sv["ob"], dmix, sv["gB"], carry=riders.plan(ride))
    riders.landed(ride, slots)
    G["sb_out_g"] = dga.sum((0, 1, 2))
    G["ca_out_g"] = dgb.sum((0, 1, 2))
    G["rel_bias"] = rel_bias_grad(dbias)
    dh = [t.reshape(T, -1) for t in (dqa, dka, dva, dqb, dkb, dvb)]
    dxa, dr, dg, db = mm_nn_ln_bwd(list(zip(dh, _row_blocks(W["w_in"], 6))), dxa, sv["u1"], P["ln1_g"][l], FFN_RESIDUAL,
                                   name="proj_in_dx_ln")
    G["ln1_g"], G["ln1_b"] = dg.sum(0), db.sum(0)
    g_in = mm_tn(dh, sv["x1b"], name="proj_in_dw")[0]

    proj = [(("w_in",), l, g_in), (("w_out",), l, g_out)]
    down, pending = _ffn_bwd(dr, dxa, sv["xb"], sv["sg1"], sv["t1"], sv["a1"], W["ffn1_w_gate"], W["ffn1_w_up"],
                             W["ffn1_w_down"], riders, [] if below is not None else proj, below,
                             ("ffn1_w_gate", "ffn1_w_up", "ffn1_w_down"), l)
    return down, G, pending + proj if below is not None else pending


def kernel(x, ffn1_w_gate, ffn1_w_up, ffn1_w_down, ln1_g, ln1_b, w_in, rel_bias, sb_out_g, ca_out_g, w_out, ln2_g, ln2_b, ffn2_w_gate, ffn2_w_up, ffn2_w_down, ln3_g, ln3_b, loss_target, m_ffn1_w_gate, m_ffn1_w_up, m_ffn1_w_down, m_ln1_g, m_ln1_b, m_w_in, m_rel_bias, m_sb_out_g, m_ca_out_g, m_w_out, m_ln2_g, m_ln2_b, m_ffn2_w_gate, m_ffn2_w_up, m_ffn2_w_down, m_ln3_g, m_ln3_b, v_ffn1_w_gate, v_ffn1_w_up, v_ffn1_w_down, v_ln1_g, v_ln1_b, v_w_in, v_rel_bias, v_sb_out_g, v_ca_out_g, v_w_out, v_ln2_g, v_ln2_b, v_ffn2_w_gate, v_ffn2_w_up, v_ffn2_w_down, v_ln3_g, v_ln3_b):
    given = dict(locals())
    P = {n: given[n] for n in WEIGHTS}
    M = {n: given["m_" + n] for n in WEIGHTS}
    V = {n: given["v_" + n] for n in WEIGHTS}
    B, S, D = x.shape
    L = ln1_g.shape[0]

    local = {n: (jnp.swapaxes(P[n], 1, 2) if n in TRANSPOSED else P[n]).astype(BF16) for n in BIG}
    bias = ca_bias(rel_bias)

    xf = x.reshape(B * S, D)
    xb = xf.astype(BF16)
    W, saved = [{} for _ in range(L)], []
    first = ["ffn1_w_gate", "ffn1_w_up"]
    W[0].update(zip(first, _run_plan(_Gather([local[n][0] for n in first]), "gather_weights")))
    for l in range(L):
        arrivals = _Arrivals()
        if l == 0:
            arrivals.ride("ffn_up", W[0], ["ffn1_w_down", "w_in", "w_out"], local, 0)
            arrivals.ride("ca_fwd", W[0], ["ffn2_w_gate", "ffn2_w_up", "ffn2_w_down"], local, 0)
            if L > 1:
                arrivals.ride("sb_fwd", W[1], BIG, local, 1)
        elif l + 1 < L:
            arrivals.ride("sb_fwd", W[l + 1], BIG, local, l + 1)
        xf, xb, sv = _layer_fwd(xf, xb, W[l], P, bias[l], l, B, S, arrivals)
        saved.append(sv)
    dy, loss_part = loss_head(xf, loss_target.reshape(B * S, D))

    big_g = {n: [None] * L for n in BIG}
    small_g = {n: [None] * L for n in SMALL}
    riders, pending = _Riders(), []
    down = ln_bwd(saved[L - 1]["u3"], dy, ln3_g[L - 1], FFN_RESIDUAL)
    for l in reversed(range(L)):
        dxa, dr, dg, db = down
        small_g["ln3_g"][l], small_g["ln3_b"][l] = dg.sum(0), db.sum(0)
        below = (saved[l - 1]["u3"], ln3_g[l - 1], FFN_RESIDUAL) if l > 0 else None
        down, G, pending = _layer_bwd(dxa, dr, saved[l], W[l], P, l, B, S, riders, pending, below)
        for n, g in G.items():
            small_g[n][l] = g
    dx = down
    riders.landed(pending, _run_plan(riders.plan(pending), "exchange_grads"))
    for n, l, slots, j in riders.received:
        big_g[n][l] = sum_slots(slots, j)

    grads = {}
    for n in BIG:
        g = jnp.stack(big_g[n])
        grads[n] = jnp.swapaxes(g, 1, 2) if n in TRANSPOSED else g
    small_like = [P[n] for n in SMALL]
    packed = _pack([jnp.stack(small_g[n]) for n in SMALL] + [loss_part.sum()])
    total = allreduce_small(packed)
    *small_vals, loss = _unpack(total, small_like + [jnp.zeros((), F32)])
    grads.update(dict(zip(SMALL, small_vals)))

    delta, new_m, new_v = {}, {}, {}
    for n in BIG:
        shape = P[n].shape
        two_d = lambda a: a.reshape(shape[0] * shape[1], shape[2])
        d, m, v = adamw(two_d(P[n]), two_d(grads[n]), two_d(M[n]), two_d(V[n]))
        delta[n], new_m[n], new_v[n] = d.reshape(shape), m.reshape(shape), v.reshape(shape)
    one = jnp.ones((), F32)
    d, m, v = adamw(_pack(small_like + [one]), total, _pack([M[n] for n in SMALL] + [one]), _pack([V[n] for n in SMALL] + [one]))
    for dst, src in ((delta, d), (new_m, m), (new_v, v)):
        dst.update(dict(zip(SMALL, _unpack(src, small_like))))

    return (loss, dx.reshape(B, S, D), *[grads[n] for n in WEIGHTS], *[delta[n] for n in WEIGHTS],
            *[new_m[n] for n in WEIGHTS], *[new_v[n] for n in WEIGHTS])
```
